```python
import jax, jax.numpy as jnp
from jax import lax
import numpy as np

D_MODEL = 1024
BATCH = 8
SEQ = 16384
DEPTH = 2

N_A_LAYERS = DEPTH // 2
N_B_LAYERS = DEPTH - N_A_LAYERS

CONV_WIDTH = 31
CONV_CH = D_MODEL

HEAD_DIM = 64
HEADS_PER_GROUP = D_MODEL // HEAD_DIM
ATTN_WIDTH = HEADS_PER_GROUP * HEAD_DIM
DILATED_GROUPS = ((128, 1), (512, 4), (2048, 16))
N_GROUPS = len(DILATED_GROUPS)
Q_WIDTH = N_GROUPS * ATTN_WIDTH
BLOCK = 128
ALIBI_MAX_EXP = 8.0

ALPHA = (2.0 * DEPTH) ** 0.25
BETA = (8.0 * DEPTH) ** -0.25
LN_EPS = 1e-5

kernel_name = "yoco_conformer_conv_dilated_attn_deepnorm"


def layer_norm(x, g, b):
    xf = x.astype(jnp.float32)
    mu = jnp.mean(xf, axis=-1, keepdims=True)
    xc = xf - mu
    var = jnp.mean(xc * xc, axis=-1, keepdims=True)
    y = xc * lax.rsqrt(var + LN_EPS) * g.astype(jnp.float32) + b.astype(jnp.float32)
    return y.astype(x.dtype)


def alibi_slopes(n_heads):
    h = jnp.arange(1, n_heads + 1, dtype=jnp.float32)
    return jnp.exp2(-ALIBI_MAX_EXP * h / n_heads)


def conformer_conv_branch(x, w_in, b_in, w_dw, b_dw, ln_g, ln_b, w_out, b_out):
    h = x @ w_in + b_in
    a, a_gate, z = jnp.split(h, 3, axis=-1)
    u = a * jax.nn.sigmoid(a_gate)
    u = lax.conv_general_dilated(
        u, w_dw[:, None, :].astype(u.dtype), window_strides=(1,),
        padding=((CONV_WIDTH - 1, 0),),
        dimension_numbers=('NWC', 'WIO', 'NWC'),
        feature_group_count=CONV_CH) + b_dw
    u = jax.nn.silu(layer_norm(u, ln_g, ln_b))
    return (u * jax.nn.silu(z)) @ w_out + b_out


def dilated_window_attention(q, k, v, slopes, window, dilation):
    B, S, H, hd = q.shape
    n_back = window // dilation
    span = dilation * BLOCK
    s_pad = -(-S // span) * span
    L = s_pad // dilation
    nb = L // BLOCK
    pad = ((0, 0), (0, s_pad - S), (0, 0), (0, 0))

    def to_blocks(t):
        t = jnp.pad(t.astype(jnp.float32), pad).reshape(B, L, dilation, H, hd)
        return t.transpose(0, 2, 1, 3, 4).reshape(B, dilation, nb, BLOCK, H, hd)

    def with_prev(t):
        prev = jnp.pad(t[:, :, :-1], ((0, 0), (0, 0), (1, 0), (0, 0), (0, 0), (0, 0)))
        return jnp.concatenate([prev, t], axis=3)

    qb = to_blocks(q)
    kk = with_prev(to_blocks(k))
    vv = with_prev(to_blocks(v))

    scores = jnp.einsum('brnqhd,brnkhd->brnhqk', qb, kk) * (hd ** -0.5)
    qi = jnp.arange(BLOCK)[:, None]
    kj = jnp.arange(2 * BLOCK)[None, :]
    dist = qi + BLOCK - kj
    band = (dist >= 0) & (dist <= n_back)
    first = (jnp.arange(nb) == 0)[:, None, None]
    valid = band[None] & ~(first & (kj < BLOCK)[None])
    bias = -slopes[:, None, None] * (dilation * dist).astype(jnp.float32)[None]
    scores = jnp.where(valid[None, None, :, None], scores + bias, -jnp.inf)

    m = jnp.max(scores, axis=-1, keepdims=True)
    p = jnp.exp(scores - m)
    denom = jnp.sum(p, axis=-1, keepdims=True)
    o = jnp.einsum('brnhqk,brnkhd->brnqhd', p / denom, vv)
    lse = (m + jnp.log(denom))[..., 0]

    o = o.reshape(B, dilation, L, H, hd).transpose(0, 2, 1, 3, 4).reshape(B, s_pad, H, hd)[:, :S]
    lse = lse.transpose(0, 1, 2, 4, 3).reshape(B, dilation, L, H).transpose(0, 2, 1, 3)
    lse = lse.reshape(B, s_pad, H)[:, :S]
    return o, lse


def dilated_attention_branch(x, w_in, w_out, b_out, k_shared, v_shared):
    B, S, _ = x.shape
    h = x @ w_in
    q = h[..., :Q_WIDTH].reshape(B, S, N_GROUPS, HEADS_PER_GROUP, HEAD_DIM)
    z = h[..., Q_WIDTH:]
    slopes = alibi_slopes(HEADS_PER_GROUP)
    outs, lses = [], []
    for g, (window, dilation) in enumerate(DILATED_GROUPS):
        o, l = dilated_window_attention(q[:, :, g], k_shared[:, :, g], v_shared[:, :, g],
                                        slopes, window, dilation)
        outs.append(o)
        lses.append(l)
    wts = jax.nn.softmax(jnp.stack(lses, axis=0), axis=0)
    o = jnp.sum(wts[..., None] * jnp.stack(outs, axis=0), axis=0)
    o = o.reshape(B, S, ATTN_WIDTH).astype(x.dtype)
    return (o * jax.nn.silu(z)) @ w_out + b_out


def _fwd_setup_inputs(seed: int = 0) -> dict:
    key = jax.random.key(seed)
    ks = jax.random.split(key, 16)
    f32 = jnp.float32
    nrm = lambda k, shape: jax.random.normal(k, shape, dtype=f32)
    C, D = CONV_CH, D_MODEL
    return {
        "x": nrm(ks[0], (BATCH, SEQ, D)),
        "a_w_in": nrm(ks[1], (N_A_LAYERS, D, 3 * C)) * D ** -0.5,
        "a_b_in": 0.02 * nrm(ks[2], (N_A_LAYERS, 3 * C)),
        "a_w_dw": nrm(ks[3], (N_A_LAYERS, CONV_WIDTH, C)) * CONV_WIDTH ** -0.5,
        "a_b_dw": 0.02 * nrm(ks[4], (N_A_LAYERS, C)),
        "a_ln_g": 1.0 + 0.02 * nrm(ks[5], (N_A_LAYERS, C)),
        "a_ln_b": 0.02 * nrm(ks[6], (N_A_LAYERS, C)),
        "a_w_out": nrm(ks[7], (N_A_LAYERS, C, D)) * (C ** -0.5 * BETA),
        "a_b_out": 0.02 * nrm(ks[8], (N_A_LAYERS, D)),
        "kv_w": nrm(ks[9], (D, 2 * Q_WIDTH)) * D ** -0.5,
        "b_w_in": nrm(ks[10], (N_B_LAYERS, D, Q_WIDTH + ATTN_WIDTH)) * D ** -0.5,
        "b_w_out": nrm(ks[11], (N_B_LAYERS, ATTN_WIDTH, D)) * (ATTN_WIDTH ** -0.5 * BETA),
        "b_b_out": 0.02 * nrm(ks[12], (N_B_LAYERS, D)),
        "post_ln_g": 1.0 + 0.02 * nrm(ks[13], (DEPTH, D)),
        "post_ln_b": 0.02 * nrm(ks[14], (DEPTH, D)),
    }


def _fwd_reference(x, a_w_in, a_b_in, a_w_dw, a_b_dw, a_ln_g, a_ln_b, a_w_out, a_b_out,
              kv_w, b_w_in, b_w_out, b_b_out, post_ln_g, post_ln_b):
    B, S, _ = x.shape
    k_shared = v_shared = None
    for layer in range(DEPTH):
        if layer < N_A_LAYERS:
            i = layer
            y = conformer_conv_branch(x, a_w_in[i], a_b_in[i], a_w_dw[i], a_b_dw[i],
                                      a_ln_g[i], a_ln_b[i], a_w_out[i], a_b_out[i])
        else:
            if layer == N_A_LAYERS:
                kv = (x @ kv_w).reshape(B, S, 2, N_GROUPS, HEADS_PER_GROUP, HEAD_DIM)
                k_shared, v_shared = kv[:, :, 0], kv[:, :, 1]
            i = layer - N_A_LAYERS
            y = dilated_attention_branch(x, b_w_in[i], b_w_out[i], b_b_out[i], k_shared, v_shared)
        x = layer_norm(ALPHA * x + y, post_ln_g[layer], post_ln_b[layer])
    return x


import jax as _jax
import jax.numpy as _jnp

TWIN_FORMAT = 'train_step'
FWD_PARAMS = ['x', 'a_w_in', 'a_b_in', 'a_w_dw', 'a_b_dw', 'a_ln_g', 'a_ln_b', 'a_w_out', 'a_b_out', 'kv_w', 'b_w_in', 'b_w_out', 'b_b_out', 'post_ln_g', 'post_ln_b']
TWIN_WEIGHTS = ['a_w_in', 'a_b_in', 'a_w_dw', 'a_b_dw', 'a_ln_g', 'a_ln_b', 'a_w_out', 'a_b_out', 'kv_w', 'b_w_in', 'b_w_out', 'b_b_out', 'post_ln_g', 'post_ln_b']
TWIN_DIFF_INPUT = 'x'
TWIN_INPUTS = ['x', 'a_w_in', 'a_b_in', 'a_w_dw', 'a_b_dw', 'a_ln_g', 'a_ln_b', 'a_w_out', 'a_b_out', 'kv_w', 'b_w_in', 'b_w_out', 'b_b_out', 'post_ln_g', 'post_ln_b', 'loss_target', 'm_a_w_in', 'm_a_b_in', 'm_a_w_dw', 'm_a_b_dw', 'm_a_ln_g', 'm_a_ln_b', 'm_a_w_out', 'm_a_b_out', 'm_kv_w', 'm_b_w_in', 'm_b_w_out', 'm_b_b_out', 'm_post_ln_g', 'm_post_ln_b', 'v_a_w_in', 'v_a_b_in', 'v_a_w_dw', 'v_a_b_dw', 'v_a_ln_g', 'v_a_ln_b', 'v_a_w_out', 'v_a_b_out', 'v_kv_w', 'v_b_w_in', 'v_b_w_out', 'v_b_b_out', 'v_post_ln_g', 'v_post_ln_b']
TWIN_OUTPUTS = ['loss', 'grad_x', 'grad_a_w_in', 'grad_a_b_in', 'grad_a_w_dw', 'grad_a_b_dw', 'grad_a_ln_g', 'grad_a_ln_b', 'grad_a_w_out', 'grad_a_b_out', 'grad_kv_w', 'grad_b_w_in', 'grad_b_w_out', 'grad_b_b_out', 'grad_post_ln_g', 'grad_post_ln_b', 'delta_a_w_in', 'delta_a_b_in', 'delta_a_w_dw', 'delta_a_b_dw', 'delta_a_ln_g', 'delta_a_ln_b', 'delta_a_w_out', 'delta_a_b_out', 'delta_kv_w', 'delta_b_w_in', 'delta_b_w_out', 'delta_b_b_out', 'delta_post_ln_g', 'delta_post_ln_b', 'new_m_a_w_in', 'new_m_a_b_in', 'new_m_a_w_dw', 'new_m_a_b_dw', 'new_m_a_ln_g', 'new_m_a_ln_b', 'new_m_a_w_out', 'new_m_a_b_out', 'new_m_kv_w', 'new_m_b_w_in', 'new_m_b_w_out', 'new_m_b_b_out', 'new_m_post_ln_g', 'new_m_post_ln_b', 'new_v_a_w_in', 'new_v_a_b_in', 'new_v_a_w_dw', 'new_v_a_b_dw', 'new_v_a_ln_g', 'new_v_a_ln_b', 'new_v_a_w_out', 'new_v_a_b_out', 'new_v_kv_w', 'new_v_b_w_in', 'new_v_b_w_out', 'new_v_b_b_out', 'new_v_post_ln_g', 'new_v_post_ln_b']
TWIN_LEAF_KINDS = {'loss': 'loss', 'grad_x': 'grad_x', 'grad_a_w_in': 'grad_w', 'grad_a_b_in': 'grad_w', 'grad_a_w_dw': 'grad_w', 'grad_a_b_dw': 'grad_w', 'grad_a_ln_g': 'grad_w', 'grad_a_ln_b': 'grad_w', 'grad_a_w_out': 'grad_w', 'grad_a_b_out': 'grad_w', 'grad_kv_w': 'grad_w', 'grad_b_w_in': 'grad_w', 'grad_b_w_out': 'grad_w', 'grad_b_b_out': 'grad_w', 'grad_post_ln_g': 'grad_w', 'grad_post_ln_b': 'grad_w', 'delta_a_w_in': 'delta_w', 'delta_a_b_in': 'delta_w', 'delta_a_w_dw': 'delta_w', 'delta_a_b_dw': 'delta_w', 'delta_a_ln_g': 'delta_w', 'delta_a_ln_b': 'delta_w', 'delta_a_w_out': 'delta_w', 'delta_a_b_out': 'delta_w', 'delta_kv_w': 'delta_w', 'delta_b_w_in': 'delta_w', 'delta_b_w_out': 'delta_w', 'delta_b_b_out': 'delta_w', 'delta_post_ln_g': 'delta_w', 'delta_post_ln_b': 'delta_w', 'new_m_a_w_in': 'new_m', 'new_m_a_b_in': 'new_m', 'new_m_a_w_dw': 'new_m', 'new_m_a_b_dw': 'new_m', 'new_m_a_ln_g': 'new_m', 'new_m_a_ln_b': 'new_m', 'new_m_a_w_out': 'new_m', 'new_m_a_b_out': 'new_m', 'new_m_kv_w': 'new_m', 'new_m_b_w_in': 'new_m', 'new_m_b_w_out': 'new_m', 'new_m_b_b_out': 'new_m', 'new_m_post_ln_g': 'new_m', 'new_m_post_ln_b': 'new_m', 'new_v_a_w_in': 'new_v', 'new_v_a_b_in': 'new_v', 'new_v_a_w_dw': 'new_v', 'new_v_a_b_dw': 'new_v', 'new_v_a_ln_g': 'new_v', 'new_v_a_ln_b': 'new_v', 'new_v_a_w_out': 'new_v', 'new_v_a_b_out': 'new_v', 'new_v_kv_w': 'new_v', 'new_v_b_w_in': 'new_v', 'new_v_b_w_out': 'new_v', 'new_v_b_b_out': 'new_v', 'new_v_post_ln_g': 'new_v', 'new_v_post_ln_b': 'new_v'}


def _forward(args):
    return _fwd_reference(*[args[k] for k in FWD_PARAMS])


def _output_shape():
    def fwd():
        inp = _fwd_setup_inputs(0)
        return _fwd_reference(*[inp[k] for k in FWD_PARAMS])
    out = _jax.eval_shape(fwd)
    return out.shape, out.dtype

N_MICROBATCH = 1
ADAM_LR = 0.001
ADAM_B1 = 0.9
ADAM_B2 = 0.999
ADAM_EPS = 1e-08
ADAM_WD = 0.01
ADAM_STEP = 10
PER_EXAMPLE_BATCH_AXIS = {'x': 0, 'loss_target': 0}
SHARED_INPUTS = []
_WEIGHT_DTYPES = {'a_w_in': _jnp.float32, 'a_b_in': _jnp.float32, 'a_w_dw': _jnp.float32, 'a_b_dw': _jnp.float32, 'a_ln_g': _jnp.float32, 'a_ln_b': _jnp.float32, 'a_w_out': _jnp.float32, 'a_b_out': _jnp.float32, 'kv_w': _jnp.float32, 'b_w_in': _jnp.float32, 'b_w_out': _jnp.float32, 'b_b_out': _jnp.float32, 'post_ln_g': _jnp.float32, 'post_ln_b': _jnp.float32}
MOMENT_SCALE = {'a_w_in': 3.930981e-02, 'a_b_in': 5.377307e-02, 'a_w_dw': 4.664449e-02, 'a_b_dw': 1.041786e-01, 'a_ln_g': 6.443933e-02, 'a_ln_b': 6.495159e-02, 'a_w_out': 9.537368e-02, 'a_b_out': 1.347711e+00, 'kv_w': 1.258361e-02, 'b_w_in': 1.570162e-02, 'b_w_out': 4.870335e-02, 'b_b_out': 1.357124e+00, 'post_ln_g': 9.065119e+01, 'post_ln_b': 2.952527e+00}


def _to_microbatches(a, axis):
    t = _jnp.moveaxis(a, axis, 0)
    t = t.reshape((N_MICROBATCH, t.shape[0] // N_MICROBATCH) + t.shape[1:])
    return _jnp.moveaxis(t, 1, axis + 1)


def setup_inputs(seed: int = 0) -> dict:
    inp = _fwd_setup_inputs(seed)
    key = _jax.random.fold_in(_jax.random.key(seed), 7919)
    shape, _ = _output_shape()
    out = dict(inp)
    out["loss_target"] = _jax.random.normal(_jax.random.fold_in(key, 0), shape, _jnp.float32)
    for i, name in enumerate(TWIN_WEIGHTS):
        w = inp[name].astype(_jnp.float32)
        if MOMENT_SCALE is None:
            s = _jnp.sqrt(_jnp.mean(_jnp.square(w)) + 1e-30)
        else:
            s = MOMENT_SCALE[name]
        km, kv = _jax.random.split(_jax.random.fold_in(key, i + 1))
        out[name] = w
        out["m_" + name] = s * _jax.random.normal(km, w.shape, _jnp.float32)
        out["v_" + name] = (s * s) * _jax.random.uniform(kv, w.shape, _jnp.float32, 0.5, 1.5)
    if N_MICROBATCH > 1:
        for name, axis in PER_EXAMPLE_BATCH_AXIS.items():
            out[name] = _to_microbatches(out[name], axis)
    return {'x': out['x'], 'a_w_in': out['a_w_in'], 'a_b_in': out['a_b_in'], 'a_w_dw': out['a_w_dw'], 'a_b_dw': out['a_b_dw'], 'a_ln_g': out['a_ln_g'], 'a_ln_b': out['a_ln_b'], 'a_w_out': out['a_w_out'], 'a_b_out': out['a_b_out'], 'kv_w': out['kv_w'], 'b_w_in': out['b_w_in'], 'b_w_out': out['b_w_out'], 'b_b_out': out['b_b_out'], 'post_ln_g': out['post_ln_g'], 'post_ln_b': out['post_ln_b'], 'loss_target': out['loss_target'], 'm_a_w_in': out['m_a_w_in'], 'm_a_b_in': out['m_a_b_in'], 'm_a_w_dw': out['m_a_w_dw'], 'm_a_b_dw': out['m_a_b_dw'], 'm_a_ln_g': out['m_a_ln_g'], 'm_a_ln_b': out['m_a_ln_b'], 'm_a_w_out': out['m_a_w_out'], 'm_a_b_out': out['m_a_b_out'], 'm_kv_w': out['m_kv_w'], 'm_b_w_in': out['m_b_w_in'], 'm_b_w_out': out['m_b_w_out'], 'm_b_b_out': out['m_b_b_out'], 'm_post_ln_g': out['m_post_ln_g'], 'm_post_ln_b': out['m_post_ln_b'], 'v_a_w_in': out['v_a_w_in'], 'v_a_b_in': out['v_a_b_in'], 'v_a_w_dw': out['v_a_w_dw'], 'v_a_b_dw': out['v_a_b_dw'], 'v_a_ln_g': out['v_a_ln_g'], 'v_a_ln_b': out['v_a_ln_b'], 'v_a_w_out': out['v_a_w_out'], 'v_a_b_out': out['v_a_b_out'], 'v_kv_w': out['v_kv_w'], 'v_b_w_in': out['v_b_w_in'], 'v_b_w_out': out['v_b_w_out'], 'v_b_b_out': out['v_b_b_out'], 'v_post_ln_g': out['v_post_ln_g'], 'v_post_ln_b': out['v_post_ln_b']}


def _loss(weights, diff, rest, loss_target):
    with _jax.named_scope("forward"):
        args = {**rest, TWIN_DIFF_INPUT: diff, **{k: w.astype(_WEIGHT_DTYPES[k]) for k, w in weights.items()}}
        y = _forward(args)
    with _jax.named_scope("loss_head"):
        err = _jnp.square(y.astype(_jnp.float32) - loss_target)
        return 0.5 * _jnp.sum(_jnp.mean(err, axis=-1)) if err.ndim else 0.5 * err


def _adamw(w, g, m, v):
    m = ADAM_B1 * m + (1.0 - ADAM_B1) * g
    v = ADAM_B2 * v + (1.0 - ADAM_B2) * _jnp.square(g)
    m_hat = m / (1.0 - ADAM_B1 ** ADAM_STEP)
    v_hat = v / (1.0 - ADAM_B2 ** ADAM_STEP)
    delta = -ADAM_LR * (m_hat / (_jnp.sqrt(v_hat) + ADAM_EPS) + ADAM_WD * w)
    return delta, m, v


def reference(x, a_w_in, a_b_in, a_w_dw, a_b_dw, a_ln_g, a_ln_b, a_w_out, a_b_out, kv_w, b_w_in, b_w_out, b_b_out, post_ln_g, post_ln_b, loss_target, m_a_w_in, m_a_b_in, m_a_w_dw, m_a_b_dw, m_a_ln_g, m_a_ln_b, m_a_w_out, m_a_b_out, m_kv_w, m_b_w_in, m_b_w_out, m_b_b_out, m_post_ln_g, m_post_ln_b, v_a_w_in, v_a_b_in, v_a_w_dw, v_a_b_dw, v_a_ln_g, v_a_ln_b, v_a_w_out, v_a_b_out, v_kv_w, v_b_w_in, v_b_w_out, v_b_b_out, v_post_ln_g, v_post_ln_b):
    given = dict(x=x, a_w_in=a_w_in, a_b_in=a_b_in, a_w_dw=a_w_dw, a_b_dw=a_b_dw, a_ln_g=a_ln_g, a_ln_b=a_ln_b, a_w_out=a_w_out, a_b_out=a_b_out, kv_w=kv_w, b_w_in=b_w_in, b_w_out=b_w_out, b_b_out=b_b_out, post_ln_g=post_ln_g, post_ln_b=post_ln_b, loss_target=loss_target, m_a_w_in=m_a_w_in, m_a_b_in=m_a_b_in, m_a_w_dw=m_a_w_dw, m_a_b_dw=m_a_b_dw, m_a_ln_g=m_a_ln_g, m_a_ln_b=m_a_ln_b, m_a_w_out=m_a_w_out, m_a_b_out=m_a_b_out, m_kv_w=m_kv_w, m_b_w_in=m_b_w_in, m_b_w_out=m_b_w_out, m_b_b_out=m_b_b_out, m_post_ln_g=m_post_ln_g, m_post_ln_b=m_post_ln_b, v_a_w_in=v_a_w_in, v_a_b_in=v_a_b_in, v_a_w_dw=v_a_w_dw, v_a_b_dw=v_a_b_dw, v_a_ln_g=v_a_ln_g, v_a_ln_b=v_a_ln_b, v_a_w_out=v_a_w_out, v_a_b_out=v_a_b_out, v_kv_w=v_kv_w, v_b_w_in=v_b_w_in, v_b_w_out=v_b_w_out, v_b_b_out=v_b_b_out, v_post_ln_g=v_post_ln_g, v_post_ln_b=v_post_ln_b)
    weights = {n: given[n] for n in TWIN_WEIGHTS}
    shared = {n: given[n] for n in SHARED_INPUTS}
    per_example = {n: given[n] for n in ['x']}
    grad_fn = _jax.value_and_grad(_loss, argnums=(0, 1))

    def one_microbatch(ex, loss_target):
        ex = dict(ex)
        diff = ex.pop(TWIN_DIFF_INPUT)
        return grad_fn(weights, diff, {**shared, **ex}, loss_target)

    if N_MICROBATCH == 1:
        loss, (grad_w, grad_x) = one_microbatch(per_example, given["loss_target"])
    else:
        def body(carry, xs):
            loss_sum, grad_sum = carry
            l_k, (gw_k, gx_k) = one_microbatch(xs[0], xs[1])
            with _jax.named_scope("update"):
                return (loss_sum + l_k, _jax.tree.map(_jnp.add, grad_sum, gw_k)), gx_k

        init = (_jnp.zeros((), _jnp.float32), _jax.tree.map(_jnp.zeros_like, weights))
        (loss, grad_w), grad_x = _jax.lax.scan(body, init, (per_example, given["loss_target"]))
    with _jax.named_scope("update"):
        delta_w, new_m, new_v = {}, {}, {}
        for n in TWIN_WEIGHTS:
            delta_w[n], new_m[n], new_v[n] = _adamw(weights[n], grad_w[n], given["m_" + n], given["v_" + n])
    return (loss, grad_x, *[grad_w[n] for n in TWIN_WEIGHTS], *[delta_w[n] for n in TWIN_WEIGHTS],
            *[new_m[n] for n in TWIN_WEIGHTS], *[new_v[n] for n in TWIN_WEIGHTS])
```

```python
import functools
import math

import numpy as np
import jax
import jax.numpy as jnp
from jax import lax
from jax.experimental import pallas as pl
from jax.experimental.pallas import tpu as pltpu

F32 = jnp.float32
BF16 = jnp.bfloat16
MESH = pl.DeviceIdType.MESH

D = 1024
N_DEV = 8
HEAD_DIM = 64
N_HEADS = 16
DILATIONS = (1, 4, 16)
BLK = 128
CONV_W = 31
ALPHA = (2.0 * 2) ** 0.25
LN_EPS = 1e-5
SLOPES = tuple(2.0 ** (-8.0 * (h + 1) / N_HEADS) for h in range(N_HEADS))
NEG = -1e30

ADAM_LR = 0.001
ADAM_B1 = 0.9
ADAM_B2 = 0.999
ADAM_EPS = 1e-08
ADAM_WD = 0.01
ADAM_STEP = 10

VMEM_CAP_MB = 64


def _params(vmem_mb, n_grid=0):
    sem = ("arbitrary",) * n_grid if n_grid else None
    return pltpu.CompilerParams(dimension_semantics=sem, vmem_limit_bytes=min(vmem_mb, VMEM_CAP_MB - 6) * 2 ** 20)


def _sds(shape, dtype):
    return jax.ShapeDtypeStruct(tuple(shape), dtype)


def _sigmoid(v):
    return jax.nn.sigmoid(v)


def _dsilu(v, s):
    return s * (1.0 + v * (1.0 - s))


def _ln_stats(r):
    mu = jnp.mean(r, axis=-1, keepdims=True)
    xc = r - mu
    var = jnp.mean(xc * xc, axis=-1, keepdims=True)
    rstd = lax.rsqrt(var + LN_EPS)
    return xc * rstd, rstd


def _ln_bwd(dn, n, rstd):
    m1 = jnp.mean(dn, axis=-1, keepdims=True)
    m2 = jnp.mean(dn * n, axis=-1, keepdims=True)
    return rstd * (dn - m1 - n * m2)


def _rowsum8(v):
    tm, c = v.shape
    return v.reshape(tm // 8, 8, c).sum(axis=0)


def _acc_init(i, *refs):
    @pl.when(i == 0)
    def _():
        for r in refs:
            r[...] = jnp.zeros(r.shape, r.dtype)


def _acc_finish(i, last, *refs):
    @pl.when(i == last)
    def _():
        for r in refs:
            r[...] = jnp.broadcast_to(jnp.sum(r[...], axis=0, keepdims=True), r.shape)


def _dot(a, b):
    return jnp.dot(a, b, preferred_element_type=F32)


def _dot_nt(a, b):
    return lax.dot_general(a, b, (((1,), (1,)), ((), ())), preferred_element_type=F32)


def _dot_tn(a, b):
    return lax.dot_general(a, b, (((0,), (0,)), ((), ())), preferred_element_type=F32)


def _place():
    return lax.axis_index("x"), lax.axis_index("y"), lax.axis_index("c")


def _all_gather(name, arrays, dtypes):
    n = len(arrays)

    def body(*refs):
        ins, outs, stages = refs[:n], refs[n:2 * n], refs[2 * n:3 * n]
        send_sems, recv_sems, local_sems = refs[3 * n:]
        x, y, c = _place()
        me, sibling = (x, y, c), (x, y, 1 - c)
        chips = [(1 - x, y), (x, 1 - y), (1 - x, 1 - y)]

        def slot(ref, p):
            return ref.at[4 * p[0] + 2 * p[1] + p[2]]

        def copy(a, k, block, to, src=None):
            return pltpu.make_async_remote_copy(
                src_ref=slot(outs[a], block) if src is None else src, dst_ref=slot(outs[a], block),
                send_sem=send_sems.at[a, k], recv_sem=recv_sems.at[a, k], device_id=to, device_id_type=MESH)

        first, mine = [], []
        for a in range(n):
            stages[a][...] = ins[a][...].astype(stages[a].dtype)
            cp = pltpu.make_async_copy(stages[a], slot(outs[a], me), local_sems.at[a])
            cp.start()
            mine.append(cp)
            first.append(copy(a, 0, me, sibling, src=stages[a]))
            first += [copy(a, 1 + j, me, (*chip, c), src=stages[a]) for j, chip in enumerate(chips)]
        for cp in first:
            cp.start()
        passed = []
        for j, chip in enumerate(chips):
            for a in range(n):
                copy(a, 1 + j, (*chip, c), me).wait_recv()
                cp = copy(a, 4 + j, (*chip, c), sibling)
                cp.start()
                passed.append(cp)
        for a in range(n):
            copy(a, 0, sibling, me).wait_recv()
            for j, chip in enumerate(chips):
                copy(a, 4 + j, (*chip, 1 - c), me).wait_recv()
        for cp in first + passed:
            cp.wait_send()
        for cp in mine:
            cp.wait()

    stage_bytes = sum(math.prod(a.shape) * (jnp.dtype(a.dtype).itemsize + jnp.dtype(dt).itemsize)
                      for a, dt in zip(arrays, dtypes))
    return pl.pallas_call(
        body, name=name,
        out_shape=[_sds((N_DEV,) + a.shape, dt) for a, dt in zip(arrays, dtypes)],
        in_specs=[pl.BlockSpec(memory_space=pltpu.VMEM)] * n,
        out_specs=[pl.BlockSpec(memory_space=pl.ANY)] * n,
        scratch_shapes=[pltpu.VMEM(a.shape, dt) for a, dt in zip(arrays, dtypes)]
        + [pltpu.SemaphoreType.DMA((n, 7)), pltpu.SemaphoreType.DMA((n, 7)), pltpu.SemaphoreType.DMA((n,))],
        compiler_params=_params(stage_bytes // 2 ** 20 + 8),
    )(*arrays)


def _exchange_sibling(name, parts):
    n = len(parts)

    def body(*refs):
        ins, outs = refs[:n], refs[n:2 * n]
        send_sems, recv_sems = refs[2 * n:]
        x, y, c = _place()
        copies = []
        for a in range(n):
            for p in range(4):
                copies.append(pltpu.make_async_remote_copy(
                    src_ref=ins[a].at[2 * p + 1 - c], dst_ref=outs[a].at[p],
                    send_sem=send_sems.at[a, p], recv_sem=recv_sems.at[a, p],
                    device_id=(x, y, 1 - c), device_id_type=MESH))
        for cp in copies:
            cp.start()
        for cp in copies:
            cp.wait_recv()
        for cp in copies:
            cp.wait_send()

    return pl.pallas_call(
        body, name=name,
        out_shape=[_sds((4,) + p.shape[1:], p.dtype) for p in parts],
        in_specs=[pl.BlockSpec(memory_space=pl.ANY)] * n,
        out_specs=[pl.BlockSpec(memory_space=pl.ANY)] * n,
        scratch_shapes=[pltpu.SemaphoreType.DMA((n, 4)), pltpu.SemaphoreType.DMA((n, 4))],
    )(*parts)


def _exchange_chips(name, sums):
    n = len(sums)

    def body(*refs):
        ins, outs = refs[:n], refs[n:2 * n]
        send_sems, recv_sems, local_sems = refs[2 * n:]
        x, y, c = _place()
        my_chip = 2 * x + y
        chips = [(1 - x, y), (x, 1 - y), (1 - x, 1 - y)]
        copies, mine = [], []
        for a in range(n):
            cp = pltpu.make_async_copy(ins[a].at[my_chip], outs[a].at[my_chip], local_sems.at[a])
            cp.start()
            mine.append(cp)
            for k, (px, py) in enumerate(chips):
                copies.append(pltpu.make_async_remote_copy(
                    src_ref=ins[a].at[2 * px + py], dst_ref=outs[a].at[my_chip],
                    send_sem=send_sems.at[a, k], recv_sem=recv_sems.at[a, k],
                    device_id=(px, py, c), device_id_type=MESH))
        for cp in copies:
            cp.start()
        for cp in copies:
            cp.wait_recv()
        for cp in copies:
            cp.wait_send()
        for cp in mine:
            cp.wait()

    return pl.pallas_call(
        body, name=name,
        out_shape=[_sds(s.shape, s.dtype) for s in sums],
        in_specs=[pl.BlockSpec(memory_space=pl.ANY)] * n,
        out_specs=[pl.BlockSpec(memory_space=pl.ANY)] * n,
        scratch_shapes=[pltpu.SemaphoreType.DMA((n, 3)), pltpu.SemaphoreType.DMA((n, 3)),
                        pltpu.SemaphoreType.DMA((n,))],
    )(*sums)


def _pair_add(name, part, recv, c_idx):
    _, r, c = part.shape
    tr = min(r, 256)

    def body(c_ref, a_ref, b_ref, o_ref):
        o_ref[...] = a_ref[...] + b_ref[...]

    return pl.pallas_call(
        body, name=name,
        grid_spec=pltpu.PrefetchScalarGridSpec(
            num_scalar_prefetch=1, grid=(4, r // tr),
            in_specs=[pl.BlockSpec((1, tr, c), lambda p, i, cr: (2 * p + cr[0], i, 0)),
                      pl.BlockSpec((1, tr, c), lambda p, i, cr: (p, i, 0))],
            out_specs=pl.BlockSpec((1, tr, c), lambda p, i, cr: (p, i, 0))),
        out_shape=_sds((4, r, c), F32),
        compiler_params=_params(16, 2),
    )(c_idx, part, recv)


def _adamw_math(w, g, m, v):
    m = ADAM_B1 * m + (1.0 - ADAM_B1) * g
    v = ADAM_B2 * v + (1.0 - ADAM_B2) * (g * g)
    m_hat = m / (1.0 - ADAM_B1 ** ADAM_STEP)
    v_hat = v / (1.0 - ADAM_B2 ** ADAM_STEP)
    delta = -ADAM_LR * (m_hat / (jnp.sqrt(v_hat) + ADAM_EPS) + ADAM_WD * w)
    return delta, m, v


def _sum_adamw(name, recv, w, m, v):
    r, c = w.shape
    tr = min(r, 256)

    def body(p_ref, w_ref, m_ref, v_ref, g_ref, d_ref, nm_ref, nv_ref):
        g = (p_ref[0] + p_ref[1]) + (p_ref[2] + p_ref[3])
        g_ref[...] = g
        d_ref[...], nm_ref[...], nv_ref[...] = _adamw_math(w_ref[...], g, m_ref[...], v_ref[...])

    blk = pl.BlockSpec((tr, c), lambda i: (i, 0))
    return pl.pallas_call(
        body, name=name, grid=(r // tr,),
        in_specs=[pl.BlockSpec((4, tr, c), lambda i: (0, i, 0)), blk, blk, blk],
        out_specs=[blk] * 4, out_shape=[_sds((r, c), F32)] * 4,
        compiler_params=_params(24, 1),
    )(recv, w, m, v)


def _small_sum(name, gathered):
    _, r, c = gathered.shape

    def body(p_ref, o_ref):
        acc = p_ref[0]
        for j in range(1, N_DEV):
            acc = acc + p_ref[j]
        o_ref[...] = acc

    return pl.pallas_call(body, name=name, out_shape=_sds((r, c), F32),
                          in_specs=[pl.BlockSpec(memory_space=pltpu.VMEM)],
                          out_specs=pl.BlockSpec(memory_space=pltpu.VMEM))(gathered)


def _small_adamw(name, w, g, m, v):
    def body(w_ref, g_ref, m_ref, v_ref, d_ref, nm_ref, nv_ref):
        d_ref[...], nm_ref[...], nv_ref[...] = _adamw_math(w_ref[...], g_ref[...], m_ref[...], v_ref[...])

    vm = pl.BlockSpec(memory_space=pltpu.VMEM)
    return pl.pallas_call(body, name=name, out_shape=[_sds(w.shape, F32)] * 3,
                          in_specs=[vm] * 4, out_specs=[vm] * 3)(w, g, m, v)


def _a_in_proj(x, w_g, b_full, tm):
    s = x.shape[0]
    npd = w_g.shape[2]

    def body(x_ref, w_ref, b_ref, h_ref, u0_ref, xb_ref):
        xb = x_ref[...].astype(BF16)
        xb_ref[...] = xb
        for j in range(N_DEV):
            sl = slice(npd * j, npd * (j + 1))
            h_ref[:, sl] = _dot(xb, w_ref[j]) + b_ref[:, sl]
        u0_ref[...] = h_ref[:, 0:D] * _sigmoid(h_ref[:, D:2 * D])

    row = lambda i: (i, 0)
    return pl.pallas_call(
        body, name="a_in_proj", grid=(s // tm,),
        in_specs=[pl.BlockSpec((tm, D), row), pl.BlockSpec(w_g.shape, lambda i: (0, 0, 0)),
                  pl.BlockSpec((1, 3 * D), lambda i: (0, 0))],
        out_specs=[pl.BlockSpec((tm, 3 * D), row), pl.BlockSpec((tm, D), row), pl.BlockSpec((tm, D), row)],
        out_shape=[_sds((s, 3 * D), F32), _sds((s, D), F32), _sds((s, D), BF16)],
        compiler_params=_params(44, 1),
    )(x, w_g, b_full)


CONV_HALO = 32
CONV_CHUNK = 32


def _a_conv_gate(u0, h, w_dw, b_dw, ln_g, ln_b, tm):
    s = u0.shape[0]
    per = tm // CONV_HALO

    def body(u0_ref, halo_ref, z_ref, w_ref, bdw_ref, g_ref, b_ref, u1_ref, ga_ref, buf):
        i = pl.program_id(0)
        buf[0:CONV_HALO, :] = jnp.where(i > 0, halo_ref[...], 0.0)
        buf[CONV_HALO:, :] = u0_ref[...]
        off = CONV_HALO - (CONV_W - 1)
        for ch in range(tm // CONV_CHUNK):
            r0 = ch * CONV_CHUNK
            acc = jnp.broadcast_to(bdw_ref[...], (CONV_CHUNK, D))
            for j in range(CONV_W):
                acc = acc + w_ref[j:j + 1, :] * buf[r0 + off + j:r0 + off + j + CONV_CHUNK, :]
            u1_ref[r0:r0 + CONV_CHUNK, :] = acc
            n, _ = _ln_stats(acc)
            pre = n * g_ref[...] + b_ref[...]
            z = z_ref[r0:r0 + CONV_CHUNK, :]
            ga_ref[r0:r0 + CONV_CHUNK, :] = ((pre * _sigmoid(pre)) * (z * _sigmoid(z))).astype(BF16)

    row = lambda i: (i, 0)
    vec = pl.BlockSpec((1, D), lambda i: (0, 0))
    return pl.pallas_call(
        body, name="a_conv_gate", grid=(s // tm,),
        in_specs=[pl.BlockSpec((tm, D), row),
                  pl.BlockSpec((CONV_HALO, D), lambda i: (jnp.maximum(i * per - 1, 0), 0)),
                  pl.BlockSpec((tm, D), lambda i: (i, 2)),
                  pl.BlockSpec((32, D), lambda i: (0, 0)), vec, vec, vec],
        out_specs=[pl.BlockSpec((tm, D), row), pl.BlockSpec((tm, D), row)],
        out_shape=[_sds((s, D), F32), _sds((s, D), BF16)],
        scratch_shapes=[pltpu.VMEM((tm + CONV_HALO, D), F32)],
        compiler_params=_params(56, 1),
    )(u0, u0, h, w_dw, b_dw, ln_g, ln_b)


def _a_out_proj(ga, w, b, x, pg, pb, tm):
    s = x.shape[0]

    def body(ga_ref, w_ref, b_ref, x_ref, pg_ref, pb_ref, n_ref, rstd_ref, xb_ref):
        r = ALPHA * x_ref[...] + (_dot(ga_ref[...], w_ref[...]) + b_ref[...])
        n, rstd = _ln_stats(r)
        n_ref[...] = n
        rstd_ref[...] = rstd
        xb_ref[...] = (n * pg_ref[...] + pb_ref[...]).astype(BF16)

    row = lambda i: (i, 0)
    vec = pl.BlockSpec((1, D), lambda i: (0, 0))
    return pl.pallas_call(
        body, name="a_out_proj", grid=(s // tm,),
        in_specs=[pl.BlockSpec((tm, D), row), pl.BlockSpec((D, D), lambda i: (0, 0)), vec,
                  pl.BlockSpec((tm, D), row), vec, vec],
        out_specs=[pl.BlockSpec((tm, D), row), pl.BlockSpec((tm, 1), row), pl.BlockSpec((tm, D), row)],
        out_shape=[_sds((s, D), F32), _sds((s, 1), F32), _sds((s, D), BF16)],
        compiler_params=_params(32, 1),
    )(ga, w, b, x, pg, pb)


KV_CHUNK = 256


def _kv_col(t):
    b, w = divmod(t, D // KV_CHUNK)
    nb = 2 * b if b < 3 else 2 * (b - 3) + 1
    return nb * D + w * KV_CHUNK


def _kv_proj(xb, w_g, tm):
    s = xb.shape[0]
    npd = w_g.shape[2]
    per = npd // KV_CHUNK

    def body(x_ref, w_ref, o_ref):
        xv = x_ref[...]
        for j in range(N_DEV):
            hj = _dot(xv, w_ref[j]).astype(BF16)
            for cc in range(per):
                c0 = _kv_col(j * per + cc)
                o_ref[:, c0:c0 + KV_CHUNK] = hj[:, cc * KV_CHUNK:(cc + 1) * KV_CHUNK]

    return pl.pallas_call(
        body, name="kv_proj", grid=(s // tm,),
        in_specs=[pl.BlockSpec((tm, D), lambda i: (i, 0)), pl.BlockSpec(w_g.shape, lambda i: (0, 0, 0))],
        out_specs=pl.BlockSpec((tm, 6 * D), lambda i: (i, 0)),
        out_shape=_sds((s, 6 * D), BF16),
        compiler_params=_params(52, 1),
    )(xb, w_g)


def _b_in_proj(xb, w_g, tm):
    s = xb.shape[0]
    npd = w_g.shape[2]
    scale = HEAD_DIM ** -0.5

    def body(x_ref, w_ref, q_ref, z_ref):
        xv = x_ref[...]
        for j in range(N_DEV):
            hj = _dot(xv, w_ref[j])
            if j < 6:
                q_ref[:, npd * j:npd * (j + 1)] = (hj.astype(BF16) * scale).astype(BF16)
            else:
                z_ref[:, npd * (j - 6):npd * (j - 5)] = hj

    row = lambda i: (i, 0)
    return pl.pallas_call(
        body, name="b_in_proj", grid=(s // tm,),
        in_specs=[pl.BlockSpec((tm, D), row), pl.BlockSpec(w_g.shape, lambda i: (0, 0, 0))],
        out_specs=[pl.BlockSpec((tm, 3 * D), row), pl.BlockSpec((tm, D), row)],
        out_shape=[_sds((s, 3 * D), BF16), _sds((s, D), F32)],
        compiler_params=_params(44, 1),
    )(xb, w_g)


def _band_table(d):
    qi = np.arange(BLK)[:, None]
    kj = np.arange(2 * BLK)[None, :]
    dist = qi + BLK - kj
    ok = (dist >= 0) & (dist <= BLK)
    return jnp.asarray(np.where(ok, -(d * dist).astype(np.float32), np.float32(NEG)), dtype=F32)


def _head_masks():
    lane = lax.broadcasted_iota(jnp.int32, (1, BLK), 1)
    lo = (lane < HEAD_DIM).astype(BF16)
    return (lo, (1.0 - lo).astype(BF16))


def _pick_col(tile, lane, h):
    return jnp.sum(jnp.where(lane == h, tile, 0.0), axis=1, keepdims=True)


def _attn_fwd(q, kv, o_acc, lse_acc, z, g, first, last):
    d = DILATIONS[g]
    s = q.shape[0]
    lsub = s // d
    nb = lsub // BLK

    def body(*refs):
        q_ref, kvp_ref, kvc_ref, nd_ref = refs[:4]
        k = 4
        if not first:
            oa_ref, la_ref = refs[k:k + 2]
            k += 2
        if last:
            z_ref = refs[k]
            k += 1
        o_ref, l_ref = refs[k:k + 2]
        if last:
            gb_ref = refs[k + 2]
        n = pl.program_id(1)
        col = lax.broadcasted_iota(jnp.int32, (BLK, 2 * BLK), 1)
        nd = nd_ref[...] + jnp.where((n == 0) & (col < BLK), NEG, 0.0)
        lane = lax.broadcasted_iota(jnp.int32, (BLK, BLK), 1)
        masks = _head_masks()
        if not first:
            la = la_ref[...]
        lse_tile = jnp.zeros((BLK, BLK), F32)
        for hp in range(N_HEADS // 2):
            cs = slice(BLK * hp, BLK * (hp + 1))
            vs = slice(D + BLK * hp, D + BLK * (hp + 1))
            q2 = q_ref[:, cs]
            kcat = jnp.concatenate([kvp_ref[:, cs], kvc_ref[:, cs]], axis=0)
            vcat = jnp.concatenate([kvp_ref[:, vs], kvc_ref[:, vs]], axis=0)
            o2 = jnp.zeros((BLK, BLK), F32)
            for e in range(2):
                h = 2 * hp + e
                sc = _dot_nt(q2 * masks[e], kcat) + SLOPES[h] * nd
                m = jnp.max(sc, axis=1, keepdims=True)
                p = jnp.exp(sc - m)
                l = jnp.sum(p, axis=1, keepdims=True)
                oh = _dot(p.astype(BF16), vcat) / l
                lse = m + jnp.log(l)
                if not first:
                    old = _pick_col(la, lane, h)
                    mx = jnp.maximum(old, lse)
                    new = mx + jnp.log(jnp.exp(old - mx) + jnp.exp(lse - mx))
                    oh = oa_ref[:, cs] * jnp.exp(old - new) + oh * jnp.exp(lse - new)
                    lse = new
                in_head = (lane < HEAD_DIM) if e == 0 else (lane >= HEAD_DIM)
                o2 = jnp.where(in_head, oh, o2)
                lse_tile = jnp.where(lane == h, lse, lse_tile)
            o_ref[:, cs] = o2
            if last:
                zz = z_ref[:, cs]
                gb_ref[:, cs] = (o2 * (zz * _sigmoid(zz))).astype(BF16)
        l_ref[...] = lse_tile

    prev = lambda n: jnp.maximum(n - 1, 0)
    big = lambda f: pl.BlockSpec((BLK, D), f)
    kvv = kv.reshape(lsub, d * 6 * D)
    in_specs = [big(lambda r, n: (n, r * 3 + g)),
                pl.BlockSpec((BLK, 2 * D), lambda r, n: (prev(n), r * 3 + g)),
                pl.BlockSpec((BLK, 2 * D), lambda r, n: (n, r * 3 + g)),
                pl.BlockSpec((BLK, 2 * BLK), lambda r, n: (0, 0))]
    args = [q.reshape(lsub, d * 3 * D), kvv, kvv, _band_table(d)]
    own = big(lambda r, n: (n, r))
    own_l = pl.BlockSpec((BLK, BLK), lambda r, n: (n, r))
    if not first:
        in_specs += [own, own_l]
        args += [o_acc.reshape(lsub, d * D), lse_acc.reshape(lsub, d * BLK)]
    if last:
        in_specs += [own]
        args += [z.reshape(lsub, d * D)]
    out_specs = [own, own_l] + ([own] if last else [])
    out_shape = [_sds((lsub, d * D), F32), _sds((lsub, d * BLK), F32)] + ([_sds((lsub, d * D), BF16)] if last else [])
    outs = pl.pallas_call(
        body, name=f"attn_fwd_g{g}", grid=(d, nb), in_specs=in_specs, out_specs=out_specs, out_shape=out_shape,
        compiler_params=_params(24, 2),
    )(*args)
    return [o.reshape(s, -1) for o in outs]


def _b_out_loss(gb, w, b, n1, pg0, pb0, pg1, pb1, tgt, tm):
    s = gb.shape[0]
    last = s // tm - 1

    def body(gb_ref, w_ref, b_ref, n1_ref, pg0_ref, pb0_ref, pg1_ref, pb1_ref, t_ref,
             dr_ref, drb_ref, loss_ref, dpg_ref, dpb_ref, dbo_ref):
        i = pl.program_id(0)
        _acc_init(i, loss_ref, dpg_ref, dpb_ref, dbo_ref)
        x1 = n1_ref[...] * pg0_ref[...] + pb0_ref[...]
        r = ALPHA * x1 + (_dot(gb_ref[...], w_ref[...]) + b_ref[...])
        n, rstd = _ln_stats(r)
        err = (n * pg1_ref[...] + pb1_ref[...]) - t_ref[...]
        loss_ref[...] += _rowsum8(err * err)
        dx2 = err * (1.0 / D)
        dpg_ref[...] += _rowsum8(dx2 * n)
        dpb_ref[...] += _rowsum8(dx2)
        dr = _ln_bwd(dx2 * pg1_ref[...], n, rstd)
        dr_ref[...] = dr
        drb_ref[...] = dr.astype(BF16)
        dbo_ref[...] += _rowsum8(dr)
        _acc_finish(i, last, dpg_ref, dpb_ref, dbo_ref)

        @pl.when(i == last)
        def _():
            loss_ref[...] = jnp.broadcast_to((0.5 / D) * jnp.sum(loss_ref[...], keepdims=True), loss_ref.shape)

    row = lambda i: (i, 0)
    vec = pl.BlockSpec((1, D), lambda i: (0, 0))
    acc = pl.BlockSpec((8, D), lambda i: (0, 0))
    return pl.pallas_call(
        body, name="b_out_loss", grid=(s // tm,),
        in_specs=[pl.BlockSpec((tm, D), row), pl.BlockSpec((D, D), lambda i: (0, 0)), vec,
                  pl.BlockSpec((tm, D), row), vec, vec, vec, vec, pl.BlockSpec((tm, D), row)],
        out_specs=[pl.BlockSpec((tm, D), row), pl.BlockSpec((tm, D), row), acc, acc, acc, acc],
        out_shape=[_sds((s, D), F32), _sds((s, D), BF16)] + [_sds((8, D), F32)] * 4,
        compiler_params=_params(36, 1),
    )(gb, w, b, n1, pg0, pb0, pg1, pb1, tgt)


def _head_selector():
    sel = (np.arange(D)[:, None] // HEAD_DIM == np.arange(BLK)[None, :]).astype(np.float32)
    return jnp.asarray(sel, dtype=BF16)


def _b_out_bwd(drb, w, z, o, tm):
    s = drb.shape[0]

    def body(dr_ref, w_ref, z_ref, o_ref, sel_ref, do_ref, dh_ref, dd_ref):
        dg = _dot_nt(dr_ref[...], w_ref[...])
        zz = z_ref[...]
        sg = _sigmoid(zz)
        do = dg * (zz * sg)
        ov = o_ref[...]
        do_ref[...] = do.astype(BF16)
        dh_ref[...] = (dg * ov * _dsilu(zz, sg)).astype(BF16)
        prod = do * ov
        hi = prod.astype(BF16)
        lo = (prod - hi.astype(F32)).astype(BF16)
        dd_ref[...] = _dot(hi, sel_ref[...]) + _dot(lo, sel_ref[...])

    row = lambda i: (i, 0)
    return pl.pallas_call(
        body, name="b_out_bwd", grid=(s // tm,),
        in_specs=[pl.BlockSpec((tm, D), row), pl.BlockSpec((D, D), lambda i: (0, 0)),
                  pl.BlockSpec((tm, D), row), pl.BlockSpec((tm, D), row), pl.BlockSpec((D, BLK), lambda i: (0, 0))],
        out_specs=[pl.BlockSpec((tm, D), row), pl.BlockSpec((tm, D), lambda i: (i, 3)),
                   pl.BlockSpec((tm, BLK), row)],
        out_shape=[_sds((s, D), BF16), _sds((s, 4 * D), BF16), _sds((s, BLK), F32)],
        compiler_params=_params(36, 1),
    )(drb, w, z, o, _head_selector())


def _attn_bwd(q, kv, do, lse, dd, dhb, dkv, g):
    d = DILATIONS[g]
    s = q.shape[0]
    lsub = s // d
    nb = lsub // BLK
    first = dkv is None

    def body(*refs):
        q_ref, kvp_ref, kvc_ref, nd_ref, do_ref, l_ref, dd_ref = refs[:7]
        k = 7 + (1 if first else 2)
        dq_ref, dkv_ref, ck, cv = refs[k:k + 4]
        n = pl.program_id(1)

        @pl.when(n == 0)
        def _():
            ck[...] = jnp.zeros(ck.shape, F32)
            cv[...] = jnp.zeros(cv.shape, F32)

        @pl.when(n < nb)
        def _():
            col = lax.broadcasted_iota(jnp.int32, (BLK, 2 * BLK), 1)
            nd = nd_ref[...] + jnp.where((n == 0) & (col < BLK), NEG, 0.0)
            lane = lax.broadcasted_iota(jnp.int32, (BLK, BLK), 1)
            masks = _head_masks()
            lt = l_ref[...]
            dt = dd_ref[...]
            for hp in range(N_HEADS // 2):
                cs = slice(BLK * hp, BLK * (hp + 1))
                vs = slice(D + BLK * hp, D + BLK * (hp + 1))
                q2 = q_ref[:, cs]
                do2 = do_ref[:, cs]
                kcat = jnp.concatenate([kvp_ref[:, cs], kvc_ref[:, cs]], axis=0)
                vcat = jnp.concatenate([kvp_ref[:, vs], kvc_ref[:, vs]], axis=0)
                dq2 = jnp.zeros((BLK, BLK), F32)
                dkp = jnp.zeros((2 * BLK, BLK), F32)
                dvp = jnp.zeros((2 * BLK, BLK), F32)
                for e in range(2):
                    h = 2 * hp + e
                    qm = q2 * masks[e]
                    dom = do2 * masks[e]
                    sc = _dot_nt(qm, kcat) + SLOPES[h] * nd
                    p = jnp.exp(sc - _pick_col(lt, lane, h))
                    ds = p * (_dot_nt(dom, vcat) - _pick_col(dt, lane, h))
                    dsb = ds.astype(BF16)
                    in_head = (lane < HEAD_DIM) if e == 0 else (lane >= HEAD_DIM)
                    dq2 = jnp.where(in_head, _dot(dsb, kcat), dq2)
                    dkp = dkp + _dot(dsb.T, qm)
                    dvp = dvp + _dot(p.astype(BF16).T, dom)
                dq_ref[:, cs] = ((HEAD_DIM ** -0.5) * dq2).astype(BF16)
                dkv_ref[:, cs] = (ck[:, cs] + dkp[0:BLK]).astype(BF16)
                dkv_ref[:, vs] = (cv[:, cs] + dvp[0:BLK]).astype(BF16)
                ck[:, cs] = dkp[BLK:]
                cv[:, cs] = dvp[BLK:]

        @pl.when(n == nb)
        def _():
            dkv_ref[:, 0:D] = ck[...].astype(BF16)
            dkv_ref[:, D:2 * D] = cv[...].astype(BF16)

    cur = lambda n: jnp.minimum(n, nb - 1)
    prev = lambda n: jnp.maximum(jnp.minimum(n, nb - 1) - 1, 0)
    out_k = lambda n: jnp.maximum(n - 1, 0)
    big = lambda f: pl.BlockSpec((BLK, D), f)
    small = pl.BlockSpec((BLK, BLK), lambda r, n: (cur(n), r))
    hbm = pl.BlockSpec(memory_space=pl.ANY)
    kvv = kv.reshape(lsub, d * 6 * D)
    in_specs = [big(lambda r, n: (cur(n), r * 3 + g)),
                pl.BlockSpec((BLK, 2 * D), lambda r, n: (prev(n), r * 3 + g)),
                pl.BlockSpec((BLK, 2 * D), lambda r, n: (cur(n), r * 3 + g)),
                pl.BlockSpec((BLK, 2 * BLK), lambda r, n: (0, 0)),
                big(lambda r, n: (cur(n), r)), small, small, hbm]
    args = [q.reshape(lsub, d * 3 * D), kvv, kvv, _band_table(d),
            do.reshape(lsub, d * D), lse.reshape(lsub, d * BLK), dd.reshape(lsub, d * BLK),
            dhb.reshape(lsub, d * 4 * D)]
    aliases = {7: 0}
    if not first:
        in_specs.append(hbm)
        args.append(dkv.reshape(lsub, d * 6 * D))
        aliases[8] = 1
    out_specs = [big(lambda r, n: (cur(n), r * 4 + g)),
                 pl.BlockSpec((BLK, 2 * D), lambda r, n: (out_k(n), r * 3 + g))]
    out_shape = [_sds((lsub, d * 4 * D), BF16), _sds((lsub, d * 6 * D), BF16)]
    dhb, dkv = pl.pallas_call(
        body, name=f"attn_bwd_g{g}", grid=(d, nb + 1), in_specs=in_specs, out_specs=out_specs, out_shape=out_shape,
        scratch_shapes=[pltpu.VMEM((BLK, D), F32), pltpu.VMEM((BLK, D), F32)],
        input_output_aliases=aliases, compiler_params=_params(24, 2),
    )(*args)
    return dhb.reshape(s, 4 * D), dkv.reshape(s, 6 * D)


def _b_in_bwd(dr2, dhb, dkv, wb_g, wkv_g, n1, rstd1, pg0, tm):
    s = dr2.shape[0]
    last = s // tm - 1
    nb_, nkv = wb_g.shape[2], wkv_g.shape[2]
    per = nkv // KV_CHUNK

    def body(dr2_ref, dh_ref, dkv_ref, wb_hbm, wkv_hbm, n_ref, rstd_ref, pg_ref,
             dr_ref, drb_ref, dpg_ref, dpb_ref, dbo_ref, wb, wkv):
        i = pl.program_id(0)

        @pl.when(i == 0)
        def _():
            pltpu.sync_copy(wb_hbm, wb)
            pltpu.sync_copy(wkv_hbm, wkv)

        _acc_init(i, dpg_ref, dpb_ref, dbo_ref)
        acc = ALPHA * dr2_ref[...]
        for j in range(N_DEV):
            acc = acc + _dot_nt(dh_ref[:, nb_ * j:nb_ * (j + 1)], wb[j])
            for cc in range(per):
                c0 = _kv_col(j * per + cc)
                acc = acc + _dot_nt(dkv_ref[:, c0:c0 + KV_CHUNK], wkv[j, :, cc * KV_CHUNK:(cc + 1) * KV_CHUNK])
        n = n_ref[...]
        dpg_ref[...] += _rowsum8(acc * n)
        dpb_ref[...] += _rowsum8(acc)
        dr = _ln_bwd(acc * pg_ref[...], n, rstd_ref[...])
        dr_ref[...] = dr
        drb_ref[...] = dr.astype(BF16)
        dbo_ref[...] += _rowsum8(dr)
        _acc_finish(i, last, dpg_ref, dpb_ref, dbo_ref)

    row = lambda i: (i, 0)
    hbm = pl.BlockSpec(memory_space=pl.ANY)
    acc_spec = pl.BlockSpec((8, D), lambda i: (0, 0))
    return pl.pallas_call(
        body, name="b_in_bwd", grid=(s // tm,),
        in_specs=[pl.BlockSpec((tm, D), row), pl.BlockSpec((tm, 4 * D), row), pl.BlockSpec((tm, 6 * D), row),
                  hbm, hbm, pl.BlockSpec((tm, D), row), pl.BlockSpec((tm, 1), row),
                  pl.BlockSpec((1, D), lambda i: (0, 0))],
        out_specs=[pl.BlockSpec((tm, D), row), pl.BlockSpec((tm, D), row), acc_spec, acc_spec, acc_spec],
        out_shape=[_sds((s, D), F32), _sds((s, D), BF16)] + [_sds((8, D), F32)] * 3,
        scratch_shapes=[pltpu.VMEM(wb_g.shape, BF16), pltpu.VMEM(wkv_g.shape, BF16)],
        compiler_params=_params(56, 1),
    )(dr2, dhb, dkv, wb_g, wkv_g, n1, rstd1, pg0)


def _a_out_bwd(drb, w, u1, h, ln_g, ln_b, tm):
    s = drb.shape[0]
    last = s // tm - 1

    def body(dr_ref, w_ref, u1_ref, z_ref, g_ref, b_ref, du1_ref, dh_ref, dg_ref, db_ref, dbz_ref):
        i = pl.program_id(0)
        _acc_init(i, dg_ref, db_ref, dbz_ref)
        dga = _dot_nt(dr_ref[...], w_ref[...])
        n, rstd = _ln_stats(u1_ref[...])
        pre = n * g_ref[...] + b_ref[...]
        sp = _sigmoid(pre)
        zz = z_ref[...]
        sz = _sigmoid(zz)
        dz = dga * (pre * sp) * _dsilu(zz, sz)
        dh_ref[...] = dz.astype(BF16)
        dbz_ref[...] += _rowsum8(dz)
        dpre = dga * (zz * sz) * _dsilu(pre, sp)
        dg_ref[...] += _rowsum8(dpre * n)
        db_ref[...] += _rowsum8(dpre)
        du1_ref[...] = _ln_bwd(dpre * g_ref[...], n, rstd)
        _acc_finish(i, last, dg_ref, db_ref, dbz_ref)

    row = lambda i: (i, 0)
    vec = pl.BlockSpec((1, D), lambda i: (0, 0))
    acc_spec = pl.BlockSpec((8, D), lambda i: (0, 0))
    return pl.pallas_call(
        body, name="a_out_bwd", grid=(s // tm,),
        in_specs=[pl.BlockSpec((tm, D), row), pl.BlockSpec((D, D), lambda i: (0, 0)), pl.BlockSpec((tm, D), row),
                  pl.BlockSpec((tm, D), lambda i: (i, 2)), vec, vec],
        out_specs=[pl.BlockSpec((tm, D), row), pl.BlockSpec((tm, D), lambda i: (i, 2)),
                   acc_spec, acc_spec, acc_spec],
        out_shape=[_sds((s, D), F32), _sds((s, 3 * D), BF16)] + [_sds((8, D), F32)] * 3,
        compiler_params=_params(32, 1),
    )(drb, w, u1, h, ln_g, ln_b)


def _a_conv_bwd(du1, u0, h, dha, w_dw, tm):
    s = du1.shape[0]
    steps = s // tm
    per = tm // CONV_HALO
    pad = CONV_W - 1

    def body(du_ref, dun_ref, u0_ref, u0p_ref, h_ref, w_ref, dha_hbm,
             dh_ref, dw_ref, dbdw_ref, dba_ref, dbg_ref, dbuf, ubuf, wacc):
        i = pl.program_id(0)
        _acc_init(i, dbdw_ref, dba_ref, dbg_ref, wacc)
        dbuf[0:tm, :] = du_ref[...]
        dbuf[tm:, :] = jnp.where(i < steps - 1, dun_ref[...], 0.0)
        ubuf[0:CONV_HALO, :] = jnp.where(i > 0, u0p_ref[...], 0.0)
        ubuf[CONV_HALO:, :] = u0_ref[...]
        off = CONV_HALO - pad
        for ch in range(tm // CONV_CHUNK):
            r0 = ch * CONV_CHUNK
            duc = du_ref[r0:r0 + CONV_CHUNK, :]
            acc = jnp.zeros((CONV_CHUNK, D), F32)
            for j in range(CONV_W):
                acc = acc + w_ref[j:j + 1, :] * dbuf[r0 + pad - j:r0 + pad - j + CONV_CHUNK, :]
                wacc[j] += _rowsum8(duc * ubuf[r0 + off + j:r0 + off + j + CONV_CHUNK, :])
            sg = _sigmoid(h_ref[r0:r0 + CONV_CHUNK, D:2 * D])
            da = acc * sg
            dag = acc * h_ref[r0:r0 + CONV_CHUNK, 0:D] * (sg * (1.0 - sg))
            dh_ref[r0:r0 + CONV_CHUNK, 0:D] = da.astype(BF16)
            dh_ref[r0:r0 + CONV_CHUNK, D:2 * D] = dag.astype(BF16)
            dbdw_ref[...] += _rowsum8(duc)
            dba_ref[...] += _rowsum8(da)
            dbg_ref[...] += _rowsum8(dag)
        _acc_finish(i, steps - 1, dbdw_ref, dba_ref, dbg_ref)

        @pl.when(i == steps - 1)
        def _():
            for j in range(CONV_W):
                dw_ref[j:j + 1, :] = jnp.sum(wacc[j], axis=0, keepdims=True)
            dw_ref[CONV_W:, :] = jnp.zeros((32 - CONV_W, D), F32)

    row = lambda i: (i, 0)
    acc_spec = pl.BlockSpec((8, D), lambda i: (0, 0))
    outs = pl.pallas_call(
        body, name="a_conv_bwd", grid=(steps,),
        in_specs=[pl.BlockSpec((tm, D), row),
                  pl.BlockSpec((CONV_HALO, D), lambda i: (jnp.minimum((i + 1) * per, s // CONV_HALO - 1), 0)),
                  pl.BlockSpec((tm, D), row),
                  pl.BlockSpec((CONV_HALO, D), lambda i: (jnp.maximum(i * per - 1, 0), 0)),
                  pl.BlockSpec((tm, 2 * D), lambda i: (i, 0)),
                  pl.BlockSpec((32, D), lambda i: (0, 0)), pl.BlockSpec(memory_space=pl.ANY)],
        out_specs=[pl.BlockSpec((tm, 2 * D), lambda i: (i, 0)),
                   pl.BlockSpec((32, D), lambda i: (0, 0)), acc_spec, acc_spec, acc_spec],
        out_shape=[_sds((s, 3 * D), BF16), _sds((32, D), F32)] + [_sds((8, D), F32)] * 3,
        scratch_shapes=[pltpu.VMEM((tm + CONV_HALO, D), F32), pltpu.VMEM((tm + CONV_HALO, D), F32),
                        pltpu.VMEM((CONV_W, 8, D), F32)],
        input_output_aliases={6: 0}, compiler_params=_params(56, 1),
    )(du1, du1, u0, u0, h, w_dw, dha)
    return outs


def _a_in_bwd(dr1, dha, w_g, tm):
    s = dr1.shape[0]
    npd = w_g.shape[2]

    def body(dr_ref, dh_ref, w_ref, o_ref):
        acc = ALPHA * dr_ref[...]
        for j in range(N_DEV):
            acc = acc + _dot_nt(dh_ref[:, npd * j:npd * (j + 1)], w_ref[j])
        o_ref[...] = acc

    row = lambda i: (i, 0)
    return pl.pallas_call(
        body, name="a_in_bwd", grid=(s // tm,),
        in_specs=[pl.BlockSpec((tm, D), row), pl.BlockSpec((tm, 3 * D), row),
                  pl.BlockSpec(w_g.shape, lambda i: (0, 0, 0))],
        out_specs=pl.BlockSpec((tm, D), row), out_shape=_sds((s, D), F32),
        compiler_params=_params(36, 1),
    )(dr1, dha, w_g)


def _wgrad(name, a, b, cols, ts, groups=1):
    s = a.shape[0]
    n_out = len(cols)
    width = sum(wd for _, wd in cols[0])
    per = n_out // groups
    nb_cols = b.shape[1]

    def body(a_ref, b_ref, o_ref):
        gi = pl.program_id(0)
        si = pl.program_id(1)

        @pl.when(si == 0)
        def _():
            o_ref[...] = jnp.zeros(o_ref.shape, F32)

        at = a_ref[...].T
        for grp in range(groups):
            @pl.when(gi == grp)
            def _():
                for jj in range(per):
                    c = 0
                    for st, wd in cols[grp * per + jj]:
                        o_ref[jj, :, c:c + wd] += _dot(at, b_ref[:, st:st + wd])
                        c += wd

    return pl.pallas_call(
        body, name=name, grid=(groups, s // ts),
        in_specs=[pl.BlockSpec((ts, D), lambda gi, si: (si, 0)), pl.BlockSpec((ts, nb_cols), lambda gi, si: (si, 0))],
        out_specs=pl.BlockSpec((per, D, width), lambda gi, si: (gi, 0, 0)),
        out_shape=_sds((n_out, D, width), F32),
        compiler_params=_params(56, 2),
    )(a, b)


SMALL_ROWS = 40
GRAD_ROWS = 48


def _device_cols(npd):
    return [((npd * j, npd),) for j in range(N_DEV)]


def kernel(x, a_w_in, a_b_in, a_w_dw, a_b_dw, a_ln_g, a_ln_b, a_w_out, a_b_out, kv_w, b_w_in, b_w_out, b_b_out, post_ln_g, post_ln_b, loss_target, m_a_w_in, m_a_b_in, m_a_w_dw, m_a_b_dw, m_a_ln_g, m_a_ln_b, m_a_w_out, m_a_b_out, m_kv_w, m_b_w_in, m_b_w_out, m_b_b_out, m_post_ln_g, m_post_ln_b, v_a_w_in, v_a_b_in, v_a_w_dw, v_a_b_dw, v_a_ln_g, v_a_ln_b, v_a_w_out, v_a_b_out, v_kv_w, v_b_w_in, v_b_w_out, v_b_b_out, v_post_ln_g, v_post_ln_b):
    s = x.shape[1]
    assert x.shape == (1, s, D) and s % (DILATIONS[-1] * BLK) == 0
    xs = x.reshape(s, D)
    tgt = loss_target.reshape(s, D)
    me = 4 * lax.axis_index("x") + 2 * lax.axis_index("y") + lax.axis_index("c")
    c_idx = lax.axis_index("c").astype(jnp.int32).reshape(1)

    def small_pack(b_in, w_dw, b_dw, ln_g, ln_b, b_out):
        rows = [b_in.reshape(3, BLK), w_dw.reshape(CONV_W, BLK), b_dw.reshape(1, BLK), ln_g.reshape(1, BLK),
                ln_b.reshape(1, BLK), b_out.reshape(1, BLK)]
        n = sum(r.shape[0] for r in rows)
        return jnp.concatenate(rows + [jnp.zeros((SMALL_ROWS - n, BLK), F32)], axis=0)

    shards = [a_w_in[0], a_w_out[0], kv_w, b_w_in[0], b_w_out[0], small_pack(a_b_in, a_w_dw, a_b_dw, a_ln_g, a_ln_b, a_b_out)]
    wa_in, wa_out, wkv, wb_in, wb_out, sm = _all_gather("gather_weights", shards, [BF16] * 5 + [F32])
    wa_out = wa_out.reshape(D, D)
    wb_out = wb_out.reshape(D, D)
    ba_in = sm[:, 0:3, :].reshape(1, 3 * D)
    w_dw = jnp.concatenate([sm[:, 3:3 + CONV_W, :].transpose(1, 0, 2).reshape(CONV_W, D), jnp.zeros((1, D), F32)], axis=0)
    b_dw, ln_g, ln_b, ba_out = (sm[:, 34 + k, :].reshape(1, D) for k in range(4))
    pg0, pg1 = post_ln_g[0:1], post_ln_g[1:2]
    pb0, pb1 = post_ln_b[0:1], post_ln_b[1:2]

    h_a, u0, xb = _a_in_proj(xs, wa_in, ba_in, 256)
    u1, g_a = _a_conv_gate(u0, h_a, w_dw, b_dw, ln_g, ln_b, 256)
    n1, rstd1, x1b = _a_out_proj(g_a, wa_out, ba_out, xs, pg0, pb0, 512)
    kv = _kv_proj(x1b, wkv, 512)
    q, z_b = _b_in_proj(x1b, wb_in, 512)
    o, lse = _attn_fwd(q, kv, None, None, None, 0, True, False)
    o, lse = _attn_fwd(q, kv, o, lse, None, 1, False, False)
    o, lse, g_b = _attn_fwd(q, kv, o, lse, z_b, 2, False, True)
    dr2, dr2b, loss8, dpg1, dpb1, dbb_out = _b_out_loss(g_b, wb_out, b_b_out, n1, pg0, pb0, pg1, pb1, tgt, 512)

    do, dhb, dd = _b_out_bwd(dr2b, wb_out, z_b, o, 512)
    dkv = None
    for g in range(3):
        dhb, dkv = _attn_bwd(q, kv, do, lse, dd, dhb, dkv, g)
    dr1, dr1b, dpg0, dpb0, dba_out = _b_in_bwd(dr2, dhb, dkv, wb_in, wkv, n1, rstd1, pg0, 256)
    du1, dha, dln_g, dln_b, dbz = _a_out_bwd(dr1b, wa_out, u1, h_a, ln_g, ln_b, 512)
    dha, dw_dw, db_dw, dba, dbg = _a_conv_bwd(du1, u0, h_a, dha, w_dw, 256)
    grad_x = _a_in_bwd(dr1, dha, wa_in, 512)

    kv_cols = [tuple((_kv_col(j * 3 + cc), KV_CHUNK) for cc in range(3)) for j in range(N_DEV)]
    p_a_in = _wgrad("wgrad_a_in", xb, dha, _device_cols(384), 512)
    p_kv = _wgrad("wgrad_kv", x1b, dkv, kv_cols, 512, groups=2)
    p_b_in = _wgrad("wgrad_b_in", x1b, dhb, _device_cols(512), 512, groups=2)
    p_a_out = _wgrad("wgrad_a_out", g_a, dr1b, [((0, D),)], 512).reshape(N_DEV, BLK, D)
    p_b_out = _wgrad("wgrad_b_out", g_b, dr2b, [((0, D),)], 512).reshape(N_DEV, BLK, D)
    parts = [p_a_in, p_kv, p_b_in, p_a_out, p_b_out]

    from_sibling = _exchange_sibling("reduce_sibling", parts)
    chip_sums = [_pair_add(f"pair_add_{k}", p, r, c_idx) for k, (p, r) in enumerate(zip(parts, from_sibling))]
    from_chips = _exchange_chips("reduce_chips", chip_sums)

    big_w = [a_w_in[0], kv_w, b_w_in[0], a_w_out[0], b_w_out[0]]
    big_m = [m_a_w_in[0], m_kv_w, m_b_w_in[0], m_a_w_out[0], m_b_w_out[0]]
    big_v = [v_a_w_in[0], v_kv_w, v_b_w_in[0], v_a_w_out[0], v_b_w_out[0]]
    big = [_sum_adamw(f"adamw_{k}", from_chips[k], big_w[k], big_m[k], big_v[k]) for k in range(5)]

    rows = [dba[0:1], dbg[0:1], dbz[0:1], dw_dw[0:CONV_W], db_dw[0:1], dln_g[0:1], dln_b[0:1], dba_out[0:1],
            dbb_out[0:1], dpg0[0:1], dpg1[0:1], dpb0[0:1], dpb1[0:1], loss8[0:1]]
    n_rows = sum(r.shape[0] for r in rows)
    gpack = jnp.concatenate(rows + [jnp.zeros((GRAD_ROWS - n_rows, D), F32)], axis=0)
    (gall,) = _all_gather("gather_small_grads", [gpack], [F32])
    gs = _small_sum("small_sum", gall)
    loss = gs[43, 0]

    def my(vec, width):
        return lax.dynamic_slice_in_dim(vec, me * width, width, axis=-1)

    g_small = [my(gs[0:3].reshape(1, 3 * D), 384), my(gs[3:34], BLK)[None], my(gs[34:35], BLK), my(gs[35:36], BLK),
               my(gs[36:37], BLK), my(gs[37:38], BLK), gs[38:39], gs[39:41], gs[41:43]]
    w_small = [a_b_in, a_w_dw, a_b_dw, a_ln_g, a_ln_b, a_b_out, b_b_out, post_ln_g, post_ln_b]
    m_small = [m_a_b_in, m_a_w_dw, m_a_b_dw, m_a_ln_g, m_a_ln_b, m_a_b_out, m_b_b_out, m_post_ln_g, m_post_ln_b]
    v_small = [v_a_b_in, v_a_w_dw, v_a_b_dw, v_a_ln_g, v_a_ln_b, v_a_b_out, v_b_b_out, v_post_ln_g, v_post_ln_b]
    sizes = [math.prod(w.shape) for w in w_small]
    total = sum(sizes)
    padded = -(-total // (8 * BLK)) * (8 * BLK)

    def flat(parts_):
        return jnp.concatenate([p.reshape(-1) for p in parts_] + [jnp.ones((padded - total,), F32)]).reshape(-1, BLK)

    sd, sm_new, sv_new = _small_adamw("adamw_small", flat(w_small), flat(g_small), flat(m_small), flat(v_small))

    def unflat(packed):
        out, pos = [], 0
        vec = packed.reshape(-1)
        for w, n in zip(w_small, sizes):
            out.append(vec[pos:pos + n].reshape(w.shape))
            pos += n
        return out

    g_small = [g.reshape(w.shape) for g, w in zip(g_small, w_small)]
    d_small, nm_small, nv_small = unflat(sd), unflat(sm_new), unflat(sv_new)

    def ordered(bigs, smalls):
        a_in, kvw, b_in, a_out, b_out = bigs
        return [a_in[None], smalls[0], smalls[1], smalls[2], smalls[3], smalls[4], a_out[None], smalls[5],
                kvw, b_in[None], b_out[None], smalls[6], smalls[7], smalls[8]]

    grads = ordered([b[0] for b in big], g_small)
    deltas = ordered([b[1] for b in big], d_small)
    new_m = ordered([b[2] for b in big], nm_small)
    new_v = ordered([b[3] for b in big], nv_small)
    return (loss, grad_x.reshape(1, s, D), *grads, *deltas, *new_m, *new_v)
```

```python
import math

import numpy as np
import jax
import jax.numpy as jnp
from jax import lax
from jax.experimental import pallas as pl
from jax.experimental.pallas import tpu as pltpu

F32 = jnp.float32
BF16 = jnp.bfloat16
MESH = pl.DeviceIdType.MESH

D = 1024
N_DEV = 8
HEAD_DIM = 64
N_HEADS = 16
DILATIONS = (1, 4, 16)
BLK = 128
CONV_W = 31
ALPHA = (2.0 * 2) ** 0.25
LN_EPS = 1e-5
SLOPES = tuple(2.0 ** (-8.0 * (h + 1) / N_HEADS) for h in range(N_HEADS))
NEG = -1e30

ADAM_LR = 0.001
ADAM_B1 = 0.9
ADAM_B2 = 0.999
ADAM_EPS = 1e-08
ADAM_WD = 0.01
ADAM_STEP = 10

VMEM_CAP_MB = 64


def _params(vmem_mb, n_grid=0):
    sem = ("arbitrary",) * n_grid if n_grid else None
    return pltpu.CompilerParams(dimension_semantics=sem, vmem_limit_bytes=min(vmem_mb, VMEM_CAP_MB - 6) * 2 ** 20)


def _sds(shape, dtype):
    return jax.ShapeDtypeStruct(tuple(shape), dtype)


def _sigmoid(v):
    return jax.nn.sigmoid(v)


def _dsilu(v, s):
    return s * (1.0 + v * (1.0 - s))


def _ln_stats(r):
    mu = jnp.mean(r, axis=-1, keepdims=True)
    xc = r - mu
    var = jnp.mean(xc * xc, axis=-1, keepdims=True)
    rstd = lax.rsqrt(var + LN_EPS)
    return xc * rstd, rstd


def _ln_bwd(dn, n, rstd):
    m1 = jnp.mean(dn, axis=-1, keepdims=True)
    m2 = jnp.mean(dn * n, axis=-1, keepdims=True)
    return rstd * (dn - m1 - n * m2)


def _rowsum8(v):
    tm, c = v.shape
    return v.reshape(tm // 8, 8, c).sum(axis=0)


def _acc_init(i, *refs):
    @pl.when(i == 0)
    def _():
        for r in refs:
            r[...] = jnp.zeros(r.shape, r.dtype)


def _acc_finish(i, last, *refs):
    @pl.when(i == last)
    def _():
        for r in refs:
            r[...] = jnp.broadcast_to(jnp.sum(r[...], axis=0, keepdims=True), r.shape)


def _dot(a, b):
    return jnp.dot(a, b, preferred_element_type=F32)


def _dot_nt(a, b):
    return lax.dot_general(a, b, (((1,), (1,)), ((), ())), preferred_element_type=F32)


def _place():
    return lax.axis_index("x"), lax.axis_index("y"), lax.axis_index("c")


def _all_gather(name, arrays, dtypes):
    n = len(arrays)

    def body(*refs):
        ins, outs, stages = refs[:n], refs[n:2 * n], refs[2 * n:3 * n]
        send_sems, recv_sems, local_sems = refs[3 * n:]
        x, y, c = _place()
        me, sibling = (x, y, c), (x, y, 1 - c)
        chips = [(1 - x, y), (x, 1 - y), (1 - x, 1 - y)]

        def slot(ref, p):
            return ref.at[4 * p[0] + 2 * p[1] + p[2]]

        def copy(a, k, block, to, src=None):
            return pltpu.make_async_remote_copy(
                src_ref=slot(outs[a], block) if src is None else src, dst_ref=slot(outs[a], block),
                send_sem=send_sems.at[a, k], recv_sem=recv_sems.at[a, k], device_id=to, device_id_type=MESH)

        first, mine = [], []
        for a in range(n):
            stages[a][...] = ins[a][...].astype(stages[a].dtype)
            cp = pltpu.make_async_copy(stages[a], slot(outs[a], me), local_sems.at[a])
            cp.start()
            mine.append(cp)
            first.append(copy(a, 0, me, sibling, src=stages[a]))
            first += [copy(a, 1 + j, me, (*chip, c), src=stages[a]) for j, chip in enumerate(chips)]
        for cp in first:
            cp.start()
        passed = []
        for j, chip in enumerate(chips):
            for a in range(n):
                copy(a, 1 + j, (*chip, c), me).wait_recv()
                cp = copy(a, 4 + j, (*chip, c), sibling)
                cp.start()
                passed.append(cp)
        for a in range(n):
            copy(a, 0, sibling, me).wait_recv()
            for j, chip in enumerate(chips):
                copy(a, 4 + j, (*chip, 1 - c), me).wait_recv()
        for cp in first + passed:
            cp.wait_send()
        for cp in mine:
            cp.wait()

    stage_bytes = sum(math.prod(a.shape) * (jnp.dtype(a.dtype).itemsize + jnp.dtype(dt).itemsize)
                      for a, dt in zip(arrays, dtypes))
    return pl.pallas_call(
        body, name=name,
        out_shape=[_sds((N_DEV,) + a.shape, dt) for a, dt in zip(arrays, dtypes)],
        in_specs=[pl.BlockSpec(memory_space=pltpu.VMEM)] * n,
        out_specs=[pl.BlockSpec(memory_space=pl.ANY)] * n,
        scratch_shapes=[pltpu.VMEM(a.shape, dt) for a, dt in zip(arrays, dtypes)]
        + [pltpu.SemaphoreType.DMA((n, 7)), pltpu.SemaphoreType.DMA((n, 7)), pltpu.SemaphoreType.DMA((n,))],
        compiler_params=_params(stage_bytes // 2 ** 20 + 8),
    )(*arrays)


def _exchange_sibling(name, parts):
    n = len(parts)

    def body(*refs):
        ins, outs = refs[:n], refs[n:2 * n]
        send_sems, recv_sems = refs[2 * n:]
        x, y, c = _place()
        copies = []
        for a in range(n):
            for p in range(4):
                copies.append(pltpu.make_async_remote_copy(
                    src_ref=ins[a].at[2 * p + 1 - c], dst_ref=outs[a].at[p],
                    send_sem=send_sems.at[a, p], recv_sem=recv_sems.at[a, p],
                    device_id=(x, y, 1 - c), device_id_type=MESH))
        for cp in copies:
            cp.start()
        for cp in copies:
            cp.wait_recv()
        for cp in copies:
            cp.wait_send()

    return pl.pallas_call(
        body, name=name,
        out_shape=[_sds((4,) + p.shape[1:], p.dtype) for p in parts],
        in_specs=[pl.BlockSpec(memory_space=pl.ANY)] * n,
        out_specs=[pl.BlockSpec(memory_space=pl.ANY)] * n,
        scratch_shapes=[pltpu.SemaphoreType.DMA((n, 4)), pltpu.SemaphoreType.DMA((n, 4))],
    )(*parts)


def _exchange_chips(name, sums):
    n = len(sums)

    def body(*refs):
        ins, outs = refs[:n], refs[n:2 * n]
        send_sems, recv_sems, local_sems = refs[2 * n:]
        x, y, c = _place()
        my_chip = 2 * x + y
        chips = [(1 - x, y), (x, 1 - y), (1 - x, 1 - y)]
        copies, mine = [], []
        for a in range(n):
            cp = pltpu.make_async_copy(ins[a].at[my_chip], outs[a].at[my_chip], local_sems.at[a])
            cp.start()
            mine.append(cp)
            for k, (px, py) in enumerate(chips):
                copies.append(pltpu.make_async_remote_copy(
                    src_ref=ins[a].at[2 * px + py], dst_ref=outs[a].at[my_chip],
                    send_sem=send_sems.at[a, k], recv_sem=recv_sems.at[a, k],
                    device_id=(px, py, c), device_id_type=MESH))
        for cp in copies:
            cp.start()
        for cp in copies:
            cp.wait_recv()
        for cp in copies:
            cp.wait_send()
        for cp in mine:
            cp.wait()

    return pl.pallas_call(
        body, name=name,
        out_shape=[_sds(s.shape, s.dtype) for s in sums],
        in_specs=[pl.BlockSpec(memory_space=pl.ANY)] * n,
        out_specs=[pl.BlockSpec(memory_space=pl.ANY)] * n,
        scratch_shapes=[pltpu.SemaphoreType.DMA((n, 3)), pltpu.SemaphoreType.DMA((n, 3)),
                        pltpu.SemaphoreType.DMA((n,))],
    )(*sums)


def _pair_add(name, part, recv, c_idx):
    _, r, c = part.shape
    tr = min(r, 256)

    def body(c_ref, a_ref, b_ref, o_ref):
        o_ref[...] = a_ref[...] + b_ref[...]

    return pl.pallas_call(
        body, name=name,
        grid_spec=pltpu.PrefetchScalarGridSpec(
            num_scalar_prefetch=1, grid=(4, r // tr),
            in_specs=[pl.BlockSpec((1, tr, c), lambda p, i, cr: (2 * p + cr[0], i, 0)),
                      pl.BlockSpec((1, tr, c), lambda p, i, cr: (p, i, 0))],
            out_specs=pl.BlockSpec((1, tr, c), lambda p, i, cr: (p, i, 0))),
        out_shape=_sds((4, r, c), F32),
        compiler_params=_params(16, 2),
    )(c_idx, part, recv)


def _adamw_math(w, g, m, v):
    m = ADAM_B1 * m + (1.0 - ADAM_B1) * g
    v = ADAM_B2 * v + (1.0 - ADAM_B2) * (g * g)
    m_hat = m / (1.0 - ADAM_B1 ** ADAM_STEP)
    v_hat = v / (1.0 - ADAM_B2 ** ADAM_STEP)
    delta = -ADAM_LR * (m_hat / (jnp.sqrt(v_hat) + ADAM_EPS) + ADAM_WD * w)
    return delta, m, v


def _sum_adamw(name, recv, w, m, v):
    r, c = w.shape
    tr = min(r, 256)

    def body(p_ref, w_ref, m_ref, v_ref, g_ref, d_ref, nm_ref, nv_ref):
        g = (p_ref[0] + p_ref[1]) + (p_ref[2] + p_ref[3])
        g_ref[...] = g
        d_ref[...], nm_ref[...], nv_ref[...] = _adamw_math(w_ref[...], g, m_ref[...], v_ref[...])

    blk = pl.BlockSpec((tr, c), lambda i: (i, 0))
    return pl.pallas_call(
        body, name=name, grid=(r // tr,),
        in_specs=[pl.BlockSpec((4, tr, c), lambda i: (0, i, 0)), blk, blk, blk],
        out_specs=[blk] * 4, out_shape=[_sds((r, c), F32)] * 4,
        compiler_params=_params(24, 1),
    )(recv, w, m, v)


def _small_sum(name, gathered):
    _, r, c = gathered.shape

    def body(p_ref, o_ref):
        acc = p_ref[0]
        for j in range(1, N_DEV):
            acc = acc + p_ref[j]
        o_ref[...] = acc

    return pl.pallas_call(body, name=name, out_shape=_sds((r, c), F32),
                          in_specs=[pl.BlockSpec(memory_space=pltpu.VMEM)],
                          out_specs=pl.BlockSpec(memory_space=pltpu.VMEM))(gathered)


def _small_adamw(name, w, g, m, v):
    def body(w_ref, g_ref, m_ref, v_ref, d_ref, nm_ref, nv_ref):
        d_ref[...], nm_ref[...], nv_ref[...] = _adamw_math(w_ref[...], g_ref[...], m_ref[...], v_ref[...])

    vm = pl.BlockSpec(memory_space=pltpu.VMEM)
    return pl.pallas_call(body, name=name, out_shape=[_sds(w.shape, F32)] * 3,
                          in_specs=[vm] * 4, out_specs=[vm] * 3)(w, g, m, v)


def _a_in_proj(x, w_g, b_full, tm):
    s = x.shape[0]
    npd = w_g.shape[2]

    def body(x_ref, w_ref, b_ref, h_ref, u0_ref, xb_ref):
        xb = x_ref[...].astype(BF16)
        xb_ref[...] = xb
        for j in range(N_DEV):
            sl = slice(npd * j, npd * (j + 1))
            h_ref[:, sl] = _dot(xb, w_ref[j]) + b_ref[:, sl]
        u0_ref[...] = h_ref[:, 0:D] * _sigmoid(h_ref[:, D:2 * D])

    row = lambda i: (i, 0)
    return pl.pallas_call(
        body, name="a_in_proj", grid=(s // tm,),
        in_specs=[pl.BlockSpec((tm, D), row), pl.BlockSpec(w_g.shape, lambda i: (0, 0, 0)),
                  pl.BlockSpec((1, 3 * D), lambda i: (0, 0))],
        out_specs=[pl.BlockSpec((tm, 3 * D), row), pl.BlockSpec((tm, D), row), pl.BlockSpec((tm, D), row)],
        out_shape=[_sds((s, 3 * D), F32), _sds((s, D), F32), _sds((s, D), BF16)],
        compiler_params=_params(44, 1),
    )(x, w_g, b_full)


CONV_HALO = 32
CONV_CHUNK = 32


def _a_conv_gate(u0, h, w_dw, b_dw, ln_g, ln_b, tm):
    s = u0.shape[0]
    per = tm // CONV_HALO

    def body(u0_ref, halo_ref, z_ref, w_ref, bdw_ref, g_ref, b_ref, u1_ref, ga_ref, buf):
        i = pl.program_id(0)
        buf[0:CONV_HALO, :] = jnp.where(i > 0, halo_ref[...], 0.0)
        buf[CONV_HALO:, :] = u0_ref[...]
        off = CONV_HALO - (CONV_W - 1)
        for ch in range(tm // CONV_CHUNK):
            r0 = ch * CONV_CHUNK
            acc = jnp.broadcast_to(bdw_ref[...], (CONV_CHUNK, D))
            for j in range(CONV_W):
                acc = acc + w_ref[j:j + 1, :] * buf[r0 + off + j:r0 + off + j + CONV_CHUNK, :]
            u1_ref[r0:r0 + CONV_CHUNK, :] = acc
            n, _ = _ln_stats(acc)
            pre = n * g_ref[...] + b_ref[...]
            z = z_ref[r0:r0 + CONV_CHUNK, :]
            ga_ref[r0:r0 + CONV_CHUNK, :] = ((pre * _sigmoid(pre)) * (z * _sigmoid(z))).astype(BF16)

    row = lambda i: (i, 0)
    vec = pl.BlockSpec((1, D), lambda i: (0, 0))
    return pl.pallas_call(
        body, name="a_conv_gate", grid=(s // tm,),
        in_specs=[pl.BlockSpec((tm, D), row),
                  pl.BlockSpec((CONV_HALO, D), lambda i: (jnp.maximum(i * per - 1, 0), 0)),
                  pl.BlockSpec((tm, D), lambda i: (i, 2)),
                  pl.BlockSpec((32, D), lambda i: (0, 0)), vec, vec, vec],
        out_specs=[pl.BlockSpec((tm, D), row), pl.BlockSpec((tm, D), row)],
        out_shape=[_sds((s, D), F32), _sds((s, D), BF16)],
        scratch_shapes=[pltpu.VMEM((tm + CONV_HALO, D), F32)],
        compiler_params=_params(56, 1),
    )(u0, u0, h, w_dw, b_dw, ln_g, ln_b)


def _a_out_proj(ga, w, b, x, pg, pb, tm):
    s = x.shape[0]

    def body(ga_ref, w_ref, b_ref, x_ref, pg_ref, pb_ref, n_ref, rstd_ref, xb_ref):
        r = ALPHA * x_ref[...] + (_dot(ga_ref[...], w_ref[...]) + b_ref[...])
        n, rstd = _ln_stats(r)
        n_ref[...] = n
        rstd_ref[...] = rstd
        xb_ref[...] = (n * pg_ref[...] + pb_ref[...]).astype(BF16)

    row = lambda i: (i, 0)
    vec = pl.BlockSpec((1, D), lambda i: (0, 0))
    return pl.pallas_call(
        body, name="a_out_proj", grid=(s // tm,),
        in_specs=[pl.BlockSpec((tm, D), row), pl.BlockSpec((D, D), lambda i: (0, 0)), vec,
                  pl.BlockSpec((tm, D), row), vec, vec],
        out_specs=[pl.BlockSpec((tm, D), row), pl.BlockSpec((tm, 1), row), pl.BlockSpec((tm, D), row)],
        out_shape=[_sds((s, D), F32), _sds((s, 1), F32), _sds((s, D), BF16)],
        compiler_params=_params(32, 1),
    )(ga, w, b, x, pg, pb)


def _kv_proj(xb, w_g, tm):
    s = xb.shape[0]
    npd = w_g.shape[2]
    half = N_DEV // 2

    def body(x_ref, w_ref, k_ref, v_ref):
        xv = x_ref[...]
        for j in range(N_DEV):
            o_ref = k_ref if j < half else v_ref
            jj = j % half
            o_ref[:, npd * jj:npd * (jj + 1)] = _dot(xv, w_ref[j]).astype(BF16)

    row = lambda i: (i, 0)
    return pl.pallas_call(
        body, name="kv_proj", grid=(s // tm,),
        in_specs=[pl.BlockSpec((tm, D), row), pl.BlockSpec(w_g.shape, lambda i: (0, 0, 0))],
        out_specs=[pl.BlockSpec((tm, 3 * D), row), pl.BlockSpec((tm, 3 * D), row)],
        out_shape=[_sds((s, 3 * D), BF16), _sds((s, 3 * D), BF16)],
        compiler_params=_params(52, 1),
    )(xb, w_g)


def _b_in_proj(xb, w_g, tm):
    s = xb.shape[0]
    npd = w_g.shape[2]
    scale = HEAD_DIM ** -0.5

    def body(x_ref, w_ref, q_ref, z_ref):
        xv = x_ref[...]
        for j in range(N_DEV):
            hj = _dot(xv, w_ref[j])
            if j < 6:
                q_ref[:, npd * j:npd * (j + 1)] = (hj.astype(BF16) * scale).astype(BF16)
            else:
                z_ref[:, npd * (j - 6):npd * (j - 5)] = hj

    row = lambda i: (i, 0)
    return pl.pallas_call(
        body, name="b_in_proj", grid=(s // tm,),
        in_specs=[pl.BlockSpec((tm, D), row), pl.BlockSpec(w_g.shape, lambda i: (0, 0, 0))],
        out_specs=[pl.BlockSpec((tm, 3 * D), row), pl.BlockSpec((tm, D), row)],
        out_shape=[_sds((s, 3 * D), BF16), _sds((s, D), F32)],
        compiler_params=_params(44, 1),
    )(xb, w_g)


ATTN_NQ = {1: 8, 4: 2, 16: 1}
N_PAIRS = N_HEADS // 2


def _band_table(d):
    qi = np.arange(BLK)[:, None]
    kj = np.arange(2 * BLK)[None, :]
    dist = qi + BLK - kj
    ok = (dist >= 0) & (dist <= BLK)
    return jnp.asarray(np.where(ok, -(d * dist).astype(np.float32), np.float32(NEG)), dtype=F32)


def _slope_table():
    t = np.zeros((N_PAIRS, 8, 2 * BLK), np.float32)
    for h in range(N_HEADS):
        t[h // 2, h % 2, :] = SLOPES[h]
    return jnp.asarray(t)


def _head_masks():
    lane = lax.broadcasted_iota(jnp.int32, (1, BLK), 1)
    lo = (lane < HEAD_DIM).astype(BF16)
    return (lo, (1.0 - lo).astype(BF16))


def _pick_col(tile, lane, h):
    return jnp.sum(jnp.where(lane == h, tile, 0.0), axis=1, keepdims=True)


def _rows(base, n, d):
    return pl.ds(base, n) if d == 1 else pl.ds(base, n, stride=d)


def _attn_fwd(q, k_all, v_all, o_acc, lse_acc, z, g, first, last):
    d = DILATIONS[g]
    s = q.shape[0]
    nq = ATTN_NQ[d]
    halo = BLK * d
    tile = nq * halo
    assert s % tile == 0

    def body(*refs):
        q_ref, k_ref, kh_ref, v_ref, vh_ref, nd_ref, sl_ref = refs[:7]
        k = 7
        if not first:
            oa_ref, la_ref = refs[k:k + 2]
            k += 2
        if last:
            z_ref = refs[k]
            k += 1
        o_ref, l_ref = refs[k:k + 2]
        k += 2
        if last:
            gb_ref = refs[k]
            k += 1
        qf, kf, vf = refs[k:k + 3]
        if last:
            gf = refs[k + 3]
        n = pl.program_id(0)
        hp = pl.program_id(1)
        qf[...] = q_ref[...].astype(F32)
        kf[0:halo, :] = kh_ref[...].astype(F32)
        kf[halo:, :] = k_ref[...].astype(F32)
        vf[0:halo, :] = vh_ref[...].astype(F32)
        vf[halo:, :] = v_ref[...].astype(F32)

        @pl.when(hp == 0)
        def _():
            l_ref[...] = jnp.zeros(l_ref.shape, F32) if first else la_ref[...]

        col = lax.broadcasted_iota(jnp.int32, (BLK, 2 * BLK), 1)
        lane = lax.broadcasted_iota(jnp.int32, (BLK, BLK), 1)
        masks = _head_masks()
        bias = [sl_ref[0, e:e + 1, :] * nd_ref[...] for e in range(2)]
        no_prev = jnp.where((n == 0) & (col < BLK), NEG, 0.0)
        bias0 = [bb + no_prev for bb in bias]
        for b in range(nq):
            for r in range(d):
                rq = _rows(b * halo + r, BLK, d)
                rk = _rows(b * halo + r, 2 * BLK, d)
                q2 = qf[rq, :].astype(BF16)
                kcat = kf[rk, :].astype(BF16)
                vcat = vf[rk, :].astype(BF16)
                lt = l_ref[rq, :]
                o2 = jnp.zeros((BLK, BLK), F32)
                for e in range(2):
                    h = 2 * hp + e
                    sc = _dot_nt(q2 * masks[e], kcat) + (bias0[e] if b == 0 else bias[e])
                    m = jnp.max(sc, axis=1, keepdims=True)
                    p = jnp.exp(sc - m)
                    l = jnp.sum(p, axis=1, keepdims=True)
                    oh = _dot(p.astype(BF16), vcat) / l
                    lse = m + jnp.log(l)
                    if not first:
                        old = _pick_col(lt, lane, h)
                        mx = jnp.maximum(old, lse)
                        new = mx + jnp.log(jnp.exp(old - mx) + jnp.exp(lse - mx))
                        oh = oa_ref[rq, :] * jnp.exp(old - new) + oh * jnp.exp(lse - new)
                        lse = new
                    in_head = (lane < HEAD_DIM) if e == 0 else (lane >= HEAD_DIM)
                    o2 = jnp.where(in_head, oh, o2)
                    lt = jnp.where(lane == h, lse, lt)
                o_ref[rq, :] = o2
                l_ref[rq, :] = lt
                if last:
                    zz = z_ref[rq, :]
                    gf[rq, :] = o2 * (zz * _sigmoid(zz))
        if last:
            gb_ref[...] = gf[...].astype(BF16)

    col_blk = lambda n, hp: (n, g * N_PAIRS + hp)
    halo_blk = lambda n, hp: (jnp.maximum(n * nq - 1, 0), g * N_PAIRS + hp)
    own = pl.BlockSpec((tile, BLK), lambda n, hp: (n, hp))
    own_l = pl.BlockSpec((tile, BLK), lambda n, hp: (n, 0))
    in_specs = [pl.BlockSpec((tile, BLK), col_blk),
                pl.BlockSpec((tile, BLK), col_blk), pl.BlockSpec((halo, BLK), halo_blk),
                pl.BlockSpec((tile, BLK), col_blk), pl.BlockSpec((halo, BLK), halo_blk),
                pl.BlockSpec((BLK, 2 * BLK), lambda n, hp: (0, 0)),
                pl.BlockSpec((1, 8, 2 * BLK), lambda n, hp: (hp, 0, 0))]
    args = [q, k_all, k_all, v_all, v_all, _band_table(d), _slope_table()]
    if not first:
        in_specs += [own, own_l]
        args += [o_acc, lse_acc]
    if last:
        in_specs += [own]
        args += [z]
    out_specs = [own, own_l] + ([own] if last else [])
    out_shape = [_sds((s, D), F32), _sds((s, BLK), F32)] + ([_sds((s, D), BF16)] if last else [])
    scratch = [pltpu.VMEM((tile, BLK), F32), pltpu.VMEM((tile + halo, BLK), F32), pltpu.VMEM((tile + halo, BLK), F32)]
    if last:
        scratch.append(pltpu.VMEM((tile, BLK), F32))
    return pl.pallas_call(
        body, name=f"attn_fwd_g{g}", grid=(s // tile, N_PAIRS), in_specs=in_specs, out_specs=out_specs,
        out_shape=out_shape, scratch_shapes=scratch, compiler_params=_params(32, 2),
    )(*args)


def _b_out_loss(gb, w, b, n1, pg0, pb0, pg1, pb1, tgt, tm):
    s = gb.shape[0]
    last = s // tm - 1

    def body(gb_ref, w_ref, b_ref, n1_ref, pg0_ref, pb0_ref, pg1_ref, pb1_ref, t_ref,
             dr_ref, drb_ref, loss_ref, dpg_ref, dpb_ref, dbo_ref):
        i = pl.program_id(0)
        _acc_init(i, loss_ref, dpg_ref, dpb_ref, dbo_ref)
        x1 = n1_ref[...] * pg0_ref[...] + pb0_ref[...]
        r = ALPHA * x1 + (_dot(gb_ref[...], w_ref[...]) + b_ref[...])
        n, rstd = _ln_stats(r)
        err = (n * pg1_ref[...] + pb1_ref[...]) - t_ref[...]
        loss_ref[...] += _rowsum8(err * err)
        dx2 = err * (1.0 / D)
        dpg_ref[...] += _rowsum8(dx2 * n)
        dpb_ref[...] += _rowsum8(dx2)
        dr = _ln_bwd(dx2 * pg1_ref[...], n, rstd)
        dr_ref[...] = dr
        drb_ref[...] = dr.astype(BF16)
        dbo_ref[...] += _rowsum8(dr)
        _acc_finish(i, last, dpg_ref, dpb_ref, dbo_ref)

        @pl.when(i == last)
        def _():
            loss_ref[...] = jnp.broadcast_to((0.5 / D) * jnp.sum(loss_ref[...], keepdims=True), loss_ref.shape)

    row = lambda i: (i, 0)
    vec = pl.BlockSpec((1, D), lambda i: (0, 0))
    acc = pl.BlockSpec((8, D), lambda i: (0, 0))
    return pl.pallas_call(
        body, name="b_out_loss", grid=(s // tm,),
        in_specs=[pl.BlockSpec((tm, D), row), pl.BlockSpec((D, D), lambda i: (0, 0)), vec,
                  pl.BlockSpec((tm, D), row), vec, vec, vec, vec, pl.BlockSpec((tm, D), row)],
        out_specs=[pl.BlockSpec((tm, D), row), pl.BlockSpec((tm, D), row), acc, acc, acc, acc],
        out_shape=[_sds((s, D), F32), _sds((s, D), BF16)] + [_sds((8, D), F32)] * 4,
        compiler_params=_params(36, 1),
    )(gb, w, b, n1, pg0, pb0, pg1, pb1, tgt)


def _head_selector():
    sel = (np.arange(D)[:, None] // HEAD_DIM == np.arange(BLK)[None, :]).astype(np.float32)
    return jnp.asarray(sel, dtype=BF16)


def _b_out_bwd(drb, w, z, o, tm):
    s = drb.shape[0]

    def body(dr_ref, w_ref, z_ref, o_ref, sel_ref, do_ref, dh_ref, dd_ref):
        dg = _dot_nt(dr_ref[...], w_ref[...])
        zz = z_ref[...]
        sg = _sigmoid(zz)
        do = dg * (zz * sg)
        ov = o_ref[...]
        do_ref[...] = do.astype(BF16)
        dh_ref[...] = (dg * ov * _dsilu(zz, sg)).astype(BF16)
        prod = do * ov
        hi = prod.astype(BF16)
        lo = (prod - hi.astype(F32)).astype(BF16)
        dd_ref[...] = _dot(hi, sel_ref[...]) + _dot(lo, sel_ref[...])

    row = lambda i: (i, 0)
    return pl.pallas_call(
        body, name="b_out_bwd", grid=(s // tm,),
        in_specs=[pl.BlockSpec((tm, D), row), pl.BlockSpec((D, D), lambda i: (0, 0)),
                  pl.BlockSpec((tm, D), row), pl.BlockSpec((tm, D), row), pl.BlockSpec((D, BLK), lambda i: (0, 0))],
        out_specs=[pl.BlockSpec((tm, D), row), pl.BlockSpec((tm, D), lambda i: (i, 3)),
                   pl.BlockSpec((tm, BLK), row)],
        out_shape=[_sds((s, D), BF16), _sds((s, 4 * D), BF16), _sds((s, BLK), F32)],
        compiler_params=_params(36, 1),
    )(drb, w, z, o, _head_selector())


def _attn_bwd(q, k_all, v_all, do, lse, dd, dhb, dk_all, dv_all, g):
    d = DILATIONS[g]
    s = q.shape[0]
    nq = ATTN_NQ[d]
    halo = BLK * d
    tile = nq * halo
    nt = s // tile
    first = dk_all is None

    def body(*refs):
        q_ref, k_ref, kh_ref, v_ref, vh_ref, nd_ref, sl_ref, do_ref, l_ref, dd_ref = refs[:10]
        k = 10 + (1 if first else 3)
        dq_ref, dk_ref, dv_ref = refs[k:k + 3]
        qf, dof, kf, vf, dqf, dkf, dvf, ck, cv = refs[k + 3:k + 12]
        hp = pl.program_id(0)
        n = pl.program_id(1)

        @pl.when(n == 0)
        def _():
            ck[...] = jnp.zeros(ck.shape, F32)
            cv[...] = jnp.zeros(cv.shape, F32)

        dkf[...] = jnp.zeros(dkf.shape, F32)
        dvf[...] = jnp.zeros(dvf.shape, F32)

        @pl.when(n < nt)
        def _():
            qf[...] = q_ref[...].astype(F32)
            dof[...] = do_ref[...].astype(F32)
            kf[0:halo, :] = kh_ref[...].astype(F32)
            kf[halo:, :] = k_ref[...].astype(F32)
            vf[0:halo, :] = vh_ref[...].astype(F32)
            vf[halo:, :] = v_ref[...].astype(F32)
            col = lax.broadcasted_iota(jnp.int32, (BLK, 2 * BLK), 1)
            lane = lax.broadcasted_iota(jnp.int32, (BLK, BLK), 1)
            masks = _head_masks()
            bias = [sl_ref[0, e:e + 1, :] * nd_ref[...] for e in range(2)]
            no_prev = jnp.where((n == 0) & (col < BLK), NEG, 0.0)
            bias0 = [bb + no_prev for bb in bias]
            for b in range(nq):
                for r in range(d):
                    rq = _rows(b * halo + r, BLK, d)
                    rk = _rows(b * halo + r, 2 * BLK, d)
                    q2 = qf[rq, :].astype(BF16)
                    do2 = dof[rq, :].astype(BF16)
                    kcat = kf[rk, :].astype(BF16)
                    vcat = vf[rk, :].astype(BF16)
                    lt = l_ref[rq, :]
                    dt = dd_ref[rq, :]
                    dq2 = jnp.zeros((BLK, BLK), F32)
                    dkp = jnp.zeros((2 * BLK, BLK), F32)
                    dvp = jnp.zeros((2 * BLK, BLK), F32)
                    for e in range(2):
                        h = 2 * hp + e
                        qm = q2 * masks[e]
                        dom = do2 * masks[e]
                        sc = _dot_nt(qm, kcat) + (bias0[e] if b == 0 else bias[e])
                        p = jnp.exp(sc - _pick_col(lt, lane, h))
                        ds = p * (_dot_nt(dom, vcat) - _pick_col(dt, lane, h))
                        dsb = ds.astype(BF16)
                        in_head = (lane < HEAD_DIM) if e == 0 else (lane >= HEAD_DIM)
                        dq2 = jnp.where(in_head, _dot(dsb, kcat), dq2)
                        dkp = dkp + _dot(dsb.T, qm)
                        dvp = dvp + _dot(p.astype(BF16).T, dom)
                    dqf[rq, :] = (HEAD_DIM ** -0.5) * dq2
                    dkf[rk, :] += dkp
                    dvf[rk, :] += dvp
            dq_ref[...] = dqf[...].astype(BF16)

        if tile > halo:
            dk_ref[0:tile - halo, :] = ck[0:tile - halo, :].astype(BF16)
            dv_ref[0:tile - halo, :] = cv[0:tile - halo, :].astype(BF16)
        dk_ref[tile - halo:, :] = (ck[tile - halo:, :] + dkf[0:halo, :]).astype(BF16)
        dv_ref[tile - halo:, :] = (cv[tile - halo:, :] + dvf[0:halo, :]).astype(BF16)
        ck[...] = dkf[halo:, :]
        cv[...] = dvf[halo:, :]

    cur = lambda n: jnp.minimum(n, nt - 1)
    col_blk = lambda hp, n: (cur(n), g * N_PAIRS + hp)
    halo_blk = lambda hp, n: (jnp.maximum(cur(n) * nq - 1, 0), g * N_PAIRS + hp)
    out_kv = lambda hp, n: (jnp.maximum(n - 1, 0), g * N_PAIRS + hp)
    small = pl.BlockSpec((tile, BLK), lambda hp, n: (cur(n), 0))
    hbm = pl.BlockSpec(memory_space=pl.ANY)
    in_specs = [pl.BlockSpec((tile, BLK), col_blk),
                pl.BlockSpec((tile, BLK), col_blk), pl.BlockSpec((halo, BLK), halo_blk),
                pl.BlockSpec((tile, BLK), col_blk), pl.BlockSpec((halo, BLK), halo_blk),
                pl.BlockSpec((BLK, 2 * BLK), lambda hp, n: (0, 0)),
                pl.BlockSpec((1, 8, 2 * BLK), lambda hp, n: (hp, 0, 0)),
                pl.BlockSpec((tile, BLK), lambda hp, n: (cur(n), hp)), small, small, hbm]
    args = [q, k_all, k_all, v_all, v_all, _band_table(d), _slope_table(), do, lse, dd, dhb]
    aliases = {10: 0}
    if not first:
        in_specs += [hbm, hbm]
        args += [dk_all, dv_all]
        aliases.update({11: 1, 12: 2})
    tile_f32 = pltpu.VMEM((tile, BLK), F32)
    wide_f32 = pltpu.VMEM((tile + halo, BLK), F32)
    return pl.pallas_call(
        body, name=f"attn_bwd_g{g}", grid=(N_PAIRS, nt + 1), in_specs=in_specs,
        out_specs=[pl.BlockSpec((tile, BLK), col_blk), pl.BlockSpec((tile, BLK), out_kv),
                   pl.BlockSpec((tile, BLK), out_kv)],
        out_shape=[_sds((s, 4 * D), BF16), _sds((s, 3 * D), BF16), _sds((s, 3 * D), BF16)],
        scratch_shapes=[tile_f32, tile_f32, wide_f32, wide_f32, tile_f32, wide_f32, wide_f32, tile_f32, tile_f32],
        input_output_aliases=aliases, compiler_params=_params(40, 2),
    )(*args)


def _b_in_bwd(dr2, dhb, dk_all, dv_all, wb_g, wkv_g, n1, rstd1, pg0, tm):
    s = dr2.shape[0]
    last = s // tm - 1
    nb_, nkv = wb_g.shape[2], wkv_g.shape[2]
    half = N_DEV // 2

    def body(dr2_ref, dh_ref, dk_ref, dv_ref, wb_hbm, wkv_hbm, n_ref, rstd_ref, pg_ref,
             dr_ref, drb_ref, dpg_ref, dpb_ref, dbo_ref, wb, wkv):
        i = pl.program_id(0)

        @pl.when(i == 0)
        def _():
            pltpu.sync_copy(wb_hbm, wb)
            pltpu.sync_copy(wkv_hbm, wkv)

        _acc_init(i, dpg_ref, dpb_ref, dbo_ref)
        acc = ALPHA * dr2_ref[...]
        for j in range(N_DEV):
            acc = acc + _dot_nt(dh_ref[:, nb_ * j:nb_ * (j + 1)], wb[j])
            src = dk_ref if j < half else dv_ref
            jj = j % half
            acc = acc + _dot_nt(src[:, nkv * jj:nkv * (jj + 1)], wkv[j])
        n = n_ref[...]
        dpg_ref[...] += _rowsum8(acc * n)
        dpb_ref[...] += _rowsum8(acc)
        dr = _ln_bwd(acc * pg_ref[...], n, rstd_ref[...])
        dr_ref[...] = dr
        drb_ref[...] = dr.astype(BF16)
        dbo_ref[...] += _rowsum8(dr)
        _acc_finish(i, last, dpg_ref, dpb_ref, dbo_ref)

    row = lambda i: (i, 0)
    hbm = pl.BlockSpec(memory_space=pl.ANY)
    acc_spec = pl.BlockSpec((8, D), lambda i: (0, 0))
    return pl.pallas_call(
        body, name="b_in_bwd", grid=(s // tm,),
        in_specs=[pl.BlockSpec((tm, D), row), pl.BlockSpec((tm, 4 * D), row), pl.BlockSpec((tm, 3 * D), row),
                  pl.BlockSpec((tm, 3 * D), row), hbm, hbm, pl.BlockSpec((tm, D), row), pl.BlockSpec((tm, 1), row),
                  pl.BlockSpec((1, D), lambda i: (0, 0))],
        out_specs=[pl.BlockSpec((tm, D), row), pl.BlockSpec((tm, D), row), acc_spec, acc_spec, acc_spec],
        out_shape=[_sds((s, D), F32), _sds((s, D), BF16)] + [_sds((8, D), F32)] * 3,
        scratch_shapes=[pltpu.VMEM(wb_g.shape, BF16), pltpu.VMEM(wkv_g.shape, BF16)],
        compiler_params=_params(56, 1),
    )(dr2, dhb, dk_all, dv_all, wb_g, wkv_g, n1, rstd1, pg0)


def _a_out_bwd(drb, w, u1, h, ln_g, ln_b, tm):
    s = drb.shape[0]
    last = s // tm - 1

    def body(dr_ref, w_ref, u1_ref, z_ref, g_ref, b_ref, du1_ref, dh_ref, dg_ref, db_ref, dbz_ref):
        i = pl.program_id(0)
        _acc_init(i, dg_ref, db_ref, dbz_ref)
        dga = _dot_nt(dr_ref[...], w_ref[...])
        n, rstd = _ln_stats(u1_ref[...])
        pre = n * g_ref[...] + b_ref[...]
        sp = _sigmoid(pre)
        zz = z_ref[...]
        sz = _sigmoid(zz)
        dz = dga * (pre * sp) * _dsilu(zz, sz)
        dh_ref[...] = dz.astype(BF16)
        dbz_ref[...] += _rowsum8(dz)
        dpre = dga * (zz * sz) * _dsilu(pre, sp)
        dg_ref[...] += _rowsum8(dpre * n)
        db_ref[...] += _rowsum8(dpre)
        du1_ref[...] = _ln_bwd(dpre * g_ref[...], n, rstd)
        _acc_finish(i, last, dg_ref, db_ref, dbz_ref)

    row = lambda i: (i, 0)
    vec = pl.BlockSpec((1, D), lambda i: (0, 0))
    acc_spec = pl.BlockSpec((8, D), lambda i: (0, 0))
    return pl.pallas_call(
        body, name="a_out_bwd", grid=(s // tm,),
        in_specs=[pl.BlockSpec((tm, D), row), pl.BlockSpec((D, D), lambda i: (0, 0)), pl.BlockSpec((tm, D), row),
                  pl.BlockSpec((tm, D), lambda i: (i, 2)), vec, vec],
        out_specs=[pl.BlockSpec((tm, D), row), pl.BlockSpec((tm, D), lambda i: (i, 2)),
                   acc_spec, acc_spec, acc_spec],
        out_shape=[_sds((s, D), F32), _sds((s, 3 * D), BF16)] + [_sds((8, D), F32)] * 3,
        compiler_params=_params(32, 1),
    )(drb, w, u1, h, ln_g, ln_b)


def _a_conv_bwd(du1, u0, h, dha, w_dw, tm):
    s = du1.shape[0]
    steps = s // tm
    per = tm // CONV_HALO
    pad = CONV_W - 1

    def body(du_ref, dun_ref, u0_ref, u0p_ref, h_ref, w_ref, dha_hbm,
             dh_ref, dw_ref, dbdw_ref, dba_ref, dbg_ref, dbuf, ubuf, wacc):
        i = pl.program_id(0)
        _acc_init(i, dbdw_ref, dba_ref, dbg_ref, wacc)
        dbuf[0:tm, :] = du_ref[...]
        dbuf[tm:, :] = jnp.where(i < steps - 1, dun_ref[...], 0.0)
        ubuf[0:CONV_HALO, :] = jnp.where(i > 0, u0p_ref[...], 0.0)
        ubuf[CONV_HALO:, :] = u0_ref[...]
        off = CONV_HALO - pad
        for ch in range(tm // CONV_CHUNK):
            r0 = ch * CONV_CHUNK
            duc = du_ref[r0:r0 + CONV_CHUNK, :]
            acc = jnp.zeros((CONV_CHUNK, D), F32)
            for j in range(CONV_W):
                acc = acc + w_ref[j:j + 1, :] * dbuf[r0 + pad - j:r0 + pad - j + CONV_CHUNK, :]
                wacc[j] += _rowsum8(duc * ubuf[r0 + off + j:r0 + off + j + CONV_CHUNK, :])
            sg = _sigmoid(h_ref[r0:r0 + CONV_CHUNK, D:2 * D])
            da = acc * sg
            dag = acc * h_ref[r0:r0 + CONV_CHUNK, 0:D] * (sg * (1.0 - sg))
            dh_ref[r0:r0 + CONV_CHUNK, 0:D] = da.astype(BF16)
            dh_ref[r0:r0 + CONV_CHUNK, D:2 * D] = dag.astype(BF16)
            dbdw_ref[...] += _rowsum8(duc)
            dba_ref[...] += _rowsum8(da)
            dbg_ref[...] += _rowsum8(dag)
        _acc_finish(i, steps - 1, dbdw_ref, dba_ref, dbg_ref)

        @pl.when(i == steps - 1)
        def _():
            for j in range(CONV_W):
                dw_ref[j:j + 1, :] = jnp.sum(wacc[j], axis=0, keepdims=True)
            dw_ref[CONV_W:, :] = jnp.zeros((32 - CONV_W, D), F32)

    row = lambda i: (i, 0)
    acc_spec = pl.BlockSpec((8, D), lambda i: (0, 0))
    return pl.pallas_call(
        body, name="a_conv_bwd", grid=(steps,),
        in_specs=[pl.BlockSpec((tm, D), row),
                  pl.BlockSpec((CONV_HALO, D), lambda i: (jnp.minimum((i + 1) * per, s // CONV_HALO - 1), 0)),
                  pl.BlockSpec((tm, D), row),
                  pl.BlockSpec((CONV_HALO, D), lambda i: (jnp.maximum(i * per - 1, 0), 0)),
                  pl.BlockSpec((tm, 2 * D), lambda i: (i, 0)),
                  pl.BlockSpec((32, D), lambda i: (0, 0)), pl.BlockSpec(memory_space=pl.ANY)],
        out_specs=[pl.BlockSpec((tm, 2 * D), lambda i: (i, 0)),
                   pl.BlockSpec((32, D), lambda i: (0, 0)), acc_spec, acc_spec, acc_spec],
        out_shape=[_sds((s, 3 * D), BF16), _sds((32, D), F32)] + [_sds((8, D), F32)] * 3,
        scratch_shapes=[pltpu.VMEM((tm + CONV_HALO, D), F32), pltpu.VMEM((tm + CONV_HALO, D), F32),
                        pltpu.VMEM((CONV_W, 8, D), F32)],
        input_output_aliases={6: 0}, compiler_params=_params(56, 1),
    )(du1, du1, u0, u0, h, w_dw, dha)


def _a_in_bwd(dr1, dha, w_g, tm):
    s = dr1.shape[0]
    npd = w_g.shape[2]

    def body(dr_ref, dh_ref, w_ref, o_ref):
        acc = ALPHA * dr_ref[...]
        for j in range(N_DEV):
            acc = acc + _dot_nt(dh_ref[:, npd * j:npd * (j + 1)], w_ref[j])
        o_ref[...] = acc

    row = lambda i: (i, 0)
    return pl.pallas_call(
        body, name="a_in_bwd", grid=(s // tm,),
        in_specs=[pl.BlockSpec((tm, D), row), pl.BlockSpec((tm, 3 * D), row),
                  pl.BlockSpec(w_g.shape, lambda i: (0, 0, 0))],
        out_specs=pl.BlockSpec((tm, D), row), out_shape=_sds((s, D), F32),
        compiler_params=_params(36, 1),
    )(dr1, dha, w_g)


def _wgrad(name, a, b, npd, ts, total=None, at=0, into=None):
    s = a.shape[0]
    n_blk = b.shape[1] // npd
    total = n_blk if total is None else total
    assert at % n_blk == 0

    def body(*refs):
        a_ref, b_ref = refs[:2]
        o_ref = refs[-1]
        si = pl.program_id(0)

        @pl.when(si == 0)
        def _():
            o_ref[...] = jnp.zeros(o_ref.shape, F32)

        a_t = a_ref[...].T
        for j in range(n_blk):
            o_ref[j] += _dot(a_t, b_ref[:, npd * j:npd * (j + 1)])

    in_specs = [pl.BlockSpec((ts, D), lambda si: (si, 0)), pl.BlockSpec((ts, n_blk * npd), lambda si: (si, 0))]
    args = [a, b]
    aliases = {}
    if into is not None:
        in_specs.append(pl.BlockSpec(memory_space=pl.ANY))
        args.append(into)
        aliases = {2: 0}
    return pl.pallas_call(
        body, name=name, grid=(s // ts,), in_specs=in_specs,
        out_specs=pl.BlockSpec((n_blk, D, npd), lambda si: (at // n_blk, 0, 0)),
        out_shape=_sds((total, D, npd), F32), input_output_aliases=aliases,
        compiler_params=_params(56, 1),
    )(*args)


SMALL_ROWS = 40
GRAD_ROWS = 48


def kernel(x, a_w_in, a_b_in, a_w_dw, a_b_dw, a_ln_g, a_ln_b, a_w_out, a_b_out, kv_w, b_w_in, b_w_out, b_b_out, post_ln_g, post_ln_b, loss_target, m_a_w_in, m_a_b_in, m_a_w_dw, m_a_b_dw, m_a_ln_g, m_a_ln_b, m_a_w_out, m_a_b_out, m_kv_w, m_b_w_in, m_b_w_out, m_b_b_out, m_post_ln_g, m_post_ln_b, v_a_w_in, v_a_b_in, v_a_w_dw, v_a_b_dw, v_a_ln_g, v_a_ln_b, v_a_w_out, v_a_b_out, v_kv_w, v_b_w_in, v_b_w_out, v_b_b_out, v_post_ln_g, v_post_ln_b):
    s = x.shape[1]
    assert x.shape == (1, s, D) and s % (DILATIONS[-1] * BLK) == 0
    xs = x.reshape(s, D)
    tgt = loss_target.reshape(s, D)
    me = 4 * lax.axis_index("x") + 2 * lax.axis_index("y") + lax.axis_index("c")
    c_idx = lax.axis_index("c").astype(jnp.int32).reshape(1)

    def small_pack(b_in, w_dw, b_dw, ln_g, ln_b, b_out):
        rows = [b_in.reshape(3, BLK), w_dw.reshape(CONV_W, BLK), b_dw.reshape(1, BLK), ln_g.reshape(1, BLK),
                ln_b.reshape(1, BLK), b_out.reshape(1, BLK)]
        n = sum(r.shape[0] for r in rows)
        return jnp.concatenate(rows + [jnp.zeros((SMALL_ROWS - n, BLK), F32)], axis=0)

    shards = [a_w_in[0], a_w_out[0], kv_w, b_w_in[0], b_w_out[0], small_pack(a_b_in, a_w_dw, a_b_dw, a_ln_g, a_ln_b, a_b_out)]
    wa_in, wa_out, wkv, wb_in, wb_out, sm = _all_gather("gather_weights", shards, [BF16] * 5 + [F32])
    wa_out = wa_out.reshape(D, D)
    wb_out = wb_out.reshape(D, D)
    ba_in = sm[:, 0:3, :].reshape(1, 3 * D)
    w_dw = jnp.concatenate([sm[:, 3:3 + CONV_W, :].transpose(1, 0, 2).reshape(CONV_W, D), jnp.zeros((1, D), F32)], axis=0)
    b_dw, ln_g, ln_b, ba_out = (sm[:, 34 + k, :].reshape(1, D) for k in range(4))
    pg0, pg1 = post_ln_g[0:1], post_ln_g[1:2]
    pb0, pb1 = post_ln_b[0:1], post_ln_b[1:2]

    h_a, u0, xb = _a_in_proj(xs, wa_in, ba_in, 256)
    u1, g_a = _a_conv_gate(u0, h_a, w_dw, b_dw, ln_g, ln_b, 256)
    n1, rstd1, x1b = _a_out_proj(g_a, wa_out, ba_out, xs, pg0, pb0, 512)
    k_all, v_all = _kv_proj(x1b, wkv, 512)
    q, z_b = _b_in_proj(x1b, wb_in, 512)
    o, lse = _attn_fwd(q, k_all, v_all, None, None, None, 0, True, False)
    o, lse = _attn_fwd(q, k_all, v_all, o, lse, None, 1, False, False)
    o, lse, g_b = _attn_fwd(q, k_all, v_all, o, lse, z_b, 2, False, True)
    dr2, dr2b, loss8, dpg1, dpb1, dbb_out = _b_out_loss(g_b, wb_out, b_b_out, n1, pg0, pb0, pg1, pb1, tgt, 512)

    do, dhb, dd = _b_out_bwd(dr2b, wb_out, z_b, o, 512)
    dk_all = dv_all = None
    for g in range(3):
        dhb, dk_all, dv_all = _attn_bwd(q, k_all, v_all, do, lse, dd, dhb, dk_all, dv_all, g)
    dr1, dr1b, dpg0, dpb0, dba_out = _b_in_bwd(dr2, dhb, dk_all, dv_all, wb_in, wkv, n1, rstd1, pg0, 256)
    du1, dha, dln_g, dln_b, dbz = _a_out_bwd(dr1b, wa_out, u1, h_a, ln_g, ln_b, 512)
    dha, dw_dw, db_dw, dba, dbg = _a_conv_bwd(du1, u0, h_a, dha, w_dw, 256)
    grad_x = _a_in_bwd(dr1, dha, wa_in, 512)

    p_a_in = _wgrad("wgrad_a_in", xb, dha, 384, 512)
    p_kv = _wgrad("wgrad_k", x1b, dk_all, 768, 512, total=N_DEV)
    p_kv = _wgrad("wgrad_v", x1b, dv_all, 768, 512, total=N_DEV, at=N_DEV // 2, into=p_kv)
    p_b_in = _wgrad("wgrad_b_in", x1b, dhb, 512, 512)
    p_a_out = _wgrad("wgrad_a_out", g_a, dr1b, D, 512).reshape(N_DEV, BLK, D)
    p_b_out = _wgrad("wgrad_b_out", g_b, dr2b, D, 512).reshape(N_DEV, BLK, D)
    parts = [p_a_in, p_kv, p_b_in, p_a_out, p_b_out]

    from_sibling = _exchange_sibling("reduce_sibling", parts)
    chip_sums = [_pair_add(f"pair_add_{k}", p, r, c_idx) for k, (p, r) in enumerate(zip(parts, from_sibling))]
    from_chips = _exchange_chips("reduce_chips", chip_sums)

    big_w = [a_w_in[0], kv_w, b_w_in[0], a_w_out[0], b_w_out[0]]
    big_m = [m_a_w_in[0], m_kv_w, m_b_w_in[0], m_a_w_out[0], m_b_w_out[0]]
    big_v = [v_a_w_in[0], v_kv_w, v_b_w_in[0], v_a_w_out[0], v_b_w_out[0]]
    big = [_sum_adamw(f"adamw_{k}", from_chips[k], big_w[k], big_m[k], big_v[k]) for k in range(5)]

    rows = [dba[0:1], dbg[0:1], dbz[0:1], dw_dw[0:CONV_W], db_dw[0:1], dln_g[0:1], dln_b[0:1], dba_out[0:1],
            dbb_out[0:1], dpg0[0:1], dpg1[0:1], dpb0[0:1], dpb1[0:1], loss8[0:1]]
    n_rows = sum(r.shape[0] for r in rows)
    gpack = jnp.concatenate(rows + [jnp.zeros((GRAD_ROWS - n_rows, D), F32)], axis=0)
    (gall,) = _all_gather("gather_small_grads", [gpack], [F32])
    gs = _small_sum("small_sum", gall)
    loss = gs[43, 0]

    def my(vec, width):
        return lax.dynamic_slice_in_dim(vec, me * width, width, axis=-1)

    g_small = [my(gs[0:3].reshape(1, 3 * D), 384), my(gs[3:34], BLK)[None], my(gs[34:35], BLK), my(gs[35:36], BLK),
               my(gs[36:37], BLK), my(gs[37:38], BLK), gs[38:39], gs[39:41], gs[41:43]]
    w_small = [a_b_in, a_w_dw, a_b_dw, a_ln_g, a_ln_b, a_b_out, b_b_out, post_ln_g, post_ln_b]
    m_small = [m_a_b_in, m_a_w_dw, m_a_b_dw, m_a_ln_g, m_a_ln_b, m_a_b_out, m_b_b_out, m_post_ln_g, m_post_ln_b]
    v_small = [v_a_b_in, v_a_w_dw, v_a_b_dw, v_a_ln_g, v_a_ln_b, v_a_b_out, v_b_b_out, v_post_ln_g, v_post_ln_b]
    sizes = [math.prod(w.shape) for w in w_small]
    total = sum(sizes)
    padded = -(-total // (8 * BLK)) * (8 * BLK)

    def flat(parts_):
        return jnp.concatenate([p.reshape(-1) for p in parts_] + [jnp.ones((padded - total,), F32)]).reshape(-1, BLK)

    sd, sm_new, sv_new = _small_adamw("adamw_small", flat(w_small), flat(g_small), flat(m_small), flat(v_small))

    def unflat(packed):
        out, pos = [], 0
        vec = packed.reshape(-1)
        for w, n in zip(w_small, sizes):
            out.append(vec[pos:pos + n].reshape(w.shape))
            pos += n
        return out

    g_small = [g.reshape(w.shape) for g, w in zip(g_small, w_small)]
    d_small, nm_small, nv_small = unflat(sd), unflat(sm_new), unflat(sv_new)

    def ordered(bigs, smalls):
        a_in, kvw, b_in, a_out, b_out = bigs
        return [a_in[None], smalls[0], smalls[1], smalls[2], smalls[3], smalls[4], a_out[None], smalls[5],
                kvw, b_in[None], b_out[None], smalls[6], smalls[7], smalls[8]]

    grads = ordered([b[0] for b in big], g_small)
    deltas = ordered([b[1] for b in big], d_small)
    new_m = ordered([b[2] for b in big], nm_small)
    new_v = ordered([b[3] for b in big], nv_small)
    return (loss, grad_x.reshape(1, s, D), *grads, *deltas, *new_m, *new_v)
```

```python
import math

import numpy as np
import jax
import jax.numpy as jnp
from jax import lax
from jax.experimental import pallas as pl
from jax.experimental.pallas import tpu as pltpu

F32 = jnp.float32
BF16 = jnp.bfloat16
MESH = pl.DeviceIdType.MESH

D = 1024
N_DEV = 8
HEAD_DIM = 64
N_HEADS = 16
DILATIONS = (1, 4, 16)
BLK = 128
CONV_W = 31
ALPHA = (2.0 * 2) ** 0.25
LN_EPS = 1e-5
SLOPES = tuple(2.0 ** (-8.0 * (h + 1) / N_HEADS) for h in range(N_HEADS))
NEG = -1e30

ADAM_LR = 0.001
ADAM_B1 = 0.9
ADAM_B2 = 0.999
ADAM_EPS = 1e-08
ADAM_WD = 0.01
ADAM_STEP = 10

VMEM_CAP_MB = 64


def _params(vmem_mb, n_grid=0):
    sem = ("arbitrary",) * n_grid if n_grid else None
    return pltpu.CompilerParams(dimension_semantics=sem, vmem_limit_bytes=min(vmem_mb, VMEM_CAP_MB - 6) * 2 ** 20)


def _sds(shape, dtype):
    return jax.ShapeDtypeStruct(tuple(shape), dtype)


def _sigmoid(v):
    return jax.nn.sigmoid(v)


def _dsilu(v, s):
    return s * (1.0 + v * (1.0 - s))


def _ln_stats(r):
    mu = jnp.mean(r, axis=-1, keepdims=True)
    xc = r - mu
    var = jnp.mean(xc * xc, axis=-1, keepdims=True)
    rstd = lax.rsqrt(var + LN_EPS)
    return xc * rstd, rstd


def _ln_bwd(dn, n, rstd):
    m1 = jnp.mean(dn, axis=-1, keepdims=True)
    m2 = jnp.mean(dn * n, axis=-1, keepdims=True)
    return rstd * (dn - m1 - n * m2)


def _rowsum8(v):
    tm, c = v.shape
    return v.reshape(tm // 8, 8, c).sum(axis=0)


def _acc_init(i, *refs):
    @pl.when(i == 0)
    def _():
        for r in refs:
            r[...] = jnp.zeros(r.shape, r.dtype)


def _acc_finish(i, last, *refs):
    @pl.when(i == last)
    def _():
        for r in refs:
            r[...] = jnp.broadcast_to(jnp.sum(r[...], axis=0, keepdims=True), r.shape)


def _dot(a, b):
    return jnp.dot(a, b, preferred_element_type=F32)


def _dot_nt(a, b):
    return lax.dot_general(a, b, (((1,), (1,)), ((), ())), preferred_element_type=F32)


def _place():
    return lax.axis_index("x"), lax.axis_index("y"), lax.axis_index("c")


def _all_gather(name, arrays, dtypes):
    n = len(arrays)

    def body(*refs):
        ins, outs, stages = refs[:n], refs[n:2 * n], refs[2 * n:3 * n]
        send_sems, recv_sems, local_sems = refs[3 * n:]
        x, y, c = _place()
        me, sibling = (x, y, c), (x, y, 1 - c)
        chips = [(1 - x, y), (x, 1 - y), (1 - x, 1 - y)]

        def slot(ref, p):
            return ref.at[4 * p[0] + 2 * p[1] + p[2]]

        def copy(a, k, block, to, src=None):
            return pltpu.make_async_remote_copy(
                src_ref=slot(outs[a], block) if src is None else src, dst_ref=slot(outs[a], block),
                send_sem=send_sems.at[a, k], recv_sem=recv_sems.at[a, k], device_id=to, device_id_type=MESH)

        first, mine = [], []
        for a in range(n):
            stages[a][...] = ins[a][...].astype(stages[a].dtype)
            cp = pltpu.make_async_copy(stages[a], slot(outs[a], me), local_sems.at[a])
            cp.start()
            mine.append(cp)
            first.append(copy(a, 0, me, sibling, src=stages[a]))
            first += [copy(a, 1 + j, me, (*chip, c), src=stages[a]) for j, chip in enumerate(chips)]
        for cp in first:
            cp.start()
        passed = []
        for j, chip in enumerate(chips):
            for a in range(n):
                copy(a, 1 + j, (*chip, c), me).wait_recv()
                cp = copy(a, 4 + j, (*chip, c), sibling)
                cp.start()
                passed.append(cp)
        for a in range(n):
            copy(a, 0, sibling, me).wait_recv()
            for j, chip in enumerate(chips):
                copy(a, 4 + j, (*chip, 1 - c), me).wait_recv()
        for cp in first + passed:
            cp.wait_send()
        for cp in mine:
            cp.wait()

    stage_bytes = sum(math.prod(a.shape) * (jnp.dtype(a.dtype).itemsize + jnp.dtype(dt).itemsize)
                      for a, dt in zip(arrays, dtypes))
    return pl.pallas_call(
        body, name=name,
        out_shape=[_sds((N_DEV,) + a.shape, dt) for a, dt in zip(arrays, dtypes)],
        in_specs=[pl.BlockSpec(memory_space=pltpu.VMEM)] * n,
        out_specs=[pl.BlockSpec(memory_space=pl.ANY)] * n,
        scratch_shapes=[pltpu.VMEM(a.shape, dt) for a, dt in zip(arrays, dtypes)]
        + [pltpu.SemaphoreType.DMA((n, 7)), pltpu.SemaphoreType.DMA((n, 7)), pltpu.SemaphoreType.DMA((n,))],
        compiler_params=_params(stage_bytes // 2 ** 20 + 8),
    )(*arrays)


def _exchange_sibling(name, parts):
    n = len(parts)

    def body(*refs):
        ins, outs = refs[:n], refs[n:2 * n]
        send_sems, recv_sems = refs[2 * n:]
        x, y, c = _place()
        copies = []
        for a in range(n):
            for p in range(4):
                copies.append(pltpu.make_async_remote_copy(
                    src_ref=ins[a].at[2 * p + 1 - c], dst_ref=outs[a].at[p],
                    send_sem=send_sems.at[a, p], recv_sem=recv_sems.at[a, p],
                    device_id=(x, y, 1 - c), device_id_type=MESH))
        for cp in copies:
            cp.start()
        for cp in copies:
            cp.wait_recv()
        for cp in copies:
            cp.wait_send()

    return pl.pallas_call(
        body, name=name,
        out_shape=[_sds((4,) + p.shape[1:], p.dtype) for p in parts],
        in_specs=[pl.BlockSpec(memory_space=pl.ANY)] * n,
        out_specs=[pl.BlockSpec(memory_space=pl.ANY)] * n,
        scratch_shapes=[pltpu.SemaphoreType.DMA((n, 4)), pltpu.SemaphoreType.DMA((n, 4))],
    )(*parts)


def _exchange_chips(name, sums):
    n = len(sums)

    def body(*refs):
        ins, outs = refs[:n], refs[n:2 * n]
        send_sems, recv_sems, local_sems = refs[2 * n:]
        x, y, c = _place()
        my_chip = 2 * x + y
        chips = [(1 - x, y), (x, 1 - y), (1 - x, 1 - y)]
        copies, mine = [], []
        for a in range(n):
            cp = pltpu.make_async_copy(ins[a].at[my_chip], outs[a].at[my_chip], local_sems.at[a])
            cp.start()
            mine.append(cp)
            for k, (px, py) in enumerate(chips):
                copies.append(pltpu.make_async_remote_copy(
                    src_ref=ins[a].at[2 * px + py], dst_ref=outs[a].at[my_chip],
                    send_sem=send_sems.at[a, k], recv_sem=recv_sems.at[a, k],
                    device_id=(px, py, c), device_id_type=MESH))
        for cp in copies:
            cp.start()
        for cp in copies:
            cp.wait_recv()
        for cp in copies:
            cp.wait_send()
        for cp in mine:
            cp.wait()

    return pl.pallas_call(
        body, name=name,
        out_shape=[_sds(s.shape, s.dtype) for s in sums],
        in_specs=[pl.BlockSpec(memory_space=pl.ANY)] * n,
        out_specs=[pl.BlockSpec(memory_space=pl.ANY)] * n,
        scratch_shapes=[pltpu.SemaphoreType.DMA((n, 3)), pltpu.SemaphoreType.DMA((n, 3)),
                        pltpu.SemaphoreType.DMA((n,))],
    )(*sums)


def _pair_add(name, part, recv, c_idx):
    _, r, c = part.shape
    tr = min(r, 256)

    def body(c_ref, a_ref, b_ref, o_ref):
        o_ref[...] = a_ref[...] + b_ref[...]

    return pl.pallas_call(
        body, name=name,
        grid_spec=pltpu.PrefetchScalarGridSpec(
            num_scalar_prefetch=1, grid=(4, r // tr),
            in_specs=[pl.BlockSpec((1, tr, c), lambda p, i, cr: (2 * p + cr[0], i, 0)),
                      pl.BlockSpec((1, tr, c), lambda p, i, cr: (p, i, 0))],
            out_specs=pl.BlockSpec((1, tr, c), lambda p, i, cr: (p, i, 0))),
        out_shape=_sds((4, r, c), F32),
        compiler_params=_params(16, 2),
    )(c_idx, part, recv)


def _adamw_math(w, g, m, v):
    m = ADAM_B1 * m + (1.0 - ADAM_B1) * g
    v = ADAM_B2 * v + (1.0 - ADAM_B2) * (g * g)
    m_hat = m / (1.0 - ADAM_B1 ** ADAM_STEP)
    v_hat = v / (1.0 - ADAM_B2 ** ADAM_STEP)
    delta = -ADAM_LR * (m_hat / (jnp.sqrt(v_hat) + ADAM_EPS) + ADAM_WD * w)
    return delta, m, v


def _sum_adamw(name, recv, w, m, v):
    r, c = w.shape
    tr = min(r, 256)

    def body(p_ref, w_ref, m_ref, v_ref, g_ref, d_ref, nm_ref, nv_ref):
        g = (p_ref[0] + p_ref[1]) + (p_ref[2] + p_ref[3])
        g_ref[...] = g
        d_ref[...], nm_ref[...], nv_ref[...] = _adamw_math(w_ref[...], g, m_ref[...], v_ref[...])

    blk = pl.BlockSpec((tr, c), lambda i: (i, 0))
    return pl.pallas_call(
        body, name=name, grid=(r // tr,),
        in_specs=[pl.BlockSpec((4, tr, c), lambda i: (0, i, 0)), blk, blk, blk],
        out_specs=[blk] * 4, out_shape=[_sds((r, c), F32)] * 4,
        compiler_params=_params(24, 1),
    )(recv, w, m, v)


def _small_sum(name, gathered):
    _, r, c = gathered.shape

    def body(p_ref, o_ref):
        acc = p_ref[0]
        for j in range(1, N_DEV):
            acc = acc + p_ref[j]
        o_ref[...] = acc

    return pl.pallas_call(body, name=name, out_shape=_sds((r, c), F32),
                          in_specs=[pl.BlockSpec(memory_space=pltpu.VMEM)],
                          out_specs=pl.BlockSpec(memory_space=pltpu.VMEM))(gathered)


def _small_adamw(name, w, g, m, v):
    def body(w_ref, g_ref, m_ref, v_ref, d_ref, nm_ref, nv_ref):
        d_ref[...], nm_ref[...], nv_ref[...] = _adamw_math(w_ref[...], g_ref[...], m_ref[...], v_ref[...])

    vm = pl.BlockSpec(memory_space=pltpu.VMEM)
    return pl.pallas_call(body, name=name, out_shape=[_sds(w.shape, F32)] * 3,
                          in_specs=[vm] * 4, out_specs=[vm] * 3)(w, g, m, v)


def _a_in_proj(x, w_g, b_full, tm):
    s = x.shape[0]
    npd = w_g.shape[2]

    def body(x_ref, w_ref, b_ref, h_ref, u0_ref, xb_ref):
        xb = x_ref[...].astype(BF16)
        xb_ref[...] = xb
        for j in range(N_DEV):
            sl = slice(npd * j, npd * (j + 1))
            h_ref[:, sl] = _dot(xb, w_ref[j]) + b_ref[:, sl]
        u0_ref[...] = h_ref[:, 0:D] * _sigmoid(h_ref[:, D:2 * D])

    row = lambda i: (i, 0)
    return pl.pallas_call(
        body, name="a_in_proj", grid=(s // tm,),
        in_specs=[pl.BlockSpec((tm, D), row), pl.BlockSpec(w_g.shape, lambda i: (0, 0, 0)),
                  pl.BlockSpec((1, 3 * D), lambda i: (0, 0))],
        out_specs=[pl.BlockSpec((tm, 3 * D), row), pl.BlockSpec((tm, D), row), pl.BlockSpec((tm, D), row)],
        out_shape=[_sds((s, 3 * D), F32), _sds((s, D), F32), _sds((s, D), BF16)],
        compiler_params=_params(44, 1),
    )(x, w_g, b_full)


CONV_HALO = 32
CONV_CHUNK = 32
SUBLANES = 8
COPY_ROWS = 56


def _shifted_copies(buf, shs, tm):
    n = tm + CONV_HALO - SUBLANES
    for s in range(1, SUBLANES):
        for c0 in range(0, n, COPY_ROWS):
            c1 = min(c0 + COPY_ROWS, n)
            shs[s - 1, c0:c1, :] = buf[c0 + s:c1 + s, :]


def _shifted_rows(buf, shs, start):
    a8, s = divmod(start, SUBLANES)
    if s == 0:
        return buf[start:start + CONV_CHUNK, :]
    return shs[s - 1, a8 * SUBLANES:a8 * SUBLANES + CONV_CHUNK, :]


def _a_conv_gate(u0, h, w_dw, b_dw, ln_g, ln_b, tm):
    s = u0.shape[0]
    per = tm // CONV_HALO

    def body(u0_ref, halo_ref, z_ref, w_ref, bdw_ref, g_ref, b_ref, u1_ref, ga_ref, buf, shs):
        i = pl.program_id(0)
        buf[0:CONV_HALO, :] = jnp.where(i > 0, halo_ref[...], 0.0)
        buf[CONV_HALO:, :] = u0_ref[...]
        _shifted_copies(buf, shs, tm)
        off = CONV_HALO - (CONV_W - 1)
        for ch in range(tm // CONV_CHUNK):
            r0 = ch * CONV_CHUNK
            acc = jnp.broadcast_to(bdw_ref[...], (CONV_CHUNK, D))
            for j in range(CONV_W):
                acc = acc + w_ref[j:j + 1, :] * _shifted_rows(buf, shs, r0 + off + j)
            u1_ref[r0:r0 + CONV_CHUNK, :] = acc
            n, _ = _ln_stats(acc)
            pre = n * g_ref[...] + b_ref[...]
            z = z_ref[r0:r0 + CONV_CHUNK, :]
            ga_ref[r0:r0 + CONV_CHUNK, :] = ((pre * _sigmoid(pre)) * (z * _sigmoid(z))).astype(BF16)

    row = lambda i: (i, 0)
    vec = pl.BlockSpec((1, D), lambda i: (0, 0))
    return pl.pallas_call(
        body, name="a_conv_gate", grid=(s // tm,),
        in_specs=[pl.BlockSpec((tm, D), row),
                  pl.BlockSpec((CONV_HALO, D), lambda i: (jnp.maximum(i * per - 1, 0), 0)),
                  pl.BlockSpec((tm, D), lambda i: (i, 2)),
                  pl.BlockSpec((32, D), lambda i: (0, 0)), vec, vec, vec],
        out_specs=[pl.BlockSpec((tm, D), row), pl.BlockSpec((tm, D), row)],
        out_shape=[_sds((s, D), F32), _sds((s, D), BF16)],
        scratch_shapes=[pltpu.VMEM((tm + CONV_HALO, D), F32), pltpu.VMEM((SUBLANES - 1, tm + CONV_HALO, D), F32)],
        compiler_params=_params(40, 1),
    )(u0, u0, h, w_dw, b_dw, ln_g, ln_b)


def _a_out_proj(ga, w, b, x, pg, pb, tm):
    s = x.shape[0]

    def body(ga_ref, w_ref, b_ref, x_ref, pg_ref, pb_ref, n_ref, rstd_ref, xb_ref):
        r = ALPHA * x_ref[...] + (_dot(ga_ref[...], w_ref[...]) + b_ref[...])
        n, rstd = _ln_stats(r)
        n_ref[...] = n
        rstd_ref[...] = rstd
        xb_ref[...] = (n * pg_ref[...] + pb_ref[...]).astype(BF16)

    row = lambda i: (i, 0)
    vec = pl.BlockSpec((1, D), lambda i: (0, 0))
    return pl.pallas_call(
        body, name="a_out_proj", grid=(s // tm,),
        in_specs=[pl.BlockSpec((tm, D), row), pl.BlockSpec((D, D), lambda i: (0, 0)), vec,
                  pl.BlockSpec((tm, D), row), vec, vec],
        out_specs=[pl.BlockSpec((tm, D), row), pl.BlockSpec((tm, 1), row), pl.BlockSpec((tm, D), row)],
        out_shape=[_sds((s, D), F32), _sds((s, 1), F32), _sds((s, D), BF16)],
        compiler_params=_params(32, 1),
    )(ga, w, b, x, pg, pb)


def _kv_proj(xb, w_g, tm):
    s = xb.shape[0]
    npd = w_g.shape[2]
    half = N_DEV // 2

    def body(x_ref, w_ref, k_ref, v_ref):
        xv = x_ref[...]
        for j in range(N_DEV):
            o_ref = k_ref if j < half else v_ref
            jj = j % half
            o_ref[:, npd * jj:npd * (jj + 1)] = _dot(xv, w_ref[j]).astype(BF16)

    row = lambda i: (i, 0)
    return pl.pallas_call(
        body, name="kv_proj", grid=(s // tm,),
        in_specs=[pl.BlockSpec((tm, D), row), pl.BlockSpec(w_g.shape, lambda i: (0, 0, 0))],
        out_specs=[pl.BlockSpec((tm, 3 * D), row), pl.BlockSpec((tm, 3 * D), row)],
        out_shape=[_sds((s, 3 * D), BF16), _sds((s, 3 * D), BF16)],
        compiler_params=_params(52, 1),
    )(xb, w_g)


def _b_in_proj(xb, w_g, tm):
    s = xb.shape[0]
    npd = w_g.shape[2]
    scale = HEAD_DIM ** -0.5

    def body(x_ref, w_ref, q_ref, z_ref):
        xv = x_ref[...]
        for j in range(N_DEV):
            hj = _dot(xv, w_ref[j])
            if j < 6:
                q_ref[:, npd * j:npd * (j + 1)] = (hj.astype(BF16) * scale).astype(BF16)
            else:
                z_ref[:, npd * (j - 6):npd * (j - 5)] = hj

    row = lambda i: (i, 0)
    return pl.pallas_call(
        body, name="b_in_proj", grid=(s // tm,),
        in_specs=[pl.BlockSpec((tm, D), row), pl.BlockSpec(w_g.shape, lambda i: (0, 0, 0))],
        out_specs=[pl.BlockSpec((tm, 3 * D), row), pl.BlockSpec((tm, D), row)],
        out_shape=[_sds((s, 3 * D), BF16), _sds((s, D), F32)],
        compiler_params=_params(44, 1),
    )(xb, w_g)


ATTN_NQ = {1: 8, 4: 2, 16: 1}
N_PAIRS = N_HEADS // 2


def _band_table(d):
    qi = np.arange(BLK)[:, None]
    kj = np.arange(2 * BLK)[None, :]
    dist = qi + BLK - kj
    ok = (dist >= 0) & (dist <= BLK)
    return jnp.asarray(np.where(ok, -(d * dist).astype(np.float32), np.float32(NEG)), dtype=F32)


def _slope_table():
    t = np.zeros((N_PAIRS, 8, 2 * BLK), np.float32)
    for h in range(N_HEADS):
        t[h // 2, h % 2, :] = SLOPES[h]
    return jnp.asarray(t)


def _head_masks():
    lane = lax.broadcasted_iota(jnp.int32, (1, BLK), 1)
    lo = (lane < HEAD_DIM).astype(BF16)
    return (lo, (1.0 - lo).astype(BF16))


def _pick_col(tile, lane, h):
    return jnp.sum(jnp.where(lane == h, tile, 0.0), axis=1, keepdims=True)


def _rows(base, n, d):
    return pl.ds(base, n) if d == 1 else pl.ds(base, n, stride=d)


def _attn_fwd(q, k_all, v_all, o_acc, lse_acc, z, g, first, last):
    d = DILATIONS[g]
    s = q.shape[0]
    nq = ATTN_NQ[d]
    halo = BLK * d
    tile = nq * halo
    assert s % tile == 0

    def body(*refs):
        q_ref, k_ref, kh_ref, v_ref, vh_ref, nd_ref, sl_ref = refs[:7]
        k = 7
        if not first:
            oa_ref, la_ref = refs[k:k + 2]
            k += 2
        if last:
            z_ref = refs[k]
            k += 1
        o_ref, l_ref = refs[k:k + 2]
        k += 2
        if last:
            gb_ref = refs[k]
            k += 1
        qf, kf, vf = refs[k:k + 3]
        if last:
            gf = refs[k + 3]
        n = pl.program_id(0)
        hp = pl.program_id(1)
        qf[...] = q_ref[...].astype(F32)
        kf[0:halo, :] = kh_ref[...].astype(F32)
        kf[halo:, :] = k_ref[...].astype(F32)
        vf[0:halo, :] = vh_ref[...].astype(F32)
        vf[halo:, :] = v_ref[...].astype(F32)

        @pl.when(hp == 0)
        def _():
            l_ref[...] = jnp.zeros(l_ref.shape, F32) if first else la_ref[...]

        col = lax.broadcasted_iota(jnp.int32, (2 * BLK, 2 * BLK), 1)
        lane = lax.broadcasted_iota(jnp.int32, (BLK, BLK), 1)
        masks = _head_masks()
        bias = jnp.concatenate([sl_ref[0, e:e + 1, :] * nd_ref[...] for e in range(2)], axis=0)
        bias0 = bias + jnp.where((n == 0) & (col < BLK), NEG, 0.0)
        for b in range(nq):
            for r in range(d):
                rq = _rows(b * halo + r, BLK, d)
                rk = _rows(b * halo + r, 2 * BLK, d)
                q2 = qf[rq, :].astype(BF16)
                kcat = kf[rk, :].astype(BF16)
                vcat = vf[rk, :].astype(BF16)
                lt = l_ref[rq, :]
                sc = _dot_nt(jnp.concatenate([q2 * masks[0], q2 * masks[1]], axis=0), kcat)
                sc = sc + (bias0 if b == 0 else bias)
                m = jnp.max(sc, axis=1, keepdims=True)
                p = jnp.exp(sc - m)
                l = jnp.sum(p, axis=1, keepdims=True)
                oh = _dot(p.astype(BF16), vcat) / l
                lse = m + jnp.log(l)
                if not first:
                    old = jnp.concatenate([_pick_col(lt, lane, 2 * hp + e) for e in range(2)], axis=0)
                    mx = jnp.maximum(old, lse)
                    new = mx + jnp.log(jnp.exp(old - mx) + jnp.exp(lse - mx))
                    keep = jnp.exp(old - new)
                    oh = oh * jnp.exp(lse - new)
                    lse = new
                o2 = jnp.where(lane < HEAD_DIM, oh[0:BLK], oh[BLK:])
                if not first:
                    o2 = o2 + oa_ref[rq, :] * jnp.where(lane < HEAD_DIM, keep[0:BLK], keep[BLK:])
                lt = jnp.where(lane == 2 * hp, lse[0:BLK], lt)
                lt = jnp.where(lane == 2 * hp + 1, lse[BLK:], lt)
                o_ref[rq, :] = o2
                l_ref[rq, :] = lt
                if last:
                    zz = z_ref[rq, :]
                    gf[rq, :] = o2 * (zz * _sigmoid(zz))
        if last:
            gb_ref[...] = gf[...].astype(BF16)

    col_blk = lambda n, hp: (n, g * N_PAIRS + hp)
    halo_blk = lambda n, hp: (jnp.maximum(n * nq - 1, 0), g * N_PAIRS + hp)
    own = pl.BlockSpec((tile, BLK), lambda n, hp: (n, hp))
    own_l = pl.BlockSpec((tile, BLK), lambda n, hp: (n, 0))
    in_specs = [pl.BlockSpec((tile, BLK), col_blk),
                pl.BlockSpec((tile, BLK), col_blk), pl.BlockSpec((halo, BLK), halo_blk),
                pl.BlockSpec((tile, BLK), col_blk), pl.BlockSpec((halo, BLK), halo_blk),
                pl.BlockSpec((BLK, 2 * BLK), lambda n, hp: (0, 0)),
                pl.BlockSpec((1, 8, 2 * BLK), lambda n, hp: (hp, 0, 0))]
    args = [q, k_all, k_all, v_all, v_all, _band_table(d), _slope_table()]
    if not first:
        in_specs += [own, own_l]
        args += [o_acc, lse_acc]
    if last:
        in_specs += [own]
        args += [z]
    out_specs = [own, own_l] + ([own] if last else [])
    out_shape = [_sds((s, D), F32), _sds((s, BLK), F32)] + ([_sds((s, D), BF16)] if last else [])
    scratch = [pltpu.VMEM((tile, BLK), F32), pltpu.VMEM((tile + halo, BLK), F32), pltpu.VMEM((tile + halo, BLK), F32)]
    if last:
        scratch.append(pltpu.VMEM((tile, BLK), F32))
    return pl.pallas_call(
        body, name=f"attn_fwd_g{g}", grid=(s // tile, N_PAIRS), in_specs=in_specs, out_specs=out_specs,
        out_shape=out_shape, scratch_shapes=scratch, compiler_params=_params(32, 2),
    )(*args)


ATTN_PRE = {1: 1, 4: 1, 16: 4}


def _regroup(src, dst, d1):
    n = src.shape[0] // d1
    for r1 in range(d1):
        dst[r1] = src[pl.ds(r1, n, stride=d1), :]


def _ungroup(src, dst, d1):
    n = dst.shape[0] // d1
    for r1 in range(d1):
        dst[pl.ds(r1, n, stride=d1), :] = src[r1]


def _grouped(shape, d1):
    return pltpu.VMEM((d1, shape[0] // d1, shape[1]), F32)


def _attn_forward(q, k_all, v_all, o_acc, lse_acc, z, g, first, last):
    d = DILATIONS[g]
    s = q.shape[0]
    nq = ATTN_NQ[d]
    d1 = ATTN_PRE[d]
    d2 = d // d1
    halo = BLK * d
    tile = nq * halo
    assert s % tile == 0
    pre = d1 > 1

    def body(*refs):
        refs = list(refs)
        q_ref, k_ref, kh_ref, v_ref, vh_ref, nd_ref, sl_ref = refs[:7]
        del refs[:7]
        oa_ref, la_ref = (refs.pop(0), refs.pop(0)) if not first else (None, None)
        z_ref = refs.pop(0) if last else None
        o_ref, l_ref = refs.pop(0), refs.pop(0)
        gb_ref = refs.pop(0) if last else None
        qf, kf, vf = refs.pop(0), refs.pop(0), refs.pop(0)
        gf = refs.pop(0) if last else None
        q1 = k1 = v1 = l1 = o1 = oa1 = z1 = g1 = None
        if pre:
            q1, k1, v1, l1, o1 = (refs.pop(0) for _ in range(5))
            oa1 = refs.pop(0) if not first else None
            z1, g1 = (refs.pop(0), refs.pop(0)) if last else (None, None)
        n = pl.program_id(0)
        hp = pl.program_id(1)
        qf[...] = q_ref[...].astype(F32)
        kf[0:halo, :] = kh_ref[...].astype(F32)
        kf[halo:, :] = k_ref[...].astype(F32)
        vf[0:halo, :] = vh_ref[...].astype(F32)
        vf[halo:, :] = v_ref[...].astype(F32)

        @pl.when(hp == 0)
        def _():
            l_ref[...] = jnp.zeros(l_ref.shape, F32) if first else la_ref[...]

        if pre:
            for src, dst in ((qf, q1), (kf, k1), (vf, v1), (l_ref, l1), (oa_ref, oa1), (z_ref, z1)):
                if src is not None:
                    _regroup(src, dst, d1)

        def pick(nat, grp, r1):
            return grp.at[r1] if pre else nat

        def halves(colv):
            return jnp.where(lane < HEAD_DIM, colv[0:BLK], colv[BLK:])

        def put(tile_v, colv):
            tile_v = jnp.where(lane == 2 * hp, colv[0:BLK], tile_v)
            return jnp.where(lane == 2 * hp + 1, colv[BLK:], tile_v)

        col = lax.broadcasted_iota(jnp.int32, (2 * BLK, 2 * BLK), 1)
        lane = lax.broadcasted_iota(jnp.int32, (BLK, BLK), 1)
        masks = _head_masks()
        bias = jnp.concatenate([sl_ref[0, e:e + 1, :] * nd_ref[...] for e in range(2)], axis=0)
        bias0 = bias + jnp.where((n == 0) & (col < BLK), NEG, 0.0)
        for b in range(nq):
            for r in range(d):
                r1, r2 = r % d1, r // d1
                rq = _rows(b * (halo // d1) + r2, BLK, d2)
                rk = _rows(b * (halo // d1) + r2, 2 * BLK, d2)
                q2 = pick(qf, q1, r1)[rq, :].astype(BF16)
                kcat = pick(kf, k1, r1)[rk, :].astype(BF16)
                vcat = pick(vf, v1, r1)[rk, :].astype(BF16)
                sc = _dot_nt(jnp.concatenate([q2 * masks[0], q2 * masks[1]], axis=0), kcat)
                sc = sc + (bias0 if b == 0 else bias)
                m = jnp.max(sc, axis=1, keepdims=True)
                p = jnp.exp(sc - m)
                l = jnp.sum(p, axis=1, keepdims=True)
                oh = _dot(p.astype(BF16), vcat) / l
                lse = m + jnp.log(l)
                lt = pick(l_ref, l1, r1)[rq, :]
                if not first:
                    old = jnp.concatenate([_pick_col(lt, lane, 2 * hp + e) for e in range(2)], axis=0)
                    mx = jnp.maximum(old, lse)
                    new = mx + jnp.log(jnp.exp(old - mx) + jnp.exp(lse - mx))
                    o2 = halves(oh) * halves(jnp.exp(lse - new)) + pick(oa_ref, oa1, r1)[rq, :] * halves(jnp.exp(old - new))
                    lse = new
                else:
                    o2 = halves(oh)
                pick(o_ref, o1, r1)[rq, :] = o2
                pick(l_ref, l1, r1)[rq, :] = put(lt, lse)
                if last:
                    zz = pick(z_ref, z1, r1)[rq, :]
                    pick(gf, g1, r1)[rq, :] = o2 * (zz * _sigmoid(zz))
        if pre:
            _ungroup(o1, o_ref, d1)
            _ungroup(l1, l_ref, d1)
            if last:
                _ungroup(g1, gf, d1)
        if last:
            gb_ref[...] = gf[...].astype(BF16)

    col_blk = lambda n, hp: (n, g * N_PAIRS + hp)
    halo_blk = lambda n, hp: (jnp.maximum(n * nq - 1, 0), g * N_PAIRS + hp)
    own = pl.BlockSpec((tile, BLK), lambda n, hp: (n, hp))
    own_l = pl.BlockSpec((tile, BLK), lambda n, hp: (n, 0))
    in_specs = [pl.BlockSpec((tile, BLK), col_blk),
                pl.BlockSpec((tile, BLK), col_blk), pl.BlockSpec((halo, BLK), halo_blk),
                pl.BlockSpec((tile, BLK), col_blk), pl.BlockSpec((halo, BLK), halo_blk),
                pl.BlockSpec((BLK, 2 * BLK), lambda n, hp: (0, 0)),
                pl.BlockSpec((1, 8, 2 * BLK), lambda n, hp: (hp, 0, 0))]
    args = [q, k_all, k_all, v_all, v_all, _band_table(d), _slope_table()]
    if not first:
        in_specs += [own, own_l]
        args += [o_acc, lse_acc]
    if last:
        in_specs += [own]
        args += [z]
    out_specs = [own, own_l] + ([own] if last else [])
    out_shape = [_sds((s, D), F32), _sds((s, BLK), F32)] + ([_sds((s, D), BF16)] if last else [])
    t_shape, w_shape = (tile, BLK), (tile + halo, BLK)
    scratch = [pltpu.VMEM(t_shape, F32), pltpu.VMEM(w_shape, F32), pltpu.VMEM(w_shape, F32)]
    if last:
        scratch.append(pltpu.VMEM(t_shape, F32))
    if pre:
        scratch += [_grouped(t_shape, d1), _grouped(w_shape, d1), _grouped(w_shape, d1)]
        scratch += [_grouped(t_shape, d1)] * (2 + (0 if first else 1) + (2 if last else 0))
    return pl.pallas_call(
        body, name=f"attn_fwd_g{g}", grid=(s // tile, N_PAIRS), in_specs=in_specs, out_specs=out_specs,
        out_shape=out_shape, scratch_shapes=scratch, compiler_params=_params(56, 2),
    )(*args)


def _attn_backward(q, k_all, v_all, do, lse, dd, dhb, dk_all, dv_all, g):
    d = DILATIONS[g]
    s = q.shape[0]
    nq = ATTN_NQ[d]
    d1 = ATTN_PRE[d]
    d2 = d // d1
    halo = BLK * d
    tile = nq * halo
    nt = s // tile
    first = dk_all is None
    pre = d1 > 1

    def body(*refs):
        refs = list(refs)
        q_ref, k_ref, kh_ref, v_ref, vh_ref, nd_ref, sl_ref, do_ref, l_ref, dd_ref = refs[:10]
        del refs[:10 + (1 if first else 3)]
        dq_ref, dk_ref, dv_ref = refs[:3]
        qf, dof, kf, vf, dqf, dkf, dvf, ck, cv = refs[3:12]
        del refs[:12]
        if pre:
            q1, do1, k1, v1, l1, dd1, dq1, dk1, dv1 = refs
        else:
            q1 = do1 = k1 = v1 = l1 = dd1 = dq1 = None
            dk1, dv1 = dkf, dvf
        hp = pl.program_id(0)
        n = pl.program_id(1)

        @pl.when(n == 0)
        def _():
            ck[...] = jnp.zeros(ck.shape, F32)
            cv[...] = jnp.zeros(cv.shape, F32)

        dk1[...] = jnp.zeros(dk1.shape, F32)
        dv1[...] = jnp.zeros(dv1.shape, F32)

        def pick(nat, grp, r1):
            return grp.at[r1] if pre else nat

        @pl.when(n < nt)
        def _():
            qf[...] = q_ref[...].astype(F32)
            dof[...] = do_ref[...].astype(F32)
            kf[0:halo, :] = kh_ref[...].astype(F32)
            kf[halo:, :] = k_ref[...].astype(F32)
            vf[0:halo, :] = vh_ref[...].astype(F32)
            vf[halo:, :] = v_ref[...].astype(F32)
            if pre:
                for src, dst in ((qf, q1), (dof, do1), (kf, k1), (vf, v1), (l_ref, l1), (dd_ref, dd1)):
                    _regroup(src, dst, d1)
            col = lax.broadcasted_iota(jnp.int32, (2 * BLK, 2 * BLK), 1)
            lane = lax.broadcasted_iota(jnp.int32, (BLK, BLK), 1)
            masks = _head_masks()
            bias = jnp.concatenate([sl_ref[0, e:e + 1, :] * nd_ref[...] for e in range(2)], axis=0)
            bias0 = bias + jnp.where((n == 0) & (col < BLK), NEG, 0.0)
            for b in range(nq):
                for r in range(d):
                    r1, r2 = r % d1, r // d1
                    rq = _rows(b * (halo // d1) + r2, BLK, d2)
                    rk = _rows(b * (halo // d1) + r2, 2 * BLK, d2)
                    q2 = pick(qf, q1, r1)[rq, :].astype(BF16)
                    do2 = pick(dof, do1, r1)[rq, :].astype(BF16)
                    kcat = pick(kf, k1, r1)[rk, :].astype(BF16)
                    vcat = pick(vf, v1, r1)[rk, :].astype(BF16)
                    lt = pick(l_ref, l1, r1)[rq, :]
                    dt = pick(dd_ref, dd1, r1)[rq, :]
                    qs = jnp.concatenate([q2 * masks[0], q2 * masks[1]], axis=0)
                    dos = jnp.concatenate([do2 * masks[0], do2 * masks[1]], axis=0)
                    lcol = jnp.concatenate([_pick_col(lt, lane, 2 * hp + e) for e in range(2)], axis=0)
                    dcol = jnp.concatenate([_pick_col(dt, lane, 2 * hp + e) for e in range(2)], axis=0)
                    sc = _dot_nt(qs, kcat) + (bias0 if b == 0 else bias)
                    p = jnp.exp(sc - lcol)
                    ds = p * (_dot_nt(dos, vcat) - dcol)
                    dsb = ds.astype(BF16)
                    dq = _dot(dsb, kcat)
                    dq2 = (HEAD_DIM ** -0.5) * jnp.where(lane < HEAD_DIM, dq[0:BLK], dq[BLK:])
                    pick(dqf, dq1, r1)[rq, :] = dq2
                    pick(dkf, dk1, r1)[rk, :] += _dot(dsb.T, qs)
                    pick(dvf, dv1, r1)[rk, :] += _dot(p.astype(BF16).T, dos)
            if pre:
                _ungroup(dq1, dqf, d1)
            dq_ref[...] = dqf[...].astype(BF16)

        if pre:
            _ungroup(dk1, dkf, d1)
            _ungroup(dv1, dvf, d1)
        if tile > halo:
            dk_ref[0:tile - halo, :] = ck[0:tile - halo, :].astype(BF16)
            dv_ref[0:tile - halo, :] = cv[0:tile - halo, :].astype(BF16)
        dk_ref[tile - halo:, :] = (ck[tile - halo:, :] + dkf[0:halo, :]).astype(BF16)
        dv_ref[tile - halo:, :] = (cv[tile - halo:, :] + dvf[0:halo, :]).astype(BF16)
        ck[...] = dkf[halo:, :]
        cv[...] = dvf[halo:, :]

    cur = lambda n: jnp.minimum(n, nt - 1)
    col_blk = lambda hp, n: (cur(n), g * N_PAIRS + hp)
    halo_blk = lambda hp, n: (jnp.maximum(cur(n) * nq - 1, 0), g * N_PAIRS + hp)
    out_kv = lambda hp, n: (jnp.maximum(n - 1, 0), g * N_PAIRS + hp)
    small = pl.BlockSpec((tile, BLK), lambda hp, n: (cur(n), 0))
    hbm = pl.BlockSpec(memory_space=pl.ANY)
    in_specs = [pl.BlockSpec((tile, BLK), col_blk),
                pl.BlockSpec((tile, BLK), col_blk), pl.BlockSpec((halo, BLK), halo_blk),
                pl.BlockSpec((tile, BLK), col_blk), pl.BlockSpec((halo, BLK), halo_blk),
                pl.BlockSpec((BLK, 2 * BLK), lambda hp, n: (0, 0)),
                pl.BlockSpec((1, 8, 2 * BLK), lambda hp, n: (hp, 0, 0)),
                pl.BlockSpec((tile, BLK), lambda hp, n: (cur(n), hp)), small, small, hbm]
    args = [q, k_all, k_all, v_all, v_all, _band_table(d), _slope_table(), do, lse, dd, dhb]
    aliases = {10: 0}
    if not first:
        in_specs += [hbm, hbm]
        args += [dk_all, dv_all]
        aliases.update({11: 1, 12: 2})
    t_shape, w_shape = (tile, BLK), (tile + halo, BLK)
    tile_f32, wide_f32 = pltpu.VMEM(t_shape, F32), pltpu.VMEM(w_shape, F32)
    scratch = [tile_f32, tile_f32, wide_f32, wide_f32, tile_f32, wide_f32, wide_f32, tile_f32, tile_f32]
    if pre:
        tg, wg = _grouped(t_shape, d1), _grouped(w_shape, d1)
        scratch += [tg, tg, wg, wg, tg, tg, tg, wg, wg]
    return pl.pallas_call(
        body, name=f"attn_bwd_g{g}", grid=(N_PAIRS, nt + 1), in_specs=in_specs,
        out_specs=[pl.BlockSpec((tile, BLK), col_blk), pl.BlockSpec((tile, BLK), out_kv),
                   pl.BlockSpec((tile, BLK), out_kv)],
        out_shape=[_sds((s, 4 * D), BF16), _sds((s, 3 * D), BF16), _sds((s, 3 * D), BF16)],
        scratch_shapes=scratch, input_output_aliases=aliases, compiler_params=_params(48, 2),
    )(*args)


def _b_out_loss(gb, w, b, n1, pg0, pb0, pg1, pb1, tgt, tm):
    s = gb.shape[0]
    last = s // tm - 1

    def body(gb_ref, w_ref, b_ref, n1_ref, pg0_ref, pb0_ref, pg1_ref, pb1_ref, t_ref,
             dr_ref, drb_ref, loss_ref, dpg_ref, dpb_ref, dbo_ref):
        i = pl.program_id(0)
        _acc_init(i, loss_ref, dpg_ref, dpb_ref, dbo_ref)
        x1 = n1_ref[...] * pg0_ref[...] + pb0_ref[...]
        r = ALPHA * x1 + (_dot(gb_ref[...], w_ref[...]) + b_ref[...])
        n, rstd = _ln_stats(r)
        err = (n * pg1_ref[...] + pb1_ref[...]) - t_ref[...]
        loss_ref[...] += _rowsum8(err * err)
        dx2 = err * (1.0 / D)
        dpg_ref[...] += _rowsum8(dx2 * n)
        dpb_ref[...] += _rowsum8(dx2)
        dr = _ln_bwd(dx2 * pg1_ref[...], n, rstd)
        dr_ref[...] = dr
        drb_ref[...] = dr.astype(BF16)
        dbo_ref[...] += _rowsum8(dr)
        _acc_finish(i, last, dpg_ref, dpb_ref, dbo_ref)

        @pl.when(i == last)
        def _():
            loss_ref[...] = jnp.broadcast_to((0.5 / D) * jnp.sum(loss_ref[...], keepdims=True), loss_ref.shape)

    row = lambda i: (i, 0)
    vec = pl.BlockSpec((1, D), lambda i: (0, 0))
    acc = pl.BlockSpec((8, D), lambda i: (0, 0))
    return pl.pallas_call(
        body, name="b_out_loss", grid=(s // tm,),
        in_specs=[pl.BlockSpec((tm, D), row), pl.BlockSpec((D, D), lambda i: (0, 0)), vec,
                  pl.BlockSpec((tm, D), row), vec, vec, vec, vec, pl.BlockSpec((tm, D), row)],
        out_specs=[pl.BlockSpec((tm, D), row), pl.BlockSpec((tm, D), row), acc, acc, acc, acc],
        out_shape=[_sds((s, D), F32), _sds((s, D), BF16)] + [_sds((8, D), F32)] * 4,
        compiler_params=_params(36, 1),
    )(gb, w, b, n1, pg0, pb0, pg1, pb1, tgt)


def _head_selector():
    sel = (np.arange(D)[:, None] // HEAD_DIM == np.arange(BLK)[None, :]).astype(np.float32)
    return jnp.asarray(sel, dtype=BF16)


def _b_out_bwd(drb, w, z, o, tm):
    s = drb.shape[0]

    def body(dr_ref, w_ref, z_ref, o_ref, sel_ref, do_ref, dh_ref, dd_ref):
        dg = _dot_nt(dr_ref[...], w_ref[...])
        zz = z_ref[...]
        sg = _sigmoid(zz)
        do = dg * (zz * sg)
        ov = o_ref[...]
        do_ref[...] = do.astype(BF16)
        dh_ref[...] = (dg * ov * _dsilu(zz, sg)).astype(BF16)
        prod = do * ov
        hi = prod.astype(BF16)
        lo = (prod - hi.astype(F32)).astype(BF16)
        dd_ref[...] = _dot(hi, sel_ref[...]) + _dot(lo, sel_ref[...])

    row = lambda i: (i, 0)
    return pl.pallas_call(
        body, name="b_out_bwd", grid=(s // tm,),
        in_specs=[pl.BlockSpec((tm, D), row), pl.BlockSpec((D, D), lambda i: (0, 0)),
                  pl.BlockSpec((tm, D), row), pl.BlockSpec((tm, D), row), pl.BlockSpec((D, BLK), lambda i: (0, 0))],
        out_specs=[pl.BlockSpec((tm, D), row), pl.BlockSpec((tm, D), lambda i: (i, 3)),
                   pl.BlockSpec((tm, BLK), row)],
        out_shape=[_sds((s, D), BF16), _sds((s, 4 * D), BF16), _sds((s, BLK), F32)],
        compiler_params=_params(36, 1),
    )(drb, w, z, o, _head_selector())


def _attn_bwd(q, k_all, v_all, do, lse, dd, dhb, dk_all, dv_all, g):
    d = DILATIONS[g]
    s = q.shape[0]
    nq = ATTN_NQ[d]
    halo = BLK * d
    tile = nq * halo
    nt = s // tile
    first = dk_all is None

    def body(*refs):
        q_ref, k_ref, kh_ref, v_ref, vh_ref, nd_ref, sl_ref, do_ref, l_ref, dd_ref = refs[:10]
        k = 10 + (1 if first else 3)
        dq_ref, dk_ref, dv_ref = refs[k:k + 3]
        qf, dof, kf, vf, dqf, dkf, dvf, ck, cv = refs[k + 3:k + 12]
        hp = pl.program_id(0)
        n = pl.program_id(1)

        @pl.when(n == 0)
        def _():
            ck[...] = jnp.zeros(ck.shape, F32)
            cv[...] = jnp.zeros(cv.shape, F32)

        dkf[...] = jnp.zeros(dkf.shape, F32)
        dvf[...] = jnp.zeros(dvf.shape, F32)

        @pl.when(n < nt)
        def _():
            qf[...] = q_ref[...].astype(F32)
            dof[...] = do_ref[...].astype(F32)
            kf[0:halo, :] = kh_ref[...].astype(F32)
            kf[halo:, :] = k_ref[...].astype(F32)
            vf[0:halo, :] = vh_ref[...].astype(F32)
            vf[halo:, :] = v_ref[...].astype(F32)
            col = lax.broadcasted_iota(jnp.int32, (2 * BLK, 2 * BLK), 1)
            lane = lax.broadcasted_iota(jnp.int32, (BLK, BLK), 1)
            masks = _head_masks()
            bias = jnp.concatenate([sl_ref[0, e:e + 1, :] * nd_ref[...] for e in range(2)], axis=0)
            bias0 = bias + jnp.where((n == 0) & (col < BLK), NEG, 0.0)
            for b in range(nq):
                for r in range(d):
                    rq = _rows(b * halo + r, BLK, d)
                    rk = _rows(b * halo + r, 2 * BLK, d)
                    q2 = qf[rq, :].astype(BF16)
                    do2 = dof[rq, :].astype(BF16)
                    kcat = kf[rk, :].astype(BF16)
                    vcat = vf[rk, :].astype(BF16)
                    lt = l_ref[rq, :]
                    dt = dd_ref[rq, :]
                    qs = jnp.concatenate([q2 * masks[0], q2 * masks[1]], axis=0)
                    dos = jnp.concatenate([do2 * masks[0], do2 * masks[1]], axis=0)
                    lcol = jnp.concatenate([_pick_col(lt, lane, 2 * hp + e) for e in range(2)], axis=0)
                    dcol = jnp.concatenate([_pick_col(dt, lane, 2 * hp + e) for e in range(2)], axis=0)
                    sc = _dot_nt(qs, kcat) + (bias0 if b == 0 else bias)
                    p = jnp.exp(sc - lcol)
                    ds = p * (_dot_nt(dos, vcat) - dcol)
                    dsb = ds.astype(BF16)
                    dq = _dot(dsb, kcat)
                    dqf[rq, :] = (HEAD_DIM ** -0.5) * jnp.where(lane < HEAD_DIM, dq[0:BLK], dq[BLK:])
                    dkf[rk, :] += _dot(dsb.T, qs)
                    dvf[rk, :] += _dot(p.astype(BF16).T, dos)
            dq_ref[...] = dqf[...].astype(BF16)

        if tile > halo:
            dk_ref[0:tile - halo, :] = ck[0:tile - halo, :].astype(BF16)
            dv_ref[0:tile - halo, :] = cv[0:tile - halo, :].astype(BF16)
        dk_ref[tile - halo:, :] = (ck[tile - halo:, :] + dkf[0:halo, :]).astype(BF16)
        dv_ref[tile - halo:, :] = (cv[tile - halo:, :] + dvf[0:halo, :]).astype(BF16)
        ck[...] = dkf[halo:, :]
        cv[...] = dvf[halo:, :]

    cur = lambda n: jnp.minimum(n, nt - 1)
    col_blk = lambda hp, n: (cur(n), g * N_PAIRS + hp)
    halo_blk = lambda hp, n: (jnp.maximum(cur(n) * nq - 1, 0), g * N_PAIRS + hp)
    out_kv = lambda hp, n: (jnp.maximum(n - 1, 0), g * N_PAIRS + hp)
    small = pl.BlockSpec((tile, BLK), lambda hp, n: (cur(n), 0))
    hbm = pl.BlockSpec(memory_space=pl.ANY)
    in_specs = [pl.BlockSpec((tile, BLK), col_blk),
                pl.BlockSpec((tile, BLK), col_blk), pl.BlockSpec((halo, BLK), halo_blk),
                pl.BlockSpec((tile, BLK), col_blk), pl.BlockSpec((halo, BLK), halo_blk),
                pl.BlockSpec((BLK, 2 * BLK), lambda hp, n: (0, 0)),
                pl.BlockSpec((1, 8, 2 * BLK), lambda hp, n: (hp, 0, 0)),
                pl.BlockSpec((tile, BLK), lambda hp, n: (cur(n), hp)), small, small, hbm]
    args = [q, k_all, k_all, v_all, v_all, _band_table(d), _slope_table(), do, lse, dd, dhb]
    aliases = {10: 0}
    if not first:
        in_specs += [hbm, hbm]
        args += [dk_all, dv_all]
        aliases.update({11: 1, 12: 2})
    tile_f32 = pltpu.VMEM((tile, BLK), F32)
    wide_f32 = pltpu.VMEM((tile + halo, BLK), F32)
    return pl.pallas_call(
        body, name=f"attn_bwd_g{g}", grid=(N_PAIRS, nt + 1), in_specs=in_specs,
        out_specs=[pl.BlockSpec((tile, BLK), col_blk), pl.BlockSpec((tile, BLK), out_kv),
                   pl.BlockSpec((tile, BLK), out_kv)],
        out_shape=[_sds((s, 4 * D), BF16), _sds((s, 3 * D), BF16), _sds((s, 3 * D), BF16)],
        scratch_shapes=[tile_f32, tile_f32, wide_f32, wide_f32, tile_f32, wide_f32, wide_f32, tile_f32, tile_f32],
        input_output_aliases=aliases, compiler_params=_params(40, 2),
    )(*args)


def _b_in_bwd(dr2, dhb, dk_all, dv_all, wb_g, wkv_g, n1, rstd1, pg0, tm):
    s = dr2.shape[0]
    last = s // tm - 1
    nb_, nkv = wb_g.shape[2], wkv_g.shape[2]
    half = N_DEV // 2

    def body(dr2_ref, dh_ref, dk_ref, dv_ref, wb_hbm, wkv_hbm, n_ref, rstd_ref, pg_ref,
             dr_ref, drb_ref, dpg_ref, dpb_ref, dbo_ref, wb, wkv):
        i = pl.program_id(0)

        @pl.when(i == 0)
        def _():
            pltpu.sync_copy(wb_hbm, wb)
            pltpu.sync_copy(wkv_hbm, wkv)

        _acc_init(i, dpg_ref, dpb_ref, dbo_ref)
        acc = ALPHA * dr2_ref[...]
        for j in range(N_DEV):
            acc = acc + _dot_nt(dh_ref[:, nb_ * j:nb_ * (j + 1)], wb[j])
            src = dk_ref if j < half else dv_ref
            jj = j % half
            acc = acc + _dot_nt(src[:, nkv * jj:nkv * (jj + 1)], wkv[j])
        n = n_ref[...]
        dpg_ref[...] += _rowsum8(acc * n)
        dpb_ref[...] += _rowsum8(acc)
        dr = _ln_bwd(acc * pg_ref[...], n, rstd_ref[...])
        dr_ref[...] = dr
        drb_ref[...] = dr.astype(BF16)
        dbo_ref[...] += _rowsum8(dr)
        _acc_finish(i, last, dpg_ref, dpb_ref, dbo_ref)

    row = lambda i: (i, 0)
    hbm = pl.BlockSpec(memory_space=pl.ANY)
    acc_spec = pl.BlockSpec((8, D), lambda i: (0, 0))
    return pl.pallas_call(
        body, name="b_in_bwd", grid=(s // tm,),
        in_specs=[pl.BlockSpec((tm, D), row), pl.BlockSpec((tm, 4 * D), row), pl.BlockSpec((tm, 3 * D), row),
                  pl.BlockSpec((tm, 3 * D), row), hbm, hbm, pl.BlockSpec((tm, D), row), pl.BlockSpec((tm, 1), row),
                  pl.BlockSpec((1, D), lambda i: (0, 0))],
        out_specs=[pl.BlockSpec((tm, D), row), pl.BlockSpec((tm, D), row), acc_spec, acc_spec, acc_spec],
        out_shape=[_sds((s, D), F32), _sds((s, D), BF16)] + [_sds((8, D), F32)] * 3,
        scratch_shapes=[pltpu.VMEM(wb_g.shape, BF16), pltpu.VMEM(wkv_g.shape, BF16)],
        compiler_params=_params(56, 1),
    )(dr2, dhb, dk_all, dv_all, wb_g, wkv_g, n1, rstd1, pg0)


def _a_out_bwd(drb, w, u1, h, ln_g, ln_b, tm):
    s = drb.shape[0]
    last = s // tm - 1

    def body(dr_ref, w_ref, u1_ref, z_ref, g_ref, b_ref, du1_ref, dh_ref, dg_ref, db_ref, dbz_ref):
        i = pl.program_id(0)
        _acc_init(i, dg_ref, db_ref, dbz_ref)
        dga = _dot_nt(dr_ref[...], w_ref[...])
        n, rstd = _ln_stats(u1_ref[...])
        pre = n * g_ref[...] + b_ref[...]
        sp = _sigmoid(pre)
        zz = z_ref[...]
        sz = _sigmoid(zz)
        dz = dga * (pre * sp) * _dsilu(zz, sz)
        dh_ref[...] = dz.astype(BF16)
        dbz_ref[...] += _rowsum8(dz)
        dpre = dga * (zz * sz) * _dsilu(pre, sp)
        dg_ref[...] += _rowsum8(dpre * n)
        db_ref[...] += _rowsum8(dpre)
        du1_ref[...] = _ln_bwd(dpre * g_ref[...], n, rstd)
        _acc_finish(i, last, dg_ref, db_ref, dbz_ref)

    row = lambda i: (i, 0)
    vec = pl.BlockSpec((1, D), lambda i: (0, 0))
    acc_spec = pl.BlockSpec((8, D), lambda i: (0, 0))
    return pl.pallas_call(
        body, name="a_out_bwd", grid=(s // tm,),
        in_specs=[pl.BlockSpec((tm, D), row), pl.BlockSpec((D, D), lambda i: (0, 0)), pl.BlockSpec((tm, D), row),
                  pl.BlockSpec((tm, D), lambda i: (i, 2)), vec, vec],
        out_specs=[pl.BlockSpec((tm, D), row), pl.BlockSpec((tm, D), lambda i: (i, 2)),
                   acc_spec, acc_spec, acc_spec],
        out_shape=[_sds((s, D), F32), _sds((s, 3 * D), BF16)] + [_sds((8, D), F32)] * 3,
        compiler_params=_params(32, 1),
    )(drb, w, u1, h, ln_g, ln_b)


def _a_conv_bwd(du1, u0, h, dha, w_dw, tm):
    s = du1.shape[0]
    steps = s // tm
    per = tm // CONV_HALO
    pad = CONV_W - 1

    def body(du_ref, dun_ref, u0_ref, u0p_ref, h_ref, w_ref, dha_hbm,
             dh_ref, dw_ref, dbdw_ref, dba_ref, dbg_ref, dbuf, ubuf, wacc, dshs, ushs):
        i = pl.program_id(0)
        _acc_init(i, dbdw_ref, dba_ref, dbg_ref, wacc)
        dbuf[0:tm, :] = du_ref[...]
        dbuf[tm:, :] = jnp.where(i < steps - 1, dun_ref[...], 0.0)
        ubuf[0:CONV_HALO, :] = jnp.where(i > 0, u0p_ref[...], 0.0)
        ubuf[CONV_HALO:, :] = u0_ref[...]
        _shifted_copies(dbuf, dshs, tm)
        _shifted_copies(ubuf, ushs, tm)
        off = CONV_HALO - pad
        for ch in range(tm // CONV_CHUNK):
            r0 = ch * CONV_CHUNK
            duc = du_ref[r0:r0 + CONV_CHUNK, :]
            acc = jnp.zeros((CONV_CHUNK, D), F32)
            for j in range(CONV_W):
                acc = acc + w_ref[j:j + 1, :] * _shifted_rows(dbuf, dshs, r0 + pad - j)
                wacc[j] += _rowsum8(duc * _shifted_rows(ubuf, ushs, r0 + off + j))
            sg = _sigmoid(h_ref[r0:r0 + CONV_CHUNK, D:2 * D])
            da = acc * sg
            dag = acc * h_ref[r0:r0 + CONV_CHUNK, 0:D] * (sg * (1.0 - sg))
            dh_ref[r0:r0 + CONV_CHUNK, 0:D] = da.astype(BF16)
            dh_ref[r0:r0 + CONV_CHUNK, D:2 * D] = dag.astype(BF16)
            dbdw_ref[...] += _rowsum8(duc)
            dba_ref[...] += _rowsum8(da)
            dbg_ref[...] += _rowsum8(dag)
        _acc_finish(i, steps - 1, dbdw_ref, dba_ref, dbg_ref)

        @pl.when(i == steps - 1)
        def _():
            for j in range(CONV_W):
                dw_ref[j:j + 1, :] = jnp.sum(wacc[j], axis=0, keepdims=True)
            dw_ref[CONV_W:, :] = jnp.zeros((32 - CONV_W, D), F32)

    row = lambda i: (i, 0)
    acc_spec = pl.BlockSpec((8, D), lambda i: (0, 0))
    return pl.pallas_call(
        body, name="a_conv_bwd", grid=(steps,),
        in_specs=[pl.BlockSpec((tm, D), row),
                  pl.BlockSpec((CONV_HALO, D), lambda i: (jnp.minimum((i + 1) * per, s // CONV_HALO - 1), 0)),
                  pl.BlockSpec((tm, D), row),
                  pl.BlockSpec((CONV_HALO, D), lambda i: (jnp.maximum(i * per - 1, 0), 0)),
                  pl.BlockSpec((tm, 2 * D), lambda i: (i, 0)),
                  pl.BlockSpec((32, D), lambda i: (0, 0)), pl.BlockSpec(memory_space=pl.ANY)],
        out_specs=[pl.BlockSpec((tm, 2 * D), lambda i: (i, 0)),
                   pl.BlockSpec((32, D), lambda i: (0, 0)), acc_spec, acc_spec, acc_spec],
        out_shape=[_sds((s, 3 * D), BF16), _sds((32, D), F32)] + [_sds((8, D), F32)] * 3,
        scratch_shapes=[pltpu.VMEM((tm + CONV_HALO, D), F32), pltpu.VMEM((tm + CONV_HALO, D), F32),
                        pltpu.VMEM((CONV_W, 8, D), F32),
                        pltpu.VMEM((SUBLANES - 1, tm + CONV_HALO, D), F32),
                        pltpu.VMEM((SUBLANES - 1, tm + CONV_HALO, D), F32)],
        input_output_aliases={6: 0}, compiler_params=_params(56, 1),
    )(du1, du1, u0, u0, h, w_dw, dha)


def _a_in_bwd(dr1, dha, w_g, tm):
    s = dr1.shape[0]
    npd = w_g.shape[2]

    def body(dr_ref, dh_ref, w_ref, o_ref):
        acc = ALPHA * dr_ref[...]
        for j in range(N_DEV):
            acc = acc + _dot_nt(dh_ref[:, npd * j:npd * (j + 1)], w_ref[j])
        o_ref[...] = acc

    row = lambda i: (i, 0)
    return pl.pallas_call(
        body, name="a_in_bwd", grid=(s // tm,),
        in_specs=[pl.BlockSpec((tm, D), row), pl.BlockSpec((tm, 3 * D), row),
                  pl.BlockSpec(w_g.shape, lambda i: (0, 0, 0))],
        out_specs=pl.BlockSpec((tm, D), row), out_shape=_sds((s, D), F32),
        compiler_params=_params(36, 1),
    )(dr1, dha, w_g)


def _wgrad(name, a, b, npd, ts, total=None, at=0, into=None):
    s = a.shape[0]
    n_blk = b.shape[1] // npd
    total = n_blk if total is None else total
    assert at % n_blk == 0

    def body(*refs):
        a_ref, b_ref = refs[:2]
        o_ref = refs[-1]
        si = pl.program_id(0)

        @pl.when(si == 0)
        def _():
            o_ref[...] = jnp.zeros(o_ref.shape, F32)

        a_t = a_ref[...].T
        for j in range(n_blk):
            o_ref[j] += _dot(a_t, b_ref[:, npd * j:npd * (j + 1)])

    in_specs = [pl.BlockSpec((ts, D), lambda si: (si, 0)), pl.BlockSpec((ts, n_blk * npd), lambda si: (si, 0))]
    args = [a, b]
    aliases = {}
    if into is not None:
        in_specs.append(pl.BlockSpec(memory_space=pl.ANY))
        args.append(into)
        aliases = {2: 0}
    return pl.pallas_call(
        body, name=name, grid=(s // ts,), in_specs=in_specs,
        out_specs=pl.BlockSpec((n_blk, D, npd), lambda si: (at // n_blk, 0, 0)),
        out_shape=_sds((total, D, npd), F32), input_output_aliases=aliases,
        compiler_params=_params(56, 1),
    )(*args)


SMALL_ROWS = 40
GRAD_ROWS = 48


def kernel(x, a_w_in, a_b_in, a_w_dw, a_b_dw, a_ln_g, a_ln_b, a_w_out, a_b_out, kv_w, b_w_in, b_w_out, b_b_out, post_ln_g, post_ln_b, loss_target, m_a_w_in, m_a_b_in, m_a_w_dw, m_a_b_dw, m_a_ln_g, m_a_ln_b, m_a_w_out, m_a_b_out, m_kv_w, m_b_w_in, m_b_w_out, m_b_b_out, m_post_ln_g, m_post_ln_b, v_a_w_in, v_a_b_in, v_a_w_dw, v_a_b_dw, v_a_ln_g, v_a_ln_b, v_a_w_out, v_a_b_out, v_kv_w, v_b_w_in, v_b_w_out, v_b_b_out, v_post_ln_g, v_post_ln_b):
    s = x.shape[1]
    assert x.shape == (1, s, D) and s % (DILATIONS[-1] * BLK) == 0
    xs = x.reshape(s, D)
    tgt = loss_target.reshape(s, D)
    me = 4 * lax.axis_index("x") + 2 * lax.axis_index("y") + lax.axis_index("c")
    c_idx = lax.axis_index("c").astype(jnp.int32).reshape(1)

    def small_pack(b_in, w_dw, b_dw, ln_g, ln_b, b_out):
        rows = [b_in.reshape(3, BLK), w_dw.reshape(CONV_W, BLK), b_dw.reshape(1, BLK), ln_g.reshape(1, BLK),
                ln_b.reshape(1, BLK), b_out.reshape(1, BLK)]
        n = sum(r.shape[0] for r in rows)
        return jnp.concatenate(rows + [jnp.zeros((SMALL_ROWS - n, BLK), F32)], axis=0)

    shards = [a_w_in[0], a_w_out[0], kv_w, b_w_in[0], b_w_out[0], small_pack(a_b_in, a_w_dw, a_b_dw, a_ln_g, a_ln_b, a_b_out)]
    wa_in, wa_out, wkv, wb_in, wb_out, sm = _all_gather("gather_weights", shards, [BF16] * 5 + [F32])
    wa_out = wa_out.reshape(D, D)
    wb_out = wb_out.reshape(D, D)
    ba_in = sm[:, 0:3, :].reshape(1, 3 * D)
    w_dw = jnp.concatenate([sm[:, 3:3 + CONV_W, :].transpose(1, 0, 2).reshape(CONV_W, D), jnp.zeros((1, D), F32)], axis=0)
    b_dw, ln_g, ln_b, ba_out = (sm[:, 34 + k, :].reshape(1, D) for k in range(4))
    pg0, pg1 = post_ln_g[0:1], post_ln_g[1:2]
    pb0, pb1 = post_ln_b[0:1], post_ln_b[1:2]

    h_a, u0, xb = _a_in_proj(xs, wa_in, ba_in, 256)
    u1, g_a = _a_conv_gate(u0, h_a, w_dw, b_dw, ln_g, ln_b, 256)
    n1, rstd1, x1b = _a_out_proj(g_a, wa_out, ba_out, xs, pg0, pb0, 512)
    k_all, v_all = _kv_proj(x1b, wkv, 512)
    q, z_b = _b_in_proj(x1b, wb_in, 512)
    o, lse = _attn_forward(q, k_all, v_all, None, None, None, 0, True, False)
    o, lse = _attn_forward(q, k_all, v_all, o, lse, None, 1, False, False)
    o, lse, g_b = _attn_forward(q, k_all, v_all, o, lse, z_b, 2, False, True)
    dr2, dr2b, loss8, dpg1, dpb1, dbb_out = _b_out_loss(g_b, wb_out, b_b_out, n1, pg0, pb0, pg1, pb1, tgt, 512)

    do, dhb, dd = _b_out_bwd(dr2b, wb_out, z_b, o, 512)
    dk_all = dv_all = None
    for g in range(3):
        dhb, dk_all, dv_all = _attn_backward(q, k_all, v_all, do, lse, dd, dhb, dk_all, dv_all, g)
    dr1, dr1b, dpg0, dpb0, dba_out = _b_in_bwd(dr2, dhb, dk_all, dv_all, wb_in, wkv, n1, rstd1, pg0, 256)
    du1, dha, dln_g, dln_b, dbz = _a_out_bwd(dr1b, wa_out, u1, h_a, ln_g, ln_b, 512)
    dha, dw_dw, db_dw, dba, dbg = _a_conv_bwd(du1, u0, h_a, dha, w_dw, 256)
    grad_x = _a_in_bwd(dr1, dha, wa_in, 512)

    p_a_in = _wgrad("wgrad_a_in", xb, dha, 384, 512)
    p_kv = _wgrad("wgrad_k", x1b, dk_all, 768, 512, total=N_DEV)
    p_kv = _wgrad("wgrad_v", x1b, dv_all, 768, 512, total=N_DEV, at=N_DEV // 2, into=p_kv)
    p_b_in = _wgrad("wgrad_b_in", x1b, dhb, 512, 512)
    p_a_out = _wgrad("wgrad_a_out", g_a, dr1b, D, 512).reshape(N_DEV, BLK, D)
    p_b_out = _wgrad("wgrad_b_out", g_b, dr2b, D, 512).reshape(N_DEV, BLK, D)
    parts = [p_a_in, p_kv, p_b_in, p_a_out, p_b_out]

    from_sibling = _exchange_sibling("reduce_sibling", parts)
    chip_sums = [_pair_add(f"pair_add_{k}", p, r, c_idx) for k, (p, r) in enumerate(zip(parts, from_sibling))]
    from_chips = _exchange_chips("reduce_chips", chip_sums)

    big_w = [a_w_in[0], kv_w, b_w_in[0], a_w_out[0], b_w_out[0]]
    big_m = [m_a_w_in[0], m_kv_w, m_b_w_in[0], m_a_w_out[0], m_b_w_out[0]]
    big_v = [v_a_w_in[0], v_kv_w, v_b_w_in[0], v_a_w_out[0], v_b_w_out[0]]
    big = [_sum_adamw(f"adamw_{k}", from_chips[k], big_w[k], big_m[k], big_v[k]) for k in range(5)]

    rows = [dba[0:1], dbg[0:1], dbz[0:1], dw_dw[0:CONV_W], db_dw[0:1], dln_g[0:1], dln_b[0:1], dba_out[0:1],
            dbb_out[0:1], dpg0[0:1], dpg1[0:1], dpb0[0:1], dpb1[0:1], loss8[0:1]]
    n_rows = sum(r.shape[0] for r in rows)
    gpack = jnp.concatenate(rows + [jnp.zeros((GRAD_ROWS - n_rows, D), F32)], axis=0)
    (gall,) = _all_gather("gather_small_grads", [gpack], [F32])
    gs = _small_sum("small_sum", gall)
    loss = gs[43, 0]

    def my(vec, width):
        return lax.dynamic_slice_in_dim(vec, me * width, width, axis=-1)

    g_small = [my(gs[0:3].reshape(1, 3 * D), 384), my(gs[3:34], BLK)[None], my(gs[34:35], BLK), my(gs[35:36], BLK),
               my(gs[36:37], BLK), my(gs[37:38], BLK), gs[38:39], gs[39:41], gs[41:43]]
    w_small = [a_b_in, a_w_dw, a_b_dw, a_ln_g, a_ln_b, a_b_out, b_b_out, post_ln_g, post_ln_b]
    m_small = [m_a_b_in, m_a_w_dw, m_a_b_dw, m_a_ln_g, m_a_ln_b, m_a_b_out, m_b_b_out, m_post_ln_g, m_post_ln_b]
    v_small = [v_a_b_in, v_a_w_dw, v_a_b_dw, v_a_ln_g, v_a_ln_b, v_a_b_out, v_b_b_out, v_post_ln_g, v_post_ln_b]
    sizes = [math.prod(w.shape) for w in w_small]
    total = sum(sizes)
    padded = -(-total // (8 * BLK)) * (8 * BLK)

    def flat(parts_):
        return jnp.concatenate([p.reshape(-1) for p in parts_] + [jnp.ones((padded - total,), F32)]).reshape(-1, BLK)

    sd, sm_new, sv_new = _small_adamw("adamw_small", flat(w_small), flat(g_small), flat(m_small), flat(v_small))

    def unflat(packed):
        out, pos = [], 0
        vec = packed.reshape(-1)
        for w, n in zip(w_small, sizes):
            out.append(vec[pos:pos + n].reshape(w.shape))
            pos += n
        return out

    g_small = [g.reshape(w.shape) for g, w in zip(g_small, w_small)]
    d_small, nm_small, nv_small = unflat(sd), unflat(sm_new), unflat(sv_new)

    def ordered(bigs, smalls):
        a_in, kvw, b_in, a_out, b_out = bigs
        return [a_in[None], smalls[0], smalls[1], smalls[2], smalls[3], smalls[4], a_out[None], smalls[5],
                kvw, b_in[None], b_out[None], smalls[6], smalls[7], smalls[8]]

    grads = ordered([b[0] for b in big], g_small)
    deltas = ordered([b[1] for b in big], d_small)
    new_m = ordered([b[2] for b in big], nm_small)
    new_v = ordered([b[3] for b in big], nv_small)
    return (loss, grad_x.reshape(1, s, D), *grads, *deltas, *new_m, *new_v)
```

```python
import math

import numpy as np
import jax
import jax.numpy as jnp
from jax import lax
from jax.experimental import pallas as pl
from jax.experimental.pallas import tpu as pltpu

F32 = jnp.float32
BF16 = jnp.bfloat16
MESH = pl.DeviceIdType.MESH

D = 1024
N_DEV = 8
HEAD_DIM = 64
N_HEADS = 16
DILATIONS = (1, 4, 16)
BLK = 128
CONV_W = 31
ALPHA = (2.0 * 2) ** 0.25
LN_EPS = 1e-5
SLOPES = tuple(2.0 ** (-8.0 * (h + 1) / N_HEADS) for h in range(N_HEADS))
NEG = -1e30

ADAM_LR = 0.001
ADAM_B1 = 0.9
ADAM_B2 = 0.999
ADAM_EPS = 1e-08
ADAM_WD = 0.01
ADAM_STEP = 10

VMEM_CAP_MB = 64


def _params(vmem_mb, n_grid=0):
    sem = ("arbitrary",) * n_grid if n_grid else None
    return pltpu.CompilerParams(dimension_semantics=sem, vmem_limit_bytes=min(vmem_mb, VMEM_CAP_MB - 6) * 2 ** 20)


def _sds(shape, dtype):
    return jax.ShapeDtypeStruct(tuple(shape), dtype)


def _sigmoid(v):
    return jax.nn.sigmoid(v)


def _dsilu(v, s):
    return s * (1.0 + v * (1.0 - s))


def _ln_stats(r):
    mu = jnp.mean(r, axis=-1, keepdims=True)
    xc = r - mu
    var = jnp.mean(xc * xc, axis=-1, keepdims=True)
    rstd = lax.rsqrt(var + LN_EPS)
    return xc * rstd, rstd


def _ln_bwd(dn, n, rstd):
    m1 = jnp.mean(dn, axis=-1, keepdims=True)
    m2 = jnp.mean(dn * n, axis=-1, keepdims=True)
    return rstd * (dn - m1 - n * m2)


def _rowsum8(v):
    tm, c = v.shape
    return v.reshape(tm // 8, 8, c).sum(axis=0)


def _acc_init(i, *refs):
    @pl.when(i == 0)
    def _():
        for r in refs:
            r[...] = jnp.zeros(r.shape, r.dtype)


def _acc_finish(i, last, *refs):
    @pl.when(i == last)
    def _():
        for r in refs:
            r[...] = jnp.broadcast_to(jnp.sum(r[...], axis=0, keepdims=True), r.shape)


def _dot(a, b):
    return jnp.dot(a, b, preferred_element_type=F32)


def _dot_nt(a, b):
    return lax.dot_general(a, b, (((1,), (1,)), ((), ())), preferred_element_type=F32)


def _place():
    return lax.axis_index("x"), lax.axis_index("y"), lax.axis_index("c")


def _all_gather(name, arrays, dtypes, casts=()):
    n = len(arrays)
    nc = len(casts)

    def body(*refs):
        ins, cast_ins = refs[:n], refs[n:n + nc]
        outs, cast_outs = refs[n + nc:2 * n + nc], refs[2 * n + nc:2 * (n + nc)]
        stages = refs[2 * (n + nc):3 * n + 2 * nc]
        send_sems, recv_sems, local_sems = refs[3 * n + 2 * nc:]
        x, y, c = _place()
        me, sibling = (x, y, c), (x, y, 1 - c)
        chips = [(1 - x, y), (x, 1 - y), (1 - x, 1 - y)]

        def slot(ref, p):
            return ref.at[4 * p[0] + 2 * p[1] + p[2]]

        def copy(a, k, block, to, src=None):
            return pltpu.make_async_remote_copy(
                src_ref=slot(outs[a], block) if src is None else src, dst_ref=slot(outs[a], block),
                send_sem=send_sems.at[a, k], recv_sem=recv_sems.at[a, k], device_id=to, device_id_type=MESH)

        first, mine = [], []
        for a in range(n):
            stages[a][...] = ins[a][...].astype(stages[a].dtype)
            cp = pltpu.make_async_copy(stages[a], slot(outs[a], me), local_sems.at[a])
            cp.start()
            mine.append(cp)
            first.append(copy(a, 0, me, sibling, src=stages[a]))
            first += [copy(a, 1 + j, me, (*chip, c), src=stages[a]) for j, chip in enumerate(chips)]
        for cp in first:
            cp.start()
        for src, dst in zip(cast_ins, cast_outs):
            dst[...] = src[...].astype(BF16)
        passed = []
        for j, chip in enumerate(chips):
            for a in range(n):
                copy(a, 1 + j, (*chip, c), me).wait_recv()
                cp = copy(a, 4 + j, (*chip, c), sibling)
                cp.start()
                passed.append(cp)
        for a in range(n):
            copy(a, 0, sibling, me).wait_recv()
            for j, chip in enumerate(chips):
                copy(a, 4 + j, (*chip, 1 - c), me).wait_recv()
        for cp in first + passed:
            cp.wait_send()
        for cp in mine:
            cp.wait()

    vmem_bytes = sum(math.prod(a.shape) * (jnp.dtype(a.dtype).itemsize + jnp.dtype(dt).itemsize)
                     for a, dt in zip(arrays, dtypes)) + sum(math.prod(a.shape) * 6 for a in casts)
    vm = pl.BlockSpec(memory_space=pltpu.VMEM)
    return pl.pallas_call(
        body, name=name,
        out_shape=[_sds((N_DEV,) + a.shape, dt) for a, dt in zip(arrays, dtypes)] + [_sds(a.shape, BF16) for a in casts],
        in_specs=[vm] * (n + nc),
        out_specs=[pl.BlockSpec(memory_space=pl.ANY)] * n + [vm] * nc,
        scratch_shapes=[pltpu.VMEM(a.shape, dt) for a, dt in zip(arrays, dtypes)]
        + [pltpu.SemaphoreType.DMA((n, 7)), pltpu.SemaphoreType.DMA((n, 7)), pltpu.SemaphoreType.DMA((n,))],
        compiler_params=_params(vmem_bytes // 2 ** 20 + 8),
    )(*arrays, *casts)


class _Plan:
    def __init__(self, args, out_shape, scratch, start, mid, finish):
        self.args, self.out_shape, self.scratch = list(args), list(out_shape), list(scratch)
        self.start, self.mid, self.finish = start, mid, finish


def _gather_plan(shards):
    n = len(shards)

    def copies(ins, outs, sems):
        send_sems, recv_sems, local_sems = sems
        x, y, c = _place()
        me, sibling = (x, y, c), (x, y, 1 - c)
        chips = [(1 - x, y), (x, 1 - y), (1 - x, 1 - y)]

        def slot(ref, p):
            return ref.at[4 * p[0] + 2 * p[1] + p[2]]

        def copy(a, k, block, to, src=None):
            return pltpu.make_async_remote_copy(
                src_ref=slot(outs[a], block) if src is None else src, dst_ref=slot(outs[a], block),
                send_sem=send_sems.at[a, k], recv_sem=recv_sems.at[a, k], device_id=to, device_id_type=MESH)

        mine = [pltpu.make_async_copy(ins[a], slot(outs[a], me), local_sems.at[a]) for a in range(n)]
        first = [copy(a, 0, me, sibling, src=ins[a]) for a in range(n)]
        first += [copy(a, 1 + j, me, (*chip, c), src=ins[a]) for a in range(n) for j, chip in enumerate(chips)]
        arrive = [copy(a, 1 + j, (*chip, c), me) for j, chip in enumerate(chips) for a in range(n)]
        passed = [copy(a, 4 + j, (*chip, c), sibling) for j, chip in enumerate(chips) for a in range(n)]
        from_sibling = [copy(a, 0, sibling, me) for a in range(n)]
        from_sibling += [copy(a, 4 + j, (*chip, 1 - c), me) for a in range(n) for j, chip in enumerate(chips)]
        return mine, first, arrive, passed, from_sibling

    def start(ins, outs, sems):
        mine, first, _, _, _ = copies(ins, outs, sems)
        for cp in mine + first:
            cp.start()

    def mid(ins, outs, sems):
        _, _, arrive, passed, _ = copies(ins, outs, sems)
        for got, on in zip(arrive, passed):
            got.wait_recv()
            on.start()

    def finish(ins, outs, sems):
        mine, first, _, passed, from_sibling = copies(ins, outs, sems)
        for cp in from_sibling:
            cp.wait_recv()
        for cp in first + passed:
            cp.wait_send()
        for cp in mine:
            cp.wait()

    return _Plan(shards, [_sds((N_DEV,) + a.shape, a.dtype) for a in shards],
                 [pltpu.SemaphoreType.DMA((n, 7)), pltpu.SemaphoreType.DMA((n, 7)), pltpu.SemaphoreType.DMA((n,))],
                 start, mid, finish)


def _sibling_plan(parts):
    n = len(parts)

    def copies(ins, outs, sems):
        send_sems, recv_sems = sems
        x, y, c = _place()
        return [pltpu.make_async_remote_copy(
            src_ref=ins[a].at[2 * p + 1 - c], dst_ref=outs[a].at[p], send_sem=send_sems.at[a, p],
            recv_sem=recv_sems.at[a, p], device_id=(x, y, 1 - c), device_id_type=MESH)
            for a in range(n) for p in range(4)]

    def start(ins, outs, sems):
        for cp in copies(ins, outs, sems):
            cp.start()

    def finish(ins, outs, sems):
        cps = copies(ins, outs, sems)
        for cp in cps:
            cp.wait_recv()
        for cp in cps:
            cp.wait_send()

    return _Plan(parts, [_sds((4,) + p.shape[1:], p.dtype) for p in parts],
                 [pltpu.SemaphoreType.DMA((n, 4)), pltpu.SemaphoreType.DMA((n, 4))], start, None, finish)


def _chips_plan(sums):
    n = len(sums)

    def copies(ins, outs, sems):
        send_sems, recv_sems, local_sems = sems
        x, y, c = _place()
        my_chip = 2 * x + y
        chips = [(1 - x, y), (x, 1 - y), (1 - x, 1 - y)]
        mine = [pltpu.make_async_copy(ins[a].at[my_chip], outs[a].at[my_chip], local_sems.at[a]) for a in range(n)]
        remote = [pltpu.make_async_remote_copy(
            src_ref=ins[a].at[2 * px + py], dst_ref=outs[a].at[my_chip], send_sem=send_sems.at[a, k],
            recv_sem=recv_sems.at[a, k], device_id=(px, py, c), device_id_type=MESH)
            for a in range(n) for k, (px, py) in enumerate(chips)]
        return mine, remote

    def start(ins, outs, sems):
        mine, remote = copies(ins, outs, sems)
        for cp in mine + remote:
            cp.start()

    def finish(ins, outs, sems):
        mine, remote = copies(ins, outs, sems)
        for cp in remote:
            cp.wait_recv()
        for cp in remote:
            cp.wait_send()
        for cp in mine:
            cp.wait()

    return _Plan(sums, [_sds(s.shape, s.dtype) for s in sums],
                 [pltpu.SemaphoreType.DMA((n, 3)), pltpu.SemaphoreType.DMA((n, 3)), pltpu.SemaphoreType.DMA((n,))],
                 start, None, finish)


def _planned_call(plan, args, body, *, name, grid, in_specs, out_specs, out_shape, scratch_shapes=(), mid_step=None,
                  **kw):
    in_specs, out_specs, out_shape = list(in_specs), list(out_specs), list(out_shape)
    scratch_shapes = list(scratch_shapes)
    if plan is None:
        res = pl.pallas_call(body, name=name, grid=grid, in_specs=in_specs, out_specs=out_specs, out_shape=out_shape,
                             scratch_shapes=scratch_shapes, **kw)(*args)
        return list(res), []
    n_in, n_out, n_scr = len(in_specs), len(out_specs), len(scratch_shapes)
    p_in, p_out = len(plan.args), len(plan.out_shape)
    steps = grid[0]

    def fused(*refs):
        ins, pins = refs[:n_in], refs[n_in:n_in + p_in]
        o0 = n_in + p_in
        outs, pouts = refs[o0:o0 + n_out], refs[o0 + n_out:o0 + n_out + p_out]
        s0 = o0 + n_out + p_out
        scr, pscr = refs[s0:s0 + n_scr], refs[s0 + n_scr:]
        i = pl.program_id(0)

        @pl.when(i == 0)
        def _():
            plan.start(pins, pouts, pscr)

        body(*ins, *outs, *scr)
        if plan.mid is not None:
            @pl.when(i == mid_step)
            def _():
                plan.mid(pins, pouts, pscr)

        @pl.when(i == steps - 1)
        def _():
            plan.finish(pins, pouts, pscr)

    hbm = pl.BlockSpec(memory_space=pl.ANY)
    res = pl.pallas_call(
        fused, name=name, grid=grid, in_specs=in_specs + [hbm] * p_in, out_specs=out_specs + [hbm] * p_out,
        out_shape=out_shape + plan.out_shape, scratch_shapes=scratch_shapes + plan.scratch, **kw)(*args, *plan.args)
    return list(res[:n_out]), list(res[n_out:])


def _exchange_sibling(name, parts):
    n = len(parts)

    def body(*refs):
        ins, outs = refs[:n], refs[n:2 * n]
        send_sems, recv_sems = refs[2 * n:]
        x, y, c = _place()
        copies = []
        for a in range(n):
            for p in range(4):
                copies.append(pltpu.make_async_remote_copy(
                    src_ref=ins[a].at[2 * p + 1 - c], dst_ref=outs[a].at[p],
                    send_sem=send_sems.at[a, p], recv_sem=recv_sems.at[a, p],
                    device_id=(x, y, 1 - c), device_id_type=MESH))
        for cp in copies:
            cp.start()
        for cp in copies:
            cp.wait_recv()
        for cp in copies:
            cp.wait_send()

    return pl.pallas_call(
        body, name=name,
        out_shape=[_sds((4,) + p.shape[1:], p.dtype) for p in parts],
        in_specs=[pl.BlockSpec(memory_space=pl.ANY)] * n,
        out_specs=[pl.BlockSpec(memory_space=pl.ANY)] * n,
        scratch_shapes=[pltpu.SemaphoreType.DMA((n, 4)), pltpu.SemaphoreType.DMA((n, 4))],
    )(*parts)


def _exchange_chips(name, sums):
    n = len(sums)

    def body(*refs):
        ins, outs = refs[:n], refs[n:2 * n]
        send_sems, recv_sems, local_sems = refs[2 * n:]
        x, y, c = _place()
        my_chip = 2 * x + y
        chips = [(1 - x, y), (x, 1 - y), (1 - x, 1 - y)]
        copies, mine = [], []
        for a in range(n):
            cp = pltpu.make_async_copy(ins[a].at[my_chip], outs[a].at[my_chip], local_sems.at[a])
            cp.start()
            mine.append(cp)
            for k, (px, py) in enumerate(chips):
                copies.append(pltpu.make_async_remote_copy(
                    src_ref=ins[a].at[2 * px + py], dst_ref=outs[a].at[my_chip],
                    send_sem=send_sems.at[a, k], recv_sem=recv_sems.at[a, k],
                    device_id=(px, py, c), device_id_type=MESH))
        for cp in copies:
            cp.start()
        for cp in copies:
            cp.wait_recv()
        for cp in copies:
            cp.wait_send()
        for cp in mine:
            cp.wait()

    return pl.pallas_call(
        body, name=name,
        out_shape=[_sds(s.shape, s.dtype) for s in sums],
        in_specs=[pl.BlockSpec(memory_space=pl.ANY)] * n,
        out_specs=[pl.BlockSpec(memory_space=pl.ANY)] * n,
        scratch_shapes=[pltpu.SemaphoreType.DMA((n, 3)), pltpu.SemaphoreType.DMA((n, 3)),
                        pltpu.SemaphoreType.DMA((n,))],
    )(*sums)


def _pair_add(name, part, recv, c_idx):
    _, r, c = part.shape
    tr = min(r, 256)

    def body(c_ref, a_ref, b_ref, o_ref):
        o_ref[...] = a_ref[...] + b_ref[...]

    return pl.pallas_call(
        body, name=name,
        grid_spec=pltpu.PrefetchScalarGridSpec(
            num_scalar_prefetch=1, grid=(4, r // tr),
            in_specs=[pl.BlockSpec((1, tr, c), lambda p, i, cr: (2 * p + cr[0], i, 0)),
                      pl.BlockSpec((1, tr, c), lambda p, i, cr: (p, i, 0))],
            out_specs=pl.BlockSpec((1, tr, c), lambda p, i, cr: (p, i, 0))),
        out_shape=_sds((4, r, c), F32),
        compiler_params=_params(16, 2),
    )(c_idx, part, recv)


def _adamw_math(w, g, m, v):
    m = ADAM_B1 * m + (1.0 - ADAM_B1) * g
    v = ADAM_B2 * v + (1.0 - ADAM_B2) * (g * g)
    m_hat = m / (1.0 - ADAM_B1 ** ADAM_STEP)
    v_hat = v / (1.0 - ADAM_B2 ** ADAM_STEP)
    delta = -ADAM_LR * (m_hat / (jnp.sqrt(v_hat) + ADAM_EPS) + ADAM_WD * w)
    return delta, m, v


def _sum_adamw(name, recv, w, m, v):
    r, c = w.shape
    tr = min(r, 256)

    def body(p_ref, w_ref, m_ref, v_ref, g_ref, d_ref, nm_ref, nv_ref):
        g = (p_ref[0] + p_ref[1]) + (p_ref[2] + p_ref[3])
        g_ref[...] = g
        d_ref[...], nm_ref[...], nv_ref[...] = _adamw_math(w_ref[...], g, m_ref[...], v_ref[...])

    blk = pl.BlockSpec((tr, c), lambda i: (i, 0))
    return pl.pallas_call(
        body, name=name, grid=(r // tr,),
        in_specs=[pl.BlockSpec((4, tr, c), lambda i: (0, i, 0)), blk, blk, blk],
        out_specs=[blk] * 4, out_shape=[_sds((r, c), F32)] * 4,
        compiler_params=_params(24, 1),
    )(recv, w, m, v)


def _small_sum(name, gathered):
    _, r, c = gathered.shape

    def body(p_ref, o_ref):
        acc = p_ref[0]
        for j in range(1, N_DEV):
            acc = acc + p_ref[j]
        o_ref[...] = acc

    return pl.pallas_call(body, name=name, out_shape=_sds((r, c), F32),
                          in_specs=[pl.BlockSpec(memory_space=pltpu.VMEM)],
                          out_specs=pl.BlockSpec(memory_space=pltpu.VMEM))(gathered)


def _small_adamw(name, w, g, m, v):
    def body(w_ref, g_ref, m_ref, v_ref, d_ref, nm_ref, nv_ref):
        d_ref[...], nm_ref[...], nv_ref[...] = _adamw_math(w_ref[...], g_ref[...], m_ref[...], v_ref[...])

    vm = pl.BlockSpec(memory_space=pltpu.VMEM)
    return pl.pallas_call(body, name=name, out_shape=[_sds(w.shape, F32)] * 3,
                          in_specs=[vm] * 4, out_specs=[vm] * 3)(w, g, m, v)


def _a_in_proj(x, w_g, b_full, tm):
    s = x.shape[0]
    npd = w_g.shape[2]

    def body(x_ref, w_ref, b_ref, h_ref, u0_ref, xb_ref):
        xb = x_ref[...].astype(BF16)
        xb_ref[...] = xb
        for j in range(N_DEV):
            sl = slice(npd * j, npd * (j + 1))
            h_ref[:, sl] = _dot(xb, w_ref[j]) + b_ref[:, sl]
        u0_ref[...] = h_ref[:, 0:D] * _sigmoid(h_ref[:, D:2 * D])

    row = lambda i: (i, 0)
    return pl.pallas_call(
        body, name="a_in_proj", grid=(s // tm,),
        in_specs=[pl.BlockSpec((tm, D), row), pl.BlockSpec(w_g.shape, lambda i: (0, 0, 0)),
                  pl.BlockSpec((1, 3 * D), lambda i: (0, 0))],
        out_specs=[pl.BlockSpec((tm, 3 * D), row), pl.BlockSpec((tm, D), row), pl.BlockSpec((tm, D), row)],
        out_shape=[_sds((s, 3 * D), F32), _sds((s, D), F32), _sds((s, D), BF16)],
        compiler_params=_params(44, 1),
    )(x, w_g, b_full)


CONV_HALO = 32
CONV_CHUNK = 32
SUBLANES = 8
COPY_ROWS = 56


def _shifted_copies(buf, shs, tm):
    n = tm + CONV_HALO - SUBLANES
    for s in range(1, SUBLANES):
        for c0 in range(0, n, COPY_ROWS):
            c1 = min(c0 + COPY_ROWS, n)
            shs[s - 1, c0:c1, :] = buf[c0 + s:c1 + s, :]


def _shifted_rows(buf, shs, start):
    a8, s = divmod(start, SUBLANES)
    if s == 0:
        return buf[start:start + CONV_CHUNK, :]
    return shs[s - 1, a8 * SUBLANES:a8 * SUBLANES + CONV_CHUNK, :]


def _a_conv_gate(u0, h, w_dw, b_dw, ln_g, ln_b, tm, plan=None):
    s = u0.shape[0]
    per = tm // CONV_HALO

    def body(u0_ref, halo_ref, z_ref, w_ref, bdw_ref, g_ref, b_ref, u1_ref, ga_ref, buf, shs):
        i = pl.program_id(0)
        buf[0:CONV_HALO, :] = jnp.where(i > 0, halo_ref[...], 0.0)
        buf[CONV_HALO:, :] = u0_ref[...]
        _shifted_copies(buf, shs, tm)
        off = CONV_HALO - (CONV_W - 1)
        for ch in range(tm // CONV_CHUNK):
            r0 = ch * CONV_CHUNK
            acc = jnp.broadcast_to(bdw_ref[...], (CONV_CHUNK, D))
            for j in range(CONV_W):
                acc = acc + w_ref[j:j + 1, :] * _shifted_rows(buf, shs, r0 + off + j)
            u1_ref[r0:r0 + CONV_CHUNK, :] = acc
            n, _ = _ln_stats(acc)
            pre = n * g_ref[...] + b_ref[...]
            z = z_ref[r0:r0 + CONV_CHUNK, :]
            ga_ref[r0:r0 + CONV_CHUNK, :] = ((pre * _sigmoid(pre)) * (z * _sigmoid(z))).astype(BF16)

    row = lambda i: (i, 0)
    vec = pl.BlockSpec((1, D), lambda i: (0, 0))
    steps = s // tm
    return _planned_call(
        plan, (u0, u0, h, w_dw, b_dw, ln_g, ln_b), body, name="a_conv_gate", grid=(steps,), mid_step=steps // 2,
        in_specs=[pl.BlockSpec((tm, D), row),
                  pl.BlockSpec((CONV_HALO, D), lambda i: (jnp.maximum(i * per - 1, 0), 0)),
                  pl.BlockSpec((tm, D), lambda i: (i, 2)),
                  pl.BlockSpec((32, D), lambda i: (0, 0)), vec, vec, vec],
        out_specs=[pl.BlockSpec((tm, D), row), pl.BlockSpec((tm, D), row)],
        out_shape=[_sds((s, D), F32), _sds((s, D), BF16)],
        scratch_shapes=[pltpu.VMEM((tm + CONV_HALO, D), F32), pltpu.VMEM((SUBLANES - 1, tm + CONV_HALO, D), F32)],
        compiler_params=_params(40, 1))


def _a_out_proj(ga, w, b, x, pg, pb, tm):
    s = x.shape[0]

    def body(ga_ref, w_ref, b_ref, x_ref, pg_ref, pb_ref, n_ref, rstd_ref, xb_ref):
        r = ALPHA * x_ref[...] + (_dot(ga_ref[...], w_ref[...]) + b_ref[...])
        n, rstd = _ln_stats(r)
        n_ref[...] = n
        rstd_ref[...] = rstd
        xb_ref[...] = (n * pg_ref[...] + pb_ref[...]).astype(BF16)

    row = lambda i: (i, 0)
    vec = pl.BlockSpec((1, D), lambda i: (0, 0))
    return pl.pallas_call(
        body, name="a_out_proj", grid=(s // tm,),
        in_specs=[pl.BlockSpec((tm, D), row), pl.BlockSpec((D, D), lambda i: (0, 0)), vec,
                  pl.BlockSpec((tm, D), row), vec, vec],
        out_specs=[pl.BlockSpec((tm, D), row), pl.BlockSpec((tm, 1), row), pl.BlockSpec((tm, D), row)],
        out_shape=[_sds((s, D), F32), _sds((s, 1), F32), _sds((s, D), BF16)],
        compiler_params=_params(32, 1),
    )(ga, w, b, x, pg, pb)


def _kv_proj(xb, w_g, tm):
    s = xb.shape[0]
    npd = w_g.shape[2]
    half = N_DEV // 2

    def body(x_ref, w_ref, k_ref, v_ref):
        xv = x_ref[...]
        for j in range(N_DEV):
            o_ref = k_ref if j < half else v_ref
            jj = j % half
            o_ref[:, npd * jj:npd * (jj + 1)] = _dot(xv, w_ref[j]).astype(BF16)

    row = lambda i: (i, 0)
    return pl.pallas_call(
        body, name="kv_proj", grid=(s // tm,),
        in_specs=[pl.BlockSpec((tm, D), row), pl.BlockSpec(w_g.shape, lambda i: (0, 0, 0))],
        out_specs=[pl.BlockSpec((tm, 3 * D), row), pl.BlockSpec((tm, 3 * D), row)],
        out_shape=[_sds((s, 3 * D), BF16), _sds((s, 3 * D), BF16)],
        compiler_params=_params(52, 1),
    )(xb, w_g)


def _b_in_proj(xb, w_g, tm):
    s = xb.shape[0]
    npd = w_g.shape[2]
    scale = HEAD_DIM ** -0.5

    def body(x_ref, w_ref, q_ref, z_ref):
        xv = x_ref[...]
        for j in range(N_DEV):
            hj = _dot(xv, w_ref[j])
            if j < 6:
                q_ref[:, npd * j:npd * (j + 1)] = (hj.astype(BF16) * scale).astype(BF16)
            else:
                z_ref[:, npd * (j - 6):npd * (j - 5)] = hj

    row = lambda i: (i, 0)
    return pl.pallas_call(
        body, name="b_in_proj", grid=(s // tm,),
        in_specs=[pl.BlockSpec((tm, D), row), pl.BlockSpec(w_g.shape, lambda i: (0, 0, 0))],
        out_specs=[pl.BlockSpec((tm, 3 * D), row), pl.BlockSpec((tm, D), row)],
        out_shape=[_sds((s, 3 * D), BF16), _sds((s, D), F32)],
        compiler_params=_params(44, 1),
    )(xb, w_g)


ATTN_NQ = {1: 8, 4: 2, 16: 1}
N_PAIRS = N_HEADS // 2


def _band_table(d):
    qi = np.arange(BLK)[:, None]
    kj = np.arange(2 * BLK)[None, :]
    dist = qi + BLK - kj
    ok = (dist >= 0) & (dist <= BLK)
    return jnp.asarray(np.where(ok, -(d * dist).astype(np.float32), np.float32(NEG)), dtype=F32)


def _slope_table():
    t = np.zeros((N_PAIRS, 8, 2 * BLK), np.float32)
    for h in range(N_HEADS):
        t[h // 2, h % 2, :] = SLOPES[h]
    return jnp.asarray(t)


def _head_masks():
    lane = lax.broadcasted_iota(jnp.int32, (1, BLK), 1)
    lo = (lane < HEAD_DIM).astype(BF16)
    return (lo, (1.0 - lo).astype(BF16))


def _pick_col(tile, lane, h):
    return jnp.sum(jnp.where(lane == h, tile, 0.0), axis=1, keepdims=True)


def _rows(base, n, d):
    return pl.ds(base, n) if d == 1 else pl.ds(base, n, stride=d)


def _attn_fwd(q, k_all, v_all, o_acc, lse_acc, z, g, first, last):
    d = DILATIONS[g]
    s = q.shape[0]
    nq = ATTN_NQ[d]
    halo = BLK * d
    tile = nq * halo
    assert s % tile == 0

    def body(*refs):
        q_ref, k_ref, kh_ref, v_ref, vh_ref, nd_ref, sl_ref = refs[:7]
        k = 7
        if not first:
            oa_ref, la_ref = refs[k:k + 2]
            k += 2
        if last:
            z_ref = refs[k]
            k += 1
        o_ref, l_ref = refs[k:k + 2]
        k += 2
        if last:
            gb_ref = refs[k]
            k += 1
        qf, kf, vf = refs[k:k + 3]
        if last:
            gf = refs[k + 3]
        n = pl.program_id(0)
        hp = pl.program_id(1)
        qf[...] = q_ref[...].astype(F32)
        kf[0:halo, :] = kh_ref[...].astype(F32)
        kf[halo:, :] = k_ref[...].astype(F32)
        vf[0:halo, :] = vh_ref[...].astype(F32)
        vf[halo:, :] = v_ref[...].astype(F32)

        @pl.when(hp == 0)
        def _():
            l_ref[...] = jnp.zeros(l_ref.shape, F32) if first else la_ref[...]

        col = lax.broadcasted_iota(jnp.int32, (2 * BLK, 2 * BLK), 1)
        lane = lax.broadcasted_iota(jnp.int32, (BLK, BLK), 1)
        masks = _head_masks()
        bias = jnp.concatenate([sl_ref[0, e:e + 1, :] * nd_ref[...] for e in range(2)], axis=0)
        bias0 = bias + jnp.where((n == 0) & (col < BLK), NEG, 0.0)
        for b in range(nq):
            for r in range(d):
                rq = _rows(b * halo + r, BLK, d)
                rk = _rows(b * halo + r, 2 * BLK, d)
                q2 = qf[rq, :].astype(BF16)
                kcat = kf[rk, :].astype(BF16)
                vcat = vf[rk, :].astype(BF16)
                lt = l_ref[rq, :]
                sc = _dot_nt(jnp.concatenate([q2 * masks[0], q2 * masks[1]], axis=0), kcat)
                sc = sc + (bias0 if b == 0 else bias)
                m = jnp.max(sc, axis=1, keepdims=True)
                p = jnp.exp(sc - m)
                l = jnp.sum(p, axis=1, keepdims=True)
                oh = _dot(p.astype(BF16), vcat) / l
                lse = m + jnp.log(l)
                if not first:
                    old = jnp.concatenate([_pick_col(lt, lane, 2 * hp + e) for e in range(2)], axis=0)
                    mx = jnp.maximum(old, lse)
                    new = mx + jnp.log(jnp.exp(old - mx) + jnp.exp(lse - mx))
                    keep = jnp.exp(old - new)
                    oh = oh * jnp.exp(lse - new)
                    lse = new
                o2 = jnp.where(lane < HEAD_DIM, oh[0:BLK], oh[BLK:])
                if not first:
                    o2 = o2 + oa_ref[rq, :] * jnp.where(lane < HEAD_DIM, keep[0:BLK], keep[BLK:])
                lt = jnp.where(lane == 2 * hp, lse[0:BLK], lt)
                lt = jnp.where(lane == 2 * hp + 1, lse[BLK:], lt)
                o_ref[rq, :] = o2
                l_ref[rq, :] = lt
                if last:
                    zz = z_ref[rq, :]
                    gf[rq, :] = o2 * (zz * _sigmoid(zz))
        if last:
            gb_ref[...] = gf[...].astype(BF16)

    col_blk = lambda n, hp: (n, g * N_PAIRS + hp)
    halo_blk = lambda n, hp: (jnp.maximum(n * nq - 1, 0), g * N_PAIRS + hp)
    own = pl.BlockSpec((tile, BLK), lambda n, hp: (n, hp))
    own_l = pl.BlockSpec((tile, BLK), lambda n, hp: (n, 0))
    in_specs = [pl.BlockSpec((tile, BLK), col_blk),
                pl.BlockSpec((tile, BLK), col_blk), pl.BlockSpec((halo, BLK), halo_blk),
                pl.BlockSpec((tile, BLK), col_blk), pl.BlockSpec((halo, BLK), halo_blk),
                pl.BlockSpec((BLK, 2 * BLK), lambda n, hp: (0, 0)),
                pl.BlockSpec((1, 8, 2 * BLK), lambda n, hp: (hp, 0, 0))]
    args = [q, k_all, k_all, v_all, v_all, _band_table(d), _slope_table()]
    if not first:
        in_specs += [own, own_l]
        args += [o_acc, lse_acc]
    if last:
        in_specs += [own]
        args += [z]
    out_specs = [own, own_l] + ([own] if last else [])
    out_shape = [_sds((s, D), F32), _sds((s, BLK), F32)] + ([_sds((s, D), BF16)] if last else [])
    scratch = [pltpu.VMEM((tile, BLK), F32), pltpu.VMEM((tile + halo, BLK), F32), pltpu.VMEM((tile + halo, BLK), F32)]
    if last:
        scratch.append(pltpu.VMEM((tile, BLK), F32))
    return pl.pallas_call(
        body, name=f"attn_fwd_g{g}", grid=(s // tile, N_PAIRS), in_specs=in_specs, out_specs=out_specs,
        out_shape=out_shape, scratch_shapes=scratch, compiler_params=_params(32, 2),
    )(*args)


ATTN_PRE = {1: 1, 4: 1, 16: 4}


def _regroup(src, dst, d1):
    n = src.shape[0] // d1
    for r1 in range(d1):
        dst[r1] = src[pl.ds(r1, n, stride=d1), :]


def _ungroup(src, dst, d1):
    n = dst.shape[0] // d1
    for r1 in range(d1):
        dst[pl.ds(r1, n, stride=d1), :] = src[r1]


def _grouped(shape, d1):
    return pltpu.VMEM((d1, shape[0] // d1, shape[1]), F32)


def _attn_forward(q, k_all, v_all, o_acc, lse_acc, z, g, first, last):
    d = DILATIONS[g]
    s = q.shape[0]
    nq = ATTN_NQ[d]
    d1 = ATTN_PRE[d]
    d2 = d // d1
    halo = BLK * d
    tile = nq * halo
    assert s % tile == 0
    pre = d1 > 1

    def body(*refs):
        refs = list(refs)
        q_ref, k_ref, kh_ref, v_ref, vh_ref, nd_ref, sl_ref = refs[:7]
        del refs[:7]
        oa_ref, la_ref = (refs.pop(0), refs.pop(0)) if not first else (None, None)
        z_ref = refs.pop(0) if last else None
        o_ref, l_ref = refs.pop(0), refs.pop(0)
        gb_ref = refs.pop(0) if last else None
        qf, kf, vf = refs.pop(0), refs.pop(0), refs.pop(0)
        gf = refs.pop(0) if last else None
        q1 = k1 = v1 = l1 = o1 = oa1 = z1 = g1 = None
        if pre:
            q1, k1, v1, l1, o1 = (refs.pop(0) for _ in range(5))
            oa1 = refs.pop(0) if not first else None
            z1, g1 = (refs.pop(0), refs.pop(0)) if last else (None, None)
        n = pl.program_id(0)
        hp = pl.program_id(1)
        qf[...] = q_ref[...].astype(F32)
        kf[0:halo, :] = kh_ref[...].astype(F32)
        kf[halo:, :] = k_ref[...].astype(F32)
        vf[0:halo, :] = vh_ref[...].astype(F32)
        vf[halo:, :] = v_ref[...].astype(F32)

        @pl.when(hp == 0)
        def _():
            l_ref[...] = jnp.zeros(l_ref.shape, F32) if first else la_ref[...]

        if pre:
            for src, dst in ((qf, q1), (kf, k1), (vf, v1), (l_ref, l1), (oa_ref, oa1), (z_ref, z1)):
                if src is not None:
                    _regroup(src, dst, d1)

        def pick(nat, grp, r1):
            return grp.at[r1] if pre else nat

        def halves(colv):
            return jnp.where(lane < HEAD_DIM, colv[0:BLK], colv[BLK:])

        def put(tile_v, colv):
            tile_v = jnp.where(lane == 2 * hp, colv[0:BLK], tile_v)
            return jnp.where(lane == 2 * hp + 1, colv[BLK:], tile_v)

        col = lax.broadcasted_iota(jnp.int32, (2 * BLK, 2 * BLK), 1)
        lane = lax.broadcasted_iota(jnp.int32, (BLK, BLK), 1)
        masks = _head_masks()
        bias = jnp.concatenate([sl_ref[0, e:e + 1, :] * nd_ref[...] for e in range(2)], axis=0)
        bias0 = bias + jnp.where((n == 0) & (col < BLK), NEG, 0.0)
        for b in range(nq):
            for r in range(d):
                r1, r2 = r % d1, r // d1
                rq = _rows(b * (halo // d1) + r2, BLK, d2)
                rk = _rows(b * (halo // d1) + r2, 2 * BLK, d2)
                q2 = pick(qf, q1, r1)[rq, :].astype(BF16)
                kcat = pick(kf, k1, r1)[rk, :].astype(BF16)
                vcat = pick(vf, v1, r1)[rk, :].astype(BF16)
                sc = _dot_nt(jnp.concatenate([q2 * masks[0], q2 * masks[1]], axis=0), kcat)
                sc = sc + (bias0 if b == 0 else bias)
                m = jnp.max(sc, axis=1, keepdims=True)
                p = jnp.exp(sc - m)
                l = jnp.sum(p, axis=1, keepdims=True)
                oh = _dot(p.astype(BF16), vcat) / l
                lse = m + jnp.log(l)
                lt = pick(l_ref, l1, r1)[rq, :]
                if not first:
                    old = jnp.concatenate([_pick_col(lt, lane, 2 * hp + e) for e in range(2)], axis=0)
                    mx = jnp.maximum(old, lse)
                    new = mx + jnp.log(jnp.exp(old - mx) + jnp.exp(lse - mx))
                    o2 = halves(oh) * halves(jnp.exp(lse - new)) + pick(oa_ref, oa1, r1)[rq, :] * halves(jnp.exp(old - new))
                    lse = new
                else:
                    o2 = halves(oh)
                pick(o_ref, o1, r1)[rq, :] = o2
                pick(l_ref, l1, r1)[rq, :] = put(lt, lse)
                if last:
                    zz = pick(z_ref, z1, r1)[rq, :]
                    pick(gf, g1, r1)[rq, :] = o2 * (zz * _sigmoid(zz))
        if pre:
            _ungroup(o1, o_ref, d1)
            _ungroup(l1, l_ref, d1)
            if last:
                _ungroup(g1, gf, d1)
        if last:
            gb_ref[...] = gf[...].astype(BF16)

    col_blk = lambda n, hp: (n, g * N_PAIRS + hp)
    halo_blk = lambda n, hp: (jnp.maximum(n * nq - 1, 0), g * N_PAIRS + hp)
    own = pl.BlockSpec((tile, BLK), lambda n, hp: (n, hp))
    own_l = pl.BlockSpec((tile, BLK), lambda n, hp: (n, 0))
    in_specs = [pl.BlockSpec((tile, BLK), col_blk),
                pl.BlockSpec((tile, BLK), col_blk), pl.BlockSpec((halo, BLK), halo_blk),
                pl.BlockSpec((tile, BLK), col_blk), pl.BlockSpec((halo, BLK), halo_blk),
                pl.BlockSpec((BLK, 2 * BLK), lambda n, hp: (0, 0)),
                pl.BlockSpec((1, 8, 2 * BLK), lambda n, hp: (hp, 0, 0))]
    args = [q, k_all, k_all, v_all, v_all, _band_table(d), _slope_table()]
    if not first:
        in_specs += [own, own_l]
        args += [o_acc, lse_acc]
    if last:
        in_specs += [own]
        args += [z]
    out_specs = [own, own_l] + ([own] if last else [])
    out_shape = [_sds((s, D), F32), _sds((s, BLK), F32)] + ([_sds((s, D), BF16)] if last else [])
    t_shape, w_shape = (tile, BLK), (tile + halo, BLK)
    scratch = [pltpu.VMEM(t_shape, F32), pltpu.VMEM(w_shape, F32), pltpu.VMEM(w_shape, F32)]
    if last:
        scratch.append(pltpu.VMEM(t_shape, F32))
    if pre:
        scratch += [_grouped(t_shape, d1), _grouped(w_shape, d1), _grouped(w_shape, d1)]
        scratch += [_grouped(t_shape, d1)] * (2 + (0 if first else 1) + (2 if last else 0))
    return pl.pallas_call(
        body, name=f"attn_fwd_g{g}", grid=(s // tile, N_PAIRS), in_specs=in_specs, out_specs=out_specs,
        out_shape=out_shape, scratch_shapes=scratch, compiler_params=_params(56, 2),
    )(*args)


def _attn_backward(q, k_all, v_all, do, lse, dd, dhb, dk_all, dv_all, g):
    d = DILATIONS[g]
    s = q.shape[0]
    nq = ATTN_NQ[d]
    d1 = ATTN_PRE[d]
    d2 = d // d1
    halo = BLK * d
    tile = nq * halo
    nt = s // tile
    first = dk_all is None
    pre = d1 > 1

    def body(*refs):
        refs = list(refs)
        q_ref, k_ref, kh_ref, v_ref, vh_ref, nd_ref, sl_ref, do_ref, l_ref, dd_ref = refs[:10]
        del refs[:10 + (1 if first else 3)]
        dq_ref, dk_ref, dv_ref = refs[:3]
        qf, dof, kf, vf, dqf, dkf, dvf, ck, cv = refs[3:12]
        del refs[:12]
        if pre:
            q1, do1, k1, v1, l1, dd1, dq1, dk1, dv1 = refs
        else:
            q1 = do1 = k1 = v1 = l1 = dd1 = dq1 = None
            dk1, dv1 = dkf, dvf
        hp = pl.program_id(0)
        n = pl.program_id(1)

        @pl.when(n == 0)
        def _():
            ck[...] = jnp.zeros(ck.shape, F32)
            cv[...] = jnp.zeros(cv.shape, F32)

        dk1[...] = jnp.zeros(dk1.shape, F32)
        dv1[...] = jnp.zeros(dv1.shape, F32)

        def pick(nat, grp, r1):
            return grp.at[r1] if pre else nat

        @pl.when(n < nt)
        def _():
            qf[...] = q_ref[...].astype(F32)
            dof[...] = do_ref[...].astype(F32)
            kf[0:halo, :] = kh_ref[...].astype(F32)
            kf[halo:, :] = k_ref[...].astype(F32)
            vf[0:halo, :] = vh_ref[...].astype(F32)
            vf[halo:, :] = v_ref[...].astype(F32)
            if pre:
                for src, dst in ((qf, q1), (dof, do1), (kf, k1), (vf, v1), (l_ref, l1), (dd_ref, dd1)):
                    _regroup(src, dst, d1)
            col = lax.broadcasted_iota(jnp.int32, (2 * BLK, 2 * BLK), 1)
            lane = lax.broadcasted_iota(jnp.int32, (BLK, BLK), 1)
            masks = _head_masks()
            bias = jnp.concatenate([sl_ref[0, e:e + 1, :] * nd_ref[...] for e in range(2)], axis=0)
            bias0 = bias + jnp.where((n == 0) & (col < BLK), NEG, 0.0)
            for b in range(nq):
                for r in range(d):
                    r1, r2 = r % d1, r // d1
                    rq = _rows(b * (halo // d1) + r2, BLK, d2)
                    rk = _rows(b * (halo // d1) + r2, 2 * BLK, d2)
                    q2 = pick(qf, q1, r1)[rq, :].astype(BF16)
                    do2 = pick(dof, do1, r1)[rq, :].astype(BF16)
                    kcat = pick(kf, k1, r1)[rk, :].astype(BF16)
                    vcat = pick(vf, v1, r1)[rk, :].astype(BF16)
                    lt = pick(l_ref, l1, r1)[rq, :]
                    dt = pick(dd_ref, dd1, r1)[rq, :]
                    qs = jnp.concatenate([q2 * masks[0], q2 * masks[1]], axis=0)
                    dos = jnp.concatenate([do2 * masks[0], do2 * masks[1]], axis=0)
                    lcol = jnp.concatenate([_pick_col(lt, lane, 2 * hp + e) for e in range(2)], axis=0)
                    dcol = jnp.concatenate([_pick_col(dt, lane, 2 * hp + e) for e in range(2)], axis=0)
                    sc = _dot_nt(qs, kcat) + (bias0 if b == 0 else bias)
                    p = jnp.exp(sc - lcol)
                    ds = p * (_dot_nt(dos, vcat) - dcol)
                    dsb = ds.astype(BF16)
                    dq = _dot(dsb, kcat)
                    dq2 = (HEAD_DIM ** -0.5) * jnp.where(lane < HEAD_DIM, dq[0:BLK], dq[BLK:])
                    pick(dqf, dq1, r1)[rq, :] = dq2
                    pick(dkf, dk1, r1)[rk, :] += _dot(dsb.T, qs)
                    pick(dvf, dv1, r1)[rk, :] += _dot(p.astype(BF16).T, dos)
            if pre:
                _ungroup(dq1, dqf, d1)
            dq_ref[...] = dqf[...].astype(BF16)

        if pre:
            _ungroup(dk1, dkf, d1)
            _ungroup(dv1, dvf, d1)
        if tile > halo:
            dk_ref[0:tile - halo, :] = ck[0:tile - halo, :].astype(BF16)
            dv_ref[0:tile - halo, :] = cv[0:tile - halo, :].astype(BF16)
        dk_ref[tile - halo:, :] = (ck[tile - halo:, :] + dkf[0:halo, :]).astype(BF16)
        dv_ref[tile - halo:, :] = (cv[tile - halo:, :] + dvf[0:halo, :]).astype(BF16)
        ck[...] = dkf[halo:, :]
        cv[...] = dvf[halo:, :]

    cur = lambda n: jnp.minimum(n, nt - 1)
    col_blk = lambda hp, n: (cur(n), g * N_PAIRS + hp)
    halo_blk = lambda hp, n: (jnp.maximum(cur(n) * nq - 1, 0), g * N_PAIRS + hp)
    out_kv = lambda hp, n: (jnp.maximum(n - 1, 0), g * N_PAIRS + hp)
    small = pl.BlockSpec((tile, BLK), lambda hp, n: (cur(n), 0))
    hbm = pl.BlockSpec(memory_space=pl.ANY)
    in_specs = [pl.BlockSpec((tile, BLK), col_blk),
                pl.BlockSpec((tile, BLK), col_blk), pl.BlockSpec((halo, BLK), halo_blk),
                pl.BlockSpec((tile, BLK), col_blk), pl.BlockSpec((halo, BLK), halo_blk),
                pl.BlockSpec((BLK, 2 * BLK), lambda hp, n: (0, 0)),
                pl.BlockSpec((1, 8, 2 * BLK), lambda hp, n: (hp, 0, 0)),
                pl.BlockSpec((tile, BLK), lambda hp, n: (cur(n), hp)), small, small, hbm]
    args = [q, k_all, k_all, v_all, v_all, _band_table(d), _slope_table(), do, lse, dd, dhb]
    aliases = {10: 0}
    if not first:
        in_specs += [hbm, hbm]
        args += [dk_all, dv_all]
        aliases.update({11: 1, 12: 2})
    t_shape, w_shape = (tile, BLK), (tile + halo, BLK)
    tile_f32, wide_f32 = pltpu.VMEM(t_shape, F32), pltpu.VMEM(w_shape, F32)
    scratch = [tile_f32, tile_f32, wide_f32, wide_f32, tile_f32, wide_f32, wide_f32, tile_f32, tile_f32]
    if pre:
        tg, wg = _grouped(t_shape, d1), _grouped(w_shape, d1)
        scratch += [tg, tg, wg, wg, tg, tg, tg, wg, wg]
    return pl.pallas_call(
        body, name=f"attn_bwd_g{g}", grid=(N_PAIRS, nt + 1), in_specs=in_specs,
        out_specs=[pl.BlockSpec((tile, BLK), col_blk), pl.BlockSpec((tile, BLK), out_kv),
                   pl.BlockSpec((tile, BLK), out_kv)],
        out_shape=[_sds((s, 4 * D), BF16), _sds((s, 3 * D), BF16), _sds((s, 3 * D), BF16)],
        scratch_shapes=scratch, input_output_aliases=aliases, compiler_params=_params(48, 2),
    )(*args)


def _b_out_loss(gb, w, b, n1, pg0, pb0, pg1, pb1, tgt, tm):
    s = gb.shape[0]
    last = s // tm - 1

    def body(gb_ref, w_ref, b_ref, n1_ref, pg0_ref, pb0_ref, pg1_ref, pb1_ref, t_ref,
             dr_ref, drb_ref, loss_ref, dpg_ref, dpb_ref, dbo_ref):
        i = pl.program_id(0)
        _acc_init(i, loss_ref, dpg_ref, dpb_ref, dbo_ref)
        x1 = n1_ref[...] * pg0_ref[...] + pb0_ref[...]
        r = ALPHA * x1 + (_dot(gb_ref[...], w_ref[...]) + b_ref[...])
        n, rstd = _ln_stats(r)
        err = (n * pg1_ref[...] + pb1_ref[...]) - t_ref[...]
        loss_ref[...] += _rowsum8(err * err)
        dx2 = err * (1.0 / D)
        dpg_ref[...] += _rowsum8(dx2 * n)
        dpb_ref[...] += _rowsum8(dx2)
        dr = _ln_bwd(dx2 * pg1_ref[...], n, rstd)
        dr_ref[...] = dr
        drb_ref[...] = dr.astype(BF16)
        dbo_ref[...] += _rowsum8(dr)
        _acc_finish(i, last, dpg_ref, dpb_ref, dbo_ref)

        @pl.when(i == last)
        def _():
            loss_ref[...] = jnp.broadcast_to((0.5 / D) * jnp.sum(loss_ref[...], keepdims=True), loss_ref.shape)

    row = lambda i: (i, 0)
    vec = pl.BlockSpec((1, D), lambda i: (0, 0))
    acc = pl.BlockSpec((8, D), lambda i: (0, 0))
    return pl.pallas_call(
        body, name="b_out_loss", grid=(s // tm,),
        in_specs=[pl.BlockSpec((tm, D), row), pl.BlockSpec((D, D), lambda i: (0, 0)), vec,
                  pl.BlockSpec((tm, D), row), vec, vec, vec, vec, pl.BlockSpec((tm, D), row)],
        out_specs=[pl.BlockSpec((tm, D), row), pl.BlockSpec((tm, D), row), acc, acc, acc, acc],
        out_shape=[_sds((s, D), F32), _sds((s, D), BF16)] + [_sds((8, D), F32)] * 4,
        compiler_params=_params(36, 1),
    )(gb, w, b, n1, pg0, pb0, pg1, pb1, tgt)


def _head_selector():
    sel = (np.arange(D)[:, None] // HEAD_DIM == np.arange(BLK)[None, :]).astype(np.float32)
    return jnp.asarray(sel, dtype=BF16)


def _b_out_bwd(drb, w, z, o, tm):
    s = drb.shape[0]

    def body(dr_ref, w_ref, z_ref, o_ref, sel_ref, do_ref, dh_ref, dd_ref):
        dg = _dot_nt(dr_ref[...], w_ref[...])
        zz = z_ref[...]
        sg = _sigmoid(zz)
        do = dg * (zz * sg)
        ov = o_ref[...]
        do_ref[...] = do.astype(BF16)
        dh_ref[...] = (dg * ov * _dsilu(zz, sg)).astype(BF16)
        prod = do * ov
        hi = prod.astype(BF16)
        lo = (prod - hi.astype(F32)).astype(BF16)
        dd_ref[...] = _dot(hi, sel_ref[...]) + _dot(lo, sel_ref[...])

    row = lambda i: (i, 0)
    return pl.pallas_call(
        body, name="b_out_bwd", grid=(s // tm,),
        in_specs=[pl.BlockSpec((tm, D), row), pl.BlockSpec((D, D), lambda i: (0, 0)),
                  pl.BlockSpec((tm, D), row), pl.BlockSpec((tm, D), row), pl.BlockSpec((D, BLK), lambda i: (0, 0))],
        out_specs=[pl.BlockSpec((tm, D), row), pl.BlockSpec((tm, D), lambda i: (i, 3)),
                   pl.BlockSpec((tm, BLK), row)],
        out_shape=[_sds((s, D), BF16), _sds((s, 4 * D), BF16), _sds((s, BLK), F32)],
        compiler_params=_params(36, 1),
    )(drb, w, z, o, _head_selector())


def _attn_bwd(q, k_all, v_all, do, lse, dd, dhb, dk_all, dv_all, g):
    d = DILATIONS[g]
    s = q.shape[0]
    nq = ATTN_NQ[d]
    halo = BLK * d
    tile = nq * halo
    nt = s // tile
    first = dk_all is None

    def body(*refs):
        q_ref, k_ref, kh_ref, v_ref, vh_ref, nd_ref, sl_ref, do_ref, l_ref, dd_ref = refs[:10]
        k = 10 + (1 if first else 3)
        dq_ref, dk_ref, dv_ref = refs[k:k + 3]
        qf, dof, kf, vf, dqf, dkf, dvf, ck, cv = refs[k + 3:k + 12]
        hp = pl.program_id(0)
        n = pl.program_id(1)

        @pl.when(n == 0)
        def _():
            ck[...] = jnp.zeros(ck.shape, F32)
            cv[...] = jnp.zeros(cv.shape, F32)

        dkf[...] = jnp.zeros(dkf.shape, F32)
        dvf[...] = jnp.zeros(dvf.shape, F32)

        @pl.when(n < nt)
        def _():
            qf[...] = q_ref[...].astype(F32)
            dof[...] = do_ref[...].astype(F32)
            kf[0:halo, :] = kh_ref[...].astype(F32)
            kf[halo:, :] = k_ref[...].astype(F32)
            vf[0:halo, :] = vh_ref[...].astype(F32)
            vf[halo:, :] = v_ref[...].astype(F32)
            col = lax.broadcasted_iota(jnp.int32, (2 * BLK, 2 * BLK), 1)
            lane = lax.broadcasted_iota(jnp.int32, (BLK, BLK), 1)
            masks = _head_masks()
            bias = jnp.concatenate([sl_ref[0, e:e + 1, :] * nd_ref[...] for e in range(2)], axis=0)
            bias0 = bias + jnp.where((n == 0) & (col < BLK), NEG, 0.0)
            for b in range(nq):
                for r in range(d):
                    rq = _rows(b * halo + r, BLK, d)
                    rk = _rows(b * halo + r, 2 * BLK, d)
                    q2 = qf[rq, :].astype(BF16)
                    do2 = dof[rq, :].astype(BF16)
                    kcat = kf[rk, :].astype(BF16)
                    vcat = vf[rk, :].astype(BF16)
                    lt = l_ref[rq, :]
                    dt = dd_ref[rq, :]
                    qs = jnp.concatenate([q2 * masks[0], q2 * masks[1]], axis=0)
                    dos = jnp.concatenate([do2 * masks[0], do2 * masks[1]], axis=0)
                    lcol = jnp.concatenate([_pick_col(lt, lane, 2 * hp + e) for e in range(2)], axis=0)
                    dcol = jnp.concatenate([_pick_col(dt, lane, 2 * hp + e) for e in range(2)], axis=0)
                    sc = _dot_nt(qs, kcat) + (bias0 if b == 0 else bias)
                    p = jnp.exp(sc - lcol)
                    ds = p * (_dot_nt(dos, vcat) - dcol)
                    dsb = ds.astype(BF16)
                    dq = _dot(dsb, kcat)
                    dqf[rq, :] = (HEAD_DIM ** -0.5) * jnp.where(lane < HEAD_DIM, dq[0:BLK], dq[BLK:])
                    dkf[rk, :] += _dot(dsb.T, qs)
                    dvf[rk, :] += _dot(p.astype(BF16).T, dos)
            dq_ref[...] = dqf[...].astype(BF16)

        if tile > halo:
            dk_ref[0:tile - halo, :] = ck[0:tile - halo, :].astype(BF16)
            dv_ref[0:tile - halo, :] = cv[0:tile - halo, :].astype(BF16)
        dk_ref[tile - halo:, :] = (ck[tile - halo:, :] + dkf[0:halo, :]).astype(BF16)
        dv_ref[tile - halo:, :] = (cv[tile - halo:, :] + dvf[0:halo, :]).astype(BF16)
        ck[...] = dkf[halo:, :]
        cv[...] = dvf[halo:, :]

    cur = lambda n: jnp.minimum(n, nt - 1)
    col_blk = lambda hp, n: (cur(n), g * N_PAIRS + hp)
    halo_blk = lambda hp, n: (jnp.maximum(cur(n) * nq - 1, 0), g * N_PAIRS + hp)
    out_kv = lambda hp, n: (jnp.maximum(n - 1, 0), g * N_PAIRS + hp)
    small = pl.BlockSpec((tile, BLK), lambda hp, n: (cur(n), 0))
    hbm = pl.BlockSpec(memory_space=pl.ANY)
    in_specs = [pl.BlockSpec((tile, BLK), col_blk),
                pl.BlockSpec((tile, BLK), col_blk), pl.BlockSpec((halo, BLK), halo_blk),
                pl.BlockSpec((tile, BLK), col_blk), pl.BlockSpec((halo, BLK), halo_blk),
                pl.BlockSpec((BLK, 2 * BLK), lambda hp, n: (0, 0)),
                pl.BlockSpec((1, 8, 2 * BLK), lambda hp, n: (hp, 0, 0)),
                pl.BlockSpec((tile, BLK), lambda hp, n: (cur(n), hp)), small, small, hbm]
    args = [q, k_all, k_all, v_all, v_all, _band_table(d), _slope_table(), do, lse, dd, dhb]
    aliases = {10: 0}
    if not first:
        in_specs += [hbm, hbm]
        args += [dk_all, dv_all]
        aliases.update({11: 1, 12: 2})
    tile_f32 = pltpu.VMEM((tile, BLK), F32)
    wide_f32 = pltpu.VMEM((tile + halo, BLK), F32)
    return pl.pallas_call(
        body, name=f"attn_bwd_g{g}", grid=(N_PAIRS, nt + 1), in_specs=in_specs,
        out_specs=[pl.BlockSpec((tile, BLK), col_blk), pl.BlockSpec((tile, BLK), out_kv),
                   pl.BlockSpec((tile, BLK), out_kv)],
        out_shape=[_sds((s, 4 * D), BF16), _sds((s, 3 * D), BF16), _sds((s, 3 * D), BF16)],
        scratch_shapes=[tile_f32, tile_f32, wide_f32, wide_f32, tile_f32, wide_f32, wide_f32, tile_f32, tile_f32],
        input_output_aliases=aliases, compiler_params=_params(40, 2),
    )(*args)


def _b_in_bwd(dr2, dhb, dk_all, dv_all, wb_g, wkv_g, n1, rstd1, pg0, tm):
    s = dr2.shape[0]
    last = s // tm - 1
    nb_, nkv = wb_g.shape[2], wkv_g.shape[2]
    half = N_DEV // 2

    def body(dr2_ref, dh_ref, dk_ref, dv_ref, wb_hbm, wkv_hbm, n_ref, rstd_ref, pg_ref,
             dr_ref, drb_ref, dpg_ref, dpb_ref, dbo_ref, wb, wkv):
        i = pl.program_id(0)

        @pl.when(i == 0)
        def _():
            pltpu.sync_copy(wb_hbm, wb)
            pltpu.sync_copy(wkv_hbm, wkv)

        _acc_init(i, dpg_ref, dpb_ref, dbo_ref)
        acc = ALPHA * dr2_ref[...]
        for j in range(N_DEV):
            acc = acc + _dot_nt(dh_ref[:, nb_ * j:nb_ * (j + 1)], wb[j])
            src = dk_ref if j < half else dv_ref
            jj = j % half
            acc = acc + _dot_nt(src[:, nkv * jj:nkv * (jj + 1)], wkv[j])
        n = n_ref[...]
        dpg_ref[...] += _rowsum8(acc * n)
        dpb_ref[...] += _rowsum8(acc)
        dr = _ln_bwd(acc * pg_ref[...], n, rstd_ref[...])
        dr_ref[...] = dr
        drb_ref[...] = dr.astype(BF16)
        dbo_ref[...] += _rowsum8(dr)
        _acc_finish(i, last, dpg_ref, dpb_ref, dbo_ref)

    row = lambda i: (i, 0)
    hbm = pl.BlockSpec(memory_space=pl.ANY)
    acc_spec = pl.BlockSpec((8, D), lambda i: (0, 0))
    return pl.pallas_call(
        body, name="b_in_bwd", grid=(s // tm,),
        in_specs=[pl.BlockSpec((tm, D), row), pl.BlockSpec((tm, 4 * D), row), pl.BlockSpec((tm, 3 * D), row),
                  pl.BlockSpec((tm, 3 * D), row), hbm, hbm, pl.BlockSpec((tm, D), row), pl.BlockSpec((tm, 1), row),
                  pl.BlockSpec((1, D), lambda i: (0, 0))],
        out_specs=[pl.BlockSpec((tm, D), row), pl.BlockSpec((tm, D), row), acc_spec, acc_spec, acc_spec],
        out_shape=[_sds((s, D), F32), _sds((s, D), BF16)] + [_sds((8, D), F32)] * 3,
        scratch_shapes=[pltpu.VMEM(wb_g.shape, BF16), pltpu.VMEM(wkv_g.shape, BF16)],
        compiler_params=_params(56, 1),
    )(dr2, dhb, dk_all, dv_all, wb_g, wkv_g, n1, rstd1, pg0)


def _a_out_bwd(drb, w, u1, h, ln_g, ln_b, tm, plan=None):
    s = drb.shape[0]
    last = s // tm - 1

    def body(dr_ref, w_ref, u1_ref, z_ref, g_ref, b_ref, du1_ref, dh_ref, dg_ref, db_ref, dbz_ref):
        i = pl.program_id(0)
        _acc_init(i, dg_ref, db_ref, dbz_ref)
        dga = _dot_nt(dr_ref[...], w_ref[...])
        n, rstd = _ln_stats(u1_ref[...])
        pre = n * g_ref[...] + b_ref[...]
        sp = _sigmoid(pre)
        zz = z_ref[...]
        sz = _sigmoid(zz)
        dz = dga * (pre * sp) * _dsilu(zz, sz)
        dh_ref[...] = dz.astype(BF16)
        dbz_ref[...] += _rowsum8(dz)
        dpre = dga * (zz * sz) * _dsilu(pre, sp)
        dg_ref[...] += _rowsum8(dpre * n)
        db_ref[...] += _rowsum8(dpre)
        du1_ref[...] = _ln_bwd(dpre * g_ref[...], n, rstd)
        _acc_finish(i, last, dg_ref, db_ref, dbz_ref)

    row = lambda i: (i, 0)
    vec = pl.BlockSpec((1, D), lambda i: (0, 0))
    acc_spec = pl.BlockSpec((8, D), lambda i: (0, 0))
    return _planned_call(
        plan, (drb, w, u1, h, ln_g, ln_b), body, name="a_out_bwd", grid=(s // tm,),
        in_specs=[pl.BlockSpec((tm, D), row), pl.BlockSpec((D, D), lambda i: (0, 0)), pl.BlockSpec((tm, D), row),
                  pl.BlockSpec((tm, D), lambda i: (i, 2)), vec, vec],
        out_specs=[pl.BlockSpec((tm, D), row), pl.BlockSpec((tm, D), lambda i: (i, 2)),
                   acc_spec, acc_spec, acc_spec],
        out_shape=[_sds((s, D), F32), _sds((s, 3 * D), BF16)] + [_sds((8, D), F32)] * 3,
        compiler_params=_params(32, 1))


def _a_conv_bwd(du1, u0, h, dha, w_dw, tm, plan=None):
    s = du1.shape[0]
    steps = s // tm
    per = tm // CONV_HALO
    pad = CONV_W - 1

    def body(du_ref, dun_ref, u0_ref, u0p_ref, h_ref, w_ref, dha_hbm,
             dh_ref, dw_ref, dbdw_ref, dba_ref, dbg_ref, dbuf, ubuf, wacc, dshs, ushs):
        i = pl.program_id(0)
        _acc_init(i, dbdw_ref, dba_ref, dbg_ref, wacc)
        dbuf[0:tm, :] = du_ref[...]
        dbuf[tm:, :] = jnp.where(i < steps - 1, dun_ref[...], 0.0)
        ubuf[0:CONV_HALO, :] = jnp.where(i > 0, u0p_ref[...], 0.0)
        ubuf[CONV_HALO:, :] = u0_ref[...]
        _shifted_copies(dbuf, dshs, tm)
        _shifted_copies(ubuf, ushs, tm)
        off = CONV_HALO - pad
        for ch in range(tm // CONV_CHUNK):
            r0 = ch * CONV_CHUNK
            duc = du_ref[r0:r0 + CONV_CHUNK, :]
            acc = jnp.zeros((CONV_CHUNK, D), F32)
            for j in range(CONV_W):
                acc = acc + w_ref[j:j + 1, :] * _shifted_rows(dbuf, dshs, r0 + pad - j)
                wacc[j] += _rowsum8(duc * _shifted_rows(ubuf, ushs, r0 + off + j))
            sg = _sigmoid(h_ref[r0:r0 + CONV_CHUNK, D:2 * D])
            da = acc * sg
            dag = acc * h_ref[r0:r0 + CONV_CHUNK, 0:D] * (sg * (1.0 - sg))
            dh_ref[r0:r0 + CONV_CHUNK, 0:D] = da.astype(BF16)
            dh_ref[r0:r0 + CONV_CHUNK, D:2 * D] = dag.astype(BF16)
            dbdw_ref[...] += _rowsum8(duc)
            dba_ref[...] += _rowsum8(da)
            dbg_ref[...] += _rowsum8(dag)
        _acc_finish(i, steps - 1, dbdw_ref, dba_ref, dbg_ref)

        @pl.when(i == steps - 1)
        def _():
            for j in range(CONV_W):
                dw_ref[j:j + 1, :] = jnp.sum(wacc[j], axis=0, keepdims=True)
            dw_ref[CONV_W:, :] = jnp.zeros((32 - CONV_W, D), F32)

    row = lambda i: (i, 0)
    acc_spec = pl.BlockSpec((8, D), lambda i: (0, 0))
    return _planned_call(
        plan, (du1, du1, u0, u0, h, w_dw, dha), body, name="a_conv_bwd", grid=(steps,),
        in_specs=[pl.BlockSpec((tm, D), row),
                  pl.BlockSpec((CONV_HALO, D), lambda i: (jnp.minimum((i + 1) * per, s // CONV_HALO - 1), 0)),
                  pl.BlockSpec((tm, D), row),
                  pl.BlockSpec((CONV_HALO, D), lambda i: (jnp.maximum(i * per - 1, 0), 0)),
                  pl.BlockSpec((tm, 2 * D), lambda i: (i, 0)),
                  pl.BlockSpec((32, D), lambda i: (0, 0)), pl.BlockSpec(memory_space=pl.ANY)],
        out_specs=[pl.BlockSpec((tm, 2 * D), lambda i: (i, 0)),
                   pl.BlockSpec((32, D), lambda i: (0, 0)), acc_spec, acc_spec, acc_spec],
        out_shape=[_sds((s, 3 * D), BF16), _sds((32, D), F32)] + [_sds((8, D), F32)] * 3,
        scratch_shapes=[pltpu.VMEM((tm + CONV_HALO, D), F32), pltpu.VMEM((tm + CONV_HALO, D), F32),
                        pltpu.VMEM((CONV_W, 8, D), F32),
                        pltpu.VMEM((SUBLANES - 1, tm + CONV_HALO, D), F32),
                        pltpu.VMEM((SUBLANES - 1, tm + CONV_HALO, D), F32)],
        input_output_aliases={6: 0}, compiler_params=_params(56, 1))


def _a_in_bwd(dr1, dha, w_g, tm, plan=None):
    s = dr1.shape[0]
    npd = w_g.shape[2]

    def body(dr_ref, dh_ref, w_ref, o_ref):
        acc = ALPHA * dr_ref[...]
        for j in range(N_DEV):
            acc = acc + _dot_nt(dh_ref[:, npd * j:npd * (j + 1)], w_ref[j])
        o_ref[...] = acc

    row = lambda i: (i, 0)
    return _planned_call(
        plan, (dr1, dha, w_g), body, name="a_in_bwd", grid=(s // tm,),
        in_specs=[pl.BlockSpec((tm, D), row), pl.BlockSpec((tm, 3 * D), row),
                  pl.BlockSpec(w_g.shape, lambda i: (0, 0, 0))],
        out_specs=[pl.BlockSpec((tm, D), row)], out_shape=[_sds((s, D), F32)],
        compiler_params=_params(36, 1))


def _wgrad(name, a, b, npd, ts, total=None, at=0, into=None):
    s = a.shape[0]
    n_blk = b.shape[1] // npd
    total = n_blk if total is None else total
    assert at % n_blk == 0

    def body(*refs):
        a_ref, b_ref = refs[:2]
        o_ref = refs[-1]
        si = pl.program_id(0)

        @pl.when(si == 0)
        def _():
            o_ref[...] = jnp.zeros(o_ref.shape, F32)

        a_t = a_ref[...].T
        for j in range(n_blk):
            o_ref[j] += _dot(a_t, b_ref[:, npd * j:npd * (j + 1)])

    in_specs = [pl.BlockSpec((ts, D), lambda si: (si, 0)), pl.BlockSpec((ts, n_blk * npd), lambda si: (si, 0))]
    args = [a, b]
    aliases = {}
    if into is not None:
        in_specs.append(pl.BlockSpec(memory_space=pl.ANY))
        args.append(into)
        aliases = {2: 0}
    return pl.pallas_call(
        body, name=name, grid=(s // ts,), in_specs=in_specs,
        out_specs=pl.BlockSpec((n_blk, D, npd), lambda si: (at // n_blk, 0, 0)),
        out_shape=_sds((total, D, npd), F32), input_output_aliases=aliases,
        compiler_params=_params(56, 1),
    )(*args)


SMALL_ROWS = 40
GRAD_ROWS = 48


def kernel(x, a_w_in, a_b_in, a_w_dw, a_b_dw, a_ln_g, a_ln_b, a_w_out, a_b_out, kv_w, b_w_in, b_w_out, b_b_out, post_ln_g, post_ln_b, loss_target, m_a_w_in, m_a_b_in, m_a_w_dw, m_a_b_dw, m_a_ln_g, m_a_ln_b, m_a_w_out, m_a_b_out, m_kv_w, m_b_w_in, m_b_w_out, m_b_b_out, m_post_ln_g, m_post_ln_b, v_a_w_in, v_a_b_in, v_a_w_dw, v_a_b_dw, v_a_ln_g, v_a_ln_b, v_a_w_out, v_a_b_out, v_kv_w, v_b_w_in, v_b_w_out, v_b_b_out, v_post_ln_g, v_post_ln_b):
    s = x.shape[1]
    assert x.shape == (1, s, D) and s % (DILATIONS[-1] * BLK) == 0
    xs = x.reshape(s, D)
    tgt = loss_target.reshape(s, D)
    me = 4 * lax.axis_index("x") + 2 * lax.axis_index("y") + lax.axis_index("c")
    c_idx = lax.axis_index("c").astype(jnp.int32).reshape(1)

    def small_pack(b_in, w_dw, b_dw, ln_g, ln_b, b_out):
        rows = [b_in.reshape(3, BLK), w_dw.reshape(CONV_W, BLK), b_dw.reshape(1, BLK), ln_g.reshape(1, BLK),
                ln_b.reshape(1, BLK), b_out.reshape(1, BLK)]
        n = sum(r.shape[0] for r in rows)
        return jnp.concatenate(rows + [jnp.zeros((SMALL_ROWS - n, BLK), F32)], axis=0)

    wa_in, sm, *later = _all_gather(
        "gather_first", [a_w_in[0], small_pack(a_b_in, a_w_dw, a_b_dw, a_ln_g, a_ln_b, a_b_out)], [BF16, F32],
        casts=[a_w_out[0], kv_w, b_w_in[0], b_w_out[0]])
    ba_in = sm[:, 0:3, :].reshape(1, 3 * D)
    w_dw = jnp.concatenate([sm[:, 3:3 + CONV_W, :].transpose(1, 0, 2).reshape(CONV_W, D), jnp.zeros((1, D), F32)], axis=0)
    b_dw, ln_g, ln_b, ba_out = (sm[:, 34 + k, :].reshape(1, D) for k in range(4))
    pg0, pg1 = post_ln_g[0:1], post_ln_g[1:2]
    pb0, pb1 = post_ln_b[0:1], post_ln_b[1:2]

    h_a, u0, xb = _a_in_proj(xs, wa_in, ba_in, 256)
    (u1, g_a), (wa_out, wkv, wb_in, wb_out) = _a_conv_gate(u0, h_a, w_dw, b_dw, ln_g, ln_b, 256, _gather_plan(later))
    wa_out = wa_out.reshape(D, D)
    wb_out = wb_out.reshape(D, D)
    n1, rstd1, x1b = _a_out_proj(g_a, wa_out, ba_out, xs, pg0, pb0, 512)
    k_all, v_all = _kv_proj(x1b, wkv, 512)
    q, z_b = _b_in_proj(x1b, wb_in, 512)
    o, lse = _attn_forward(q, k_all, v_all, None, None, None, 0, True, False)
    o, lse = _attn_forward(q, k_all, v_all, o, lse, None, 1, False, False)
    o, lse, g_b = _attn_forward(q, k_all, v_all, o, lse, z_b, 2, False, True)
    dr2, dr2b, loss8, dpg1, dpb1, dbb_out = _b_out_loss(g_b, wb_out, b_b_out, n1, pg0, pb0, pg1, pb1, tgt, 512)

    do, dhb, dd = _b_out_bwd(dr2b, wb_out, z_b, o, 512)
    dk_all = dv_all = None
    for g in range(3):
        dhb, dk_all, dv_all = _attn_backward(q, k_all, v_all, do, lse, dd, dhb, dk_all, dv_all, g)
    dr1, dr1b, dpg0, dpb0, dba_out = _b_in_bwd(dr2, dhb, dk_all, dv_all, wb_in, wkv, n1, rstd1, pg0, 256)

    p_kv = _wgrad("wgrad_k", x1b, dk_all, 768, 512, total=N_DEV)
    p_kv = _wgrad("wgrad_v", x1b, dv_all, 768, 512, total=N_DEV, at=N_DEV // 2, into=p_kv)
    p_b_in = _wgrad("wgrad_b_in", x1b, dhb, 512, 512)
    p_a_out = _wgrad("wgrad_a_out", g_a, dr1b, D, 512).reshape(N_DEV, BLK, D)
    p_b_out = _wgrad("wgrad_b_out", g_b, dr2b, D, 512).reshape(N_DEV, BLK, D)
    parts = [p_kv, p_b_in, p_a_out, p_b_out]
    (du1, dha, dln_g, dln_b, dbz), from_sibling = _a_out_bwd(dr1b, wa_out, u1, h_a, ln_g, ln_b, 512, _sibling_plan(parts))
    chip_sums = [_pair_add(f"pair_add_{k}", p, r, c_idx) for k, (p, r) in enumerate(zip(parts, from_sibling))]
    (dha, dw_dw, db_dw, dba, dbg), from_chips = _a_conv_bwd(du1, u0, h_a, dha, w_dw, 256, _chips_plan(chip_sums))
    p_a_in = _wgrad("wgrad_a_in", xb, dha, 384, 512)
    (from_sibling_a,) = _exchange_sibling("reduce_sibling_a_in", [p_a_in])
    sum_a = _pair_add("pair_add_a_in", p_a_in, from_sibling_a, c_idx)
    (grad_x,), (from_chips_a,) = _a_in_bwd(dr1, dha, wa_in, 512, _chips_plan([sum_a]))

    reduced = [from_chips_a] + from_chips
    big_w = [a_w_in[0], kv_w, b_w_in[0], a_w_out[0], b_w_out[0]]
    big_m = [m_a_w_in[0], m_kv_w, m_b_w_in[0], m_a_w_out[0], m_b_w_out[0]]
    big_v = [v_a_w_in[0], v_kv_w, v_b_w_in[0], v_a_w_out[0], v_b_w_out[0]]
    big = [_sum_adamw(f"adamw_{k}", reduced[k], big_w[k], big_m[k], big_v[k]) for k in range(5)]

    rows = [dba[0:1], dbg[0:1], dbz[0:1], dw_dw[0:CONV_W], db_dw[0:1], dln_g[0:1], dln_b[0:1], dba_out[0:1],
            dbb_out[0:1], dpg0[0:1], dpg1[0:1], dpb0[0:1], dpb1[0:1], loss8[0:1]]
    n_rows = sum(r.shape[0] for r in rows)
    gpack = jnp.concatenate(rows + [jnp.zeros((GRAD_ROWS - n_rows, D), F32)], axis=0)
    (gall,) = _all_gather("gather_small_grads", [gpack], [F32])
    gs = _small_sum("small_sum", gall)
    loss = gs[43, 0]

    def my(vec, width):
        return lax.dynamic_slice_in_dim(vec, me * width, width, axis=-1)

    g_small = [my(gs[0:3].reshape(1, 3 * D), 384), my(gs[3:34], BLK)[None], my(gs[34:35], BLK), my(gs[35:36], BLK),
               my(gs[36:37], BLK), my(gs[37:38], BLK), gs[38:39], gs[39:41], gs[41:43]]
    w_small = [a_b_in, a_w_dw, a_b_dw, a_ln_g, a_ln_b, a_b_out, b_b_out, post_ln_g, post_ln_b]
    m_small = [m_a_b_in, m_a_w_dw, m_a_b_dw, m_a_ln_g, m_a_ln_b, m_a_b_out, m_b_b_out, m_post_ln_g, m_post_ln_b]
    v_small = [v_a_b_in, v_a_w_dw, v_a_b_dw, v_a_ln_g, v_a_ln_b, v_a_b_out, v_b_b_out, v_post_ln_g, v_post_ln_b]
    sizes = [math.prod(w.shape) for w in w_small]
    total = sum(sizes)
    padded = -(-total // (8 * BLK)) * (8 * BLK)

    def flat(parts_):
        return jnp.concatenate([p.reshape(-1) for p in parts_] + [jnp.ones((padded - total,), F32)]).reshape(-1, BLK)

    sd, sm_new, sv_new = _small_adamw("adamw_small", flat(w_small), flat(g_small), flat(m_small), flat(v_small))

    def unflat(packed):
        out, pos = [], 0
        vec = packed.reshape(-1)
        for w, n in zip(w_small, sizes):
            out.append(vec[pos:pos + n].reshape(w.shape))
            pos += n
        return out

    g_small = [g.reshape(w.shape) for g, w in zip(g_small, w_small)]
    d_small, nm_small, nv_small = unflat(sd), unflat(sm_new), unflat(sv_new)

    def ordered(bigs, smalls):
        a_in, kvw, b_in, a_out, b_out = bigs
        return [a_in[None], smalls[0], smalls[1], smalls[2], smalls[3], smalls[4], a_out[None], smalls[5],
                kvw, b_in[None], b_out[None], smalls[6], smalls[7], smalls[8]]

    grads = ordered([b[0] for b in big], g_small)
    deltas = ordered([b[1] for b in big], d_small)
    new_m = ordered([b[2] for b in big], nm_small)
    new_v = ordered([b[3] for b in big], nv_small)
    return (loss, grad_x.reshape(1, s, D), *grads, *deltas, *new_m, *new_v)
```

```python
import math

import numpy as np
import jax
import jax.numpy as jnp
from jax import lax
from jax.experimental import pallas as pl
from jax.experimental.pallas import tpu as pltpu

F32 = jnp.float32
BF16 = jnp.bfloat16
MESH = pl.DeviceIdType.MESH

D = 1024
N_DEV = 8
HEAD_DIM = 64
N_HEADS = 16
DILATIONS = (1, 4, 16)
BLK = 128
CONV_W = 31
ALPHA = (2.0 * 2) ** 0.25
LN_EPS = 1e-5
SLOPES = tuple(2.0 ** (-8.0 * (h + 1) / N_HEADS) for h in range(N_HEADS))
NEG = -1e30

ADAM_LR = 0.001
ADAM_B1 = 0.9
ADAM_B2 = 0.999
ADAM_EPS = 1e-08
ADAM_WD = 0.01
ADAM_STEP = 10

VMEM_CAP_MB = 64


def _params(vmem_mb, n_grid=0):
    sem = ("arbitrary",) * n_grid if n_grid else None
    return pltpu.CompilerParams(dimension_semantics=sem, vmem_limit_bytes=min(vmem_mb, VMEM_CAP_MB - 6) * 2 ** 20)


def _sds(shape, dtype):
    return jax.ShapeDtypeStruct(tuple(shape), dtype)


def _sigmoid(v):
    return jax.nn.sigmoid(v)


def _dsilu(v, s):
    return s * (1.0 + v * (1.0 - s))


def _ln_stats(r):
    mu = jnp.mean(r, axis=-1, keepdims=True)
    xc = r - mu
    var = jnp.mean(xc * xc, axis=-1, keepdims=True)
    rstd = lax.rsqrt(var + LN_EPS)
    return xc * rstd, rstd


def _ln_bwd(dn, n, rstd):
    m1 = jnp.mean(dn, axis=-1, keepdims=True)
    m2 = jnp.mean(dn * n, axis=-1, keepdims=True)
    return rstd * (dn - m1 - n * m2)


def _rowsum8(v):
    tm, c = v.shape
    return v.reshape(tm // 8, 8, c).sum(axis=0)


def _acc_init(i, *refs):
    @pl.when(i == 0)
    def _():
        for r in refs:
            r[...] = jnp.zeros(r.shape, r.dtype)


def _acc_finish(i, last, *refs):
    @pl.when(i == last)
    def _():
        for r in refs:
            r[...] = jnp.broadcast_to(jnp.sum(r[...], axis=0, keepdims=True), r.shape)


def _dot(a, b):
    return jnp.dot(a, b, preferred_element_type=F32)


def _dot_nt(a, b):
    return lax.dot_general(a, b, (((1,), (1,)), ((), ())), preferred_element_type=F32)


def _place():
    return lax.axis_index("x"), lax.axis_index("y"), lax.axis_index("c")


def _all_gather(name, arrays, dtypes, casts=()):
    n = len(arrays)
    nc = len(casts)

    def body(*refs):
        ins, cast_ins = refs[:n], refs[n:n + nc]
        outs, cast_outs = refs[n + nc:2 * n + nc], refs[2 * n + nc:2 * (n + nc)]
        stages = refs[2 * (n + nc):3 * n + 2 * nc]
        send_sems, recv_sems, local_sems = refs[3 * n + 2 * nc:]
        x, y, c = _place()
        me, sibling = (x, y, c), (x, y, 1 - c)
        chips = [(1 - x, y), (x, 1 - y), (1 - x, 1 - y)]

        def slot(ref, p):
            return ref.at[4 * p[0] + 2 * p[1] + p[2]]

        def copy(a, k, block, to, src=None):
            return pltpu.make_async_remote_copy(
                src_ref=slot(outs[a], block) if src is None else src, dst_ref=slot(outs[a], block),
                send_sem=send_sems.at[a, k], recv_sem=recv_sems.at[a, k], device_id=to, device_id_type=MESH)

        first, mine = [], []
        for a in range(n):
            stages[a][...] = ins[a][...].astype(stages[a].dtype)
            cp = pltpu.make_async_copy(stages[a], slot(outs[a], me), local_sems.at[a])
            cp.start()
            mine.append(cp)
            first.append(copy(a, 0, me, sibling, src=stages[a]))
            first += [copy(a, 1 + j, me, (*chip, c), src=stages[a]) for j, chip in enumerate(chips)]
        for cp in first:
            cp.start()
        for src, dst in zip(cast_ins, cast_outs):
            dst[...] = src[...].astype(BF16)
        passed = []
        for j, chip in enumerate(chips):
            for a in range(n):
                copy(a, 1 + j, (*chip, c), me).wait_recv()
                cp = copy(a, 4 + j, (*chip, c), sibling)
                cp.start()
                passed.append(cp)
        for a in range(n):
            copy(a, 0, sibling, me).wait_recv()
            for j, chip in enumerate(chips):
                copy(a, 4 + j, (*chip, 1 - c), me).wait_recv()
        for cp in first + passed:
            cp.wait_send()
        for cp in mine:
            cp.wait()

    vmem_bytes = sum(math.prod(a.shape) * (jnp.dtype(a.dtype).itemsize + jnp.dtype(dt).itemsize)
                     for a, dt in zip(arrays, dtypes)) + sum(math.prod(a.shape) * 6 for a in casts)
    vm = pl.BlockSpec(memory_space=pltpu.VMEM)
    return pl.pallas_call(
        body, name=name,
        out_shape=[_sds((N_DEV,) + a.shape, dt) for a, dt in zip(arrays, dtypes)] + [_sds(a.shape, BF16) for a in casts],
        in_specs=[vm] * (n + nc),
        out_specs=[pl.BlockSpec(memory_space=pl.ANY)] * n + [vm] * nc,
        scratch_shapes=[pltpu.VMEM(a.shape, dt) for a, dt in zip(arrays, dtypes)]
        + [pltpu.SemaphoreType.DMA((n, 7)), pltpu.SemaphoreType.DMA((n, 7)), pltpu.SemaphoreType.DMA((n,))],
        compiler_params=_params(vmem_bytes // 2 ** 20 + 8),
    )(*arrays, *casts)


class _Plan:
    def __init__(self, args, out_shape, scratch, start, mid, finish):
        self.args, self.out_shape, self.scratch = list(args), list(out_shape), list(scratch)
        self.start, self.mid, self.finish = start, mid, finish


def _gather_plan(shards):
    n = len(shards)

    def copies(ins, outs, sems):
        send_sems, recv_sems, local_sems = sems
        x, y, c = _place()
        me, sibling = (x, y, c), (x, y, 1 - c)
        chips = [(1 - x, y), (x, 1 - y), (1 - x, 1 - y)]

        def slot(ref, p):
            return ref.at[4 * p[0] + 2 * p[1] + p[2]]

        def copy(a, k, block, to, src=None):
            return pltpu.make_async_remote_copy(
                src_ref=slot(outs[a], block) if src is None else src, dst_ref=slot(outs[a], block),
                send_sem=send_sems.at[a, k], recv_sem=recv_sems.at[a, k], device_id=to, device_id_type=MESH)

        mine = [pltpu.make_async_copy(ins[a], slot(outs[a], me), local_sems.at[a]) for a in range(n)]
        first = [copy(a, 0, me, sibling, src=ins[a]) for a in range(n)]
        first += [copy(a, 1 + j, me, (*chip, c), src=ins[a]) for a in range(n) for j, chip in enumerate(chips)]
        arrive = [copy(a, 1 + j, (*chip, c), me) for j, chip in enumerate(chips) for a in range(n)]
        passed = [copy(a, 4 + j, (*chip, c), sibling) for j, chip in enumerate(chips) for a in range(n)]
        from_sibling = [copy(a, 0, sibling, me) for a in range(n)]
        from_sibling += [copy(a, 4 + j, (*chip, 1 - c), me) for a in range(n) for j, chip in enumerate(chips)]
        return mine, first, arrive, passed, from_sibling

    def start(ins, outs, sems):
        mine, first, _, _, _ = copies(ins, outs, sems)
        for cp in mine + first:
            cp.start()

    def mid(ins, outs, sems):
        _, _, arrive, passed, _ = copies(ins, outs, sems)
        for got, on in zip(arrive, passed):
            got.wait_recv()
            on.start()

    def finish(ins, outs, sems):
        mine, first, _, passed, from_sibling = copies(ins, outs, sems)
        for cp in from_sibling:
            cp.wait_recv()
        for cp in first + passed:
            cp.wait_send()
        for cp in mine:
            cp.wait()

    return _Plan(shards, [_sds((N_DEV,) + a.shape, a.dtype) for a in shards],
                 [pltpu.SemaphoreType.DMA((n, 7)), pltpu.SemaphoreType.DMA((n, 7)), pltpu.SemaphoreType.DMA((n,))],
                 start, mid, finish)


def _sibling_plan(parts):
    n = len(parts)

    def copies(ins, outs, sems):
        send_sems, recv_sems = sems
        x, y, c = _place()
        return [pltpu.make_async_remote_copy(
            src_ref=ins[a].at[2 * p + 1 - c], dst_ref=outs[a].at[p], send_sem=send_sems.at[a, p],
            recv_sem=recv_sems.at[a, p], device_id=(x, y, 1 - c), device_id_type=MESH)
            for a in range(n) for p in range(4)]

    def start(ins, outs, sems):
        for cp in copies(ins, outs, sems):
            cp.start()

    def finish(ins, outs, sems):
        cps = copies(ins, outs, sems)
        for cp in cps:
            cp.wait_recv()
        for cp in cps:
            cp.wait_send()

    return _Plan(parts, [_sds((4,) + p.shape[1:], p.dtype) for p in parts],
                 [pltpu.SemaphoreType.DMA((n, 4)), pltpu.SemaphoreType.DMA((n, 4))], start, None, finish)


def _chips_plan(sums):
    n = len(sums)

    def copies(ins, outs, sems):
        send_sems, recv_sems, local_sems = sems
        x, y, c = _place()
        my_chip = 2 * x + y
        chips = [(1 - x, y), (x, 1 - y), (1 - x, 1 - y)]
        mine = [pltpu.make_async_copy(ins[a].at[my_chip], outs[a].at[my_chip], local_sems.at[a]) for a in range(n)]
        remote = [pltpu.make_async_remote_copy(
            src_ref=ins[a].at[2 * px + py], dst_ref=outs[a].at[my_chip], send_sem=send_sems.at[a, k],
            recv_sem=recv_sems.at[a, k], device_id=(px, py, c), device_id_type=MESH)
            for a in range(n) for k, (px, py) in enumerate(chips)]
        return mine, remote

    def start(ins, outs, sems):
        mine, remote = copies(ins, outs, sems)
        for cp in mine + remote:
            cp.start()

    def finish(ins, outs, sems):
        mine, remote = copies(ins, outs, sems)
        for cp in remote:
            cp.wait_recv()
        for cp in remote:
            cp.wait_send()
        for cp in mine:
            cp.wait()

    return _Plan(sums, [_sds(s.shape, s.dtype) for s in sums],
                 [pltpu.SemaphoreType.DMA((n, 3)), pltpu.SemaphoreType.DMA((n, 3)), pltpu.SemaphoreType.DMA((n,))],
                 start, None, finish)


def _planned_call(plan, args, body, *, name, grid, in_specs, out_specs, out_shape, scratch_shapes=(), mid_step=None,
                  **kw):
    in_specs, out_specs, out_shape = list(in_specs), list(out_specs), list(out_shape)
    scratch_shapes = list(scratch_shapes)
    if plan is None:
        res = pl.pallas_call(body, name=name, grid=grid, in_specs=in_specs, out_specs=out_specs, out_shape=out_shape,
                             scratch_shapes=scratch_shapes, **kw)(*args)
        return list(res), []
    n_in, n_out, n_scr = len(in_specs), len(out_specs), len(scratch_shapes)
    p_in, p_out = len(plan.args), len(plan.out_shape)
    steps = grid[0]

    def fused(*refs):
        ins, pins = refs[:n_in], refs[n_in:n_in + p_in]
        o0 = n_in + p_in
        outs, pouts = refs[o0:o0 + n_out], refs[o0 + n_out:o0 + n_out + p_out]
        s0 = o0 + n_out + p_out
        scr, pscr = refs[s0:s0 + n_scr], refs[s0 + n_scr:]
        i = pl.program_id(0)

        @pl.when(i == 0)
        def _():
            plan.start(pins, pouts, pscr)

        body(*ins, *outs, *scr)
        if plan.mid is not None:
            @pl.when(i == mid_step)
            def _():
                plan.mid(pins, pouts, pscr)

        @pl.when(i == steps - 1)
        def _():
            plan.finish(pins, pouts, pscr)

    hbm = pl.BlockSpec(memory_space=pl.ANY)
    res = pl.pallas_call(
        fused, name=name, grid=grid, in_specs=in_specs + [hbm] * p_in, out_specs=out_specs + [hbm] * p_out,
        out_shape=out_shape + plan.out_shape, scratch_shapes=scratch_shapes + plan.scratch, **kw)(*args, *plan.args)
    return list(res[:n_out]), list(res[n_out:])


def _exchange_sibling(name, parts):
    n = len(parts)

    def body(*refs):
        ins, outs = refs[:n], refs[n:2 * n]
        send_sems, recv_sems = refs[2 * n:]
        x, y, c = _place()
        copies = []
        for a in range(n):
            for p in range(4):
                copies.append(pltpu.make_async_remote_copy(
                    src_ref=ins[a].at[2 * p + 1 - c], dst_ref=outs[a].at[p],
                    send_sem=send_sems.at[a, p], recv_sem=recv_sems.at[a, p],
                    device_id=(x, y, 1 - c), device_id_type=MESH))
        for cp in copies:
            cp.start()
        for cp in copies:
            cp.wait_recv()
        for cp in copies:
            cp.wait_send()

    return pl.pallas_call(
        body, name=name,
        out_shape=[_sds((4,) + p.shape[1:], p.dtype) for p in parts],
        in_specs=[pl.BlockSpec(memory_space=pl.ANY)] * n,
        out_specs=[pl.BlockSpec(memory_space=pl.ANY)] * n,
        scratch_shapes=[pltpu.SemaphoreType.DMA((n, 4)), pltpu.SemaphoreType.DMA((n, 4))],
    )(*parts)


def _exchange_chips(name, sums):
    n = len(sums)

    def body(*refs):
        ins, outs = refs[:n], refs[n:2 * n]
        send_sems, recv_sems, local_sems = refs[2 * n:]
        x, y, c = _place()
        my_chip = 2 * x + y
        chips = [(1 - x, y), (x, 1 - y), (1 - x, 1 - y)]
        copies, mine = [], []
        for a in range(n):
            cp = pltpu.make_async_copy(ins[a].at[my_chip], outs[a].at[my_chip], local_sems.at[a])
            cp.start()
            mine.append(cp)
            for k, (px, py) in enumerate(chips):
                copies.append(pltpu.make_async_remote_copy(
                    src_ref=ins[a].at[2 * px + py], dst_ref=outs[a].at[my_chip],
                    send_sem=send_sems.at[a, k], recv_sem=recv_sems.at[a, k],
                    device_id=(px, py, c), device_id_type=MESH))
        for cp in copies:
            cp.start()
        for cp in copies:
            cp.wait_recv()
        for cp in copies:
            cp.wait_send()
        for cp in mine:
            cp.wait()

    return pl.pallas_call(
        body, name=name,
        out_shape=[_sds(s.shape, s.dtype) for s in sums],
        in_specs=[pl.BlockSpec(memory_space=pl.ANY)] * n,
        out_specs=[pl.BlockSpec(memory_space=pl.ANY)] * n,
        scratch_shapes=[pltpu.SemaphoreType.DMA((n, 3)), pltpu.SemaphoreType.DMA((n, 3)),
                        pltpu.SemaphoreType.DMA((n,))],
    )(*sums)


def _pair_add(name, part, recv, c_idx):
    _, r, c = part.shape
    tr = min(r, 256)

    def body(c_ref, a_ref, b_ref, o_ref):
        o_ref[...] = a_ref[...] + b_ref[...]

    return pl.pallas_call(
        body, name=name,
        grid_spec=pltpu.PrefetchScalarGridSpec(
            num_scalar_prefetch=1, grid=(4, r // tr),
            in_specs=[pl.BlockSpec((1, tr, c), lambda p, i, cr: (2 * p + cr[0], i, 0)),
                      pl.BlockSpec((1, tr, c), lambda p, i, cr: (p, i, 0))],
            out_specs=pl.BlockSpec((1, tr, c), lambda p, i, cr: (p, i, 0))),
        out_shape=_sds((4, r, c), F32),
        compiler_params=_params(16, 2),
    )(c_idx, part, recv)


def _adamw_math(w, g, m, v):
    m = ADAM_B1 * m + (1.0 - ADAM_B1) * g
    v = ADAM_B2 * v + (1.0 - ADAM_B2) * (g * g)
    m_hat = m / (1.0 - ADAM_B1 ** ADAM_STEP)
    v_hat = v / (1.0 - ADAM_B2 ** ADAM_STEP)
    delta = -ADAM_LR * (m_hat / (jnp.sqrt(v_hat) + ADAM_EPS) + ADAM_WD * w)
    return delta, m, v


def _sum_adamw(name, recv, w, m, v):
    r, c = w.shape
    tr = min(r, 256)

    def body(p_ref, w_ref, m_ref, v_ref, g_ref, d_ref, nm_ref, nv_ref):
        g = (p_ref[0] + p_ref[1]) + (p_ref[2] + p_ref[3])
        g_ref[...] = g
        d_ref[...], nm_ref[...], nv_ref[...] = _adamw_math(w_ref[...], g, m_ref[...], v_ref[...])

    blk = pl.BlockSpec((tr, c), lambda i: (i, 0))
    return pl.pallas_call(
        body, name=name, grid=(r // tr,),
        in_specs=[pl.BlockSpec((4, tr, c), lambda i: (0, i, 0)), blk, blk, blk],
        out_specs=[blk] * 4, out_shape=[_sds((r, c), F32)] * 4,
        compiler_params=_params(24, 1),
    )(recv, w, m, v)


def _small_sum(name, gathered):
    _, r, c = gathered.shape

    def body(p_ref, o_ref):
        acc = p_ref[0]
        for j in range(1, N_DEV):
            acc = acc + p_ref[j]
        o_ref[...] = acc

    return pl.pallas_call(body, name=name, out_shape=_sds((r, c), F32),
                          in_specs=[pl.BlockSpec(memory_space=pltpu.VMEM)],
                          out_specs=pl.BlockSpec(memory_space=pltpu.VMEM))(gathered)


def _small_adamw(name, w, g, m, v):
    def body(w_ref, g_ref, m_ref, v_ref, d_ref, nm_ref, nv_ref):
        d_ref[...], nm_ref[...], nv_ref[...] = _adamw_math(w_ref[...], g_ref[...], m_ref[...], v_ref[...])

    vm = pl.BlockSpec(memory_space=pltpu.VMEM)
    return pl.pallas_call(body, name=name, out_shape=[_sds(w.shape, F32)] * 3,
                          in_specs=[vm] * 4, out_specs=[vm] * 3)(w, g, m, v)


def _a_in_proj(x, w_g, b_full, tm):
    s = x.shape[0]
    npd = w_g.shape[2]

    def body(x_ref, w_ref, b_ref, h_ref, u0_ref, xb_ref):
        xb = x_ref[...].astype(BF16)
        xb_ref[...] = xb
        for j in range(N_DEV):
            sl = slice(npd * j, npd * (j + 1))
            h_ref[:, sl] = _dot(xb, w_ref[j]) + b_ref[:, sl]
        u0_ref[...] = h_ref[:, 0:D] * _sigmoid(h_ref[:, D:2 * D])

    row = lambda i: (i, 0)
    return pl.pallas_call(
        body, name="a_in_proj", grid=(s // tm,),
        in_specs=[pl.BlockSpec((tm, D), row), pl.BlockSpec(w_g.shape, lambda i: (0, 0, 0)),
                  pl.BlockSpec((1, 3 * D), lambda i: (0, 0))],
        out_specs=[pl.BlockSpec((tm, 3 * D), row), pl.BlockSpec((tm, D), row), pl.BlockSpec((tm, D), row)],
        out_shape=[_sds((s, 3 * D), F32), _sds((s, D), F32), _sds((s, D), BF16)],
        compiler_params=_params(44, 1),
    )(x, w_g, b_full)


CONV_HALO = 32
CONV_CHUNK = 32
SUBLANES = 8
COPY_ROWS = 56


def _shifted_copies(buf, shs, tm):
    n = tm + CONV_HALO - SUBLANES
    for s in range(1, SUBLANES):
        for c0 in range(0, n, COPY_ROWS):
            c1 = min(c0 + COPY_ROWS, n)
            shs[s - 1, c0:c1, :] = buf[c0 + s:c1 + s, :]


def _shifted_rows(buf, shs, start):
    a8, s = divmod(start, SUBLANES)
    if s == 0:
        return buf[start:start + CONV_CHUNK, :]
    return shs[s - 1, a8 * SUBLANES:a8 * SUBLANES + CONV_CHUNK, :]


def _a_conv_gate(u0, h, w_dw, b_dw, ln_g, ln_b, tm, plan=None):
    s = u0.shape[0]
    per = tm // CONV_HALO

    def body(u0_ref, halo_ref, z_ref, w_ref, bdw_ref, g_ref, b_ref, u1_ref, ga_ref, buf, shs):
        i = pl.program_id(0)
        buf[0:CONV_HALO, :] = jnp.where(i > 0, halo_ref[...], 0.0)
        buf[CONV_HALO:, :] = u0_ref[...]
        _shifted_copies(buf, shs, tm)
        off = CONV_HALO - (CONV_W - 1)
        for ch in range(tm // CONV_CHUNK):
            r0 = ch * CONV_CHUNK
            acc = jnp.broadcast_to(bdw_ref[...], (CONV_CHUNK, D))
            for j in range(CONV_W):
                acc = acc + w_ref[j:j + 1, :] * _shifted_rows(buf, shs, r0 + off + j)
            u1_ref[r0:r0 + CONV_CHUNK, :] = acc
            n, _ = _ln_stats(acc)
            pre = n * g_ref[...] + b_ref[...]
            z = z_ref[r0:r0 + CONV_CHUNK, :]
            ga_ref[r0:r0 + CONV_CHUNK, :] = ((pre * _sigmoid(pre)) * (z * _sigmoid(z))).astype(BF16)

    row = lambda i: (i, 0)
    vec = pl.BlockSpec((1, D), lambda i: (0, 0))
    steps = s // tm
    return _planned_call(
        plan, (u0, u0, h, w_dw, b_dw, ln_g, ln_b), body, name="a_conv_gate", grid=(steps,), mid_step=steps // 2,
        in_specs=[pl.BlockSpec((tm, D), row),
                  pl.BlockSpec((CONV_HALO, D), lambda i: (jnp.maximum(i * per - 1, 0), 0)),
                  pl.BlockSpec((tm, D), lambda i: (i, 2)),
                  pl.BlockSpec((32, D), lambda i: (0, 0)), vec, vec, vec],
        out_specs=[pl.BlockSpec((tm, D), row), pl.BlockSpec((tm, D), row)],
        out_shape=[_sds((s, D), F32), _sds((s, D), BF16)],
        scratch_shapes=[pltpu.VMEM((tm + CONV_HALO, D), F32), pltpu.VMEM((SUBLANES - 1, tm + CONV_HALO, D), F32)],
        compiler_params=_params(40, 1))


def _a_out_proj(ga, w, b, x, pg, pb, tm):
    s = x.shape[0]

    def body(ga_ref, w_ref, b_ref, x_ref, pg_ref, pb_ref, n_ref, rstd_ref, xb_ref):
        r = ALPHA * x_ref[...] + (_dot(ga_ref[...], w_ref[...]) + b_ref[...])
        n, rstd = _ln_stats(r)
        n_ref[...] = n
        rstd_ref[...] = rstd
        xb_ref[...] = (n * pg_ref[...] + pb_ref[...]).astype(BF16)

    row = lambda i: (i, 0)
    vec = pl.BlockSpec((1, D), lambda i: (0, 0))
    return pl.pallas_call(
        body, name="a_out_proj", grid=(s // tm,),
        in_specs=[pl.BlockSpec((tm, D), row), pl.BlockSpec((D, D), lambda i: (0, 0)), vec,
                  pl.BlockSpec((tm, D), row), vec, vec],
        out_specs=[pl.BlockSpec((tm, D), row), pl.BlockSpec((tm, 1), row), pl.BlockSpec((tm, D), row)],
        out_shape=[_sds((s, D), F32), _sds((s, 1), F32), _sds((s, D), BF16)],
        compiler_params=_params(32, 1),
    )(ga, w, b, x, pg, pb)


def _kv_proj(xb, w_g, tm):
    s = xb.shape[0]
    npd = w_g.shape[2]
    half = N_DEV // 2

    def body(x_ref, w_ref, k_ref, v_ref):
        xv = x_ref[...]
        for j in range(N_DEV):
            o_ref = k_ref if j < half else v_ref
            jj = j % half
            o_ref[:, npd * jj:npd * (jj + 1)] = _dot(xv, w_ref[j]).astype(BF16)

    row = lambda i: (i, 0)
    return pl.pallas_call(
        body, name="kv_proj", grid=(s // tm,),
        in_specs=[pl.BlockSpec((tm, D), row), pl.BlockSpec(w_g.shape, lambda i: (0, 0, 0))],
        out_specs=[pl.BlockSpec((tm, 3 * D), row), pl.BlockSpec((tm, 3 * D), row)],
        out_shape=[_sds((s, 3 * D), BF16), _sds((s, 3 * D), BF16)],
        compiler_params=_params(52, 1),
    )(xb, w_g)


def _b_in_proj(xb, w_g, tm):
    s = xb.shape[0]
    npd = w_g.shape[2]
    scale = HEAD_DIM ** -0.5

    def body(x_ref, w_ref, q_ref, z_ref):
        xv = x_ref[...]
        for j in range(N_DEV):
            hj = _dot(xv, w_ref[j])
            if j < 6:
                q_ref[:, npd * j:npd * (j + 1)] = (hj.astype(BF16) * scale).astype(BF16)
            else:
                z_ref[:, npd * (j - 6):npd * (j - 5)] = hj

    row = lambda i: (i, 0)
    return pl.pallas_call(
        body, name="b_in_proj", grid=(s // tm,),
        in_specs=[pl.BlockSpec((tm, D), row), pl.BlockSpec(w_g.shape, lambda i: (0, 0, 0))],
        out_specs=[pl.BlockSpec((tm, 3 * D), row), pl.BlockSpec((tm, D), row)],
        out_shape=[_sds((s, 3 * D), BF16), _sds((s, D), F32)],
        compiler_params=_params(44, 1),
    )(xb, w_g)


ATTN_NQ = {1: 8, 4: 2, 16: 1}
N_PAIRS = N_HEADS // 2


def _band_table(d):
    qi = np.arange(BLK)[:, None]
    kj = np.arange(2 * BLK)[None, :]
    dist = qi + BLK - kj
    ok = (dist >= 0) & (dist <= BLK)
    return jnp.asarray(np.where(ok, -(d * dist).astype(np.float32), np.float32(NEG)), dtype=F32)


def _slope_table():
    t = np.zeros((N_PAIRS, 8, 2 * BLK), np.float32)
    for h in range(N_HEADS):
        t[h // 2, h % 2, :] = SLOPES[h]
    return jnp.asarray(t)


def _head_masks():
    lane = lax.broadcasted_iota(jnp.int32, (1, BLK), 1)
    lo = (lane < HEAD_DIM).astype(BF16)
    return (lo, (1.0 - lo).astype(BF16))


def _pick_col(tile, lane, h):
    return jnp.sum(jnp.where(lane == h, tile, 0.0), axis=1, keepdims=True)


def _rows(base, n, d):
    return pl.ds(base, n) if d == 1 else pl.ds(base, n, stride=d)


def _attn_fwd(q, k_all, v_all, o_acc, lse_acc, z, g, first, last):
    d = DILATIONS[g]
    s = q.shape[0]
    nq = ATTN_NQ[d]
    halo = BLK * d
    tile = nq * halo
    assert s % tile == 0

    def body(*refs):
        q_ref, k_ref, kh_ref, v_ref, vh_ref, nd_ref, sl_ref = refs[:7]
        k = 7
        if not first:
            oa_ref, la_ref = refs[k:k + 2]
            k += 2
        if last:
            z_ref = refs[k]
            k += 1
        o_ref, l_ref = refs[k:k + 2]
        k += 2
        if last:
            gb_ref = refs[k]
            k += 1
        qf, kf, vf = refs[k:k + 3]
        if last:
            gf = refs[k + 3]
        n = pl.program_id(0)
        hp = pl.program_id(1)
        qf[...] = q_ref[...].astype(F32)
        kf[0:halo, :] = kh_ref[...].astype(F32)
        kf[halo:, :] = k_ref[...].astype(F32)
        vf[0:halo, :] = vh_ref[...].astype(F32)
        vf[halo:, :] = v_ref[...].astype(F32)

        @pl.when(hp == 0)
        def _():
            l_ref[...] = jnp.zeros(l_ref.shape, F32) if first else la_ref[...]

        col = lax.broadcasted_iota(jnp.int32, (2 * BLK, 2 * BLK), 1)
        lane = lax.broadcasted_iota(jnp.int32, (BLK, BLK), 1)
        masks = _head_masks()
        bias = jnp.concatenate([sl_ref[0, e:e + 1, :] * nd_ref[...] for e in range(2)], axis=0)
        bias0 = bias + jnp.where((n == 0) & (col < BLK), NEG, 0.0)
        for b in range(nq):
            for r in range(d):
                rq = _rows(b * halo + r, BLK, d)
                rk = _rows(b * halo + r, 2 * BLK, d)
                q2 = qf[rq, :].astype(BF16)
                kcat = kf[rk, :].astype(BF16)
                vcat = vf[rk, :].astype(BF16)
                lt = l_ref[rq, :]
                sc = _dot_nt(jnp.concatenate([q2 * masks[0], q2 * masks[1]], axis=0), kcat)
                sc = sc + (bias0 if b == 0 else bias)
                m = jnp.max(sc, axis=1, keepdims=True)
                p = jnp.exp(sc - m)
                l = jnp.sum(p, axis=1, keepdims=True)
                oh = _dot(p.astype(BF16), vcat) / l
                lse = m + jnp.log(l)
                if not first:
                    old = jnp.concatenate([_pick_col(lt, lane, 2 * hp + e) for e in range(2)], axis=0)
                    mx = jnp.maximum(old, lse)
                    new = mx + jnp.log(jnp.exp(old - mx) + jnp.exp(lse - mx))
                    keep = jnp.exp(old - new)
                    oh = oh * jnp.exp(lse - new)
                    lse = new
                o2 = jnp.where(lane < HEAD_DIM, oh[0:BLK], oh[BLK:])
                if not first:
                    o2 = o2 + oa_ref[rq, :] * jnp.where(lane < HEAD_DIM, keep[0:BLK], keep[BLK:])
                lt = jnp.where(lane == 2 * hp, lse[0:BLK], lt)
                lt = jnp.where(lane == 2 * hp + 1, lse[BLK:], lt)
                o_ref[rq, :] = o2
                l_ref[rq, :] = lt
                if last:
                    zz = z_ref[rq, :]
                    gf[rq, :] = o2 * (zz * _sigmoid(zz))
        if last:
            gb_ref[...] = gf[...].astype(BF16)

    col_blk = lambda n, hp: (n, g * N_PAIRS + hp)
    halo_blk = lambda n, hp: (jnp.maximum(n * nq - 1, 0), g * N_PAIRS + hp)
    own = pl.BlockSpec((tile, BLK), lambda n, hp: (n, hp))
    own_l = pl.BlockSpec((tile, BLK), lambda n, hp: (n, 0))
    in_specs = [pl.BlockSpec((tile, BLK), col_blk),
                pl.BlockSpec((tile, BLK), col_blk), pl.BlockSpec((halo, BLK), halo_blk),
                pl.BlockSpec((tile, BLK), col_blk), pl.BlockSpec((halo, BLK), halo_blk),
                pl.BlockSpec((BLK, 2 * BLK), lambda n, hp: (0, 0)),
                pl.BlockSpec((1, 8, 2 * BLK), lambda n, hp: (hp, 0, 0))]
    args = [q, k_all, k_all, v_all, v_all, _band_table(d), _slope_table()]
    if not first:
        in_specs += [own, own_l]
        args += [o_acc, lse_acc]
    if last:
        in_specs += [own]
        args += [z]
    out_specs = [own, own_l] + ([own] if last else [])
    out_shape = [_sds((s, D), F32), _sds((s, BLK), F32)] + ([_sds((s, D), BF16)] if last else [])
    scratch = [pltpu.VMEM((tile, BLK), F32), pltpu.VMEM((tile + halo, BLK), F32), pltpu.VMEM((tile + halo, BLK), F32)]
    if last:
        scratch.append(pltpu.VMEM((tile, BLK), F32))
    return pl.pallas_call(
        body, name=f"attn_fwd_g{g}", grid=(s // tile, N_PAIRS), in_specs=in_specs, out_specs=out_specs,
        out_shape=out_shape, scratch_shapes=scratch, compiler_params=_params(32, 2),
    )(*args)


ATTN_PRE = {1: 1, 4: 1, 16: 4}


def _regroup(src, dst, d1):
    n = src.shape[0] // d1
    for r1 in range(d1):
        dst[r1] = src[pl.ds(r1, n, stride=d1), :]


def _ungroup(src, dst, d1):
    n = dst.shape[0] // d1
    for r1 in range(d1):
        dst[pl.ds(r1, n, stride=d1), :] = src[r1]


def _grouped(shape, d1):
    return pltpu.VMEM((d1, shape[0] // d1, shape[1]), F32)


def _with_halo(halo_ref, tile_ref, b):
    if b == 0:
        return jnp.concatenate([halo_ref[...], tile_ref[0:BLK, :]], axis=0)
    return tile_ref[(b - 1) * BLK:(b + 1) * BLK, :]


def _attn_forward(q, k_all, v_all, o_acc, lse_acc, z, g, first, last):
    d = DILATIONS[g]
    s = q.shape[0]
    nq = ATTN_NQ[d]
    d1 = ATTN_PRE[d]
    d2 = d // d1
    halo = BLK * d
    tile = nq * halo
    assert s % tile == 0
    pre = d1 > 1

    def body(*refs):
        refs = list(refs)
        q_ref, k_ref, kh_ref, v_ref, vh_ref, nd_ref, sl_ref = refs[:7]
        del refs[:7]
        oa_ref, la_ref = (refs.pop(0), refs.pop(0)) if not first else (None, None)
        z_ref = refs.pop(0) if last else None
        o_ref, l_ref = refs.pop(0), refs.pop(0)
        gb_ref = refs.pop(0) if last else None
        qf, kf, vf = refs.pop(0), refs.pop(0), refs.pop(0)
        gf = refs.pop(0) if last else None
        q1 = k1 = v1 = l1 = o1 = oa1 = z1 = g1 = None
        if pre:
            q1, k1, v1, l1, o1 = (refs.pop(0) for _ in range(5))
            oa1 = refs.pop(0) if not first else None
            z1, g1 = (refs.pop(0), refs.pop(0)) if last else (None, None)
        n = pl.program_id(0)
        hp = pl.program_id(1)
        if d > 1:
            qf[...] = q_ref[...].astype(F32)
            kf[0:halo, :] = kh_ref[...].astype(F32)
            kf[halo:, :] = k_ref[...].astype(F32)
            vf[0:halo, :] = vh_ref[...].astype(F32)
            vf[halo:, :] = v_ref[...].astype(F32)

        @pl.when(hp == 0)
        def _():
            l_ref[...] = jnp.zeros(l_ref.shape, F32) if first else la_ref[...]

        if pre:
            for src, dst in ((qf, q1), (kf, k1), (vf, v1), (l_ref, l1), (oa_ref, oa1), (z_ref, z1)):
                if src is not None:
                    _regroup(src, dst, d1)

        def pick(nat, grp, r1):
            return grp.at[r1] if pre else nat

        def halves(colv):
            return jnp.where(lane < HEAD_DIM, colv[0:BLK], colv[BLK:])

        def put(tile_v, colv):
            tile_v = jnp.where(lane == 2 * hp, colv[0:BLK], tile_v)
            return jnp.where(lane == 2 * hp + 1, colv[BLK:], tile_v)

        col = lax.broadcasted_iota(jnp.int32, (2 * BLK, 2 * BLK), 1)
        lane = lax.broadcasted_iota(jnp.int32, (BLK, BLK), 1)
        masks = _head_masks()
        bias = jnp.concatenate([sl_ref[0, e:e + 1, :] * nd_ref[...] for e in range(2)], axis=0)
        bias0 = bias + jnp.where((n == 0) & (col < BLK), NEG, 0.0)
        for b in range(nq):
            for r in range(d):
                r1, r2 = r % d1, r // d1
                rq = _rows(b * (halo // d1) + r2, BLK, d2)
                rk = _rows(b * (halo // d1) + r2, 2 * BLK, d2)
                if d > 1:
                    q2 = pick(qf, q1, r1)[rq, :].astype(BF16)
                    kcat = pick(kf, k1, r1)[rk, :].astype(BF16)
                    vcat = pick(vf, v1, r1)[rk, :].astype(BF16)
                else:
                    q2 = q_ref[rq, :]
                    kcat = _with_halo(kh_ref, k_ref, b)
                    vcat = _with_halo(vh_ref, v_ref, b)
                sc = _dot_nt(jnp.concatenate([q2 * masks[0], q2 * masks[1]], axis=0), kcat)
                sc = sc + (bias0 if b == 0 else bias)
                m = jnp.max(sc, axis=1, keepdims=True)
                p = jnp.exp(sc - m)
                l = jnp.sum(p, axis=1, keepdims=True)
                oh = _dot(p.astype(BF16), vcat) / l
                lse = m + jnp.log(l)
                lt = pick(l_ref, l1, r1)[rq, :]
                if not first:
                    old = jnp.concatenate([_pick_col(lt, lane, 2 * hp + e) for e in range(2)], axis=0)
                    mx = jnp.maximum(old, lse)
                    new = mx + jnp.log(jnp.exp(old - mx) + jnp.exp(lse - mx))
                    o2 = halves(oh) * halves(jnp.exp(lse - new)) + pick(oa_ref, oa1, r1)[rq, :] * halves(jnp.exp(old - new))
                    lse = new
                else:
                    o2 = halves(oh)
                pick(o_ref, o1, r1)[rq, :] = o2
                pick(l_ref, l1, r1)[rq, :] = put(lt, lse)
                if last:
                    zz = pick(z_ref, z1, r1)[rq, :]
                    pick(gf, g1, r1)[rq, :] = o2 * (zz * _sigmoid(zz))
        if pre:
            _ungroup(o1, o_ref, d1)
            _ungroup(l1, l_ref, d1)
            if last:
                _ungroup(g1, gf, d1)
        if last:
            gb_ref[...] = gf[...].astype(BF16)

    col_blk = lambda n, hp: (n, g * N_PAIRS + hp)
    halo_blk = lambda n, hp: (jnp.maximum(n * nq - 1, 0), g * N_PAIRS + hp)
    own = pl.BlockSpec((tile, BLK), lambda n, hp: (n, hp))
    own_l = pl.BlockSpec((tile, BLK), lambda n, hp: (n, 0))
    in_specs = [pl.BlockSpec((tile, BLK), col_blk),
                pl.BlockSpec((tile, BLK), col_blk), pl.BlockSpec((halo, BLK), halo_blk),
                pl.BlockSpec((tile, BLK), col_blk), pl.BlockSpec((halo, BLK), halo_blk),
                pl.BlockSpec((BLK, 2 * BLK), lambda n, hp: (0, 0)),
                pl.BlockSpec((1, 8, 2 * BLK), lambda n, hp: (hp, 0, 0))]
    args = [q, k_all, k_all, v_all, v_all, _band_table(d), _slope_table()]
    if not first:
        in_specs += [own, own_l]
        args += [o_acc, lse_acc]
    if last:
        in_specs += [own]
        args += [z]
    out_specs = [own, own_l] + ([own] if last else [])
    out_shape = [_sds((s, D), F32), _sds((s, BLK), F32)] + ([_sds((s, D), BF16)] if last else [])
    t_shape, w_shape = (tile, BLK), (tile + halo, BLK)
    scratch = [pltpu.VMEM(t_shape, F32), pltpu.VMEM(w_shape, F32), pltpu.VMEM(w_shape, F32)]
    if last:
        scratch.append(pltpu.VMEM(t_shape, F32))
    if pre:
        scratch += [_grouped(t_shape, d1), _grouped(w_shape, d1), _grouped(w_shape, d1)]
        scratch += [_grouped(t_shape, d1)] * (2 + (0 if first else 1) + (2 if last else 0))
    return pl.pallas_call(
        body, name=f"attn_fwd_g{g}", grid=(s // tile, N_PAIRS), in_specs=in_specs, out_specs=out_specs,
        out_shape=out_shape, scratch_shapes=scratch, compiler_params=_params(56, 2),
    )(*args)


def _attn_backward(q, k_all, v_all, do, lse, dd, dhb, dk_all, dv_all, g):
    d = DILATIONS[g]
    s = q.shape[0]
    nq = ATTN_NQ[d]
    d1 = ATTN_PRE[d]
    d2 = d // d1
    halo = BLK * d
    tile = nq * halo
    nt = s // tile
    first = dk_all is None
    pre = d1 > 1

    def body(*refs):
        refs = list(refs)
        q_ref, k_ref, kh_ref, v_ref, vh_ref, nd_ref, sl_ref, do_ref, l_ref, dd_ref = refs[:10]
        del refs[:10 + (1 if first else 3)]
        dq_ref, dk_ref, dv_ref = refs[:3]
        qf, dof, kf, vf, dqf, dkf, dvf, ck, cv = refs[3:12]
        del refs[:12]
        if pre:
            q1, do1, k1, v1, l1, dd1, dq1, dk1, dv1 = refs
        else:
            q1 = do1 = k1 = v1 = l1 = dd1 = dq1 = None
            dk1, dv1 = dkf, dvf
        hp = pl.program_id(0)
        n = pl.program_id(1)

        @pl.when(n == 0)
        def _():
            ck[...] = jnp.zeros(ck.shape, F32)
            cv[...] = jnp.zeros(cv.shape, F32)

        dk1[...] = jnp.zeros(dk1.shape, F32)
        dv1[...] = jnp.zeros(dv1.shape, F32)

        def pick(nat, grp, r1):
            return grp.at[r1] if pre else nat

        @pl.when(n < nt)
        def _():
            if d > 1:
                qf[...] = q_ref[...].astype(F32)
                dof[...] = do_ref[...].astype(F32)
                kf[0:halo, :] = kh_ref[...].astype(F32)
                kf[halo:, :] = k_ref[...].astype(F32)
                vf[0:halo, :] = vh_ref[...].astype(F32)
                vf[halo:, :] = v_ref[...].astype(F32)
            if pre:
                for src, dst in ((qf, q1), (dof, do1), (kf, k1), (vf, v1), (l_ref, l1), (dd_ref, dd1)):
                    _regroup(src, dst, d1)
            col = lax.broadcasted_iota(jnp.int32, (2 * BLK, 2 * BLK), 1)
            lane = lax.broadcasted_iota(jnp.int32, (BLK, BLK), 1)
            masks = _head_masks()
            bias = jnp.concatenate([sl_ref[0, e:e + 1, :] * nd_ref[...] for e in range(2)], axis=0)
            bias0 = bias + jnp.where((n == 0) & (col < BLK), NEG, 0.0)
            for b in range(nq):
                for r in range(d):
                    r1, r2 = r % d1, r // d1
                    rq = _rows(b * (halo // d1) + r2, BLK, d2)
                    rk = _rows(b * (halo // d1) + r2, 2 * BLK, d2)
                    if d > 1:
                        q2 = pick(qf, q1, r1)[rq, :].astype(BF16)
                        do2 = pick(dof, do1, r1)[rq, :].astype(BF16)
                        kcat = pick(kf, k1, r1)[rk, :].astype(BF16)
                        vcat = pick(vf, v1, r1)[rk, :].astype(BF16)
                    else:
                        q2 = q_ref[rq, :]
                        do2 = do_ref[rq, :]
                        kcat = _with_halo(kh_ref, k_ref, b)
                        vcat = _with_halo(vh_ref, v_ref, b)
                    lt = pick(l_ref, l1, r1)[rq, :]
                    dt = pick(dd_ref, dd1, r1)[rq, :]
                    qs = jnp.concatenate([q2 * masks[0], q2 * masks[1]], axis=0)
                    dos = jnp.concatenate([do2 * masks[0], do2 * masks[1]], axis=0)
                    lcol = jnp.concatenate([_pick_col(lt, lane, 2 * hp + e) for e in range(2)], axis=0)
                    dcol = jnp.concatenate([_pick_col(dt, lane, 2 * hp + e) for e in range(2)], axis=0)
                    sc = _dot_nt(qs, kcat) + (bias0 if b == 0 else bias)
                    p = jnp.exp(sc - lcol)
                    ds = p * (_dot_nt(dos, vcat) - dcol)
                    dsb = ds.astype(BF16)
                    dq = _dot(dsb, kcat)
                    dq2 = (HEAD_DIM ** -0.5) * jnp.where(lane < HEAD_DIM, dq[0:BLK], dq[BLK:])
                    pick(dqf, dq1, r1)[rq, :] = dq2
                    pick(dkf, dk1, r1)[rk, :] += _dot(dsb.T, qs)
                    pick(dvf, dv1, r1)[rk, :] += _dot(p.astype(BF16).T, dos)
            if pre:
                _ungroup(dq1, dqf, d1)
            dq_ref[...] = dqf[...].astype(BF16)

        if pre:
            _ungroup(dk1, dkf, d1)
            _ungroup(dv1, dvf, d1)
        if tile > halo:
            dk_ref[0:tile - halo, :] = ck[0:tile - halo, :].astype(BF16)
            dv_ref[0:tile - halo, :] = cv[0:tile - halo, :].astype(BF16)
        dk_ref[tile - halo:, :] = (ck[tile - halo:, :] + dkf[0:halo, :]).astype(BF16)
        dv_ref[tile - halo:, :] = (cv[tile - halo:, :] + dvf[0:halo, :]).astype(BF16)
        ck[...] = dkf[halo:, :]
        cv[...] = dvf[halo:, :]

    cur = lambda n: jnp.minimum(n, nt - 1)
    col_blk = lambda hp, n: (cur(n), g * N_PAIRS + hp)
    halo_blk = lambda hp, n: (jnp.maximum(cur(n) * nq - 1, 0), g * N_PAIRS + hp)
    out_kv = lambda hp, n: (jnp.maximum(n - 1, 0), g * N_PAIRS + hp)
    small = pl.BlockSpec((tile, BLK), lambda hp, n: (cur(n), 0))
    hbm = pl.BlockSpec(memory_space=pl.ANY)
    in_specs = [pl.BlockSpec((tile, BLK), col_blk),
                pl.BlockSpec((tile, BLK), col_blk), pl.BlockSpec((halo, BLK), halo_blk),
                pl.BlockSpec((tile, BLK), col_blk), pl.BlockSpec((halo, BLK), halo_blk),
                pl.BlockSpec((BLK, 2 * BLK), lambda hp, n: (0, 0)),
                pl.BlockSpec((1, 8, 2 * BLK), lambda hp, n: (hp, 0, 0)),
                pl.BlockSpec((tile, BLK), lambda hp, n: (cur(n), hp)), small, small, hbm]
    args = [q, k_all, k_all, v_all, v_all, _band_table(d), _slope_table(), do, lse, dd, dhb]
    aliases = {10: 0}
    if not first:
        in_specs += [hbm, hbm]
        args += [dk_all, dv_all]
        aliases.update({11: 1, 12: 2})
    t_shape, w_shape = (tile, BLK), (tile + halo, BLK)
    tile_f32, wide_f32 = pltpu.VMEM(t_shape, F32), pltpu.VMEM(w_shape, F32)
    scratch = [tile_f32, tile_f32, wide_f32, wide_f32, tile_f32, wide_f32, wide_f32, tile_f32, tile_f32]
    if pre:
        tg, wg = _grouped(t_shape, d1), _grouped(w_shape, d1)
        scratch += [tg, tg, wg, wg, tg, tg, tg, wg, wg]
    return pl.pallas_call(
        body, name=f"attn_bwd_g{g}", grid=(N_PAIRS, nt + 1), in_specs=in_specs,
        out_specs=[pl.BlockSpec((tile, BLK), col_blk), pl.BlockSpec((tile, BLK), out_kv),
                   pl.BlockSpec((tile, BLK), out_kv)],
        out_shape=[_sds((s, 4 * D), BF16), _sds((s, 3 * D), BF16), _sds((s, 3 * D), BF16)],
        scratch_shapes=scratch, input_output_aliases=aliases, compiler_params=_params(48, 2),
    )(*args)


def _b_out_loss(gb, w, b, n1, pg0, pb0, pg1, pb1, tgt, tm):
    s = gb.shape[0]
    last = s // tm - 1

    def body(gb_ref, w_ref, b_ref, n1_ref, pg0_ref, pb0_ref, pg1_ref, pb1_ref, t_ref,
             dr_ref, drb_ref, loss_ref, dpg_ref, dpb_ref, dbo_ref):
        i = pl.program_id(0)
        _acc_init(i, loss_ref, dpg_ref, dpb_ref, dbo_ref)
        x1 = n1_ref[...] * pg0_ref[...] + pb0_ref[...]
        r = ALPHA * x1 + (_dot(gb_ref[...], w_ref[...]) + b_ref[...])
        n, rstd = _ln_stats(r)
        err = (n * pg1_ref[...] + pb1_ref[...]) - t_ref[...]
        loss_ref[...] += _rowsum8(err * err)
        dx2 = err * (1.0 / D)
        dpg_ref[...] += _rowsum8(dx2 * n)
        dpb_ref[...] += _rowsum8(dx2)
        dr = _ln_bwd(dx2 * pg1_ref[...], n, rstd)
        dr_ref[...] = dr
        drb_ref[...] = dr.astype(BF16)
        dbo_ref[...] += _rowsum8(dr)
        _acc_finish(i, last, dpg_ref, dpb_ref, dbo_ref)

        @pl.when(i == last)
        def _():
            loss_ref[...] = jnp.broadcast_to((0.5 / D) * jnp.sum(loss_ref[...], keepdims=True), loss_ref.shape)

    row = lambda i: (i, 0)
    vec = pl.BlockSpec((1, D), lambda i: (0, 0))
    acc = pl.BlockSpec((8, D), lambda i: (0, 0))
    return pl.pallas_call(
        body, name="b_out_loss", grid=(s // tm,),
        in_specs=[pl.BlockSpec((tm, D), row), pl.BlockSpec((D, D), lambda i: (0, 0)), vec,
                  pl.BlockSpec((tm, D), row), vec, vec, vec, vec, pl.BlockSpec((tm, D), row)],
        out_specs=[pl.BlockSpec((tm, D), row), pl.BlockSpec((tm, D), row), acc, acc, acc, acc],
        out_shape=[_sds((s, D), F32), _sds((s, D), BF16)] + [_sds((8, D), F32)] * 4,
        compiler_params=_params(36, 1),
    )(gb, w, b, n1, pg0, pb0, pg1, pb1, tgt)


def _head_selector():
    sel = (np.arange(D)[:, None] // HEAD_DIM == np.arange(BLK)[None, :]).astype(np.float32)
    return jnp.asarray(sel, dtype=BF16)


def _b_out_bwd(drb, w, z, o, tm):
    s = drb.shape[0]

    def body(dr_ref, w_ref, z_ref, o_ref, sel_ref, do_ref, dh_ref, dd_ref):
        dg = _dot_nt(dr_ref[...], w_ref[...])
        zz = z_ref[...]
        sg = _sigmoid(zz)
        do = dg * (zz * sg)
        ov = o_ref[...]
        do_ref[...] = do.astype(BF16)
        dh_ref[...] = (dg * ov * _dsilu(zz, sg)).astype(BF16)
        prod = do * ov
        hi = prod.astype(BF16)
        lo = (prod - hi.astype(F32)).astype(BF16)
        dd_ref[...] = _dot(hi, sel_ref[...]) + _dot(lo, sel_ref[...])

    row = lambda i: (i, 0)
    return pl.pallas_call(
        body, name="b_out_bwd", grid=(s // tm,),
        in_specs=[pl.BlockSpec((tm, D), row), pl.BlockSpec((D, D), lambda i: (0, 0)),
                  pl.BlockSpec((tm, D), row), pl.BlockSpec((tm, D), row), pl.BlockSpec((D, BLK), lambda i: (0, 0))],
        out_specs=[pl.BlockSpec((tm, D), row), pl.BlockSpec((tm, D), lambda i: (i, 3)),
                   pl.BlockSpec((tm, BLK), row)],
        out_shape=[_sds((s, D), BF16), _sds((s, 4 * D), BF16), _sds((s, BLK), F32)],
        compiler_params=_params(36, 1),
    )(drb, w, z, o, _head_selector())


def _attn_bwd(q, k_all, v_all, do, lse, dd, dhb, dk_all, dv_all, g):
    d = DILATIONS[g]
    s = q.shape[0]
    nq = ATTN_NQ[d]
    halo = BLK * d
    tile = nq * halo
    nt = s // tile
    first = dk_all is None

    def body(*refs):
        q_ref, k_ref, kh_ref, v_ref, vh_ref, nd_ref, sl_ref, do_ref, l_ref, dd_ref = refs[:10]
        k = 10 + (1 if first else 3)
        dq_ref, dk_ref, dv_ref = refs[k:k + 3]
        qf, dof, kf, vf, dqf, dkf, dvf, ck, cv = refs[k + 3:k + 12]
        hp = pl.program_id(0)
        n = pl.program_id(1)

        @pl.when(n == 0)
        def _():
            ck[...] = jnp.zeros(ck.shape, F32)
            cv[...] = jnp.zeros(cv.shape, F32)

        dkf[...] = jnp.zeros(dkf.shape, F32)
        dvf[...] = jnp.zeros(dvf.shape, F32)

        @pl.when(n < nt)
        def _():
            qf[...] = q_ref[...].astype(F32)
            dof[...] = do_ref[...].astype(F32)
            kf[0:halo, :] = kh_ref[...].astype(F32)
            kf[halo:, :] = k_ref[...].astype(F32)
            vf[0:halo, :] = vh_ref[...].astype(F32)
            vf[halo:, :] = v_ref[...].astype(F32)
            col = lax.broadcasted_iota(jnp.int32, (2 * BLK, 2 * BLK), 1)
            lane = lax.broadcasted_iota(jnp.int32, (BLK, BLK), 1)
            masks = _head_masks()
            bias = jnp.concatenate([sl_ref[0, e:e + 1, :] * nd_ref[...] for e in range(2)], axis=0)
            bias0 = bias + jnp.where((n == 0) & (col < BLK), NEG, 0.0)
            for b in range(nq):
                for r in range(d):
                    rq = _rows(b * halo + r, BLK, d)
                    rk = _rows(b * halo + r, 2 * BLK, d)
                    q2 = qf[rq, :].astype(BF16)
                    do2 = dof[rq, :].astype(BF16)
                    kcat = kf[rk, :].astype(BF16)
                    vcat = vf[rk, :].astype(BF16)
                    lt = l_ref[rq, :]
                    dt = dd_ref[rq, :]
                    qs = jnp.concatenate([q2 * masks[0], q2 * masks[1]], axis=0)
                    dos = jnp.concatenate([do2 * masks[0], do2 * masks[1]], axis=0)
                    lcol = jnp.concatenate([_pick_col(lt, lane, 2 * hp + e) for e in range(2)], axis=0)
                    dcol = jnp.concatenate([_pick_col(dt, lane, 2 * hp + e) for e in range(2)], axis=0)
                    sc = _dot_nt(qs, kcat) + (bias0 if b == 0 else bias)
                    p = jnp.exp(sc - lcol)
                    ds = p * (_dot_nt(dos, vcat) - dcol)
                    dsb = ds.astype(BF16)
                    dq = _dot(dsb, kcat)
                    dqf[rq, :] = (HEAD_DIM ** -0.5) * jnp.where(lane < HEAD_DIM, dq[0:BLK], dq[BLK:])
                    dkf[rk, :] += _dot(dsb.T, qs)
                    dvf[rk, :] += _dot(p.astype(BF16).T, dos)
            dq_ref[...] = dqf[...].astype(BF16)

        if tile > halo:
            dk_ref[0:tile - halo, :] = ck[0:tile - halo, :].astype(BF16)
            dv_ref[0:tile - halo, :] = cv[0:tile - halo, :].astype(BF16)
        dk_ref[tile - halo:, :] = (ck[tile - halo:, :] + dkf[0:halo, :]).astype(BF16)
        dv_ref[tile - halo:, :] = (cv[tile - halo:, :] + dvf[0:halo, :]).astype(BF16)
        ck[...] = dkf[halo:, :]
        cv[...] = dvf[halo:, :]

    cur = lambda n: jnp.minimum(n, nt - 1)
    col_blk = lambda hp, n: (cur(n), g * N_PAIRS + hp)
    halo_blk = lambda hp, n: (jnp.maximum(cur(n) * nq - 1, 0), g * N_PAIRS + hp)
    out_kv = lambda hp, n: (jnp.maximum(n - 1, 0), g * N_PAIRS + hp)
    small = pl.BlockSpec((tile, BLK), lambda hp, n: (cur(n), 0))
    hbm = pl.BlockSpec(memory_space=pl.ANY)
    in_specs = [pl.BlockSpec((tile, BLK), col_blk),
                pl.BlockSpec((tile, BLK), col_blk), pl.BlockSpec((halo, BLK), halo_blk),
                pl.BlockSpec((tile, BLK), col_blk), pl.BlockSpec((halo, BLK), halo_blk),
                pl.BlockSpec((BLK, 2 * BLK), lambda hp, n: (0, 0)),
                pl.BlockSpec((1, 8, 2 * BLK), lambda hp, n: (hp, 0, 0)),
                pl.BlockSpec((tile, BLK), lambda hp, n: (cur(n), hp)), small, small, hbm]
    args = [q, k_all, k_all, v_all, v_all, _band_table(d), _slope_table(), do, lse, dd, dhb]
    aliases = {10: 0}
    if not first:
        in_specs += [hbm, hbm]
        args += [dk_all, dv_all]
        aliases.update({11: 1, 12: 2})
    tile_f32 = pltpu.VMEM((tile, BLK), F32)
    wide_f32 = pltpu.VMEM((tile + halo, BLK), F32)
    return pl.pallas_call(
        body, name=f"attn_bwd_g{g}", grid=(N_PAIRS, nt + 1), in_specs=in_specs,
        out_specs=[pl.BlockSpec((tile, BLK), col_blk), pl.BlockSpec((tile, BLK), out_kv),
                   pl.BlockSpec((tile, BLK), out_kv)],
        out_shape=[_sds((s, 4 * D), BF16), _sds((s, 3 * D), BF16), _sds((s, 3 * D), BF16)],
        scratch_shapes=[tile_f32, tile_f32, wide_f32, wide_f32, tile_f32, wide_f32, wide_f32, tile_f32, tile_f32],
        input_output_aliases=aliases, compiler_params=_params(40, 2),
    )(*args)


def _b_in_bwd(dr2, dhb, dk_all, dv_all, wb_g, wkv_g, n1, rstd1, pg0, tm):
    s = dr2.shape[0]
    last = s // tm - 1
    nb_, nkv = wb_g.shape[2], wkv_g.shape[2]
    half = N_DEV // 2

    def body(dr2_ref, dh_ref, dk_ref, dv_ref, wb_hbm, wkv_hbm, n_ref, rstd_ref, pg_ref,
             dr_ref, drb_ref, dpg_ref, dpb_ref, dbo_ref, wb, wkv):
        i = pl.program_id(0)

        @pl.when(i == 0)
        def _():
            pltpu.sync_copy(wb_hbm, wb)
            pltpu.sync_copy(wkv_hbm, wkv)

        _acc_init(i, dpg_ref, dpb_ref, dbo_ref)
        acc = ALPHA * dr2_ref[...]
        for j in range(N_DEV):
            acc = acc + _dot_nt(dh_ref[:, nb_ * j:nb_ * (j + 1)], wb[j])
            src = dk_ref if j < half else dv_ref
            jj = j % half
            acc = acc + _dot_nt(src[:, nkv * jj:nkv * (jj + 1)], wkv[j])
        n = n_ref[...]
        dpg_ref[...] += _rowsum8(acc * n)
        dpb_ref[...] += _rowsum8(acc)
        dr = _ln_bwd(acc * pg_ref[...], n, rstd_ref[...])
        dr_ref[...] = dr
        drb_ref[...] = dr.astype(BF16)
        dbo_ref[...] += _rowsum8(dr)
        _acc_finish(i, last, dpg_ref, dpb_ref, dbo_ref)

    row = lambda i: (i, 0)
    hbm = pl.BlockSpec(memory_space=pl.ANY)
    acc_spec = pl.BlockSpec((8, D), lambda i: (0, 0))
    return pl.pallas_call(
        body, name="b_in_bwd", grid=(s // tm,),
        in_specs=[pl.BlockSpec((tm, D), row), pl.BlockSpec((tm, 4 * D), row), pl.BlockSpec((tm, 3 * D), row),
                  pl.BlockSpec((tm, 3 * D), row), hbm, hbm, pl.BlockSpec((tm, D), row), pl.BlockSpec((tm, 1), row),
                  pl.BlockSpec((1, D), lambda i: (0, 0))],
        out_specs=[pl.BlockSpec((tm, D), row), pl.BlockSpec((tm, D), row), acc_spec, acc_spec, acc_spec],
        out_shape=[_sds((s, D), F32), _sds((s, D), BF16)] + [_sds((8, D), F32)] * 3,
        scratch_shapes=[pltpu.VMEM(wb_g.shape, BF16), pltpu.VMEM(wkv_g.shape, BF16)],
        compiler_params=_params(56, 1),
    )(dr2, dhb, dk_all, dv_all, wb_g, wkv_g, n1, rstd1, pg0)


def _a_out_bwd(drb, w, u1, h, ln_g, ln_b, tm, plan=None):
    s = drb.shape[0]
    last = s // tm - 1

    def body(dr_ref, w_ref, u1_ref, z_ref, g_ref, b_ref, du1_ref, dh_ref, dg_ref, db_ref, dbz_ref):
        i = pl.program_id(0)
        _acc_init(i, dg_ref, db_ref, dbz_ref)
        dga = _dot_nt(dr_ref[...], w_ref[...])
        n, rstd = _ln_stats(u1_ref[...])
        pre = n * g_ref[...] + b_ref[...]
        sp = _sigmoid(pre)
        zz = z_ref[...]
        sz = _sigmoid(zz)
        dz = dga * (pre * sp) * _dsilu(zz, sz)
        dh_ref[...] = dz.astype(BF16)
        dbz_ref[...] += _rowsum8(dz)
        dpre = dga * (zz * sz) * _dsilu(pre, sp)
        dg_ref[...] += _rowsum8(dpre * n)
        db_ref[...] += _rowsum8(dpre)
        du1_ref[...] = _ln_bwd(dpre * g_ref[...], n, rstd)
        _acc_finish(i, last, dg_ref, db_ref, dbz_ref)

    row = lambda i: (i, 0)
    vec = pl.BlockSpec((1, D), lambda i: (0, 0))
    acc_spec = pl.BlockSpec((8, D), lambda i: (0, 0))
    return _planned_call(
        plan, (drb, w, u1, h, ln_g, ln_b), body, name="a_out_bwd", grid=(s // tm,),
        in_specs=[pl.BlockSpec((tm, D), row), pl.BlockSpec((D, D), lambda i: (0, 0)), pl.BlockSpec((tm, D), row),
                  pl.BlockSpec((tm, D), lambda i: (i, 2)), vec, vec],
        out_specs=[pl.BlockSpec((tm, D), row), pl.BlockSpec((tm, D), lambda i: (i, 2)),
                   acc_spec, acc_spec, acc_spec],
        out_shape=[_sds((s, D), F32), _sds((s, 3 * D), BF16)] + [_sds((8, D), F32)] * 3,
        compiler_params=_params(32, 1))


def _a_conv_bwd(du1, u0, h, dha, w_dw, tm, plan=None):
    s = du1.shape[0]
    steps = s // tm
    per = tm // CONV_HALO
    pad = CONV_W - 1

    def body(du_ref, dun_ref, u0_ref, u0p_ref, h_ref, w_ref, dha_hbm,
             dh_ref, dw_ref, dbdw_ref, dba_ref, dbg_ref, dbuf, ubuf, wacc, dshs, ushs):
        i = pl.program_id(0)
        _acc_init(i, dbdw_ref, dba_ref, dbg_ref, wacc)
        dbuf[0:tm, :] = du_ref[...]
        dbuf[tm:, :] = jnp.where(i < steps - 1, dun_ref[...], 0.0)
        ubuf[0:CONV_HALO, :] = jnp.where(i > 0, u0p_ref[...], 0.0)
        ubuf[CONV_HALO:, :] = u0_ref[...]
        _shifted_copies(dbuf, dshs, tm)
        _shifted_copies(ubuf, ushs, tm)
        off = CONV_HALO - pad
        for ch in range(tm // CONV_CHUNK):
            r0 = ch * CONV_CHUNK
            duc = du_ref[r0:r0 + CONV_CHUNK, :]
            acc = jnp.zeros((CONV_CHUNK, D), F32)
            for j in range(CONV_W):
                acc = acc + w_ref[j:j + 1, :] * _shifted_rows(dbuf, dshs, r0 + pad - j)
                wacc[j] += _rowsum8(duc * _shifted_rows(ubuf, ushs, r0 + off + j))
            sg = _sigmoid(h_ref[r0:r0 + CONV_CHUNK, D:2 * D])
            da = acc * sg
            dag = acc * h_ref[r0:r0 + CONV_CHUNK, 0:D] * (sg * (1.0 - sg))
            dh_ref[r0:r0 + CONV_CHUNK, 0:D] = da.astype(BF16)
            dh_ref[r0:r0 + CONV_CHUNK, D:2 * D] = dag.astype(BF16)
            dbdw_ref[...] += _rowsum8(duc)
            dba_ref[...] += _rowsum8(da)
            dbg_ref[...] += _rowsum8(dag)
        _acc_finish(i, steps - 1, dbdw_ref, dba_ref, dbg_ref)

        @pl.when(i == steps - 1)
        def _():
            for j in range(CONV_W):
                dw_ref[j:j + 1, :] = jnp.sum(wacc[j], axis=0, keepdims=True)
            dw_ref[CONV_W:, :] = jnp.zeros((32 - CONV_W, D), F32)

    row = lambda i: (i, 0)
    acc_spec = pl.BlockSpec((8, D), lambda i: (0, 0))
    return _planned_call(
        plan, (du1, du1, u0, u0, h, w_dw, dha), body, name="a_conv_bwd", grid=(steps,),
        in_specs=[pl.BlockSpec((tm, D), row),
                  pl.BlockSpec((CONV_HALO, D), lambda i: (jnp.minimum((i + 1) * per, s // CONV_HALO - 1), 0)),
                  pl.BlockSpec((tm, D), row),
                  pl.BlockSpec((CONV_HALO, D), lambda i: (jnp.maximum(i * per - 1, 0), 0)),
                  pl.BlockSpec((tm, 2 * D), lambda i: (i, 0)),
                  pl.BlockSpec((32, D), lambda i: (0, 0)), pl.BlockSpec(memory_space=pl.ANY)],
        out_specs=[pl.BlockSpec((tm, 2 * D), lambda i: (i, 0)),
                   pl.BlockSpec((32, D), lambda i: (0, 0)), acc_spec, acc_spec, acc_spec],
        out_shape=[_sds((s, 3 * D), BF16), _sds((32, D), F32)] + [_sds((8, D), F32)] * 3,
        scratch_shapes=[pltpu.VMEM((tm + CONV_HALO, D), F32), pltpu.VMEM((tm + CONV_HALO, D), F32),
                        pltpu.VMEM((CONV_W, 8, D), F32),
                        pltpu.VMEM((SUBLANES - 1, tm + CONV_HALO, D), F32),
                        pltpu.VMEM((SUBLANES - 1, tm + CONV_HALO, D), F32)],
        input_output_aliases={6: 0}, compiler_params=_params(56, 1))


def _a_in_bwd(dr1, dha, w_g, tm, plan=None):
    s = dr1.shape[0]
    npd = w_g.shape[2]

    def body(dr_ref, dh_ref, w_ref, o_ref):
        acc = ALPHA * dr_ref[...]
        for j in range(N_DEV):
            acc = acc + _dot_nt(dh_ref[:, npd * j:npd * (j + 1)], w_ref[j])
        o_ref[...] = acc

    row = lambda i: (i, 0)
    return _planned_call(
        plan, (dr1, dha, w_g), body, name="a_in_bwd", grid=(s // tm,),
        in_specs=[pl.BlockSpec((tm, D), row), pl.BlockSpec((tm, 3 * D), row),
                  pl.BlockSpec(w_g.shape, lambda i: (0, 0, 0))],
        out_specs=[pl.BlockSpec((tm, D), row)], out_shape=[_sds((s, D), F32)],
        compiler_params=_params(36, 1))


def _wgrad(name, a, b, npd, ts, total=None, at=0, into=None):
    s = a.shape[0]
    n_blk = b.shape[1] // npd
    total = n_blk if total is None else total
    assert at % n_blk == 0

    def body(*refs):
        a_ref, b_ref = refs[:2]
        o_ref = refs[-1]
        si = pl.program_id(0)

        @pl.when(si == 0)
        def _():
            o_ref[...] = jnp.zeros(o_ref.shape, F32)

        a_t = a_ref[...].T
        for j in range(n_blk):
            o_ref[j] += _dot(a_t, b_ref[:, npd * j:npd * (j + 1)])

    in_specs = [pl.BlockSpec((ts, D), lambda si: (si, 0)), pl.BlockSpec((ts, n_blk * npd), lambda si: (si, 0))]
    args = [a, b]
    aliases = {}
    if into is not None:
        in_specs.append(pl.BlockSpec(memory_space=pl.ANY))
        args.append(into)
        aliases = {2: 0}
    return pl.pallas_call(
        body, name=name, grid=(s // ts,), in_specs=in_specs,
        out_specs=pl.BlockSpec((n_blk, D, npd), lambda si: (at // n_blk, 0, 0)),
        out_shape=_sds((total, D, npd), F32), input_output_aliases=aliases,
        compiler_params=_params(56, 1),
    )(*args)


SMALL_ROWS = 40
GRAD_ROWS = 48


def kernel(x, a_w_in, a_b_in, a_w_dw, a_b_dw, a_ln_g, a_ln_b, a_w_out, a_b_out, kv_w, b_w_in, b_w_out, b_b_out, post_ln_g, post_ln_b, loss_target, m_a_w_in, m_a_b_in, m_a_w_dw, m_a_b_dw, m_a_ln_g, m_a_ln_b, m_a_w_out, m_a_b_out, m_kv_w, m_b_w_in, m_b_w_out, m_b_b_out, m_post_ln_g, m_post_ln_b, v_a_w_in, v_a_b_in, v_a_w_dw, v_a_b_dw, v_a_ln_g, v_a_ln_b, v_a_w_out, v_a_b_out, v_kv_w, v_b_w_in, v_b_w_out, v_b_b_out, v_post_ln_g, v_post_ln_b):
    s = x.shape[1]
    assert x.shape == (1, s, D) and s % (DILATIONS[-1] * BLK) == 0
    xs = x.reshape(s, D)
    tgt = loss_target.reshape(s, D)
    me = 4 * lax.axis_index("x") + 2 * lax.axis_index("y") + lax.axis_index("c")
    c_idx = lax.axis_index("c").astype(jnp.int32).reshape(1)

    def small_pack(b_in, w_dw, b_dw, ln_g, ln_b, b_out):
        rows = [b_in.reshape(3, BLK), w_dw.reshape(CONV_W, BLK), b_dw.reshape(1, BLK), ln_g.reshape(1, BLK),
                ln_b.reshape(1, BLK), b_out.reshape(1, BLK)]
        n = sum(r.shape[0] for r in rows)
        return jnp.concatenate(rows + [jnp.zeros((SMALL_ROWS - n, BLK), F32)], axis=0)

    wa_in, sm, *later = _all_gather(
        "gather_first", [a_w_in[0], small_pack(a_b_in, a_w_dw, a_b_dw, a_ln_g, a_ln_b, a_b_out)], [BF16, F32],
        casts=[a_w_out[0], kv_w, b_w_in[0], b_w_out[0]])
    ba_in = sm[:, 0:3, :].reshape(1, 3 * D)
    w_dw = jnp.concatenate([sm[:, 3:3 + CONV_W, :].transpose(1, 0, 2).reshape(CONV_W, D), jnp.zeros((1, D), F32)], axis=0)
    b_dw, ln_g, ln_b, ba_out = (sm[:, 34 + k, :].reshape(1, D) for k in range(4))
    pg0, pg1 = post_ln_g[0:1], post_ln_g[1:2]
    pb0, pb1 = post_ln_b[0:1], post_ln_b[1:2]

    h_a, u0, xb = _a_in_proj(xs, wa_in, ba_in, 512)
    (u1, g_a), (wa_out, wkv, wb_in, wb_out) = _a_conv_gate(u0, h_a, w_dw, b_dw, ln_g, ln_b, 256, _gather_plan(later))
    wa_out = wa_out.reshape(D, D)
    wb_out = wb_out.reshape(D, D)
    n1, rstd1, x1b = _a_out_proj(g_a, wa_out, ba_out, xs, pg0, pb0, 512)
    k_all, v_all = _kv_proj(x1b, wkv, 512)
    q, z_b = _b_in_proj(x1b, wb_in, 512)
    o, lse = _attn_forward(q, k_all, v_all, None, None, None, 0, True, False)
    o, lse = _attn_forward(q, k_all, v_all, o, lse, None, 1, False, False)
    o, lse, g_b = _attn_forward(q, k_all, v_all, o, lse, z_b, 2, False, True)
    dr2, dr2b, loss8, dpg1, dpb1, dbb_out = _b_out_loss(g_b, wb_out, b_b_out, n1, pg0, pb0, pg1, pb1, tgt, 512)

    do, dhb, dd = _b_out_bwd(dr2b, wb_out, z_b, o, 512)
    dk_all = dv_all = None
    for g in range(3):
        dhb, dk_all, dv_all = _attn_backward(q, k_all, v_all, do, lse, dd, dhb, dk_all, dv_all, g)
    dr1, dr1b, dpg0, dpb0, dba_out = _b_in_bwd(dr2, dhb, dk_all, dv_all, wb_in, wkv, n1, rstd1, pg0, 256)

    p_kv = _wgrad("wgrad_k", x1b, dk_all, 768, 1024, total=N_DEV)
    p_kv = _wgrad("wgrad_v", x1b, dv_all, 768, 1024, total=N_DEV, at=N_DEV // 2, into=p_kv)
    p_b_in = _wgrad("wgrad_b_in", x1b, dhb, 512, 512)
    p_a_out = _wgrad("wgrad_a_out", g_a, dr1b, D, 1024).reshape(N_DEV, BLK, D)
    p_b_out = _wgrad("wgrad_b_out", g_b, dr2b, D, 1024).reshape(N_DEV, BLK, D)
    parts = [p_kv, p_b_in, p_a_out, p_b_out]
    (du1, dha, dln_g, dln_b, dbz), from_sibling = _a_out_bwd(dr1b, wa_out, u1, h_a, ln_g, ln_b, 512, _sibling_plan(parts))
    chip_sums = [_pair_add(f"pair_add_{k}", p, r, c_idx) for k, (p, r) in enumerate(zip(parts, from_sibling))]
    (dha, dw_dw, db_dw, dba, dbg), from_chips = _a_conv_bwd(du1, u0, h_a, dha, w_dw, 256, _chips_plan(chip_sums))
    p_a_in = _wgrad("wgrad_a_in", xb, dha, 384, 1024)
    (from_sibling_a,) = _exchange_sibling("reduce_sibling_a_in", [p_a_in])
    sum_a = _pair_add("pair_add_a_in", p_a_in, from_sibling_a, c_idx)
    (grad_x,), (from_chips_a,) = _a_in_bwd(dr1, dha, wa_in, 512, _chips_plan([sum_a]))

    reduced = [from_chips_a] + from_chips
    big_w = [a_w_in[0], kv_w, b_w_in[0], a_w_out[0], b_w_out[0]]
    big_m = [m_a_w_in[0], m_kv_w, m_b_w_in[0], m_a_w_out[0], m_b_w_out[0]]
    big_v = [v_a_w_in[0], v_kv_w, v_b_w_in[0], v_a_w_out[0], v_b_w_out[0]]
    big = [_sum_adamw(f"adamw_{k}", reduced[k], big_w[k], big_m[k], big_v[k]) for k in range(5)]

    rows = [dba[0:1], dbg[0:1], dbz[0:1], dw_dw[0:CONV_W], db_dw[0:1], dln_g[0:1], dln_b[0:1], dba_out[0:1],
            dbb_out[0:1], dpg0[0:1], dpg1[0:1], dpb0[0:1], dpb1[0:1], loss8[0:1]]
    n_rows = sum(r.shape[0] for r in rows)
    gpack = jnp.concatenate(rows + [jnp.zeros((GRAD_ROWS - n_rows, D), F32)], axis=0)
    (gall,) = _all_gather("gather_small_grads", [gpack], [F32])
    gs = _small_sum("small_sum", gall)
    loss = gs[43, 0]

    def my(vec, width):
        return lax.dynamic_slice_in_dim(vec, me * width, width, axis=-1)

    g_small = [my(gs[0:3].reshape(1, 3 * D), 384), my(gs[3:34], BLK)[None], my(gs[34:35], BLK), my(gs[35:36], BLK),
               my(gs[36:37], BLK), my(gs[37:38], BLK), gs[38:39], gs[39:41], gs[41:43]]
    w_small = [a_b_in, a_w_dw, a_b_dw, a_ln_g, a_ln_b, a_b_out, b_b_out, post_ln_g, post_ln_b]
    m_small = [m_a_b_in, m_a_w_dw, m_a_b_dw, m_a_ln_g, m_a_ln_b, m_a_b_out, m_b_b_out, m_post_ln_g, m_post_ln_b]
    v_small = [v_a_b_in, v_a_w_dw, v_a_b_dw, v_a_ln_g, v_a_ln_b, v_a_b_out, v_b_b_out, v_post_ln_g, v_post_ln_b]
    sizes = [math.prod(w.shape) for w in w_small]
    total = sum(sizes)
    padded = -(-total // (8 * BLK)) * (8 * BLK)

    def flat(parts_):
        return jnp.concatenate([p.reshape(-1) for p in parts_] + [jnp.ones((padded - total,), F32)]).reshape(-1, BLK)

    sd, sm_new, sv_new = _small_adamw("adamw_small", flat(w_small), flat(g_small), flat(m_small), flat(v_small))

    def unflat(packed):
        out, pos = [], 0
        vec = packed.reshape(-1)
        for w, n in zip(w_small, sizes):
            out.append(vec[pos:pos + n].reshape(w.shape))
            pos += n
        return out

    g_small = [g.reshape(w.shape) for g, w in zip(g_small, w_small)]
    d_small, nm_small, nv_small = unflat(sd), unflat(sm_new), unflat(sv_new)

    def ordered(bigs, smalls):
        a_in, kvw, b_in, a_out, b_out = bigs
        return [a_in[None], smalls[0], smalls[1], smalls[2], smalls[3], smalls[4], a_out[None], smalls[5],
                kvw, b_in[None], b_out[None], smalls[6], smalls[7], smalls[8]]

    grads = ordered([b[0] for b in big], g_small)
    deltas = ordered([b[1] for b in big], d_small)
    new_m = ordered([b[2] for b in big], nm_small)
    new_v = ordered([b[3] for b in big], nv_small)
    return (loss, grad_x.reshape(1, s, D), *grads, *deltas, *new_m, *new_v)
```

```python
import math

import numpy as np
import jax
import jax.numpy as jnp
from jax import lax
from jax.experimental import pallas as pl
from jax.experimental.pallas import tpu as pltpu

F32 = jnp.float32
BF16 = jnp.bfloat16
MESH = pl.DeviceIdType.MESH

D = 1024
N_DEV = 8
HEAD_DIM = 64
N_HEADS = 16
DILATIONS = (1, 4, 16)
BLK = 128
CONV_W = 31
ALPHA = (2.0 * 2) ** 0.25
LN_EPS = 1e-5
SLOPES = tuple(2.0 ** (-8.0 * (h + 1) / N_HEADS) for h in range(N_HEADS))
NEG = -1e30

ADAM_LR = 0.001
ADAM_B1 = 0.9
ADAM_B2 = 0.999
ADAM_EPS = 1e-08
ADAM_WD = 0.01
ADAM_STEP = 10

VMEM_CAP_MB = 64


def _params(vmem_mb, n_grid=0):
    sem = ("arbitrary",) * n_grid if n_grid else None
    return pltpu.CompilerParams(dimension_semantics=sem, vmem_limit_bytes=min(vmem_mb, VMEM_CAP_MB - 6) * 2 ** 20)


def _sds(shape, dtype):
    return jax.ShapeDtypeStruct(tuple(shape), dtype)


def _sigmoid(v):
    return jax.nn.sigmoid(v)


def _dsilu(v, s):
    return s * (1.0 + v * (1.0 - s))


def _ln_stats(r):
    mu = jnp.mean(r, axis=-1, keepdims=True)
    xc = r - mu
    var = jnp.mean(xc * xc, axis=-1, keepdims=True)
    rstd = lax.rsqrt(var + LN_EPS)
    return xc * rstd, rstd


def _ln_bwd(dn, n, rstd):
    m1 = jnp.mean(dn, axis=-1, keepdims=True)
    m2 = jnp.mean(dn * n, axis=-1, keepdims=True)
    return rstd * (dn - m1 - n * m2)


def _rowsum8(v):
    tm, c = v.shape
    return v.reshape(tm // 8, 8, c).sum(axis=0)


def _acc_init(i, *refs):
    @pl.when(i == 0)
    def _():
        for r in refs:
            r[...] = jnp.zeros(r.shape, r.dtype)


def _acc_finish(i, last, *refs):
    @pl.when(i == last)
    def _():
        for r in refs:
            r[...] = jnp.broadcast_to(jnp.sum(r[...], axis=0, keepdims=True), r.shape)


def _dot(a, b):
    return jnp.dot(a, b, preferred_element_type=F32)


def _dot_nt(a, b):
    return lax.dot_general(a, b, (((1,), (1,)), ((), ())), preferred_element_type=F32)


def _place():
    return lax.axis_index("x"), lax.axis_index("y"), lax.axis_index("c")


def _all_gather(name, arrays, dtypes, casts=()):
    n = len(arrays)
    nc = len(casts)

    def body(*refs):
        ins, cast_ins = refs[:n], refs[n:n + nc]
        outs, cast_outs = refs[n + nc:2 * n + nc], refs[2 * n + nc:2 * (n + nc)]
        stages = refs[2 * (n + nc):3 * n + 2 * nc]
        send_sems, recv_sems, local_sems = refs[3 * n + 2 * nc:]
        x, y, c = _place()
        me, sibling = (x, y, c), (x, y, 1 - c)
        chips = [(1 - x, y), (x, 1 - y), (1 - x, 1 - y)]

        def slot(ref, p):
            return ref.at[4 * p[0] + 2 * p[1] + p[2]]

        def copy(a, k, block, to, src=None):
            return pltpu.make_async_remote_copy(
                src_ref=slot(outs[a], block) if src is None else src, dst_ref=slot(outs[a], block),
                send_sem=send_sems.at[a, k], recv_sem=recv_sems.at[a, k], device_id=to, device_id_type=MESH)

        first, mine = [], []
        for a in range(n):
            stages[a][...] = ins[a][...].astype(stages[a].dtype)
            cp = pltpu.make_async_copy(stages[a], slot(outs[a], me), local_sems.at[a])
            cp.start()
            mine.append(cp)
            first.append(copy(a, 0, me, sibling, src=stages[a]))
            first += [copy(a, 1 + j, me, (*chip, c), src=stages[a]) for j, chip in enumerate(chips)]
        for cp in first:
            cp.start()
        for src, dst in zip(cast_ins, cast_outs):
            dst[...] = src[...].astype(BF16)
        passed = []
        for j, chip in enumerate(chips):
            for a in range(n):
                copy(a, 1 + j, (*chip, c), me).wait_recv()
                cp = copy(a, 4 + j, (*chip, c), sibling)
                cp.start()
                passed.append(cp)
        for a in range(n):
            copy(a, 0, sibling, me).wait_recv()
            for j, chip in enumerate(chips):
                copy(a, 4 + j, (*chip, 1 - c), me).wait_recv()
        for cp in first + passed:
            cp.wait_send()
        for cp in mine:
            cp.wait()

    vmem_bytes = sum(math.prod(a.shape) * (jnp.dtype(a.dtype).itemsize + jnp.dtype(dt).itemsize)
                     for a, dt in zip(arrays, dtypes)) + sum(math.prod(a.shape) * 6 for a in casts)
    vm = pl.BlockSpec(memory_space=pltpu.VMEM)
    return pl.pallas_call(
        body, name=name,
        out_shape=[_sds((N_DEV,) + a.shape, dt) for a, dt in zip(arrays, dtypes)] + [_sds(a.shape, BF16) for a in casts],
        in_specs=[vm] * (n + nc),
        out_specs=[pl.BlockSpec(memory_space=pl.ANY)] * n + [vm] * nc,
        scratch_shapes=[pltpu.VMEM(a.shape, dt) for a, dt in zip(arrays, dtypes)]
        + [pltpu.SemaphoreType.DMA((n, 7)), pltpu.SemaphoreType.DMA((n, 7)), pltpu.SemaphoreType.DMA((n,))],
        compiler_params=_params(vmem_bytes // 2 ** 20 + 8),
    )(*arrays, *casts)


class _Plan:
    def __init__(self, args, out_shape, scratch, start, mid, finish):
        self.args, self.out_shape, self.scratch = list(args), list(out_shape), list(scratch)
        self.start, self.mid, self.finish = start, mid, finish


def _gather_plan(shards):
    n = len(shards)

    def copies(ins, outs, sems):
        send_sems, recv_sems, local_sems = sems
        x, y, c = _place()
        me, sibling = (x, y, c), (x, y, 1 - c)
        chips = [(1 - x, y), (x, 1 - y), (1 - x, 1 - y)]

        def slot(ref, p):
            return ref.at[4 * p[0] + 2 * p[1] + p[2]]

        def copy(a, k, block, to, src=None):
            return pltpu.make_async_remote_copy(
                src_ref=slot(outs[a], block) if src is None else src, dst_ref=slot(outs[a], block),
                send_sem=send_sems.at[a, k], recv_sem=recv_sems.at[a, k], device_id=to, device_id_type=MESH)

        mine = [pltpu.make_async_copy(ins[a], slot(outs[a], me), local_sems.at[a]) for a in range(n)]
        first = [copy(a, 0, me, sibling, src=ins[a]) for a in range(n)]
        first += [copy(a, 1 + j, me, (*chip, c), src=ins[a]) for a in range(n) for j, chip in enumerate(chips)]
        arrive = [copy(a, 1 + j, (*chip, c), me) for j, chip in enumerate(chips) for a in range(n)]
        passed = [copy(a, 4 + j, (*chip, c), sibling) for j, chip in enumerate(chips) for a in range(n)]
        from_sibling = [copy(a, 0, sibling, me) for a in range(n)]
        from_sibling += [copy(a, 4 + j, (*chip, 1 - c), me) for a in range(n) for j, chip in enumerate(chips)]
        return mine, first, arrive, passed, from_sibling

    def start(ins, outs, sems):
        mine, first, _, _, _ = copies(ins, outs, sems)
        for cp in mine + first:
            cp.start()

    def mid(ins, outs, sems):
        _, _, arrive, passed, _ = copies(ins, outs, sems)
        for got, on in zip(arrive, passed):
            got.wait_recv()
            on.start()

    def finish(ins, outs, sems):
        mine, first, _, passed, from_sibling = copies(ins, outs, sems)
        for cp in from_sibling:
            cp.wait_recv()
        for cp in first + passed:
            cp.wait_send()
        for cp in mine:
            cp.wait()

    return _Plan(shards, [_sds((N_DEV,) + a.shape, a.dtype) for a in shards],
                 [pltpu.SemaphoreType.DMA((n, 7)), pltpu.SemaphoreType.DMA((n, 7)), pltpu.SemaphoreType.DMA((n,))],
                 start, mid, finish)


def _sibling_plan(parts):
    n = len(parts)

    def copies(ins, outs, sems):
        send_sems, recv_sems = sems
        x, y, c = _place()
        return [pltpu.make_async_remote_copy(
            src_ref=ins[a].at[2 * p + 1 - c], dst_ref=outs[a].at[p], send_sem=send_sems.at[a, p],
            recv_sem=recv_sems.at[a, p], device_id=(x, y, 1 - c), device_id_type=MESH)
            for a in range(n) for p in range(4)]

    def start(ins, outs, sems):
        for cp in copies(ins, outs, sems):
            cp.start()

    def finish(ins, outs, sems):
        cps = copies(ins, outs, sems)
        for cp in cps:
            cp.wait_recv()
        for cp in cps:
            cp.wait_send()

    return _Plan(parts, [_sds((4,) + p.shape[1:], p.dtype) for p in parts],
                 [pltpu.SemaphoreType.DMA((n, 4)), pltpu.SemaphoreType.DMA((n, 4))], start, None, finish)


def _chips_plan(sums):
    n = len(sums)

    def copies(ins, outs, sems):
        send_sems, recv_sems, local_sems = sems
        x, y, c = _place()
        my_chip = 2 * x + y
        chips = [(1 - x, y), (x, 1 - y), (1 - x, 1 - y)]
        mine = [pltpu.make_async_copy(ins[a].at[my_chip], outs[a].at[my_chip], local_sems.at[a]) for a in range(n)]
        remote = [pltpu.make_async_remote_copy(
            src_ref=ins[a].at[2 * px + py], dst_ref=outs[a].at[my_chip], send_sem=send_sems.at[a, k],
            recv_sem=recv_sems.at[a, k], device_id=(px, py, c), device_id_type=MESH)
            for a in range(n) for k, (px, py) in enumerate(chips)]
        return mine, remote

    def start(ins, outs, sems):
        mine, remote = copies(ins, outs, sems)
        for cp in mine + remote:
            cp.start()

    def finish(ins, outs, sems):
        mine, remote = copies(ins, outs, sems)
        for cp in remote:
            cp.wait_recv()
        for cp in remote:
            cp.wait_send()
        for cp in mine:
            cp.wait()

    return _Plan(sums, [_sds(s.shape, s.dtype) for s in sums],
                 [pltpu.SemaphoreType.DMA((n, 3)), pltpu.SemaphoreType.DMA((n, 3)), pltpu.SemaphoreType.DMA((n,))],
                 start, None, finish)


def _planned_call(plan, args, body, *, name, grid, in_specs, out_specs, out_shape, scratch_shapes=(), mid_step=None,
                  **kw):
    in_specs, out_specs, out_shape = list(in_specs), list(out_specs), list(out_shape)
    scratch_shapes = list(scratch_shapes)
    if plan is None:
        res = pl.pallas_call(body, name=name, grid=grid, in_specs=in_specs, out_specs=out_specs, out_shape=out_shape,
                             scratch_shapes=scratch_shapes, **kw)(*args)
        return list(res), []
    n_in, n_out, n_scr = len(in_specs), len(out_specs), len(scratch_shapes)
    p_in, p_out = len(plan.args), len(plan.out_shape)
    steps = grid[0]

    def fused(*refs):
        ins, pins = refs[:n_in], refs[n_in:n_in + p_in]
        o0 = n_in + p_in
        outs, pouts = refs[o0:o0 + n_out], refs[o0 + n_out:o0 + n_out + p_out]
        s0 = o0 + n_out + p_out
        scr, pscr = refs[s0:s0 + n_scr], refs[s0 + n_scr:]
        i = pl.program_id(0)

        @pl.when(i == 0)
        def _():
            plan.start(pins, pouts, pscr)

        body(*ins, *outs, *scr)
        if plan.mid is not None:
            @pl.when(i == mid_step)
            def _():
                plan.mid(pins, pouts, pscr)

        @pl.when(i == steps - 1)
        def _():
            plan.finish(pins, pouts, pscr)

    hbm = pl.BlockSpec(memory_space=pl.ANY)
    res = pl.pallas_call(
        fused, name=name, grid=grid, in_specs=in_specs + [hbm] * p_in, out_specs=out_specs + [hbm] * p_out,
        out_shape=out_shape + plan.out_shape, scratch_shapes=scratch_shapes + plan.scratch, **kw)(*args, *plan.args)
    return list(res[:n_out]), list(res[n_out:])


def _exchange_sibling(name, parts):
    n = len(parts)

    def body(*refs):
        ins, outs = refs[:n], refs[n:2 * n]
        send_sems, recv_sems = refs[2 * n:]
        x, y, c = _place()
        copies = []
        for a in range(n):
            for p in range(4):
                copies.append(pltpu.make_async_remote_copy(
                    src_ref=ins[a].at[2 * p + 1 - c], dst_ref=outs[a].at[p],
                    send_sem=send_sems.at[a, p], recv_sem=recv_sems.at[a, p],
                    device_id=(x, y, 1 - c), device_id_type=MESH))
        for cp in copies:
            cp.start()
        for cp in copies:
            cp.wait_recv()
        for cp in copies:
            cp.wait_send()

    return pl.pallas_call(
        body, name=name,
        out_shape=[_sds((4,) + p.shape[1:], p.dtype) for p in parts],
        in_specs=[pl.BlockSpec(memory_space=pl.ANY)] * n,
        out_specs=[pl.BlockSpec(memory_space=pl.ANY)] * n,
        scratch_shapes=[pltpu.SemaphoreType.DMA((n, 4)), pltpu.SemaphoreType.DMA((n, 4))],
    )(*parts)


def _exchange_chips(name, sums):
    n = len(sums)

    def body(*refs):
        ins, outs = refs[:n], refs[n:2 * n]
        send_sems, recv_sems, local_sems = refs[2 * n:]
        x, y, c = _place()
        my_chip = 2 * x + y
        chips = [(1 - x, y), (x, 1 - y), (1 - x, 1 - y)]
        copies, mine = [], []
        for a in range(n):
            cp = pltpu.make_async_copy(ins[a].at[my_chip], outs[a].at[my_chip], local_sems.at[a])
            cp.start()
            mine.append(cp)
            for k, (px, py) in enumerate(chips):
                copies.append(pltpu.make_async_remote_copy(
                    src_ref=ins[a].at[2 * px + py], dst_ref=outs[a].at[my_chip],
                    send_sem=send_sems.at[a, k], recv_sem=recv_sems.at[a, k],
                    device_id=(px, py, c), device_id_type=MESH))
        for cp in copies:
            cp.start()
        for cp in copies:
            cp.wait_recv()
        for cp in copies:
            cp.wait_send()
        for cp in mine:
            cp.wait()

    return pl.pallas_call(
        body, name=name,
        out_shape=[_sds(s.shape, s.dtype) for s in sums],
        in_specs=[pl.BlockSpec(memory_space=pl.ANY)] * n,
        out_specs=[pl.BlockSpec(memory_space=pl.ANY)] * n,
        scratch_shapes=[pltpu.SemaphoreType.DMA((n, 3)), pltpu.SemaphoreType.DMA((n, 3)),
                        pltpu.SemaphoreType.DMA((n,))],
    )(*sums)


def _pair_add(name, part, recv, c_idx):
    _, r, c = part.shape
    tr = min(r, 256)

    def body(c_ref, a_ref, b_ref, o_ref):
        o_ref[...] = a_ref[...] + b_ref[...]

    return pl.pallas_call(
        body, name=name,
        grid_spec=pltpu.PrefetchScalarGridSpec(
            num_scalar_prefetch=1, grid=(4, r // tr),
            in_specs=[pl.BlockSpec((1, tr, c), lambda p, i, cr: (2 * p + cr[0], i, 0)),
                      pl.BlockSpec((1, tr, c), lambda p, i, cr: (p, i, 0))],
            out_specs=pl.BlockSpec((1, tr, c), lambda p, i, cr: (p, i, 0))),
        out_shape=_sds((4, r, c), F32),
        compiler_params=_params(16, 2),
    )(c_idx, part, recv)


def _adamw_math(w, g, m, v):
    m = ADAM_B1 * m + (1.0 - ADAM_B1) * g
    v = ADAM_B2 * v + (1.0 - ADAM_B2) * (g * g)
    m_hat = m / (1.0 - ADAM_B1 ** ADAM_STEP)
    v_hat = v / (1.0 - ADAM_B2 ** ADAM_STEP)
    delta = -ADAM_LR * (m_hat / (jnp.sqrt(v_hat) + ADAM_EPS) + ADAM_WD * w)
    return delta, m, v


def _sum_adamw(name, recv, w, m, v):
    r, c = w.shape
    tr = min(r, 256)

    def body(p_ref, w_ref, m_ref, v_ref, g_ref, d_ref, nm_ref, nv_ref):
        g = (p_ref[0] + p_ref[1]) + (p_ref[2] + p_ref[3])
        g_ref[...] = g
        d_ref[...], nm_ref[...], nv_ref[...] = _adamw_math(w_ref[...], g, m_ref[...], v_ref[...])

    blk = pl.BlockSpec((tr, c), lambda i: (i, 0))
    return pl.pallas_call(
        body, name=name, grid=(r // tr,),
        in_specs=[pl.BlockSpec((4, tr, c), lambda i: (0, i, 0)), blk, blk, blk],
        out_specs=[blk] * 4, out_shape=[_sds((r, c), F32)] * 4,
        compiler_params=_params(24, 1),
    )(recv, w, m, v)


def _small_sum(name, gathered):
    _, r, c = gathered.shape

    def body(p_ref, o_ref):
        acc = p_ref[0]
        for j in range(1, N_DEV):
            acc = acc + p_ref[j]
        o_ref[...] = acc

    return pl.pallas_call(body, name=name, out_shape=_sds((r, c), F32),
                          in_specs=[pl.BlockSpec(memory_space=pltpu.VMEM)],
                          out_specs=pl.BlockSpec(memory_space=pltpu.VMEM))(gathered)


def _small_adamw(name, w, g, m, v):
    def body(w_ref, g_ref, m_ref, v_ref, d_ref, nm_ref, nv_ref):
        d_ref[...], nm_ref[...], nv_ref[...] = _adamw_math(w_ref[...], g_ref[...], m_ref[...], v_ref[...])

    vm = pl.BlockSpec(memory_space=pltpu.VMEM)
    return pl.pallas_call(body, name=name, out_shape=[_sds(w.shape, F32)] * 3,
                          in_specs=[vm] * 4, out_specs=[vm] * 3)(w, g, m, v)


def _a_in_proj(x, w_g, b_full, tm):
    s = x.shape[0]
    npd = w_g.shape[2]

    def body(x_ref, w_ref, b_ref, h_ref, u0_ref, xb_ref):
        xb = x_ref[...].astype(BF16)
        xb_ref[...] = xb
        for j in range(N_DEV):
            sl = slice(npd * j, npd * (j + 1))
            h_ref[:, sl] = _dot(xb, w_ref[j]) + b_ref[:, sl]
        u0_ref[...] = h_ref[:, 0:D] * _sigmoid(h_ref[:, D:2 * D])

    row = lambda i: (i, 0)
    return pl.pallas_call(
        body, name="a_in_proj", grid=(s // tm,),
        in_specs=[pl.BlockSpec((tm, D), row), pl.BlockSpec(w_g.shape, lambda i: (0, 0, 0)),
                  pl.BlockSpec((1, 3 * D), lambda i: (0, 0))],
        out_specs=[pl.BlockSpec((tm, 3 * D), row), pl.BlockSpec((tm, D), row), pl.BlockSpec((tm, D), row)],
        out_shape=[_sds((s, 3 * D), F32), _sds((s, D), F32), _sds((s, D), BF16)],
        compiler_params=_params(44, 1),
    )(x, w_g, b_full)


CONV_HALO = 32
CONV_CHUNK = 16
SUBLANES = 8
COPY_ROWS = 56


def _shifted_copies(buf, shs, tm):
    n = tm + CONV_HALO - SUBLANES
    for s in range(1, SUBLANES):
        for c0 in range(0, n, COPY_ROWS):
            c1 = min(c0 + COPY_ROWS, n)
            shs[s - 1, c0:c1, :] = buf[c0 + s:c1 + s, :]


LANES = 128
CONV_ROWS = 128


def _tap_windows(offsets):
    out = []
    for s in range(SUBLANES):
        taps = [(j, o // SUBLANES) for j, o in enumerate(offsets) if o % SUBLANES == s]
        lo, hi = min(a for _, a in taps), max(a for _, a in taps)
        out.append((s, lo, hi - lo, [(j, a - lo) for j, a in taps]))
    return out


def _window(buf, shs, s, r0, lo, span, lanes):
    n = CONV_ROWS + SUBLANES * span
    rows = pl.ds(r0 + SUBLANES * lo, n)
    v = buf[rows, lanes] if s == 0 else shs[s - 1, rows, lanes]
    return v.reshape(n // SUBLANES, SUBLANES, LANES)


def _spread_taps(i, w_ref, wb):
    @pl.when(i == 0)
    def _():
        for j in range(CONV_W):
            wb[j] = jnp.broadcast_to(w_ref[j:j + 1, :], (SUBLANES, D))


def _a_conv_gate(u0, h, w_dw, b_dw, ln_g, ln_b, tm, plan=None):
    s = u0.shape[0]
    per = tm // CONV_HALO

    def body(u0_ref, halo_ref, z_ref, w_ref, bdw_ref, g_ref, b_ref, u1_ref, ga_ref, buf, shs, wb):
        i = pl.program_id(0)
        _spread_taps(i, w_ref, wb)
        buf[0:CONV_HALO, :] = jnp.where(i > 0, halo_ref[...], 0.0)
        buf[CONV_HALO:, :] = u0_ref[...]
        _shifted_copies(buf, shs, tm)
        off = CONV_HALO - (CONV_W - 1)

        windows = _tap_windows([off + j for j in range(CONV_W)])
        grp = CONV_ROWS // SUBLANES

        def chunk(it, carry):
            r0 = pl.multiple_of((it // (D // LANES)) * CONV_ROWS, CONV_ROWS)
            lanes = pl.ds(pl.multiple_of((it % (D // LANES)) * LANES, LANES), LANES)
            acc = jnp.broadcast_to(bdw_ref[:, lanes], (CONV_ROWS, LANES)).reshape(grp, SUBLANES, LANES)
            for s, lo, span, taps in windows:
                win = _window(buf, shs, s, r0, lo, span, lanes)
                for j, a in taps:
                    acc = acc + wb[j, :, lanes][None] * win[a:a + grp]
            u1_ref[pl.ds(r0, CONV_ROWS), lanes] = acc.reshape(CONV_ROWS, LANES)
            return carry

        lax.fori_loop(0, (tm // CONV_ROWS) * (D // LANES), chunk, 0)
        n, _ = _ln_stats(u1_ref[...])
        pre = n * g_ref[...] + b_ref[...]
        z = z_ref[...]
        ga_ref[...] = ((pre * _sigmoid(pre)) * (z * _sigmoid(z))).astype(BF16)

    row = lambda i: (i, 0)
    vec = pl.BlockSpec((1, D), lambda i: (0, 0))
    steps = s // tm
    return _planned_call(
        plan, (u0, u0, h, w_dw, b_dw, ln_g, ln_b), body, name="a_conv_gate", grid=(steps,), mid_step=steps // 2,
        in_specs=[pl.BlockSpec((tm, D), row),
                  pl.BlockSpec((CONV_HALO, D), lambda i: (jnp.maximum(i * per - 1, 0), 0)),
                  pl.BlockSpec((tm, D), lambda i: (i, 2)),
                  pl.BlockSpec((32, D), lambda i: (0, 0)), vec, vec, vec],
        out_specs=[pl.BlockSpec((tm, D), row), pl.BlockSpec((tm, D), row)],
        out_shape=[_sds((s, D), F32), _sds((s, D), BF16)],
        scratch_shapes=[pltpu.VMEM((tm + CONV_HALO, D), F32), pltpu.VMEM((SUBLANES - 1, tm + CONV_HALO, D), F32),
                        pltpu.VMEM((CONV_W, SUBLANES, D), F32)],
        compiler_params=_params(40, 1))


def _a_out_proj(ga, w, b, x, pg, pb, tm):
    s = x.shape[0]

    def body(ga_ref, w_ref, b_ref, x_ref, pg_ref, pb_ref, n_ref, rstd_ref, xb_ref):
        r = ALPHA * x_ref[...] + (_dot(ga_ref[...], w_ref[...]) + b_ref[...])
        n, rstd = _ln_stats(r)
        n_ref[...] = n
        rstd_ref[...] = rstd
        xb_ref[...] = (n * pg_ref[...] + pb_ref[...]).astype(BF16)

    row = lambda i: (i, 0)
    vec = pl.BlockSpec((1, D), lambda i: (0, 0))
    return pl.pallas_call(
        body, name="a_out_proj", grid=(s // tm,),
        in_specs=[pl.BlockSpec((tm, D), row), pl.BlockSpec((D, D), lambda i: (0, 0)), vec,
                  pl.BlockSpec((tm, D), row), vec, vec],
        out_specs=[pl.BlockSpec((tm, D), row), pl.BlockSpec((tm, 1), row), pl.BlockSpec((tm, D), row)],
        out_shape=[_sds((s, D), F32), _sds((s, 1), F32), _sds((s, D), BF16)],
        compiler_params=_params(32, 1),
    )(ga, w, b, x, pg, pb)


def _kv_proj(xb, w_g, tm):
    s = xb.shape[0]
    npd = w_g.shape[2]
    half = N_DEV // 2

    def body(x_ref, w_ref, k_ref, v_ref):
        xv = x_ref[...]
        for j in range(N_DEV):
            o_ref = k_ref if j < half else v_ref
            jj = j % half
            o_ref[:, npd * jj:npd * (jj + 1)] = _dot(xv, w_ref[j]).astype(BF16)

    row = lambda i: (i, 0)
    return pl.pallas_call(
        body, name="kv_proj", grid=(s // tm,),
        in_specs=[pl.BlockSpec((tm, D), row), pl.BlockSpec(w_g.shape, lambda i: (0, 0, 0))],
        out_specs=[pl.BlockSpec((tm, 3 * D), row), pl.BlockSpec((tm, 3 * D), row)],
        out_shape=[_sds((s, 3 * D), BF16), _sds((s, 3 * D), BF16)],
        compiler_params=_params(52, 1),
    )(xb, w_g)


def _b_in_proj(xb, w_g, tm):
    s = xb.shape[0]
    npd = w_g.shape[2]
    scale = HEAD_DIM ** -0.5

    def body(x_ref, w_ref, q_ref, z_ref):
        xv = x_ref[...]
        for j in range(N_DEV):
            hj = _dot(xv, w_ref[j])
            if j < 6:
                q_ref[:, npd * j:npd * (j + 1)] = (hj.astype(BF16) * scale).astype(BF16)
            else:
                z_ref[:, npd * (j - 6):npd * (j - 5)] = hj

    row = lambda i: (i, 0)
    return pl.pallas_call(
        body, name="b_in_proj", grid=(s // tm,),
        in_specs=[pl.BlockSpec((tm, D), row), pl.BlockSpec(w_g.shape, lambda i: (0, 0, 0))],
        out_specs=[pl.BlockSpec((tm, 3 * D), row), pl.BlockSpec((tm, D), row)],
        out_shape=[_sds((s, 3 * D), BF16), _sds((s, D), F32)],
        compiler_params=_params(44, 1),
    )(xb, w_g)


ATTN_NQ = {1: 8, 4: 2, 16: 1}
N_PAIRS = N_HEADS // 2


def _band_table(d):
    qi = np.arange(BLK)[:, None]
    kj = np.arange(2 * BLK)[None, :]
    dist = qi + BLK - kj
    ok = (dist >= 0) & (dist <= BLK)
    return jnp.asarray(np.where(ok, -(d * dist).astype(np.float32), np.float32(NEG)), dtype=F32)


def _slope_table():
    t = np.zeros((N_PAIRS, 8, 2 * BLK), np.float32)
    for h in range(N_HEADS):
        t[h // 2, h % 2, :] = SLOPES[h]
    return jnp.asarray(t)


def _head_masks():
    lane = lax.broadcasted_iota(jnp.int32, (1, BLK), 1)
    lo = (lane < HEAD_DIM).astype(BF16)
    return (lo, (1.0 - lo).astype(BF16))


def _pick_col(tile, lane, h):
    return jnp.sum(jnp.where(lane == h, tile, 0.0), axis=1, keepdims=True)


def _rows(base, n, d):
    return pl.ds(base, n) if d == 1 else pl.ds(base, n, stride=d)


def _attn_fwd(q, k_all, v_all, o_acc, lse_acc, z, g, first, last):
    d = DILATIONS[g]
    s = q.shape[0]
    nq = ATTN_NQ[d]
    halo = BLK * d
    tile = nq * halo
    assert s % tile == 0

    def body(*refs):
        q_ref, k_ref, kh_ref, v_ref, vh_ref, nd_ref, sl_ref = refs[:7]
        k = 7
        if not first:
            oa_ref, la_ref = refs[k:k + 2]
            k += 2
        if last:
            z_ref = refs[k]
            k += 1
        o_ref, l_ref = refs[k:k + 2]
        k += 2
        if last:
            gb_ref = refs[k]
            k += 1
        qf, kf, vf = refs[k:k + 3]
        if last:
            gf = refs[k + 3]
        n = pl.program_id(0)
        hp = pl.program_id(1)
        qf[...] = q_ref[...].astype(F32)
        kf[0:halo, :] = kh_ref[...].astype(F32)
        kf[halo:, :] = k_ref[...].astype(F32)
        vf[0:halo, :] = vh_ref[...].astype(F32)
        vf[halo:, :] = v_ref[...].astype(F32)

        @pl.when(hp == 0)
        def _():
            l_ref[...] = jnp.zeros(l_ref.shape, F32) if first else la_ref[...]

        col = lax.broadcasted_iota(jnp.int32, (2 * BLK, 2 * BLK), 1)
        lane = lax.broadcasted_iota(jnp.int32, (BLK, BLK), 1)
        masks = _head_masks()
        bias = jnp.concatenate([sl_ref[0, e:e + 1, :] * nd_ref[...] for e in range(2)], axis=0)
        bias0 = bias + jnp.where((n == 0) & (col < BLK), NEG, 0.0)
        for b in range(nq):
            for r in range(d):
                rq = _rows(b * halo + r, BLK, d)
                rk = _rows(b * halo + r, 2 * BLK, d)
                q2 = qf[rq, :].astype(BF16)
                kcat = kf[rk, :].astype(BF16)
                vcat = vf[rk, :].astype(BF16)
                lt = l_ref[rq, :]
                sc = _dot_nt(jnp.concatenate([q2 * masks[0], q2 * masks[1]], axis=0), kcat)
                sc = sc + (bias0 if b == 0 else bias)
                m = jnp.max(sc, axis=1, keepdims=True)
                p = jnp.exp(sc - m)
                l = jnp.sum(p, axis=1, keepdims=True)
                oh = _dot(p.astype(BF16), vcat) / l
                lse = m + jnp.log(l)
                if not first:
                    old = jnp.concatenate([_pick_col(lt, lane, 2 * hp + e) for e in range(2)], axis=0)
                    mx = jnp.maximum(old, lse)
                    new = mx + jnp.log(jnp.exp(old - mx) + jnp.exp(lse - mx))
                    keep = jnp.exp(old - new)
                    oh = oh * jnp.exp(lse - new)
                    lse = new
                o2 = jnp.where(lane < HEAD_DIM, oh[0:BLK], oh[BLK:])
                if not first:
                    o2 = o2 + oa_ref[rq, :] * jnp.where(lane < HEAD_DIM, keep[0:BLK], keep[BLK:])
                lt = jnp.where(lane == 2 * hp, lse[0:BLK], lt)
                lt = jnp.where(lane == 2 * hp + 1, lse[BLK:], lt)
                o_ref[rq, :] = o2
                l_ref[rq, :] = lt
                if last:
                    zz = z_ref[rq, :]
                    gf[rq, :] = o2 * (zz * _sigmoid(zz))
        if last:
            gb_ref[...] = gf[...].astype(BF16)

    col_blk = lambda n, hp: (n, g * N_PAIRS + hp)
    halo_blk = lambda n, hp: (jnp.maximum(n * nq - 1, 0), g * N_PAIRS + hp)
    own = pl.BlockSpec((tile, BLK), lambda n, hp: (n, hp))
    own_l = pl.BlockSpec((tile, BLK), lambda n, hp: (n, 0))
    in_specs = [pl.BlockSpec((tile, BLK), col_blk),
                pl.BlockSpec((tile, BLK), col_blk), pl.BlockSpec((halo, BLK), halo_blk),
                pl.BlockSpec((tile, BLK), col_blk), pl.BlockSpec((halo, BLK), halo_blk),
                pl.BlockSpec((BLK, 2 * BLK), lambda n, hp: (0, 0)),
                pl.BlockSpec((1, 8, 2 * BLK), lambda n, hp: (hp, 0, 0))]
    args = [q, k_all, k_all, v_all, v_all, _band_table(d), _slope_table()]
    if not first:
        in_specs += [own, own_l]
        args += [o_acc, lse_acc]
    if last:
        in_specs += [own]
        args += [z]
    out_specs = [own, own_l] + ([own] if last else [])
    out_shape = [_sds((s, D), F32), _sds((s, BLK), F32)] + ([_sds((s, D), BF16)] if last else [])
    scratch = [pltpu.VMEM((tile, BLK), F32), pltpu.VMEM((tile + halo, BLK), F32), pltpu.VMEM((tile + halo, BLK), F32)]
    if last:
        scratch.append(pltpu.VMEM((tile, BLK), F32))
    return pl.pallas_call(
        body, name=f"attn_fwd_g{g}", grid=(s // tile, N_PAIRS), in_specs=in_specs, out_specs=out_specs,
        out_shape=out_shape, scratch_shapes=scratch, compiler_params=_params(32, 2),
    )(*args)


ATTN_PRE = {1: 1, 4: 1, 16: 4}


def _regroup(src, dst, d1):
    n = src.shape[0] // d1
    for r1 in range(d1):
        dst[r1] = src[pl.ds(r1, n, stride=d1), :]


def _ungroup(src, dst, d1):
    n = dst.shape[0] // d1
    for r1 in range(d1):
        dst[pl.ds(r1, n, stride=d1), :] = src[r1]


def _grouped(shape, d1):
    return pltpu.VMEM((d1, shape[0] // d1, shape[1]), F32)


def _with_halo(halo_ref, tile_ref, b):
    if b == 0:
        return jnp.concatenate([halo_ref[...], tile_ref[0:BLK, :]], axis=0)
    return tile_ref[(b - 1) * BLK:(b + 1) * BLK, :]


def _attn_forward(q, k_all, v_all, o_acc, lse_acc, z, g, first, last):
    d = DILATIONS[g]
    s = q.shape[0]
    nq = ATTN_NQ[d]
    d1 = ATTN_PRE[d]
    d2 = d // d1
    halo = BLK * d
    tile = nq * halo
    assert s % tile == 0
    pre = d1 > 1

    def body(*refs):
        refs = list(refs)
        q_ref, k_ref, kh_ref, v_ref, vh_ref, nd_ref, sl_ref = refs[:7]
        del refs[:7]
        oa_ref, la_ref = (refs.pop(0), refs.pop(0)) if not first else (None, None)
        z_ref = refs.pop(0) if last else None
        o_ref, l_ref = refs.pop(0), refs.pop(0)
        gb_ref = refs.pop(0) if last else None
        qf, kf, vf = refs.pop(0), refs.pop(0), refs.pop(0)
        gf = refs.pop(0) if last else None
        q1 = k1 = v1 = l1 = o1 = oa1 = z1 = g1 = None
        if pre:
            q1, k1, v1, l1, o1 = (refs.pop(0) for _ in range(5))
            oa1 = refs.pop(0) if not first else None
            z1, g1 = (refs.pop(0), refs.pop(0)) if last else (None, None)
        n = pl.program_id(0)
        hp = pl.program_id(1)
        if d > 1:
            qf[...] = q_ref[...].astype(F32)
            kf[0:halo, :] = kh_ref[...].astype(F32)
            kf[halo:, :] = k_ref[...].astype(F32)
            vf[0:halo, :] = vh_ref[...].astype(F32)
            vf[halo:, :] = v_ref[...].astype(F32)

        @pl.when(hp == 0)
        def _():
            l_ref[...] = jnp.zeros(l_ref.shape, F32) if first else la_ref[...]

        if pre:
            for src, dst in ((qf, q1), (kf, k1), (vf, v1), (l_ref, l1), (oa_ref, oa1), (z_ref, z1)):
                if src is not None:
                    _regroup(src, dst, d1)

        def pick(nat, grp, r1):
            return grp.at[r1] if pre else nat

        def halves(colv):
            return jnp.where(lane < HEAD_DIM, colv[0:BLK], colv[BLK:])

        def put(tile_v, colv):
            tile_v = jnp.where(lane == 2 * hp, colv[0:BLK], tile_v)
            return jnp.where(lane == 2 * hp + 1, colv[BLK:], tile_v)

        col = lax.broadcasted_iota(jnp.int32, (2 * BLK, 2 * BLK), 1)
        lane = lax.broadcasted_iota(jnp.int32, (BLK, BLK), 1)
        masks = _head_masks()
        bias = jnp.concatenate([sl_ref[0, e:e + 1, :] * nd_ref[...] for e in range(2)], axis=0)
        bias0 = bias + jnp.where((n == 0) & (col < BLK), NEG, 0.0)
        for b in range(nq):
            for r in range(d):
                r1, r2 = r % d1, r // d1
                rq = _rows(b * (halo // d1) + r2, BLK, d2)
                rk = _rows(b * (halo // d1) + r2, 2 * BLK, d2)
                if d > 1:
                    q2 = pick(qf, q1, r1)[rq, :].astype(BF16)
                    kcat = pick(kf, k1, r1)[rk, :].astype(BF16)
                    vcat = pick(vf, v1, r1)[rk, :].astype(BF16)
                else:
                    q2 = q_ref[rq, :]
                    kcat = _with_halo(kh_ref, k_ref, b)
                    vcat = _with_halo(vh_ref, v_ref, b)
                sc = _dot_nt(jnp.concatenate([q2 * masks[0], q2 * masks[1]], axis=0), kcat)
                sc = sc + (bias0 if b == 0 else bias)
                m = jnp.max(sc, axis=1, keepdims=True)
                p = jnp.exp(sc - m)
                l = jnp.sum(p, axis=1, keepdims=True)
                oh = _dot(p.astype(BF16), vcat) / l
                lse = m + jnp.log(l)
                lt = pick(l_ref, l1, r1)[rq, :]
                if not first:
                    old = jnp.concatenate([_pick_col(lt, lane, 2 * hp + e) for e in range(2)], axis=0)
                    mx = jnp.maximum(old, lse)
                    new = mx + jnp.log(jnp.exp(old - mx) + jnp.exp(lse - mx))
                    o2 = halves(oh) * halves(jnp.exp(lse - new)) + pick(oa_ref, oa1, r1)[rq, :] * halves(jnp.exp(old - new))
                    lse = new
                else:
                    o2 = halves(oh)
                pick(o_ref, o1, r1)[rq, :] = o2
                pick(l_ref, l1, r1)[rq, :] = put(lt, lse)
                if last:
                    zz = pick(z_ref, z1, r1)[rq, :]
                    pick(gf, g1, r1)[rq, :] = o2 * (zz * _sigmoid(zz))
        if pre:
            _ungroup(o1, o_ref, d1)
            _ungroup(l1, l_ref, d1)
            if last:
                _ungroup(g1, gf, d1)
        if last:
            gb_ref[...] = gf[...].astype(BF16)

    col_blk = lambda n, hp: (n, g * N_PAIRS + hp)
    halo_blk = lambda n, hp: (jnp.maximum(n * nq - 1, 0), g * N_PAIRS + hp)
    own = pl.BlockSpec((tile, BLK), lambda n, hp: (n, hp))
    own_l = pl.BlockSpec((tile, BLK), lambda n, hp: (n, 0))
    in_specs = [pl.BlockSpec((tile, BLK), col_blk),
                pl.BlockSpec((tile, BLK), col_blk), pl.BlockSpec((halo, BLK), halo_blk),
                pl.BlockSpec((tile, BLK), col_blk), pl.BlockSpec((halo, BLK), halo_blk),
                pl.BlockSpec((BLK, 2 * BLK), lambda n, hp: (0, 0)),
                pl.BlockSpec((1, 8, 2 * BLK), lambda n, hp: (hp, 0, 0))]
    args = [q, k_all, k_all, v_all, v_all, _band_table(d), _slope_table()]
    if not first:
        in_specs += [own, own_l]
        args += [o_acc, lse_acc]
    if last:
        in_specs += [own]
        args += [z]
    out_specs = [own, own_l] + ([own] if last else [])
    out_shape = [_sds((s, D), F32), _sds((s, BLK), F32)] + ([_sds((s, D), BF16)] if last else [])
    t_shape, w_shape = (tile, BLK), (tile + halo, BLK)
    scratch = [pltpu.VMEM(t_shape, F32), pltpu.VMEM(w_shape, F32), pltpu.VMEM(w_shape, F32)]
    if last:
        scratch.append(pltpu.VMEM(t_shape, F32))
    if pre:
        scratch += [_grouped(t_shape, d1), _grouped(w_shape, d1), _grouped(w_shape, d1)]
        scratch += [_grouped(t_shape, d1)] * (2 + (0 if first else 1) + (2 if last else 0))
    return pl.pallas_call(
        body, name=f"attn_fwd_g{g}", grid=(s // tile, N_PAIRS), in_specs=in_specs, out_specs=out_specs,
        out_shape=out_shape, scratch_shapes=scratch, compiler_params=_params(56, 2),
    )(*args)


def _attn_backward(q, k_all, v_all, do, lse, dd, dhb, dk_all, dv_all, g):
    d = DILATIONS[g]
    s = q.shape[0]
    nq = ATTN_NQ[d]
    d1 = ATTN_PRE[d]
    d2 = d // d1
    halo = BLK * d
    tile = nq * halo
    nt = s // tile
    first = dk_all is None
    pre = d1 > 1

    def body(*refs):
        refs = list(refs)
        q_ref, k_ref, kh_ref, v_ref, vh_ref, nd_ref, sl_ref, do_ref, l_ref, dd_ref = refs[:10]
        del refs[:10 + (1 if first else 3)]
        dq_ref, dk_ref, dv_ref = refs[:3]
        qf, dof, kf, vf, dqf, dkf, dvf, ck, cv = refs[3:12]
        del refs[:12]
        if pre:
            q1, do1, k1, v1, l1, dd1, dq1, dk1, dv1 = refs
        else:
            q1 = do1 = k1 = v1 = l1 = dd1 = dq1 = None
            dk1, dv1 = dkf, dvf
        hp = pl.program_id(0)
        n = pl.program_id(1)

        @pl.when(n == 0)
        def _():
            ck[...] = jnp.zeros(ck.shape, F32)
            cv[...] = jnp.zeros(cv.shape, F32)

        dk1[...] = jnp.zeros(dk1.shape, F32)
        dv1[...] = jnp.zeros(dv1.shape, F32)

        def pick(nat, grp, r1):
            return grp.at[r1] if pre else nat

        @pl.when(n < nt)
        def _():
            if d > 1:
                qf[...] = q_ref[...].astype(F32)
                dof[...] = do_ref[...].astype(F32)
                kf[0:halo, :] = kh_ref[...].astype(F32)
                kf[halo:, :] = k_ref[...].astype(F32)
                vf[0:halo, :] = vh_ref[...].astype(F32)
                vf[halo:, :] = v_ref[...].astype(F32)
            if pre:
                for src, dst in ((qf, q1), (dof, do1), (kf, k1), (vf, v1), (l_ref, l1), (dd_ref, dd1)):
                    _regroup(src, dst, d1)
            col = lax.broadcasted_iota(jnp.int32, (2 * BLK, 2 * BLK), 1)
            lane = lax.broadcasted_iota(jnp.int32, (BLK, BLK), 1)
            masks = _head_masks()
            bias = jnp.concatenate([sl_ref[0, e:e + 1, :] * nd_ref[...] for e in range(2)], axis=0)
            bias0 = bias + jnp.where((n == 0) & (col < BLK), NEG, 0.0)
            for b in range(nq):
                for r in range(d):
                    r1, r2 = r % d1, r // d1
                    rq = _rows(b * (halo // d1) + r2, BLK, d2)
                    rk = _rows(b * (halo // d1) + r2, 2 * BLK, d2)
                    if d > 1:
                        q2 = pick(qf, q1, r1)[rq, :].astype(BF16)
                        do2 = pick(dof, do1, r1)[rq, :].astype(BF16)
                        kcat = pick(kf, k1, r1)[rk, :].astype(BF16)
                        vcat = pick(vf, v1, r1)[rk, :].astype(BF16)
                    else:
                        q2 = q_ref[rq, :]
                        do2 = do_ref[rq, :]
                        kcat = _with_halo(kh_ref, k_ref, b)
                        vcat = _with_halo(vh_ref, v_ref, b)
                    lt = pick(l_ref, l1, r1)[rq, :]
                    dt = pick(dd_ref, dd1, r1)[rq, :]
                    qs = jnp.concatenate([q2 * masks[0], q2 * masks[1]], axis=0)
                    dos = jnp.concatenate([do2 * masks[0], do2 * masks[1]], axis=0)
                    lcol = jnp.concatenate([_pick_col(lt, lane, 2 * hp + e) for e in range(2)], axis=0)
                    dcol = jnp.concatenate([_pick_col(dt, lane, 2 * hp + e) for e in range(2)], axis=0)
                    sc = _dot_nt(qs, kcat) + (bias0 if b == 0 else bias)
                    p = jnp.exp(sc - lcol)
                    ds = p * (_dot_nt(dos, vcat) - dcol)
                    dsb = ds.astype(BF16)
                    dq = _dot(dsb, kcat)
                    dq2 = (HEAD_DIM ** -0.5) * jnp.where(lane < HEAD_DIM, dq[0:BLK], dq[BLK:])
                    pick(dqf, dq1, r1)[rq, :] = dq2
                    pick(dkf, dk1, r1)[rk, :] += _dot(dsb.T, qs)
                    pick(dvf, dv1, r1)[rk, :] += _dot(p.astype(BF16).T, dos)
            if pre:
                _ungroup(dq1, dqf, d1)
            dq_ref[...] = dqf[...].astype(BF16)

        if pre:
            _ungroup(dk1, dkf, d1)
            _ungroup(dv1, dvf, d1)
        if tile > halo:
            dk_ref[0:tile - halo, :] = ck[0:tile - halo, :].astype(BF16)
            dv_ref[0:tile - halo, :] = cv[0:tile - halo, :].astype(BF16)
        dk_ref[tile - halo:, :] = (ck[tile - halo:, :] + dkf[0:halo, :]).astype(BF16)
        dv_ref[tile - halo:, :] = (cv[tile - halo:, :] + dvf[0:halo, :]).astype(BF16)
        ck[...] = dkf[halo:, :]
        cv[...] = dvf[halo:, :]

    cur = lambda n: jnp.minimum(n, nt - 1)
    col_blk = lambda hp, n: (cur(n), g * N_PAIRS + hp)
    halo_blk = lambda hp, n: (jnp.maximum(cur(n) * nq - 1, 0), g * N_PAIRS + hp)
    out_kv = lambda hp, n: (jnp.maximum(n - 1, 0), g * N_PAIRS + hp)
    small = pl.BlockSpec((tile, BLK), lambda hp, n: (cur(n), 0))
    hbm = pl.BlockSpec(memory_space=pl.ANY)
    in_specs = [pl.BlockSpec((tile, BLK), col_blk),
                pl.BlockSpec((tile, BLK), col_blk), pl.BlockSpec((halo, BLK), halo_blk),
                pl.BlockSpec((tile, BLK), col_blk), pl.BlockSpec((halo, BLK), halo_blk),
                pl.BlockSpec((BLK, 2 * BLK), lambda hp, n: (0, 0)),
                pl.BlockSpec((1, 8, 2 * BLK), lambda hp, n: (hp, 0, 0)),
                pl.BlockSpec((tile, BLK), lambda hp, n: (cur(n), hp)), small, small, hbm]
    args = [q, k_all, k_all, v_all, v_all, _band_table(d), _slope_table(), do, lse, dd, dhb]
    aliases = {10: 0}
    if not first:
        in_specs += [hbm, hbm]
        args += [dk_all, dv_all]
        aliases.update({11: 1, 12: 2})
    t_shape, w_shape = (tile, BLK), (tile + halo, BLK)
    tile_f32, wide_f32 = pltpu.VMEM(t_shape, F32), pltpu.VMEM(w_shape, F32)
    scratch = [tile_f32, tile_f32, wide_f32, wide_f32, tile_f32, wide_f32, wide_f32, tile_f32, tile_f32]
    if pre:
        tg, wg = _grouped(t_shape, d1), _grouped(w_shape, d1)
        scratch += [tg, tg, wg, wg, tg, tg, tg, wg, wg]
    return pl.pallas_call(
        body, name=f"attn_bwd_g{g}", grid=(N_PAIRS, nt + 1), in_specs=in_specs,
        out_specs=[pl.BlockSpec((tile, BLK), col_blk), pl.BlockSpec((tile, BLK), out_kv),
                   pl.BlockSpec((tile, BLK), out_kv)],
        out_shape=[_sds((s, 4 * D), BF16), _sds((s, 3 * D), BF16), _sds((s, 3 * D), BF16)],
        scratch_shapes=scratch, input_output_aliases=aliases, compiler_params=_params(48, 2),
    )(*args)


def _b_out_loss(gb, w, b, n1, pg0, pb0, pg1, pb1, tgt, tm):
    s = gb.shape[0]
    last = s // tm - 1

    def body(gb_ref, w_ref, b_ref, n1_ref, pg0_ref, pb0_ref, pg1_ref, pb1_ref, t_ref,
             dr_ref, drb_ref, loss_ref, dpg_ref, dpb_ref, dbo_ref):
        i = pl.program_id(0)
        _acc_init(i, loss_ref, dpg_ref, dpb_ref, dbo_ref)
        x1 = n1_ref[...] * pg0_ref[...] + pb0_ref[...]
        r = ALPHA * x1 + (_dot(gb_ref[...], w_ref[...]) + b_ref[...])
        n, rstd = _ln_stats(r)
        err = (n * pg1_ref[...] + pb1_ref[...]) - t_ref[...]
        loss_ref[...] += _rowsum8(err * err)
        dx2 = err * (1.0 / D)
        dpg_ref[...] += _rowsum8(dx2 * n)
        dpb_ref[...] += _rowsum8(dx2)
        dr = _ln_bwd(dx2 * pg1_ref[...], n, rstd)
        dr_ref[...] = dr
        drb_ref[...] = dr.astype(BF16)
        dbo_ref[...] += _rowsum8(dr)
        _acc_finish(i, last, dpg_ref, dpb_ref, dbo_ref)

        @pl.when(i == last)
        def _():
            loss_ref[...] = jnp.broadcast_to((0.5 / D) * jnp.sum(loss_ref[...], keepdims=True), loss_ref.shape)

    row = lambda i: (i, 0)
    vec = pl.BlockSpec((1, D), lambda i: (0, 0))
    acc = pl.BlockSpec((8, D), lambda i: (0, 0))
    return pl.pallas_call(
        body, name="b_out_loss", grid=(s // tm,),
        in_specs=[pl.BlockSpec((tm, D), row), pl.BlockSpec((D, D), lambda i: (0, 0)), vec,
                  pl.BlockSpec((tm, D), row), vec, vec, vec, vec, pl.BlockSpec((tm, D), row)],
        out_specs=[pl.BlockSpec((tm, D), row), pl.BlockSpec((tm, D), row), acc, acc, acc, acc],
        out_shape=[_sds((s, D), F32), _sds((s, D), BF16)] + [_sds((8, D), F32)] * 4,
        compiler_params=_params(36, 1),
    )(gb, w, b, n1, pg0, pb0, pg1, pb1, tgt)


def _head_selector():
    sel = (np.arange(D)[:, None] // HEAD_DIM == np.arange(BLK)[None, :]).astype(np.float32)
    return jnp.asarray(sel, dtype=BF16)


def _b_out_bwd(drb, w, z, o, tm):
    s = drb.shape[0]

    def body(dr_ref, w_ref, z_ref, o_ref, sel_ref, do_ref, dh_ref, dd_ref):
        dg = _dot_nt(dr_ref[...], w_ref[...])
        zz = z_ref[...]
        sg = _sigmoid(zz)
        do = dg * (zz * sg)
        ov = o_ref[...]
        do_ref[...] = do.astype(BF16)
        dh_ref[...] = (dg * ov * _dsilu(zz, sg)).astype(BF16)
        prod = do * ov
        hi = prod.astype(BF16)
        lo = (prod - hi.astype(F32)).astype(BF16)
        dd_ref[...] = _dot(hi, sel_ref[...]) + _dot(lo, sel_ref[...])

    row = lambda i: (i, 0)
    return pl.pallas_call(
        body, name="b_out_bwd", grid=(s // tm,),
        in_specs=[pl.BlockSpec((tm, D), row), pl.BlockSpec((D, D), lambda i: (0, 0)),
                  pl.BlockSpec((tm, D), row), pl.BlockSpec((tm, D), row), pl.BlockSpec((D, BLK), lambda i: (0, 0))],
        out_specs=[pl.BlockSpec((tm, D), row), pl.BlockSpec((tm, D), lambda i: (i, 3)),
                   pl.BlockSpec((tm, BLK), row)],
        out_shape=[_sds((s, D), BF16), _sds((s, 4 * D), BF16), _sds((s, BLK), F32)],
        compiler_params=_params(36, 1),
    )(drb, w, z, o, _head_selector())


def _attn_bwd(q, k_all, v_all, do, lse, dd, dhb, dk_all, dv_all, g):
    d = DILATIONS[g]
    s = q.shape[0]
    nq = ATTN_NQ[d]
    halo = BLK * d
    tile = nq * halo
    nt = s // tile
    first = dk_all is None

    def body(*refs):
        q_ref, k_ref, kh_ref, v_ref, vh_ref, nd_ref, sl_ref, do_ref, l_ref, dd_ref = refs[:10]
        k = 10 + (1 if first else 3)
        dq_ref, dk_ref, dv_ref = refs[k:k + 3]
        qf, dof, kf, vf, dqf, dkf, dvf, ck, cv = refs[k + 3:k + 12]
        hp = pl.program_id(0)
        n = pl.program_id(1)

        @pl.when(n == 0)
        def _():
            ck[...] = jnp.zeros(ck.shape, F32)
            cv[...] = jnp.zeros(cv.shape, F32)

        dkf[...] = jnp.zeros(dkf.shape, F32)
        dvf[...] = jnp.zeros(dvf.shape, F32)

        @pl.when(n < nt)
        def _():
            qf[...] = q_ref[...].astype(F32)
            dof[...] = do_ref[...].astype(F32)
            kf[0:halo, :] = kh_ref[...].astype(F32)
            kf[halo:, :] = k_ref[...].astype(F32)
            vf[0:halo, :] = vh_ref[...].astype(F32)
            vf[halo:, :] = v_ref[...].astype(F32)
            col = lax.broadcasted_iota(jnp.int32, (2 * BLK, 2 * BLK), 1)
            lane = lax.broadcasted_iota(jnp.int32, (BLK, BLK), 1)
            masks = _head_masks()
            bias = jnp.concatenate([sl_ref[0, e:e + 1, :] * nd_ref[...] for e in range(2)], axis=0)
            bias0 = bias + jnp.where((n == 0) & (col < BLK), NEG, 0.0)
            for b in range(nq):
                for r in range(d):
                    rq = _rows(b * halo + r, BLK, d)
                    rk = _rows(b * halo + r, 2 * BLK, d)
                    q2 = qf[rq, :].astype(BF16)
                    do2 = dof[rq, :].astype(BF16)
                    kcat = kf[rk, :].astype(BF16)
                    vcat = vf[rk, :].astype(BF16)
                    lt = l_ref[rq, :]
                    dt = dd_ref[rq, :]
                    qs = jnp.concatenate([q2 * masks[0], q2 * masks[1]], axis=0)
                    dos = jnp.concatenate([do2 * masks[0], do2 * masks[1]], axis=0)
                    lcol = jnp.concatenate([_pick_col(lt, lane, 2 * hp + e) for e in range(2)], axis=0)
                    dcol = jnp.concatenate([_pick_col(dt, lane, 2 * hp + e) for e in range(2)], axis=0)
                    sc = _dot_nt(qs, kcat) + (bias0 if b == 0 else bias)
                    p = jnp.exp(sc - lcol)
                    ds = p * (_dot_nt(dos, vcat) - dcol)
                    dsb = ds.astype(BF16)
                    dq = _dot(dsb, kcat)
                    dqf[rq, :] = (HEAD_DIM ** -0.5) * jnp.where(lane < HEAD_DIM, dq[0:BLK], dq[BLK:])
                    dkf[rk, :] += _dot(dsb.T, qs)
                    dvf[rk, :] += _dot(p.astype(BF16).T, dos)
            dq_ref[...] = dqf[...].astype(BF16)

        if tile > halo:
            dk_ref[0:tile - halo, :] = ck[0:tile - halo, :].astype(BF16)
            dv_ref[0:tile - halo, :] = cv[0:tile - halo, :].astype(BF16)
        dk_ref[tile - halo:, :] = (ck[tile - halo:, :] + dkf[0:halo, :]).astype(BF16)
        dv_ref[tile - halo:, :] = (cv[tile - halo:, :] + dvf[0:halo, :]).astype(BF16)
        ck[...] = dkf[halo:, :]
        cv[...] = dvf[halo:, :]

    cur = lambda n: jnp.minimum(n, nt - 1)
    col_blk = lambda hp, n: (cur(n), g * N_PAIRS + hp)
    halo_blk = lambda hp, n: (jnp.maximum(cur(n) * nq - 1, 0), g * N_PAIRS + hp)
    out_kv = lambda hp, n: (jnp.maximum(n - 1, 0), g * N_PAIRS + hp)
    small = pl.BlockSpec((tile, BLK), lambda hp, n: (cur(n), 0))
    hbm = pl.BlockSpec(memory_space=pl.ANY)
    in_specs = [pl.BlockSpec((tile, BLK), col_blk),
                pl.BlockSpec((tile, BLK), col_blk), pl.BlockSpec((halo, BLK), halo_blk),
                pl.BlockSpec((tile, BLK), col_blk), pl.BlockSpec((halo, BLK), halo_blk),
                pl.BlockSpec((BLK, 2 * BLK), lambda hp, n: (0, 0)),
                pl.BlockSpec((1, 8, 2 * BLK), lambda hp, n: (hp, 0, 0)),
                pl.BlockSpec((tile, BLK), lambda hp, n: (cur(n), hp)), small, small, hbm]
    args = [q, k_all, k_all, v_all, v_all, _band_table(d), _slope_table(), do, lse, dd, dhb]
    aliases = {10: 0}
    if not first:
        in_specs += [hbm, hbm]
        args += [dk_all, dv_all]
        aliases.update({11: 1, 12: 2})
    tile_f32 = pltpu.VMEM((tile, BLK), F32)
    wide_f32 = pltpu.VMEM((tile + halo, BLK), F32)
    return pl.pallas_call(
        body, name=f"attn_bwd_g{g}", grid=(N_PAIRS, nt + 1), in_specs=in_specs,
        out_specs=[pl.BlockSpec((tile, BLK), col_blk), pl.BlockSpec((tile, BLK), out_kv),
                   pl.BlockSpec((tile, BLK), out_kv)],
        out_shape=[_sds((s, 4 * D), BF16), _sds((s, 3 * D), BF16), _sds((s, 3 * D), BF16)],
        scratch_shapes=[tile_f32, tile_f32, wide_f32, wide_f32, tile_f32, wide_f32, wide_f32, tile_f32, tile_f32],
        input_output_aliases=aliases, compiler_params=_params(40, 2),
    )(*args)


def _b_in_bwd(dr2, dhb, dk_all, dv_all, wb_g, wkv_g, n1, rstd1, pg0, tm):
    s = dr2.shape[0]
    last = s // tm - 1
    nb_, nkv = wb_g.shape[2], wkv_g.shape[2]
    half = N_DEV // 2

    def body(dr2_ref, dh_ref, dk_ref, dv_ref, wb_hbm, wkv_hbm, n_ref, rstd_ref, pg_ref,
             dr_ref, drb_ref, dpg_ref, dpb_ref, dbo_ref, wb, wkv):
        i = pl.program_id(0)

        @pl.when(i == 0)
        def _():
            pltpu.sync_copy(wb_hbm, wb)
            pltpu.sync_copy(wkv_hbm, wkv)

        _acc_init(i, dpg_ref, dpb_ref, dbo_ref)
        acc = ALPHA * dr2_ref[...]
        for j in range(N_DEV):
            acc = acc + _dot_nt(dh_ref[:, nb_ * j:nb_ * (j + 1)], wb[j])
            src = dk_ref if j < half else dv_ref
            jj = j % half
            acc = acc + _dot_nt(src[:, nkv * jj:nkv * (jj + 1)], wkv[j])
        n = n_ref[...]
        dpg_ref[...] += _rowsum8(acc * n)
        dpb_ref[...] += _rowsum8(acc)
        dr = _ln_bwd(acc * pg_ref[...], n, rstd_ref[...])
        dr_ref[...] = dr
        drb_ref[...] = dr.astype(BF16)
        dbo_ref[...] += _rowsum8(dr)
        _acc_finish(i, last, dpg_ref, dpb_ref, dbo_ref)

    row = lambda i: (i, 0)
    hbm = pl.BlockSpec(memory_space=pl.ANY)
    acc_spec = pl.BlockSpec((8, D), lambda i: (0, 0))
    return pl.pallas_call(
        body, name="b_in_bwd", grid=(s // tm,),
        in_specs=[pl.BlockSpec((tm, D), row), pl.BlockSpec((tm, 4 * D), row), pl.BlockSpec((tm, 3 * D), row),
                  pl.BlockSpec((tm, 3 * D), row), hbm, hbm, pl.BlockSpec((tm, D), row), pl.BlockSpec((tm, 1), row),
                  pl.BlockSpec((1, D), lambda i: (0, 0))],
        out_specs=[pl.BlockSpec((tm, D), row), pl.BlockSpec((tm, D), row), acc_spec, acc_spec, acc_spec],
        out_shape=[_sds((s, D), F32), _sds((s, D), BF16)] + [_sds((8, D), F32)] * 3,
        scratch_shapes=[pltpu.VMEM(wb_g.shape, BF16), pltpu.VMEM(wkv_g.shape, BF16)],
        compiler_params=_params(56, 1),
    )(dr2, dhb, dk_all, dv_all, wb_g, wkv_g, n1, rstd1, pg0)


def _a_out_bwd(drb, w, u1, h, ln_g, ln_b, tm, plan=None):
    s = drb.shape[0]
    last = s // tm - 1

    def body(dr_ref, w_ref, u1_ref, z_ref, g_ref, b_ref, du1_ref, dh_ref, dg_ref, db_ref, dbz_ref):
        i = pl.program_id(0)
        _acc_init(i, dg_ref, db_ref, dbz_ref)
        dga = _dot_nt(dr_ref[...], w_ref[...])
        n, rstd = _ln_stats(u1_ref[...])
        pre = n * g_ref[...] + b_ref[...]
        sp = _sigmoid(pre)
        zz = z_ref[...]
        sz = _sigmoid(zz)
        dz = dga * (pre * sp) * _dsilu(zz, sz)
        dh_ref[...] = dz.astype(BF16)
        dbz_ref[...] += _rowsum8(dz)
        dpre = dga * (zz * sz) * _dsilu(pre, sp)
        dg_ref[...] += _rowsum8(dpre * n)
        db_ref[...] += _rowsum8(dpre)
        du1_ref[...] = _ln_bwd(dpre * g_ref[...], n, rstd)
        _acc_finish(i, last, dg_ref, db_ref, dbz_ref)

    row = lambda i: (i, 0)
    vec = pl.BlockSpec((1, D), lambda i: (0, 0))
    acc_spec = pl.BlockSpec((8, D), lambda i: (0, 0))
    return _planned_call(
        plan, (drb, w, u1, h, ln_g, ln_b), body, name="a_out_bwd", grid=(s // tm,),
        in_specs=[pl.BlockSpec((tm, D), row), pl.BlockSpec((D, D), lambda i: (0, 0)), pl.BlockSpec((tm, D), row),
                  pl.BlockSpec((tm, D), lambda i: (i, 2)), vec, vec],
        out_specs=[pl.BlockSpec((tm, D), row), pl.BlockSpec((tm, D), lambda i: (i, 2)),
                   acc_spec, acc_spec, acc_spec],
        out_shape=[_sds((s, D), F32), _sds((s, 3 * D), BF16)] + [_sds((8, D), F32)] * 3,
        compiler_params=_params(32, 1))


def _a_conv_bwd(du1, u0, h, dha, w_dw, tm, plan=None):
    s = du1.shape[0]
    steps = s // tm
    per = tm // CONV_HALO
    pad = CONV_W - 1

    def body(du_ref, dun_ref, u0_ref, u0p_ref, h_ref, w_ref, dha_hbm,
             dh_ref, dw_ref, dbdw_ref, dba_ref, dbg_ref, dbuf, ubuf, wacc, dshs, ushs, wb, du0_buf):
        i = pl.program_id(0)
        _spread_taps(i, w_ref, wb)
        _acc_init(i, dbdw_ref, dba_ref, dbg_ref, wacc)
        dbuf[0:tm, :] = du_ref[...]
        dbuf[tm:, :] = jnp.where(i < steps - 1, dun_ref[...], 0.0)
        ubuf[0:CONV_HALO, :] = jnp.where(i > 0, u0p_ref[...], 0.0)
        ubuf[CONV_HALO:, :] = u0_ref[...]
        _shifted_copies(dbuf, dshs, tm)
        _shifted_copies(ubuf, ushs, tm)
        off = CONV_HALO - pad
        grp = CONV_ROWS // SUBLANES
        d_windows = _tap_windows([pad - j for j in range(CONV_W)])
        u_windows = _tap_windows([off + j for j in range(CONV_W)])

        def chunk(it, carry):
            r0 = pl.multiple_of((it // (D // LANES)) * CONV_ROWS, CONV_ROWS)
            lanes = pl.ds(pl.multiple_of((it % (D // LANES)) * LANES, LANES), LANES)
            rows = pl.ds(r0, CONV_ROWS)
            acc = jnp.zeros((grp, SUBLANES, LANES), F32)
            for s, lo, span, taps in d_windows:
                win = _window(dbuf, dshs, s, r0, lo, span, lanes)
                for j, a in taps:
                    acc = acc + wb[j, :, lanes][None] * win[a:a + grp]
            du0_buf[rows, lanes] = acc.reshape(CONV_ROWS, LANES)
            du3 = du_ref[rows, lanes].reshape(grp, SUBLANES, LANES)
            for s, lo, span, taps in u_windows:
                win = _window(ubuf, ushs, s, r0, lo, span, lanes)
                for j, a in taps:
                    wacc[j, :, lanes] += (du3 * win[a:a + grp]).sum(axis=0)
            return carry

        lax.fori_loop(0, (tm // CONV_ROWS) * (D // LANES), chunk, 0)
        du0 = du0_buf[...]
        sg = _sigmoid(h_ref[:, D:2 * D])
        da = du0 * sg
        dag = du0 * h_ref[:, 0:D] * (sg * (1.0 - sg))
        dh_ref[:, 0:D] = da.astype(BF16)
        dh_ref[:, D:2 * D] = dag.astype(BF16)
        dbdw_ref[...] += _rowsum8(du_ref[...])
        dba_ref[...] += _rowsum8(da)
        dbg_ref[...] += _rowsum8(dag)
        _acc_finish(i, steps - 1, dbdw_ref, dba_ref, dbg_ref)

        @pl.when(i == steps - 1)
        def _():
            for j in range(CONV_W):
                dw_ref[j:j + 1, :] = jnp.sum(wacc[j], axis=0, keepdims=True)
            dw_ref[CONV_W:, :] = jnp.zeros((32 - CONV_W, D), F32)

    row = lambda i: (i, 0)
    acc_spec = pl.BlockSpec((8, D), lambda i: (0, 0))
    return _planned_call(
        plan, (du1, du1, u0, u0, h, w_dw, dha), body, name="a_conv_bwd", grid=(steps,),
        in_specs=[pl.BlockSpec((tm, D), row),
                  pl.BlockSpec((CONV_HALO, D), lambda i: (jnp.minimum((i + 1) * per, s // CONV_HALO - 1), 0)),
                  pl.BlockSpec((tm, D), row),
                  pl.BlockSpec((CONV_HALO, D), lambda i: (jnp.maximum(i * per - 1, 0), 0)),
                  pl.BlockSpec((tm, 2 * D), lambda i: (i, 0)),
                  pl.BlockSpec((32, D), lambda i: (0, 0)), pl.BlockSpec(memory_space=pl.ANY)],
        out_specs=[pl.BlockSpec((tm, 2 * D), lambda i: (i, 0)),
                   pl.BlockSpec((32, D), lambda i: (0, 0)), acc_spec, acc_spec, acc_spec],
        out_shape=[_sds((s, 3 * D), BF16), _sds((32, D), F32)] + [_sds((8, D), F32)] * 3,
        scratch_shapes=[pltpu.VMEM((tm + CONV_HALO, D), F32), pltpu.VMEM((tm + CONV_HALO, D), F32),
                        pltpu.VMEM((CONV_W, 8, D), F32),
                        pltpu.VMEM((SUBLANES - 1, tm + CONV_HALO, D), F32),
                        pltpu.VMEM((SUBLANES - 1, tm + CONV_HALO, D), F32),
                        pltpu.VMEM((CONV_W, SUBLANES, D), F32), pltpu.VMEM((tm, D), F32)],
        input_output_aliases={6: 0}, compiler_params=_params(56, 1))


def _a_in_bwd(dr1, dha, w_g, tm, plan=None):
    s = dr1.shape[0]
    npd = w_g.shape[2]

    def body(dr_ref, dh_ref, w_ref, o_ref):
        acc = ALPHA * dr_ref[...]
        for j in range(N_DEV):
            acc = acc + _dot_nt(dh_ref[:, npd * j:npd * (j + 1)], w_ref[j])
        o_ref[...] = acc

    row = lambda i: (i, 0)
    return _planned_call(
        plan, (dr1, dha, w_g), body, name="a_in_bwd", grid=(s // tm,),
        in_specs=[pl.BlockSpec((tm, D), row), pl.BlockSpec((tm, 3 * D), row),
                  pl.BlockSpec(w_g.shape, lambda i: (0, 0, 0))],
        out_specs=[pl.BlockSpec((tm, D), row)], out_shape=[_sds((s, D), F32)],
        compiler_params=_params(36, 1))


def _wgrad(name, a, b, npd, ts, total=None, at=0, into=None):
    s = a.shape[0]
    n_blk = b.shape[1] // npd
    total = n_blk if total is None else total
    assert at % n_blk == 0

    def body(*refs):
        a_ref, b_ref = refs[:2]
        o_ref = refs[-1]
        si = pl.program_id(0)

        @pl.when(si == 0)
        def _():
            o_ref[...] = jnp.zeros(o_ref.shape, F32)

        a_t = a_ref[...].T
        for j in range(n_blk):
            o_ref[j] += _dot(a_t, b_ref[:, npd * j:npd * (j + 1)])

    in_specs = [pl.BlockSpec((ts, D), lambda si: (si, 0)), pl.BlockSpec((ts, n_blk * npd), lambda si: (si, 0))]
    args = [a, b]
    aliases = {}
    if into is not None:
        in_specs.append(pl.BlockSpec(memory_space=pl.ANY))
        args.append(into)
        aliases = {2: 0}
    return pl.pallas_call(
        body, name=name, grid=(s // ts,), in_specs=in_specs,
        out_specs=pl.BlockSpec((n_blk, D, npd), lambda si: (at // n_blk, 0, 0)),
        out_shape=_sds((total, D, npd), F32), input_output_aliases=aliases,
        compiler_params=_params(56, 1),
    )(*args)


SMALL_ROWS = 40
GRAD_ROWS = 48


def kernel(x, a_w_in, a_b_in, a_w_dw, a_b_dw, a_ln_g, a_ln_b, a_w_out, a_b_out, kv_w, b_w_in, b_w_out, b_b_out, post_ln_g, post_ln_b, loss_target, m_a_w_in, m_a_b_in, m_a_w_dw, m_a_b_dw, m_a_ln_g, m_a_ln_b, m_a_w_out, m_a_b_out, m_kv_w, m_b_w_in, m_b_w_out, m_b_b_out, m_post_ln_g, m_post_ln_b, v_a_w_in, v_a_b_in, v_a_w_dw, v_a_b_dw, v_a_ln_g, v_a_ln_b, v_a_w_out, v_a_b_out, v_kv_w, v_b_w_in, v_b_w_out, v_b_b_out, v_post_ln_g, v_post_ln_b):
    s = x.shape[1]
    assert x.shape == (1, s, D) and s % (DILATIONS[-1] * BLK) == 0
    xs = x.reshape(s, D)
    tgt = loss_target.reshape(s, D)
    me = 4 * lax.axis_index("x") + 2 * lax.axis_index("y") + lax.axis_index("c")
    c_idx = lax.axis_index("c").astype(jnp.int32).reshape(1)

    def small_pack(b_in, w_dw, b_dw, ln_g, ln_b, b_out):
        rows = [b_in.reshape(3, BLK), w_dw.reshape(CONV_W, BLK), b_dw.reshape(1, BLK), ln_g.reshape(1, BLK),
                ln_b.reshape(1, BLK), b_out.reshape(1, BLK)]
        n = sum(r.shape[0] for r in rows)
        return jnp.concatenate(rows + [jnp.zeros((SMALL_ROWS - n, BLK), F32)], axis=0)

    wa_in, sm, *later = _all_gather(
        "gather_first", [a_w_in[0], small_pack(a_b_in, a_w_dw, a_b_dw, a_ln_g, a_ln_b, a_b_out)], [BF16, F32],
        casts=[a_w_out[0], kv_w, b_w_in[0], b_w_out[0]])
    ba_in = sm[:, 0:3, :].reshape(1, 3 * D)
    w_dw = jnp.concatenate([sm[:, 3:3 + CONV_W, :].transpose(1, 0, 2).reshape(CONV_W, D), jnp.zeros((1, D), F32)], axis=0)
    b_dw, ln_g, ln_b, ba_out = (sm[:, 34 + k, :].reshape(1, D) for k in range(4))
    pg0, pg1 = post_ln_g[0:1], post_ln_g[1:2]
    pb0, pb1 = post_ln_b[0:1], post_ln_b[1:2]

    h_a, u0, xb = _a_in_proj(xs, wa_in, ba_in, 512)
    (u1, g_a), (wa_out, wkv, wb_in, wb_out) = _a_conv_gate(u0, h_a, w_dw, b_dw, ln_g, ln_b, 256, _gather_plan(later))
    wa_out = wa_out.reshape(D, D)
    wb_out = wb_out.reshape(D, D)
    n1, rstd1, x1b = _a_out_proj(g_a, wa_out, ba_out, xs, pg0, pb0, 512)
    k_all, v_all = _kv_proj(x1b, wkv, 512)
    q, z_b = _b_in_proj(x1b, wb_in, 512)
    o, lse = _attn_forward(q, k_all, v_all, None, None, None, 0, True, False)
    o, lse = _attn_forward(q, k_all, v_all, o, lse, None, 1, False, False)
    o, lse, g_b = _attn_forward(q, k_all, v_all, o, lse, z_b, 2, False, True)
    dr2, dr2b, loss8, dpg1, dpb1, dbb_out = _b_out_loss(g_b, wb_out, b_b_out, n1, pg0, pb0, pg1, pb1, tgt, 512)

    do, dhb, dd = _b_out_bwd(dr2b, wb_out, z_b, o, 512)
    dk_all = dv_all = None
    for g in range(3):
        dhb, dk_all, dv_all = _attn_backward(q, k_all, v_all, do, lse, dd, dhb, dk_all, dv_all, g)
    dr1, dr1b, dpg0, dpb0, dba_out = _b_in_bwd(dr2, dhb, dk_all, dv_all, wb_in, wkv, n1, rstd1, pg0, 256)

    p_kv = _wgrad("wgrad_k", x1b, dk_all, 768, 1024, total=N_DEV)
    p_kv = _wgrad("wgrad_v", x1b, dv_all, 768, 1024, total=N_DEV, at=N_DEV // 2, into=p_kv)
    p_b_in = _wgrad("wgrad_b_in", x1b, dhb, 512, 512)
    p_a_out = _wgrad("wgrad_a_out", g_a, dr1b, D, 1024).reshape(N_DEV, BLK, D)
    p_b_out = _wgrad("wgrad_b_out", g_b, dr2b, D, 1024).reshape(N_DEV, BLK, D)
    parts = [p_kv, p_b_in, p_a_out, p_b_out]
    (du1, dha, dln_g, dln_b, dbz), from_sibling = _a_out_bwd(dr1b, wa_out, u1, h_a, ln_g, ln_b, 512, _sibling_plan(parts))
    chip_sums = [_pair_add(f"pair_add_{k}", p, r, c_idx) for k, (p, r) in enumerate(zip(parts, from_sibling))]
    (dha, dw_dw, db_dw, dba, dbg), from_chips = _a_conv_bwd(du1, u0, h_a, dha, w_dw, 256, _chips_plan(chip_sums))
    p_a_in = _wgrad("wgrad_a_in", xb, dha, 384, 1024)
    (from_sibling_a,) = _exchange_sibling("reduce_sibling_a_in", [p_a_in])
    sum_a = _pair_add("pair_add_a_in", p_a_in, from_sibling_a, c_idx)
    (grad_x,), (from_chips_a,) = _a_in_bwd(dr1, dha, wa_in, 512, _chips_plan([sum_a]))

    reduced = [from_chips_a] + from_chips
    big_w = [a_w_in[0], kv_w, b_w_in[0], a_w_out[0], b_w_out[0]]
    big_m = [m_a_w_in[0], m_kv_w, m_b_w_in[0], m_a_w_out[0], m_b_w_out[0]]
    big_v = [v_a_w_in[0], v_kv_w, v_b_w_in[0], v_a_w_out[0], v_b_w_out[0]]
    big = [_sum_adamw(f"adamw_{k}", reduced[k], big_w[k], big_m[k], big_v[k]) for k in range(5)]

    rows = [dba[0:1], dbg[0:1], dbz[0:1], dw_dw[0:CONV_W], db_dw[0:1], dln_g[0:1], dln_b[0:1], dba_out[0:1],
            dbb_out[0:1], dpg0[0:1], dpg1[0:1], dpb0[0:1], dpb1[0:1], loss8[0:1]]
    n_rows = sum(r.shape[0] for r in rows)
    gpack = jnp.concatenate(rows + [jnp.zeros((GRAD_ROWS - n_rows, D), F32)], axis=0)
    (gall,) = _all_gather("gather_small_grads", [gpack], [F32])
    gs = _small_sum("small_sum", gall)
    loss = gs[43, 0]

    def my(vec, width):
        return lax.dynamic_slice_in_dim(vec, me * width, width, axis=-1)

    g_small = [my(gs[0:3].reshape(1, 3 * D), 384), my(gs[3:34], BLK)[None], my(gs[34:35], BLK), my(gs[35:36], BLK),
               my(gs[36:37], BLK), my(gs[37:38], BLK), gs[38:39], gs[39:41], gs[41:43]]
    w_small = [a_b_in, a_w_dw, a_b_dw, a_ln_g, a_ln_b, a_b_out, b_b_out, post_ln_g, post_ln_b]
    m_small = [m_a_b_in, m_a_w_dw, m_a_b_dw, m_a_ln_g, m_a_ln_b, m_a_b_out, m_b_b_out, m_post_ln_g, m_post_ln_b]
    v_small = [v_a_b_in, v_a_w_dw, v_a_b_dw, v_a_ln_g, v_a_ln_b, v_a_b_out, v_b_b_out, v_post_ln_g, v_post_ln_b]
    sizes = [math.prod(w.shape) for w in w_small]
    total = sum(sizes)
    padded = -(-total // (8 * BLK)) * (8 * BLK)

    def flat(parts_):
        return jnp.concatenate([p.reshape(-1) for p in parts_] + [jnp.ones((padded - total,), F32)]).reshape(-1, BLK)

    sd, sm_new, sv_new = _small_adamw("adamw_small", flat(w_small), flat(g_small), flat(m_small), flat(v_small))

    def unflat(packed):
        out, pos = [], 0
        vec = packed.reshape(-1)
        for w, n in zip(w_small, sizes):
            out.append(vec[pos:pos + n].reshape(w.shape))
            pos += n
        return out

    g_small = [g.reshape(w.shape) for g, w in zip(g_small, w_small)]
    d_small, nm_small, nv_small = unflat(sd), unflat(sm_new), unflat(sv_new)

    def ordered(bigs, smalls):
        a_in, kvw, b_in, a_out, b_out = bigs
        return [a_in[None], smalls[0], smalls[1], smalls[2], smalls[3], smalls[4], a_out[None], smalls[5],
                kvw, b_in[None], b_out[None], smalls[6], smalls[7], smalls[8]]

    grads = ordered([b[0] for b in big], g_small)
    deltas = ordered([b[1] for b in big], d_small)
    new_m = ordered([b[2] for b in big], nm_small)
    new_v = ordered([b[3] for b in big], nv_small)
    return (loss, grad_x.reshape(1, s, D), *grads, *deltas, *new_m, *new_v)
```

```python
import math

import numpy as np
import jax
import jax.numpy as jnp
from jax import lax
from jax.experimental import pallas as pl
from jax.experimental.pallas import tpu as pltpu

F32 = jnp.float32
BF16 = jnp.bfloat16
MESH = pl.DeviceIdType.MESH

D = 1024
N_DEV = 8
HEAD_DIM = 64
N_HEADS = 16
DILATIONS = (1, 4, 16)
BLK = 128
CONV_W = 31
ALPHA = (2.0 * 2) ** 0.25
LN_EPS = 1e-5
SLOPES = tuple(2.0 ** (-8.0 * (h + 1) / N_HEADS) for h in range(N_HEADS))
NEG = -1e30

ADAM_LR = 0.001
ADAM_B1 = 0.9
ADAM_B2 = 0.999
ADAM_EPS = 1e-08
ADAM_WD = 0.01
ADAM_STEP = 10

VMEM_CAP_MB = 64


def _params(vmem_mb, n_grid=0):
    sem = ("arbitrary",) * n_grid if n_grid else None
    return pltpu.CompilerParams(dimension_semantics=sem, vmem_limit_bytes=min(vmem_mb, VMEM_CAP_MB - 6) * 2 ** 20)


def _sds(shape, dtype):
    return jax.ShapeDtypeStruct(tuple(shape), dtype)


def _sigmoid(v):
    return jax.nn.sigmoid(v)


def _dsilu(v, s):
    return s * (1.0 + v * (1.0 - s))


def _ln_stats(r):
    mu = jnp.mean(r, axis=-1, keepdims=True)
    xc = r - mu
    var = jnp.mean(xc * xc, axis=-1, keepdims=True)
    rstd = lax.rsqrt(var + LN_EPS)
    return xc * rstd, rstd


def _ln_bwd(dn, n, rstd):
    m1 = jnp.mean(dn, axis=-1, keepdims=True)
    m2 = jnp.mean(dn * n, axis=-1, keepdims=True)
    return rstd * (dn - m1 - n * m2)


def _rowsum8(v):
    tm, c = v.shape
    return v.reshape(tm // 8, 8, c).sum(axis=0)


def _acc_init(i, *refs):
    @pl.when(i == 0)
    def _():
        for r in refs:
            r[...] = jnp.zeros(r.shape, r.dtype)


def _acc_finish(i, last, *refs):
    @pl.when(i == last)
    def _():
        for r in refs:
            r[...] = jnp.broadcast_to(jnp.sum(r[...], axis=0, keepdims=True), r.shape)


def _dot(a, b):
    return jnp.dot(a, b, preferred_element_type=F32)


def _dot_nt(a, b):
    return lax.dot_general(a, b, (((1,), (1,)), ((), ())), preferred_element_type=F32)


def _place():
    return lax.axis_index("x"), lax.axis_index("y"), lax.axis_index("c")


def _all_gather(name, arrays, dtypes, casts=()):
    n = len(arrays)
    nc = len(casts)

    def body(*refs):
        ins, cast_ins = refs[:n], refs[n:n + nc]
        outs, cast_outs = refs[n + nc:2 * n + nc], refs[2 * n + nc:2 * (n + nc)]
        stages = refs[2 * (n + nc):3 * n + 2 * nc]
        send_sems, recv_sems, local_sems = refs[3 * n + 2 * nc:]
        x, y, c = _place()
        me, sibling = (x, y, c), (x, y, 1 - c)
        chips = [(1 - x, y), (x, 1 - y), (1 - x, 1 - y)]

        def slot(ref, p):
            return ref.at[4 * p[0] + 2 * p[1] + p[2]]

        def copy(a, k, block, to, src=None):
            return pltpu.make_async_remote_copy(
                src_ref=slot(outs[a], block) if src is None else src, dst_ref=slot(outs[a], block),
                send_sem=send_sems.at[a, k], recv_sem=recv_sems.at[a, k], device_id=to, device_id_type=MESH)

        first, mine = [], []
        for a in range(n):
            stages[a][...] = ins[a][...].astype(stages[a].dtype)
            cp = pltpu.make_async_copy(stages[a], slot(outs[a], me), local_sems.at[a])
            cp.start()
            mine.append(cp)
            first.append(copy(a, 0, me, sibling, src=stages[a]))
            first += [copy(a, 1 + j, me, (*chip, c), src=stages[a]) for j, chip in enumerate(chips)]
        for cp in first:
            cp.start()
        for src, dst in zip(cast_ins, cast_outs):
            dst[...] = src[...].astype(BF16)
        passed = []
        for j, chip in enumerate(chips):
            for a in range(n):
                copy(a, 1 + j, (*chip, c), me).wait_recv()
                cp = copy(a, 4 + j, (*chip, c), sibling)
                cp.start()
                passed.append(cp)
        for a in range(n):
            copy(a, 0, sibling, me).wait_recv()
            for j, chip in enumerate(chips):
                copy(a, 4 + j, (*chip, 1 - c), me).wait_recv()
        for cp in first + passed:
            cp.wait_send()
        for cp in mine:
            cp.wait()

    vmem_bytes = sum(math.prod(a.shape) * (jnp.dtype(a.dtype).itemsize + jnp.dtype(dt).itemsize)
                     for a, dt in zip(arrays, dtypes)) + sum(math.prod(a.shape) * 6 for a in casts)
    vm = pl.BlockSpec(memory_space=pltpu.VMEM)
    return pl.pallas_call(
        body, name=name,
        out_shape=[_sds((N_DEV,) + a.shape, dt) for a, dt in zip(arrays, dtypes)] + [_sds(a.shape, BF16) for a in casts],
        in_specs=[vm] * (n + nc),
        out_specs=[pl.BlockSpec(memory_space=pl.ANY)] * n + [vm] * nc,
        scratch_shapes=[pltpu.VMEM(a.shape, dt) for a, dt in zip(arrays, dtypes)]
        + [pltpu.SemaphoreType.DMA((n, 7)), pltpu.SemaphoreType.DMA((n, 7)), pltpu.SemaphoreType.DMA((n,))],
        compiler_params=_params(vmem_bytes // 2 ** 20 + 8),
    )(*arrays, *casts)


class _Plan:
    def __init__(self, args, out_shape, scratch, start, mid, finish):
        self.args, self.out_shape, self.scratch = list(args), list(out_shape), list(scratch)
        self.start, self.mid, self.finish = start, mid, finish


def _gather_plan(shards):
    n = len(shards)

    def copies(ins, outs, sems):
        send_sems, recv_sems, local_sems = sems
        x, y, c = _place()
        me, sibling = (x, y, c), (x, y, 1 - c)
        chips = [(1 - x, y), (x, 1 - y), (1 - x, 1 - y)]

        def slot(ref, p):
            return ref.at[4 * p[0] + 2 * p[1] + p[2]]

        def copy(a, k, block, to, src=None):
            return pltpu.make_async_remote_copy(
                src_ref=slot(outs[a], block) if src is None else src, dst_ref=slot(outs[a], block),
                send_sem=send_sems.at[a, k], recv_sem=recv_sems.at[a, k], device_id=to, device_id_type=MESH)

        mine = [pltpu.make_async_copy(ins[a], slot(outs[a], me), local_sems.at[a]) for a in range(n)]
        first = [copy(a, 0, me, sibling, src=ins[a]) for a in range(n)]
        first += [copy(a, 1 + j, me, (*chip, c), src=ins[a]) for a in range(n) for j, chip in enumerate(chips)]
        arrive = [copy(a, 1 + j, (*chip, c), me) for j, chip in enumerate(chips) for a in range(n)]
        passed = [copy(a, 4 + j, (*chip, c), sibling) for j, chip in enumerate(chips) for a in range(n)]
        from_sibling = [copy(a, 0, sibling, me) for a in range(n)]
        from_sibling += [copy(a, 4 + j, (*chip, 1 - c), me) for a in range(n) for j, chip in enumerate(chips)]
        return mine, first, arrive, passed, from_sibling

    def start(ins, outs, sems):
        mine, first, _, _, _ = copies(ins, outs, sems)
        for cp in mine + first:
            cp.start()

    def mid(ins, outs, sems):
        _, _, arrive, passed, _ = copies(ins, outs, sems)
        for got, on in zip(arrive, passed):
            got.wait_recv()
            on.start()

    def finish(ins, outs, sems):
        mine, first, _, passed, from_sibling = copies(ins, outs, sems)
        for cp in from_sibling:
            cp.wait_recv()
        for cp in first + passed:
            cp.wait_send()
        for cp in mine:
            cp.wait()

    return _Plan(shards, [_sds((N_DEV,) + a.shape, a.dtype) for a in shards],
                 [pltpu.SemaphoreType.DMA((n, 7)), pltpu.SemaphoreType.DMA((n, 7)), pltpu.SemaphoreType.DMA((n,))],
                 start, mid, finish)


def _sibling_plan(parts):
    n = len(parts)

    def copies(ins, outs, sems):
        send_sems, recv_sems = sems
        x, y, c = _place()
        return [pltpu.make_async_remote_copy(
            src_ref=ins[a].at[2 * p + 1 - c], dst_ref=outs[a].at[p], send_sem=send_sems.at[a, p],
            recv_sem=recv_sems.at[a, p], device_id=(x, y, 1 - c), device_id_type=MESH)
            for a in range(n) for p in range(4)]

    def start(ins, outs, sems):
        for cp in copies(ins, outs, sems):
            cp.start()

    def finish(ins, outs, sems):
        cps = copies(ins, outs, sems)
        for cp in cps:
            cp.wait_recv()
        for cp in cps:
            cp.wait_send()

    return _Plan(parts, [_sds((4,) + p.shape[1:], p.dtype) for p in parts],
                 [pltpu.SemaphoreType.DMA((n, 4)), pltpu.SemaphoreType.DMA((n, 4))], start, None, finish)


def _chips_plan(sums):
    n = len(sums)

    def copies(ins, outs, sems):
        send_sems, recv_sems, local_sems = sems
        x, y, c = _place()
        my_chip = 2 * x + y
        chips = [(1 - x, y), (x, 1 - y), (1 - x, 1 - y)]
        mine = [pltpu.make_async_copy(ins[a].at[my_chip], outs[a].at[my_chip], local_sems.at[a]) for a in range(n)]
        remote = [pltpu.make_async_remote_copy(
            src_ref=ins[a].at[2 * px + py], dst_ref=outs[a].at[my_chip], send_sem=send_sems.at[a, k],
            recv_sem=recv_sems.at[a, k], device_id=(px, py, c), device_id_type=MESH)
            for a in range(n) for k, (px, py) in enumerate(chips)]
        return mine, remote

    def start(ins, outs, sems):
        mine, remote = copies(ins, outs, sems)
        for cp in mine + remote:
            cp.start()

    def finish(ins, outs, sems):
        mine, remote = copies(ins, outs, sems)
        for cp in remote:
            cp.wait_recv()
        for cp in remote:
            cp.wait_send()
        for cp in mine:
            cp.wait()

    return _Plan(sums, [_sds(s.shape, s.dtype) for s in sums],
                 [pltpu.SemaphoreType.DMA((n, 3)), pltpu.SemaphoreType.DMA((n, 3)), pltpu.SemaphoreType.DMA((n,))],
                 start, None, finish)


def _planned_call(plan, args, body, *, name, grid, in_specs, out_specs, out_shape, scratch_shapes=(), mid_step=None,
                  **kw):
    in_specs, out_specs, out_shape = list(in_specs), list(out_specs), list(out_shape)
    scratch_shapes = list(scratch_shapes)
    if plan is None:
        res = pl.pallas_call(body, name=name, grid=grid, in_specs=in_specs, out_specs=out_specs, out_shape=out_shape,
                             scratch_shapes=scratch_shapes, **kw)(*args)
        return list(res), []
    n_in, n_out, n_scr = len(in_specs), len(out_specs), len(scratch_shapes)
    p_in, p_out = len(plan.args), len(plan.out_shape)
    steps = grid[0]

    def fused(*refs):
        ins, pins = refs[:n_in], refs[n_in:n_in + p_in]
        o0 = n_in + p_in
        outs, pouts = refs[o0:o0 + n_out], refs[o0 + n_out:o0 + n_out + p_out]
        s0 = o0 + n_out + p_out
        scr, pscr = refs[s0:s0 + n_scr], refs[s0 + n_scr:]
        i = pl.program_id(0)

        @pl.when(i == 0)
        def _():
            plan.start(pins, pouts, pscr)

        body(*ins, *outs, *scr)
        if plan.mid is not None:
            @pl.when(i == mid_step)
            def _():
                plan.mid(pins, pouts, pscr)

        @pl.when(i == steps - 1)
        def _():
            plan.finish(pins, pouts, pscr)

    hbm = pl.BlockSpec(memory_space=pl.ANY)
    res = pl.pallas_call(
        fused, name=name, grid=grid, in_specs=in_specs + [hbm] * p_in, out_specs=out_specs + [hbm] * p_out,
        out_shape=out_shape + plan.out_shape, scratch_shapes=scratch_shapes + plan.scratch, **kw)(*args, *plan.args)
    return list(res[:n_out]), list(res[n_out:])


def _exchange_sibling(name, parts):
    n = len(parts)

    def body(*refs):
        ins, outs = refs[:n], refs[n:2 * n]
        send_sems, recv_sems = refs[2 * n:]
        x, y, c = _place()
        copies = []
        for a in range(n):
            for p in range(4):
                copies.append(pltpu.make_async_remote_copy(
                    src_ref=ins[a].at[2 * p + 1 - c], dst_ref=outs[a].at[p],
                    send_sem=send_sems.at[a, p], recv_sem=recv_sems.at[a, p],
                    device_id=(x, y, 1 - c), device_id_type=MESH))
        for cp in copies:
            cp.start()
        for cp in copies:
            cp.wait_recv()
        for cp in copies:
            cp.wait_send()

    return pl.pallas_call(
        body, name=name,
        out_shape=[_sds((4,) + p.shape[1:], p.dtype) for p in parts],
        in_specs=[pl.BlockSpec(memory_space=pl.ANY)] * n,
        out_specs=[pl.BlockSpec(memory_space=pl.ANY)] * n,
        scratch_shapes=[pltpu.SemaphoreType.DMA((n, 4)), pltpu.SemaphoreType.DMA((n, 4))],
    )(*parts)


def _exchange_chips(name, sums):
    n = len(sums)

    def body(*refs):
        ins, outs = refs[:n], refs[n:2 * n]
        send_sems, recv_sems, local_sems = refs[2 * n:]
        x, y, c = _place()
        my_chip = 2 * x + y
        chips = [(1 - x, y), (x, 1 - y), (1 - x, 1 - y)]
        copies, mine = [], []
        for a in range(n):
            cp = pltpu.make_async_copy(ins[a].at[my_chip], outs[a].at[my_chip], local_sems.at[a])
            cp.start()
            mine.append(cp)
            for k, (px, py) in enumerate(chips):
                copies.append(pltpu.make_async_remote_copy(
                    src_ref=ins[a].at[2 * px + py], dst_ref=outs[a].at[my_chip],
                    send_sem=send_sems.at[a, k], recv_sem=recv_sems.at[a, k],
                    device_id=(px, py, c), device_id_type=MESH))
        for cp in copies:
            cp.start()
        for cp in copies:
            cp.wait_recv()
        for cp in copies:
            cp.wait_send()
        for cp in mine:
            cp.wait()

    return pl.pallas_call(
        body, name=name,
        out_shape=[_sds(s.shape, s.dtype) for s in sums],
        in_specs=[pl.BlockSpec(memory_space=pl.ANY)] * n,
        out_specs=[pl.BlockSpec(memory_space=pl.ANY)] * n,
        scratch_shapes=[pltpu.SemaphoreType.DMA((n, 3)), pltpu.SemaphoreType.DMA((n, 3)),
                        pltpu.SemaphoreType.DMA((n,))],
    )(*sums)


def _pair_add(name, part, recv, c_idx):
    _, r, c = part.shape
    tr = min(r, 256)

    def body(c_ref, a_ref, b_ref, o_ref):
        o_ref[...] = a_ref[...] + b_ref[...]

    return pl.pallas_call(
        body, name=name,
        grid_spec=pltpu.PrefetchScalarGridSpec(
            num_scalar_prefetch=1, grid=(4, r // tr),
            in_specs=[pl.BlockSpec((1, tr, c), lambda p, i, cr: (2 * p + cr[0], i, 0)),
                      pl.BlockSpec((1, tr, c), lambda p, i, cr: (p, i, 0))],
            out_specs=pl.BlockSpec((1, tr, c), lambda p, i, cr: (p, i, 0))),
        out_shape=_sds((4, r, c), F32),
        compiler_params=_params(16, 2),
    )(c_idx, part, recv)


def _adamw_math(w, g, m, v):
    m = ADAM_B1 * m + (1.0 - ADAM_B1) * g
    v = ADAM_B2 * v + (1.0 - ADAM_B2) * (g * g)
    m_hat = m / (1.0 - ADAM_B1 ** ADAM_STEP)
    v_hat = v / (1.0 - ADAM_B2 ** ADAM_STEP)
    delta = -ADAM_LR * (m_hat / (jnp.sqrt(v_hat) + ADAM_EPS) + ADAM_WD * w)
    return delta, m, v


def _sum_adamw(name, recv, w, m, v):
    r, c = w.shape
    tr = min(r, 256)

    def body(p_ref, w_ref, m_ref, v_ref, g_ref, d_ref, nm_ref, nv_ref):
        g = (p_ref[0] + p_ref[1]) + (p_ref[2] + p_ref[3])
        g_ref[...] = g
        d_ref[...], nm_ref[...], nv_ref[...] = _adamw_math(w_ref[...], g, m_ref[...], v_ref[...])

    blk = pl.BlockSpec((tr, c), lambda i: (i, 0))
    return pl.pallas_call(
        body, name=name, grid=(r // tr,),
        in_specs=[pl.BlockSpec((4, tr, c), lambda i: (0, i, 0)), blk, blk, blk],
        out_specs=[blk] * 4, out_shape=[_sds((r, c), F32)] * 4,
        compiler_params=_params(24, 1),
    )(recv, w, m, v)


def _small_sum(name, gathered):
    _, r, c = gathered.shape

    def body(p_ref, o_ref):
        acc = p_ref[0]
        for j in range(1, N_DEV):
            acc = acc + p_ref[j]
        o_ref[...] = acc

    return pl.pallas_call(body, name=name, out_shape=_sds((r, c), F32),
                          in_specs=[pl.BlockSpec(memory_space=pltpu.VMEM)],
                          out_specs=pl.BlockSpec(memory_space=pltpu.VMEM))(gathered)


def _small_adamw(name, w, g, m, v):
    def body(w_ref, g_ref, m_ref, v_ref, d_ref, nm_ref, nv_ref):
        d_ref[...], nm_ref[...], nv_ref[...] = _adamw_math(w_ref[...], g_ref[...], m_ref[...], v_ref[...])

    vm = pl.BlockSpec(memory_space=pltpu.VMEM)
    return pl.pallas_call(body, name=name, out_shape=[_sds(w.shape, F32)] * 3,
                          in_specs=[vm] * 4, out_specs=[vm] * 3)(w, g, m, v)


def _a_in_proj(x, w_g, b_full, tm):
    s = x.shape[0]
    npd = w_g.shape[2]

    def body(x_ref, w_ref, b_ref, h_ref, u0_ref, xb_ref):
        xb = x_ref[...].astype(BF16)
        xb_ref[...] = xb
        for j in range(N_DEV):
            sl = slice(npd * j, npd * (j + 1))
            h_ref[:, sl] = _dot(xb, w_ref[j]) + b_ref[:, sl]
        u0_ref[...] = h_ref[:, 0:D] * _sigmoid(h_ref[:, D:2 * D])

    row = lambda i: (i, 0)
    return pl.pallas_call(
        body, name="a_in_proj", grid=(s // tm,),
        in_specs=[pl.BlockSpec((tm, D), row), pl.BlockSpec(w_g.shape, lambda i: (0, 0, 0)),
                  pl.BlockSpec((1, 3 * D), lambda i: (0, 0))],
        out_specs=[pl.BlockSpec((tm, 3 * D), row), pl.BlockSpec((tm, D), row), pl.BlockSpec((tm, D), row)],
        out_shape=[_sds((s, 3 * D), F32), _sds((s, D), F32), _sds((s, D), BF16)],
        compiler_params=_params(44, 1),
    )(x, w_g, b_full)


CONV_HALO = 32
CONV_CHUNK = 16
SUBLANES = 8
COPY_ROWS = 56


def _shifted_copies(buf, shs, tm):
    n = tm + CONV_HALO - SUBLANES
    for s in range(1, SUBLANES):
        for c0 in range(0, n, COPY_ROWS):
            c1 = min(c0 + COPY_ROWS, n)
            shs[s - 1, c0:c1, :] = buf[c0 + s:c1 + s, :]


LANES = 128
CONV_ROWS = 128


def _tap_windows(offsets):
    out = []
    for s in range(SUBLANES):
        taps = [(j, o // SUBLANES) for j, o in enumerate(offsets) if o % SUBLANES == s]
        lo, hi = min(a for _, a in taps), max(a for _, a in taps)
        out.append((s, lo, hi - lo, [(j, a - lo) for j, a in taps]))
    return out


def _window(buf, shs, s, r0, lo, span, lanes):
    n = CONV_ROWS + SUBLANES * span
    rows = pl.ds(r0 + SUBLANES * lo, n)
    v = buf[rows, lanes] if s == 0 else shs[s - 1, rows, lanes]
    return v.reshape(n // SUBLANES, SUBLANES, LANES)


def _spread_taps(i, w_ref, wb):
    @pl.when(i == 0)
    def _():
        for j in range(CONV_W):
            wb[j] = jnp.broadcast_to(w_ref[j:j + 1, :], (SUBLANES, D))


def _a_conv_gate(u0, h, w_dw, b_dw, ln_g, ln_b, tm, plan=None):
    s = u0.shape[0]
    per = tm // CONV_HALO

    def body(u0_ref, halo_ref, z_ref, w_ref, bdw_ref, g_ref, b_ref, u1_ref, ga_ref, buf, shs, wb):
        i = pl.program_id(0)
        _spread_taps(i, w_ref, wb)
        buf[0:CONV_HALO, :] = jnp.where(i > 0, halo_ref[...], 0.0)
        buf[CONV_HALO:, :] = u0_ref[...]
        _shifted_copies(buf, shs, tm)
        off = CONV_HALO - (CONV_W - 1)

        windows = _tap_windows([off + j for j in range(CONV_W)])
        grp = CONV_ROWS // SUBLANES

        def chunk(it, carry):
            r0 = pl.multiple_of((it // (D // LANES)) * CONV_ROWS, CONV_ROWS)
            lanes = pl.ds(pl.multiple_of((it % (D // LANES)) * LANES, LANES), LANES)
            acc = jnp.broadcast_to(bdw_ref[:, lanes], (CONV_ROWS, LANES)).reshape(grp, SUBLANES, LANES)
            for s, lo, span, taps in windows:
                win = _window(buf, shs, s, r0, lo, span, lanes)
                for j, a in taps:
                    acc = acc + wb[j, :, lanes][None] * win[a:a + grp]
            u1_ref[pl.ds(r0, CONV_ROWS), lanes] = acc.reshape(CONV_ROWS, LANES)
            return carry

        lax.fori_loop(0, (tm // CONV_ROWS) * (D // LANES), chunk, 0)
        n, _ = _ln_stats(u1_ref[...])
        pre = n * g_ref[...] + b_ref[...]
        z = z_ref[...]
        ga_ref[...] = ((pre * _sigmoid(pre)) * (z * _sigmoid(z))).astype(BF16)

    row = lambda i: (i, 0)
    vec = pl.BlockSpec((1, D), lambda i: (0, 0))
    steps = s // tm
    return _planned_call(
        plan, (u0, u0, h, w_dw, b_dw, ln_g, ln_b), body, name="a_conv_gate", grid=(steps,), mid_step=steps // 2,
        in_specs=[pl.BlockSpec((tm, D), row),
                  pl.BlockSpec((CONV_HALO, D), lambda i: (jnp.maximum(i * per - 1, 0), 0)),
                  pl.BlockSpec((tm, D), lambda i: (i, 2)),
                  pl.BlockSpec((32, D), lambda i: (0, 0)), vec, vec, vec],
        out_specs=[pl.BlockSpec((tm, D), row), pl.BlockSpec((tm, D), row)],
        out_shape=[_sds((s, D), F32), _sds((s, D), BF16)],
        scratch_shapes=[pltpu.VMEM((tm + CONV_HALO, D), F32), pltpu.VMEM((SUBLANES - 1, tm + CONV_HALO, D), F32),
                        pltpu.VMEM((CONV_W, SUBLANES, D), F32)],
        compiler_params=_params(40, 1))


def _a_out_proj(ga, w, b, x, pg, pb, tm):
    s = x.shape[0]

    def body(ga_ref, w_ref, b_ref, x_ref, pg_ref, pb_ref, n_ref, rstd_ref, xb_ref):
        r = ALPHA * x_ref[...] + (_dot(ga_ref[...], w_ref[...]) + b_ref[...])
        n, rstd = _ln_stats(r)
        n_ref[...] = n
        rstd_ref[...] = rstd
        xb_ref[...] = (n * pg_ref[...] + pb_ref[...]).astype(BF16)

    row = lambda i: (i, 0)
    vec = pl.BlockSpec((1, D), lambda i: (0, 0))
    return pl.pallas_call(
        body, name="a_out_proj", grid=(s // tm,),
        in_specs=[pl.BlockSpec((tm, D), row), pl.BlockSpec((D, D), lambda i: (0, 0)), vec,
                  pl.BlockSpec((tm, D), row), vec, vec],
        out_specs=[pl.BlockSpec((tm, D), row), pl.BlockSpec((tm, 1), row), pl.BlockSpec((tm, D), row)],
        out_shape=[_sds((s, D), F32), _sds((s, 1), F32), _sds((s, D), BF16)],
        compiler_params=_params(32, 1),
    )(ga, w, b, x, pg, pb)


def _kv_proj(xb, w_g, tm):
    s = xb.shape[0]
    npd = w_g.shape[2]
    half = N_DEV // 2

    def body(x_ref, w_ref, k_ref, v_ref):
        xv = x_ref[...]
        for j in range(N_DEV):
            o_ref = k_ref if j < half else v_ref
            jj = j % half
            o_ref[:, npd * jj:npd * (jj + 1)] = _dot(xv, w_ref[j]).astype(BF16)

    row = lambda i: (i, 0)
    return pl.pallas_call(
        body, name="kv_proj", grid=(s // tm,),
        in_specs=[pl.BlockSpec((tm, D), row), pl.BlockSpec(w_g.shape, lambda i: (0, 0, 0))],
        out_specs=[pl.BlockSpec((tm, 3 * D), row), pl.BlockSpec((tm, 3 * D), row)],
        out_shape=[_sds((s, 3 * D), BF16), _sds((s, 3 * D), BF16)],
        compiler_params=_params(52, 1),
    )(xb, w_g)


def _b_in_proj(xb, w_g, tm):
    s = xb.shape[0]
    npd = w_g.shape[2]
    scale = HEAD_DIM ** -0.5

    def body(x_ref, w_ref, q_ref, z_ref):
        xv = x_ref[...]
        for j in range(N_DEV):
            hj = _dot(xv, w_ref[j])
            if j < 6:
                q_ref[:, npd * j:npd * (j + 1)] = (hj.astype(BF16) * scale).astype(BF16)
            else:
                z_ref[:, npd * (j - 6):npd * (j - 5)] = hj

    row = lambda i: (i, 0)
    return pl.pallas_call(
        body, name="b_in_proj", grid=(s // tm,),
        in_specs=[pl.BlockSpec((tm, D), row), pl.BlockSpec(w_g.shape, lambda i: (0, 0, 0))],
        out_specs=[pl.BlockSpec((tm, 3 * D), row), pl.BlockSpec((tm, D), row)],
        out_shape=[_sds((s, 3 * D), BF16), _sds((s, D), F32)],
        compiler_params=_params(44, 1),
    )(xb, w_g)


ATTN_NQ = {1: 8, 4: 2, 16: 1}
N_PAIRS = N_HEADS // 2


def _band_table(d):
    qi = np.arange(BLK)[:, None]
    kj = np.arange(2 * BLK)[None, :]
    dist = qi + BLK - kj
    ok = (dist >= 0) & (dist <= BLK)
    return jnp.asarray(np.where(ok, -(d * dist).astype(np.float32), np.float32(NEG)), dtype=F32)


def _slope_table():
    t = np.zeros((N_PAIRS, 8, 2 * BLK), np.float32)
    for h in range(N_HEADS):
        t[h // 2, h % 2, :] = SLOPES[h]
    return jnp.asarray(t)


def _head_masks():
    lane = lax.broadcasted_iota(jnp.int32, (1, BLK), 1)
    lo = (lane < HEAD_DIM).astype(BF16)
    return (lo, (1.0 - lo).astype(BF16))


def _pick_col(tile, lane, h):
    return jnp.sum(jnp.where(lane == h, tile, 0.0), axis=1, keepdims=True)


def _rows(base, n, d):
    return pl.ds(base, n) if d == 1 else pl.ds(base, n, stride=d)


def _attn_fwd(q, k_all, v_all, o_acc, lse_acc, z, g, first, last):
    d = DILATIONS[g]
    s = q.shape[0]
    nq = ATTN_NQ[d]
    halo = BLK * d
    tile = nq * halo
    assert s % tile == 0

    def body(*refs):
        q_ref, k_ref, kh_ref, v_ref, vh_ref, nd_ref, sl_ref = refs[:7]
        k = 7
        if not first:
            oa_ref, la_ref = refs[k:k + 2]
            k += 2
        if last:
            z_ref = refs[k]
            k += 1
        o_ref, l_ref = refs[k:k + 2]
        k += 2
        if last:
            gb_ref = refs[k]
            k += 1
        qf, kf, vf = refs[k:k + 3]
        if last:
            gf = refs[k + 3]
        n = pl.program_id(0)
        hp = pl.program_id(1)
        qf[...] = q_ref[...].astype(F32)
        kf[0:halo, :] = kh_ref[...].astype(F32)
        kf[halo:, :] = k_ref[...].astype(F32)
        vf[0:halo, :] = vh_ref[...].astype(F32)
        vf[halo:, :] = v_ref[...].astype(F32)

        @pl.when(hp == 0)
        def _():
            l_ref[...] = jnp.zeros(l_ref.shape, F32) if first else la_ref[...]

        col = lax.broadcasted_iota(jnp.int32, (2 * BLK, 2 * BLK), 1)
        lane = lax.broadcasted_iota(jnp.int32, (BLK, BLK), 1)
        masks = _head_masks()
        bias = jnp.concatenate([sl_ref[0, e:e + 1, :] * nd_ref[...] for e in range(2)], axis=0)
        bias0 = bias + jnp.where((n == 0) & (col < BLK), NEG, 0.0)
        for b in range(nq):
            for r in range(d):
                rq = _rows(b * halo + r, BLK, d)
                rk = _rows(b * halo + r, 2 * BLK, d)
                q2 = qf[rq, :].astype(BF16)
                kcat = kf[rk, :].astype(BF16)
                vcat = vf[rk, :].astype(BF16)
                lt = l_ref[rq, :]
                sc = _dot_nt(jnp.concatenate([q2 * masks[0], q2 * masks[1]], axis=0), kcat)
                sc = sc + (bias0 if b == 0 else bias)
                m = jnp.max(sc, axis=1, keepdims=True)
                p = jnp.exp(sc - m)
                l = jnp.sum(p, axis=1, keepdims=True)
                oh = _dot(p.astype(BF16), vcat) / l
                lse = m + jnp.log(l)
                if not first:
                    old = jnp.concatenate([_pick_col(lt, lane, 2 * hp + e) for e in range(2)], axis=0)
                    mx = jnp.maximum(old, lse)
                    new = mx + jnp.log(jnp.exp(old - mx) + jnp.exp(lse - mx))
                    keep = jnp.exp(old - new)
                    oh = oh * jnp.exp(lse - new)
                    lse = new
                o2 = jnp.where(lane < HEAD_DIM, oh[0:BLK], oh[BLK:])
                if not first:
                    o2 = o2 + oa_ref[rq, :] * jnp.where(lane < HEAD_DIM, keep[0:BLK], keep[BLK:])
                lt = jnp.where(lane == 2 * hp, lse[0:BLK], lt)
                lt = jnp.where(lane == 2 * hp + 1, lse[BLK:], lt)
                o_ref[rq, :] = o2
                l_ref[rq, :] = lt
                if last:
                    zz = z_ref[rq, :]
                    gf[rq, :] = o2 * (zz * _sigmoid(zz))
        if last:
            gb_ref[...] = gf[...].astype(BF16)

    col_blk = lambda n, hp: (n, g * N_PAIRS + hp)
    halo_blk = lambda n, hp: (jnp.maximum(n * nq - 1, 0), g * N_PAIRS + hp)
    own = pl.BlockSpec((tile, BLK), lambda n, hp: (n, hp))
    own_l = pl.BlockSpec((tile, BLK), lambda n, hp: (n, 0))
    in_specs = [pl.BlockSpec((tile, BLK), col_blk),
                pl.BlockSpec((tile, BLK), col_blk), pl.BlockSpec((halo, BLK), halo_blk),
                pl.BlockSpec((tile, BLK), col_blk), pl.BlockSpec((halo, BLK), halo_blk),
                pl.BlockSpec((BLK, 2 * BLK), lambda n, hp: (0, 0)),
                pl.BlockSpec((1, 8, 2 * BLK), lambda n, hp: (hp, 0, 0))]
    args = [q, k_all, k_all, v_all, v_all, _band_table(d), _slope_table()]
    if not first:
        in_specs += [own, own_l]
        args += [o_acc, lse_acc]
    if last:
        in_specs += [own]
        args += [z]
    out_specs = [own, own_l] + ([own] if last else [])
    out_shape = [_sds((s, D), F32), _sds((s, BLK), F32)] + ([_sds((s, D), BF16)] if last else [])
    scratch = [pltpu.VMEM((tile, BLK), F32), pltpu.VMEM((tile + halo, BLK), F32), pltpu.VMEM((tile + halo, BLK), F32)]
    if last:
        scratch.append(pltpu.VMEM((tile, BLK), F32))
    return pl.pallas_call(
        body, name=f"attn_fwd_g{g}", grid=(s // tile, N_PAIRS), in_specs=in_specs, out_specs=out_specs,
        out_shape=out_shape, scratch_shapes=scratch, compiler_params=_params(32, 2),
    )(*args)


ATTN_PRE = {1: 1, 4: 1, 16: 4}


def _regroup(src, dst, d1):
    n = src.shape[0] // d1
    for r1 in range(d1):
        dst[r1] = src[pl.ds(r1, n, stride=d1), :]


def _ungroup(src, dst, d1):
    n = dst.shape[0] // d1
    for r1 in range(d1):
        dst[pl.ds(r1, n, stride=d1), :] = src[r1]


def _grouped(shape, d1):
    return pltpu.VMEM((d1, shape[0] // d1, shape[1]), F32)


def _with_halo(halo_ref, tile_ref, b):
    if b == 0:
        return jnp.concatenate([halo_ref[...], tile_ref[0:BLK, :]], axis=0)
    return tile_ref[(b - 1) * BLK:(b + 1) * BLK, :]


def _attn_forward(q, k_all, v_all, o_acc, lse_acc, z, g, first, last):
    d = DILATIONS[g]
    s = q.shape[0]
    nq = ATTN_NQ[d]
    d1 = ATTN_PRE[d]
    d2 = d // d1
    halo = BLK * d
    tile = nq * halo
    assert s % tile == 0
    pre = d1 > 1

    def body(*refs):
        refs = list(refs)
        q_ref, k_ref, kh_ref, v_ref, vh_ref, nd_ref, sl_ref = refs[:7]
        del refs[:7]
        oa_ref, la_ref = (refs.pop(0), refs.pop(0)) if not first else (None, None)
        z_ref = refs.pop(0) if last else None
        o_ref, l_ref = refs.pop(0), refs.pop(0)
        gb_ref = refs.pop(0) if last else None
        qf, kf, vf = refs.pop(0), refs.pop(0), refs.pop(0)
        gf = refs.pop(0) if last else None
        q1 = k1 = v1 = l1 = o1 = oa1 = z1 = g1 = None
        if pre:
            q1, k1, v1, l1, o1 = (refs.pop(0) for _ in range(5))
            oa1 = refs.pop(0) if not first else None
            z1, g1 = (refs.pop(0), refs.pop(0)) if last else (None, None)
        n = pl.program_id(0)
        hp = pl.program_id(1)
        if d > 1:
            qf[...] = q_ref[...].astype(F32)
            kf[0:halo, :] = kh_ref[...].astype(F32)
            kf[halo:, :] = k_ref[...].astype(F32)
            vf[0:halo, :] = vh_ref[...].astype(F32)
            vf[halo:, :] = v_ref[...].astype(F32)

        @pl.when(hp == 0)
        def _():
            l_ref[...] = jnp.zeros(l_ref.shape, F32) if first else la_ref[...]

        if pre:
            for src, dst in ((qf, q1), (kf, k1), (vf, v1), (l_ref, l1), (oa_ref, oa1), (z_ref, z1)):
                if src is not None:
                    _regroup(src, dst, d1)

        def pick(nat, grp, r1):
            return grp.at[r1] if pre else nat

        def halves(colv):
            return jnp.where(lane < HEAD_DIM, colv[0:BLK], colv[BLK:])

        def put(tile_v, colv):
            tile_v = jnp.where(lane == 2 * hp, colv[0:BLK], tile_v)
            return jnp.where(lane == 2 * hp + 1, colv[BLK:], tile_v)

        col = lax.broadcasted_iota(jnp.int32, (2 * BLK, 2 * BLK), 1)
        lane = lax.broadcasted_iota(jnp.int32, (BLK, BLK), 1)
        masks = _head_masks()
        bias = jnp.concatenate([sl_ref[0, e:e + 1, :] * nd_ref[...] for e in range(2)], axis=0)
        bias0 = bias + jnp.where((n == 0) & (col < BLK), NEG, 0.0)
        for b in range(nq):
            for r in range(d):
                r1, r2 = r % d1, r // d1
                rq = _rows(b * (halo // d1) + r2, BLK, d2)
                rk = _rows(b * (halo // d1) + r2, 2 * BLK, d2)
                if d > 1:
                    q2 = pick(qf, q1, r1)[rq, :].astype(BF16)
                    kcat = pick(kf, k1, r1)[rk, :].astype(BF16)
                    vcat = pick(vf, v1, r1)[rk, :].astype(BF16)
                else:
                    q2 = q_ref[rq, :]
                    kcat = _with_halo(kh_ref, k_ref, b)
                    vcat = _with_halo(vh_ref, v_ref, b)
                sc = _dot_nt(jnp.concatenate([q2 * masks[0], q2 * masks[1]], axis=0), kcat)
                sc = sc + (bias0 if b == 0 else bias)
                m = jnp.max(sc, axis=1, keepdims=True)
                p = jnp.exp(sc - m)
                l = jnp.sum(p, axis=1, keepdims=True)
                oh = _dot(p.astype(BF16), vcat) / l
                lse = m + jnp.log(l)
                lt = pick(l_ref, l1, r1)[rq, :]
                if not first:
                    old = jnp.concatenate([_pick_col(lt, lane, 2 * hp + e) for e in range(2)], axis=0)
                    mx = jnp.maximum(old, lse)
                    new = mx + jnp.log(jnp.exp(old - mx) + jnp.exp(lse - mx))
                    o2 = halves(oh) * halves(jnp.exp(lse - new)) + pick(oa_ref, oa1, r1)[rq, :] * halves(jnp.exp(old - new))
                    lse = new
                else:
                    o2 = halves(oh)
                pick(o_ref, o1, r1)[rq, :] = o2
                pick(l_ref, l1, r1)[rq, :] = put(lt, lse)
                if last:
                    zz = pick(z_ref, z1, r1)[rq, :]
                    pick(gf, g1, r1)[rq, :] = o2 * (zz * _sigmoid(zz))
        if pre:
            _ungroup(o1, o_ref, d1)
            _ungroup(l1, l_ref, d1)
            if last:
                _ungroup(g1, gf, d1)
        if last:
            gb_ref[...] = gf[...].astype(BF16)

    col_blk = lambda n, hp: (n, g * N_PAIRS + hp)
    halo_blk = lambda n, hp: (jnp.maximum(n * nq - 1, 0), g * N_PAIRS + hp)
    own = pl.BlockSpec((tile, BLK), lambda n, hp: (n, hp))
    own_l = pl.BlockSpec((tile, BLK), lambda n, hp: (n, 0))
    in_specs = [pl.BlockSpec((tile, BLK), col_blk),
                pl.BlockSpec((tile, BLK), col_blk), pl.BlockSpec((halo, BLK), halo_blk),
                pl.BlockSpec((tile, BLK), col_blk), pl.BlockSpec((halo, BLK), halo_blk),
                pl.BlockSpec((BLK, 2 * BLK), lambda n, hp: (0, 0)),
                pl.BlockSpec((1, 8, 2 * BLK), lambda n, hp: (hp, 0, 0))]
    args = [q, k_all, k_all, v_all, v_all, _band_table(d), _slope_table()]
    if not first:
        in_specs += [own, own_l]
        args += [o_acc, lse_acc]
    if last:
        in_specs += [own]
        args += [z]
    out_specs = [own, own_l] + ([own] if last else [])
    out_shape = [_sds((s, D), F32), _sds((s, BLK), F32)] + ([_sds((s, D), BF16)] if last else [])
    t_shape, w_shape = (tile, BLK), (tile + halo, BLK)
    scratch = [pltpu.VMEM(t_shape, F32), pltpu.VMEM(w_shape, F32), pltpu.VMEM(w_shape, F32)]
    if last:
        scratch.append(pltpu.VMEM(t_shape, F32))
    if pre:
        scratch += [_grouped(t_shape, d1), _grouped(w_shape, d1), _grouped(w_shape, d1)]
        scratch += [_grouped(t_shape, d1)] * (2 + (0 if first else 1) + (2 if last else 0))
    return pl.pallas_call(
        body, name=f"attn_fwd_g{g}", grid=(s // tile, N_PAIRS), in_specs=in_specs, out_specs=out_specs,
        out_shape=out_shape, scratch_shapes=scratch, compiler_params=_params(56, 2),
    )(*args)


def _attn_backward(q, k_all, v_all, do, lse, dd, dhb, dk_all, dv_all, g):
    d = DILATIONS[g]
    s = q.shape[0]
    nq = ATTN_NQ[d]
    d1 = ATTN_PRE[d]
    d2 = d // d1
    halo = BLK * d
    tile = nq * halo
    nt = s // tile
    first = dk_all is None
    pre = d1 > 1

    def body(*refs):
        refs = list(refs)
        q_ref, k_ref, kh_ref, v_ref, vh_ref, nd_ref, sl_ref, do_ref, l_ref, dd_ref = refs[:10]
        del refs[:10 + (1 if first else 3)]
        dq_ref, dk_ref, dv_ref = refs[:3]
        qf, dof, kf, vf, dqf, dkf, dvf, ck, cv = refs[3:12]
        del refs[:12]
        if pre:
            q1, do1, k1, v1, l1, dd1, dq1, dk1, dv1 = refs
        else:
            q1 = do1 = k1 = v1 = l1 = dd1 = dq1 = None
            dk1, dv1 = dkf, dvf
        hp = pl.program_id(0)
        n = pl.program_id(1)

        @pl.when(n == 0)
        def _():
            ck[...] = jnp.zeros(ck.shape, F32)
            cv[...] = jnp.zeros(cv.shape, F32)

        dk1[...] = jnp.zeros(dk1.shape, F32)
        dv1[...] = jnp.zeros(dv1.shape, F32)

        def pick(nat, grp, r1):
            return grp.at[r1] if pre else nat

        @pl.when(n < nt)
        def _():
            if d > 1:
                qf[...] = q_ref[...].astype(F32)
                dof[...] = do_ref[...].astype(F32)
                kf[0:halo, :] = kh_ref[...].astype(F32)
                kf[halo:, :] = k_ref[...].astype(F32)
                vf[0:halo, :] = vh_ref[...].astype(F32)
                vf[halo:, :] = v_ref[...].astype(F32)
            if pre:
                for src, dst in ((qf, q1), (dof, do1), (kf, k1), (vf, v1), (l_ref, l1), (dd_ref, dd1)):
                    _regroup(src, dst, d1)
            col = lax.broadcasted_iota(jnp.int32, (2 * BLK, 2 * BLK), 1)
            lane = lax.broadcasted_iota(jnp.int32, (BLK, BLK), 1)
            masks = _head_masks()
            bias = jnp.concatenate([sl_ref[0, e:e + 1, :] * nd_ref[...] for e in range(2)], axis=0)
            bias0 = bias + jnp.where((n == 0) & (col < BLK), NEG, 0.0)
            for b in range(nq):
                for r in range(d):
                    r1, r2 = r % d1, r // d1
                    rq = _rows(b * (halo // d1) + r2, BLK, d2)
                    rk = _rows(b * (halo // d1) + r2, 2 * BLK, d2)
                    if d > 1:
                        q2 = pick(qf, q1, r1)[rq, :].astype(BF16)
                        do2 = pick(dof, do1, r1)[rq, :].astype(BF16)
                        kcat = pick(kf, k1, r1)[rk, :].astype(BF16)
                        vcat = pick(vf, v1, r1)[rk, :].astype(BF16)
                    else:
                        q2 = q_ref[rq, :]
                        do2 = do_ref[rq, :]
                        kcat = _with_halo(kh_ref, k_ref, b)
                        vcat = _with_halo(vh_ref, v_ref, b)
                    lt = pick(l_ref, l1, r1)[rq, :]
                    dt = pick(dd_ref, dd1, r1)[rq, :]
                    qs = jnp.concatenate([q2 * masks[0], q2 * masks[1]], axis=0)
                    dos = jnp.concatenate([do2 * masks[0], do2 * masks[1]], axis=0)
                    lcol = jnp.concatenate([_pick_col(lt, lane, 2 * hp + e) for e in range(2)], axis=0)
                    dcol = jnp.concatenate([_pick_col(dt, lane, 2 * hp + e) for e in range(2)], axis=0)
                    sc = _dot_nt(qs, kcat) + (bias0 if b == 0 else bias)
                    p = jnp.exp(sc - lcol)
                    ds = p * (_dot_nt(dos, vcat) - dcol)
                    dsb = ds.astype(BF16)
                    dq = _dot(dsb, kcat)
                    dq2 = (HEAD_DIM ** -0.5) * jnp.where(lane < HEAD_DIM, dq[0:BLK], dq[BLK:])
                    pick(dqf, dq1, r1)[rq, :] = dq2
                    pick(dkf, dk1, r1)[rk, :] += _dot(dsb.T, qs)
                    pick(dvf, dv1, r1)[rk, :] += _dot(p.astype(BF16).T, dos)
            if pre:
                _ungroup(dq1, dqf, d1)
            dq_ref[...] = dqf[...].astype(BF16)

        if pre:
            _ungroup(dk1, dkf, d1)
            _ungroup(dv1, dvf, d1)
        if tile > halo:
            dk_ref[0:tile - halo, :] = ck[0:tile - halo, :].astype(BF16)
            dv_ref[0:tile - halo, :] = cv[0:tile - halo, :].astype(BF16)
        dk_ref[tile - halo:, :] = (ck[tile - halo:, :] + dkf[0:halo, :]).astype(BF16)
        dv_ref[tile - halo:, :] = (cv[tile - halo:, :] + dvf[0:halo, :]).astype(BF16)
        ck[...] = dkf[halo:, :]
        cv[...] = dvf[halo:, :]

    cur = lambda n: jnp.minimum(n, nt - 1)
    col_blk = lambda hp, n: (cur(n), g * N_PAIRS + hp)
    halo_blk = lambda hp, n: (jnp.maximum(cur(n) * nq - 1, 0), g * N_PAIRS + hp)
    out_kv = lambda hp, n: (jnp.maximum(n - 1, 0), g * N_PAIRS + hp)
    small = pl.BlockSpec((tile, BLK), lambda hp, n: (cur(n), 0))
    hbm = pl.BlockSpec(memory_space=pl.ANY)
    in_specs = [pl.BlockSpec((tile, BLK), col_blk),
                pl.BlockSpec((tile, BLK), col_blk), pl.BlockSpec((halo, BLK), halo_blk),
                pl.BlockSpec((tile, BLK), col_blk), pl.BlockSpec((halo, BLK), halo_blk),
                pl.BlockSpec((BLK, 2 * BLK), lambda hp, n: (0, 0)),
                pl.BlockSpec((1, 8, 2 * BLK), lambda hp, n: (hp, 0, 0)),
                pl.BlockSpec((tile, BLK), lambda hp, n: (cur(n), hp)), small, small, hbm]
    args = [q, k_all, k_all, v_all, v_all, _band_table(d), _slope_table(), do, lse, dd, dhb]
    aliases = {10: 0}
    if not first:
        in_specs += [hbm, hbm]
        args += [dk_all, dv_all]
        aliases.update({11: 1, 12: 2})
    t_shape, w_shape = (tile, BLK), (tile + halo, BLK)
    tile_f32, wide_f32 = pltpu.VMEM(t_shape, F32), pltpu.VMEM(w_shape, F32)
    scratch = [tile_f32, tile_f32, wide_f32, wide_f32, tile_f32, wide_f32, wide_f32, tile_f32, tile_f32]
    if pre:
        tg, wg = _grouped(t_shape, d1), _grouped(w_shape, d1)
        scratch += [tg, tg, wg, wg, tg, tg, tg, wg, wg]
    return pl.pallas_call(
        body, name=f"attn_bwd_g{g}", grid=(N_PAIRS, nt + 1), in_specs=in_specs,
        out_specs=[pl.BlockSpec((tile, BLK), col_blk), pl.BlockSpec((tile, BLK), out_kv),
                   pl.BlockSpec((tile, BLK), out_kv)],
        out_shape=[_sds((s, 4 * D), BF16), _sds((s, 3 * D), BF16), _sds((s, 3 * D), BF16)],
        scratch_shapes=scratch, input_output_aliases=aliases, compiler_params=_params(48, 2),
    )(*args)


def _head_spread():
    spread = (np.arange(BLK)[:, None] == np.arange(D)[None, :] // HEAD_DIM).astype(np.float32)
    return jnp.asarray(spread, dtype=BF16)


def _attn_merge(os_, lses, z, tm):
    s = z.shape[0]

    def body(o0_ref, o1_ref, o2_ref, l0_ref, l1_ref, l2_ref, z_ref, e_ref, o_ref, l_ref, gb_ref):
        ls = [l0_ref[...], l1_ref[...], l2_ref[...]]
        m = jnp.maximum(jnp.maximum(ls[0], ls[1]), ls[2])
        lse = m + jnp.log(jnp.exp(ls[0] - m) + jnp.exp(ls[1] - m) + jnp.exp(ls[2] - m))
        l_ref[...] = lse
        o = jnp.zeros((tm, D), F32)
        for l_g, og_ref in zip(ls, (o0_ref, o1_ref, o2_ref)):
            wg = jnp.exp(l_g - lse)
            hi = wg.astype(BF16)
            lo = (wg - hi.astype(F32)).astype(BF16)
            o = o + (_dot(hi, e_ref[...]) + _dot(lo, e_ref[...])) * og_ref[...]
        o_ref[...] = o
        zz = z_ref[...]
        gb_ref[...] = (o * (zz * _sigmoid(zz))).astype(BF16)

    row = lambda i: (i, 0)
    wide = pl.BlockSpec((tm, D), row)
    stat = pl.BlockSpec((tm, BLK), row)
    return pl.pallas_call(
        body, name="attn_merge", grid=(s // tm,),
        in_specs=[wide, wide, wide, stat, stat, stat, wide, pl.BlockSpec((BLK, D), lambda i: (0, 0))],
        out_specs=[wide, stat, wide],
        out_shape=[_sds((s, D), F32), _sds((s, BLK), F32), _sds((s, D), BF16)],
        compiler_params=_params(40, 1),
    )(*os_, *lses, z, _head_spread())


def _b_out_loss(gb, w, b, n1, pg0, pb0, pg1, pb1, tgt, tm):
    s = gb.shape[0]
    last = s // tm - 1

    def body(gb_ref, w_ref, b_ref, n1_ref, pg0_ref, pb0_ref, pg1_ref, pb1_ref, t_ref,
             dr_ref, drb_ref, loss_ref, dpg_ref, dpb_ref, dbo_ref):
        i = pl.program_id(0)
        _acc_init(i, loss_ref, dpg_ref, dpb_ref, dbo_ref)
        x1 = n1_ref[...] * pg0_ref[...] + pb0_ref[...]
        r = ALPHA * x1 + (_dot(gb_ref[...], w_ref[...]) + b_ref[...])
        n, rstd = _ln_stats(r)
        err = (n * pg1_ref[...] + pb1_ref[...]) - t_ref[...]
        loss_ref[...] += _rowsum8(err * err)
        dx2 = err * (1.0 / D)
        dpg_ref[...] += _rowsum8(dx2 * n)
        dpb_ref[...] += _rowsum8(dx2)
        dr = _ln_bwd(dx2 * pg1_ref[...], n, rstd)
        dr_ref[...] = dr
        drb_ref[...] = dr.astype(BF16)
        dbo_ref[...] += _rowsum8(dr)
        _acc_finish(i, last, dpg_ref, dpb_ref, dbo_ref)

        @pl.when(i == last)
        def _():
            loss_ref[...] = jnp.broadcast_to((0.5 / D) * jnp.sum(loss_ref[...], keepdims=True), loss_ref.shape)

    row = lambda i: (i, 0)
    vec = pl.BlockSpec((1, D), lambda i: (0, 0))
    acc = pl.BlockSpec((8, D), lambda i: (0, 0))
    return pl.pallas_call(
        body, name="b_out_loss", grid=(s // tm,),
        in_specs=[pl.BlockSpec((tm, D), row), pl.BlockSpec((D, D), lambda i: (0, 0)), vec,
                  pl.BlockSpec((tm, D), row), vec, vec, vec, vec, pl.BlockSpec((tm, D), row)],
        out_specs=[pl.BlockSpec((tm, D), row), pl.BlockSpec((tm, D), row), acc, acc, acc, acc],
        out_shape=[_sds((s, D), F32), _sds((s, D), BF16)] + [_sds((8, D), F32)] * 4,
        compiler_params=_params(36, 1),
    )(gb, w, b, n1, pg0, pb0, pg1, pb1, tgt)


def _head_selector():
    sel = (np.arange(D)[:, None] // HEAD_DIM == np.arange(BLK)[None, :]).astype(np.float32)
    return jnp.asarray(sel, dtype=BF16)


def _b_out_bwd(drb, w, z, o, tm):
    s = drb.shape[0]

    def body(dr_ref, w_ref, z_ref, o_ref, sel_ref, do_ref, dh_ref, dd_ref):
        dg = _dot_nt(dr_ref[...], w_ref[...])
        zz = z_ref[...]
        sg = _sigmoid(zz)
        do = dg * (zz * sg)
        ov = o_ref[...]
        do_ref[...] = do.astype(BF16)
        dh_ref[...] = (dg * ov * _dsilu(zz, sg)).astype(BF16)
        prod = do * ov
        hi = prod.astype(BF16)
        lo = (prod - hi.astype(F32)).astype(BF16)
        dd_ref[...] = _dot(hi, sel_ref[...]) + _dot(lo, sel_ref[...])

    row = lambda i: (i, 0)
    return pl.pallas_call(
        body, name="b_out_bwd", grid=(s // tm,),
        in_specs=[pl.BlockSpec((tm, D), row), pl.BlockSpec((D, D), lambda i: (0, 0)),
                  pl.BlockSpec((tm, D), row), pl.BlockSpec((tm, D), row), pl.BlockSpec((D, BLK), lambda i: (0, 0))],
        out_specs=[pl.BlockSpec((tm, D), row), pl.BlockSpec((tm, D), lambda i: (i, 3)),
                   pl.BlockSpec((tm, BLK), row)],
        out_shape=[_sds((s, D), BF16), _sds((s, 4 * D), BF16), _sds((s, BLK), F32)],
        compiler_params=_params(36, 1),
    )(drb, w, z, o, _head_selector())


def _attn_bwd(q, k_all, v_all, do, lse, dd, dhb, dk_all, dv_all, g):
    d = DILATIONS[g]
    s = q.shape[0]
    nq = ATTN_NQ[d]
    halo = BLK * d
    tile = nq * halo
    nt = s // tile
    first = dk_all is None

    def body(*refs):
        q_ref, k_ref, kh_ref, v_ref, vh_ref, nd_ref, sl_ref, do_ref, l_ref, dd_ref = refs[:10]
        k = 10 + (1 if first else 3)
        dq_ref, dk_ref, dv_ref = refs[k:k + 3]
        qf, dof, kf, vf, dqf, dkf, dvf, ck, cv = refs[k + 3:k + 12]
        hp = pl.program_id(0)
        n = pl.program_id(1)

        @pl.when(n == 0)
        def _():
            ck[...] = jnp.zeros(ck.shape, F32)
            cv[...] = jnp.zeros(cv.shape, F32)

        dkf[...] = jnp.zeros(dkf.shape, F32)
        dvf[...] = jnp.zeros(dvf.shape, F32)

        @pl.when(n < nt)
        def _():
            qf[...] = q_ref[...].astype(F32)
            dof[...] = do_ref[...].astype(F32)
            kf[0:halo, :] = kh_ref[...].astype(F32)
            kf[halo:, :] = k_ref[...].astype(F32)
            vf[0:halo, :] = vh_ref[...].astype(F32)
            vf[halo:, :] = v_ref[...].astype(F32)
            col = lax.broadcasted_iota(jnp.int32, (2 * BLK, 2 * BLK), 1)
            lane = lax.broadcasted_iota(jnp.int32, (BLK, BLK), 1)
            masks = _head_masks()
            bias = jnp.concatenate([sl_ref[0, e:e + 1, :] * nd_ref[...] for e in range(2)], axis=0)
            bias0 = bias + jnp.where((n == 0) & (col < BLK), NEG, 0.0)
            for b in range(nq):
                for r in range(d):
                    rq = _rows(b * halo + r, BLK, d)
                    rk = _rows(b * halo + r, 2 * BLK, d)
                    q2 = qf[rq, :].astype(BF16)
                    do2 = dof[rq, :].astype(BF16)
                    kcat = kf[rk, :].astype(BF16)
                    vcat = vf[rk, :].astype(BF16)
                    lt = l_ref[rq, :]
                    dt = dd_ref[rq, :]
                    qs = jnp.concatenate([q2 * masks[0], q2 * masks[1]], axis=0)
                    dos = jnp.concatenate([do2 * masks[0], do2 * masks[1]], axis=0)
                    lcol = jnp.concatenate([_pick_col(lt, lane, 2 * hp + e) for e in range(2)], axis=0)
                    dcol = jnp.concatenate([_pick_col(dt, lane, 2 * hp + e) for e in range(2)], axis=0)
                    sc = _dot_nt(qs, kcat) + (bias0 if b == 0 else bias)
                    p = jnp.exp(sc - lcol)
                    ds = p * (_dot_nt(dos, vcat) - dcol)
                    dsb = ds.astype(BF16)
                    dq = _dot(dsb, kcat)
                    dqf[rq, :] = (HEAD_DIM ** -0.5) * jnp.where(lane < HEAD_DIM, dq[0:BLK], dq[BLK:])
                    dkf[rk, :] += _dot(dsb.T, qs)
                    dvf[rk, :] += _dot(p.astype(BF16).T, dos)
            dq_ref[...] = dqf[...].astype(BF16)

        if tile > halo:
            dk_ref[0:tile - halo, :] = ck[0:tile - halo, :].astype(BF16)
            dv_ref[0:tile - halo, :] = cv[0:tile - halo, :].astype(BF16)
        dk_ref[tile - halo:, :] = (ck[tile - halo:, :] + dkf[0:halo, :]).astype(BF16)
        dv_ref[tile - halo:, :] = (cv[tile - halo:, :] + dvf[0:halo, :]).astype(BF16)
        ck[...] = dkf[halo:, :]
        cv[...] = dvf[halo:, :]

    cur = lambda n: jnp.minimum(n, nt - 1)
    col_blk = lambda hp, n: (cur(n), g * N_PAIRS + hp)
    halo_blk = lambda hp, n: (jnp.maximum(cur(n) * nq - 1, 0), g * N_PAIRS + hp)
    out_kv = lambda hp, n: (jnp.maximum(n - 1, 0), g * N_PAIRS + hp)
    small = pl.BlockSpec((tile, BLK), lambda hp, n: (cur(n), 0))
    hbm = pl.BlockSpec(memory_space=pl.ANY)
    in_specs = [pl.BlockSpec((tile, BLK), col_blk),
                pl.BlockSpec((tile, BLK), col_blk), pl.BlockSpec((halo, BLK), halo_blk),
                pl.BlockSpec((tile, BLK), col_blk), pl.BlockSpec((halo, BLK), halo_blk),
                pl.BlockSpec((BLK, 2 * BLK), lambda hp, n: (0, 0)),
                pl.BlockSpec((1, 8, 2 * BLK), lambda hp, n: (hp, 0, 0)),
                pl.BlockSpec((tile, BLK), lambda hp, n: (cur(n), hp)), small, small, hbm]
    args = [q, k_all, k_all, v_all, v_all, _band_table(d), _slope_table(), do, lse, dd, dhb]
    aliases = {10: 0}
    if not first:
        in_specs += [hbm, hbm]
        args += [dk_all, dv_all]
        aliases.update({11: 1, 12: 2})
    tile_f32 = pltpu.VMEM((tile, BLK), F32)
    wide_f32 = pltpu.VMEM((tile + halo, BLK), F32)
    return pl.pallas_call(
        body, name=f"attn_bwd_g{g}", grid=(N_PAIRS, nt + 1), in_specs=in_specs,
        out_specs=[pl.BlockSpec((tile, BLK), col_blk), pl.BlockSpec((tile, BLK), out_kv),
                   pl.BlockSpec((tile, BLK), out_kv)],
        out_shape=[_sds((s, 4 * D), BF16), _sds((s, 3 * D), BF16), _sds((s, 3 * D), BF16)],
        scratch_shapes=[tile_f32, tile_f32, wide_f32, wide_f32, tile_f32, wide_f32, wide_f32, tile_f32, tile_f32],
        input_output_aliases=aliases, compiler_params=_params(40, 2),
    )(*args)


def _b_in_bwd(dr2, dhb, dk_all, dv_all, wb_g, wkv_g, n1, rstd1, pg0, tm):
    s = dr2.shape[0]
    last = s // tm - 1
    nb_, nkv = wb_g.shape[2], wkv_g.shape[2]
    half = N_DEV // 2

    def body(dr2_ref, dh_ref, dk_ref, dv_ref, wb_hbm, wkv_hbm, n_ref, rstd_ref, pg_ref,
             dr_ref, drb_ref, dpg_ref, dpb_ref, dbo_ref, wb, wkv):
        i = pl.program_id(0)

        @pl.when(i == 0)
        def _():
            pltpu.sync_copy(wb_hbm, wb)
            pltpu.sync_copy(wkv_hbm, wkv)

        _acc_init(i, dpg_ref, dpb_ref, dbo_ref)
        acc = ALPHA * dr2_ref[...]
        for j in range(N_DEV):
            acc = acc + _dot_nt(dh_ref[:, nb_ * j:nb_ * (j + 1)], wb[j])
            src = dk_ref if j < half else dv_ref
            jj = j % half
            acc = acc + _dot_nt(src[:, nkv * jj:nkv * (jj + 1)], wkv[j])
        n = n_ref[...]
        dpg_ref[...] += _rowsum8(acc * n)
        dpb_ref[...] += _rowsum8(acc)
        dr = _ln_bwd(acc * pg_ref[...], n, rstd_ref[...])
        dr_ref[...] = dr
        drb_ref[...] = dr.astype(BF16)
        dbo_ref[...] += _rowsum8(dr)
        _acc_finish(i, last, dpg_ref, dpb_ref, dbo_ref)

    row = lambda i: (i, 0)
    hbm = pl.BlockSpec(memory_space=pl.ANY)
    acc_spec = pl.BlockSpec((8, D), lambda i: (0, 0))
    return pl.pallas_call(
        body, name="b_in_bwd", grid=(s // tm,),
        in_specs=[pl.BlockSpec((tm, D), row), pl.BlockSpec((tm, 4 * D), row), pl.BlockSpec((tm, 3 * D), row),
                  pl.BlockSpec((tm, 3 * D), row), hbm, hbm, pl.BlockSpec((tm, D), row), pl.BlockSpec((tm, 1), row),
                  pl.BlockSpec((1, D), lambda i: (0, 0))],
        out_specs=[pl.BlockSpec((tm, D), row), pl.BlockSpec((tm, D), row), acc_spec, acc_spec, acc_spec],
        out_shape=[_sds((s, D), F32), _sds((s, D), BF16)] + [_sds((8, D), F32)] * 3,
        scratch_shapes=[pltpu.VMEM(wb_g.shape, BF16), pltpu.VMEM(wkv_g.shape, BF16)],
        compiler_params=_params(56, 1),
    )(dr2, dhb, dk_all, dv_all, wb_g, wkv_g, n1, rstd1, pg0)


def _a_out_bwd(drb, w, u1, h, ln_g, ln_b, tm, plan=None):
    s = drb.shape[0]
    last = s // tm - 1

    def body(dr_ref, w_ref, u1_ref, z_ref, g_ref, b_ref, du1_ref, dh_ref, dg_ref, db_ref, dbz_ref):
        i = pl.program_id(0)
        _acc_init(i, dg_ref, db_ref, dbz_ref)
        dga = _dot_nt(dr_ref[...], w_ref[...])
        n, rstd = _ln_stats(u1_ref[...])
        pre = n * g_ref[...] + b_ref[...]
        sp = _sigmoid(pre)
        zz = z_ref[...]
        sz = _sigmoid(zz)
        dz = dga * (pre * sp) * _dsilu(zz, sz)
        dh_ref[...] = dz.astype(BF16)
        dbz_ref[...] += _rowsum8(dz)
        dpre = dga * (zz * sz) * _dsilu(pre, sp)
        dg_ref[...] += _rowsum8(dpre * n)
        db_ref[...] += _rowsum8(dpre)
        du1_ref[...] = _ln_bwd(dpre * g_ref[...], n, rstd)
        _acc_finish(i, last, dg_ref, db_ref, dbz_ref)

    row = lambda i: (i, 0)
    vec = pl.BlockSpec((1, D), lambda i: (0, 0))
    acc_spec = pl.BlockSpec((8, D), lambda i: (0, 0))
    return _planned_call(
        plan, (drb, w, u1, h, ln_g, ln_b), body, name="a_out_bwd", grid=(s // tm,),
        in_specs=[pl.BlockSpec((tm, D), row), pl.BlockSpec((D, D), lambda i: (0, 0)), pl.BlockSpec((tm, D), row),
                  pl.BlockSpec((tm, D), lambda i: (i, 2)), vec, vec],
        out_specs=[pl.BlockSpec((tm, D), row), pl.BlockSpec((tm, D), lambda i: (i, 2)),
                   acc_spec, acc_spec, acc_spec],
        out_shape=[_sds((s, D), F32), _sds((s, 3 * D), BF16)] + [_sds((8, D), F32)] * 3,
        compiler_params=_params(32, 1))


def _a_conv_bwd(du1, u0, h, dha, w_dw, tm, plan=None):
    s = du1.shape[0]
    steps = s // tm
    per = tm // CONV_HALO
    pad = CONV_W - 1

    def body(du_ref, dun_ref, u0_ref, u0p_ref, h_ref, w_ref, dha_hbm,
             dh_ref, dw_ref, dbdw_ref, dba_ref, dbg_ref, dbuf, ubuf, wacc, dshs, ushs, wb, du0_buf):
        i = pl.program_id(0)
        _spread_taps(i, w_ref, wb)
        _acc_init(i, dbdw_ref, dba_ref, dbg_ref, wacc)
        dbuf[0:tm, :] = du_ref[...]
        dbuf[tm:, :] = jnp.where(i < steps - 1, dun_ref[...], 0.0)
        ubuf[0:CONV_HALO, :] = jnp.where(i > 0, u0p_ref[...], 0.0)
        ubuf[CONV_HALO:, :] = u0_ref[...]
        _shifted_copies(dbuf, dshs, tm)
        _shifted_copies(ubuf, ushs, tm)
        off = CONV_HALO - pad
        grp = CONV_ROWS // SUBLANES
        d_windows = _tap_windows([pad - j for j in range(CONV_W)])
        u_windows = _tap_windows([off + j for j in range(CONV_W)])

        def chunk(it, carry):
            r0 = pl.multiple_of((it // (D // LANES)) * CONV_ROWS, CONV_ROWS)
            lanes = pl.ds(pl.multiple_of((it % (D // LANES)) * LANES, LANES), LANES)
            rows = pl.ds(r0, CONV_ROWS)
            acc = jnp.zeros((grp, SUBLANES, LANES), F32)
            for s, lo, span, taps in d_windows:
                win = _window(dbuf, dshs, s, r0, lo, span, lanes)
                for j, a in taps:
                    acc = acc + wb[j, :, lanes][None] * win[a:a + grp]
            du0_buf[rows, lanes] = acc.reshape(CONV_ROWS, LANES)
            du3 = du_ref[rows, lanes].reshape(grp, SUBLANES, LANES)
            for s, lo, span, taps in u_windows:
                win = _window(ubuf, ushs, s, r0, lo, span, lanes)
                for j, a in taps:
                    wacc[j, :, lanes] += (du3 * win[a:a + grp]).sum(axis=0)
            return carry

        lax.fori_loop(0, (tm // CONV_ROWS) * (D // LANES), chunk, 0)
        du0 = du0_buf[...]
        sg = _sigmoid(h_ref[:, D:2 * D])
        da = du0 * sg
        dag = du0 * h_ref[:, 0:D] * (sg * (1.0 - sg))
        dh_ref[:, 0:D] = da.astype(BF16)
        dh_ref[:, D:2 * D] = dag.astype(BF16)
        dbdw_ref[...] += _rowsum8(du_ref[...])
        dba_ref[...] += _rowsum8(da)
        dbg_ref[...] += _rowsum8(dag)
        _acc_finish(i, steps - 1, dbdw_ref, dba_ref, dbg_ref)

        @pl.when(i == steps - 1)
        def _():
            for j in range(CONV_W):
                dw_ref[j:j + 1, :] = jnp.sum(wacc[j], axis=0, keepdims=True)
            dw_ref[CONV_W:, :] = jnp.zeros((32 - CONV_W, D), F32)

    row = lambda i: (i, 0)
    acc_spec = pl.BlockSpec((8, D), lambda i: (0, 0))
    return _planned_call(
        plan, (du1, du1, u0, u0, h, w_dw, dha), body, name="a_conv_bwd", grid=(steps,),
        in_specs=[pl.BlockSpec((tm, D), row),
                  pl.BlockSpec((CONV_HALO, D), lambda i: (jnp.minimum((i + 1) * per, s // CONV_HALO - 1), 0)),
                  pl.BlockSpec((tm, D), row),
                  pl.BlockSpec((CONV_HALO, D), lambda i: (jnp.maximum(i * per - 1, 0), 0)),
                  pl.BlockSpec((tm, 2 * D), lambda i: (i, 0)),
                  pl.BlockSpec((32, D), lambda i: (0, 0)), pl.BlockSpec(memory_space=pl.ANY)],
        out_specs=[pl.BlockSpec((tm, 2 * D), lambda i: (i, 0)),
                   pl.BlockSpec((32, D), lambda i: (0, 0)), acc_spec, acc_spec, acc_spec],
        out_shape=[_sds((s, 3 * D), BF16), _sds((32, D), F32)] + [_sds((8, D), F32)] * 3,
        scratch_shapes=[pltpu.VMEM((tm + CONV_HALO, D), F32), pltpu.VMEM((tm + CONV_HALO, D), F32),
                        pltpu.VMEM((CONV_W, 8, D), F32),
                        pltpu.VMEM((SUBLANES - 1, tm + CONV_HALO, D), F32),
                        pltpu.VMEM((SUBLANES - 1, tm + CONV_HALO, D), F32),
                        pltpu.VMEM((CONV_W, SUBLANES, D), F32), pltpu.VMEM((tm, D), F32)],
        input_output_aliases={6: 0}, compiler_params=_params(56, 1))


def _a_in_bwd(dr1, dha, w_g, tm, plan=None):
    s = dr1.shape[0]
    npd = w_g.shape[2]

    def body(dr_ref, dh_ref, w_ref, o_ref):
        acc = ALPHA * dr_ref[...]
        for j in range(N_DEV):
            acc = acc + _dot_nt(dh_ref[:, npd * j:npd * (j + 1)], w_ref[j])
        o_ref[...] = acc

    row = lambda i: (i, 0)
    return _planned_call(
        plan, (dr1, dha, w_g), body, name="a_in_bwd", grid=(s // tm,),
        in_specs=[pl.BlockSpec((tm, D), row), pl.BlockSpec((tm, 3 * D), row),
                  pl.BlockSpec(w_g.shape, lambda i: (0, 0, 0))],
        out_specs=[pl.BlockSpec((tm, D), row)], out_shape=[_sds((s, D), F32)],
        compiler_params=_params(36, 1))


def _wgrad(name, a, b, npd, ts, total=None, at=0, into=None):
    s = a.shape[0]
    n_blk = b.shape[1] // npd
    total = n_blk if total is None else total
    assert at % n_blk == 0

    def body(*refs):
        a_ref, b_ref = refs[:2]
        o_ref = refs[-1]
        si = pl.program_id(0)

        @pl.when(si == 0)
        def _():
            o_ref[...] = jnp.zeros(o_ref.shape, F32)

        a_t = a_ref[...].T
        for j in range(n_blk):
            o_ref[j] += _dot(a_t, b_ref[:, npd * j:npd * (j + 1)])

    in_specs = [pl.BlockSpec((ts, D), lambda si: (si, 0)), pl.BlockSpec((ts, n_blk * npd), lambda si: (si, 0))]
    args = [a, b]
    aliases = {}
    if into is not None:
        in_specs.append(pl.BlockSpec(memory_space=pl.ANY))
        args.append(into)
        aliases = {2: 0}
    return pl.pallas_call(
        body, name=name, grid=(s // ts,), in_specs=in_specs,
        out_specs=pl.BlockSpec((n_blk, D, npd), lambda si: (at // n_blk, 0, 0)),
        out_shape=_sds((total, D, npd), F32), input_output_aliases=aliases,
        compiler_params=_params(56, 1),
    )(*args)


SMALL_ROWS = 40
GRAD_ROWS = 48


def kernel(x, a_w_in, a_b_in, a_w_dw, a_b_dw, a_ln_g, a_ln_b, a_w_out, a_b_out, kv_w, b_w_in, b_w_out, b_b_out, post_ln_g, post_ln_b, loss_target, m_a_w_in, m_a_b_in, m_a_w_dw, m_a_b_dw, m_a_ln_g, m_a_ln_b, m_a_w_out, m_a_b_out, m_kv_w, m_b_w_in, m_b_w_out, m_b_b_out, m_post_ln_g, m_post_ln_b, v_a_w_in, v_a_b_in, v_a_w_dw, v_a_b_dw, v_a_ln_g, v_a_ln_b, v_a_w_out, v_a_b_out, v_kv_w, v_b_w_in, v_b_w_out, v_b_b_out, v_post_ln_g, v_post_ln_b):
    s = x.shape[1]
    assert x.shape == (1, s, D) and s % (DILATIONS[-1] * BLK) == 0
    xs = x.reshape(s, D)
    tgt = loss_target.reshape(s, D)
    me = 4 * lax.axis_index("x") + 2 * lax.axis_index("y") + lax.axis_index("c")
    c_idx = lax.axis_index("c").astype(jnp.int32).reshape(1)

    def small_pack(b_in, w_dw, b_dw, ln_g, ln_b, b_out):
        rows = [b_in.reshape(3, BLK), w_dw.reshape(CONV_W, BLK), b_dw.reshape(1, BLK), ln_g.reshape(1, BLK),
                ln_b.reshape(1, BLK), b_out.reshape(1, BLK)]
        n = sum(r.shape[0] for r in rows)
        return jnp.concatenate(rows + [jnp.zeros((SMALL_ROWS - n, BLK), F32)], axis=0)

    wa_in, sm, *later = _all_gather(
        "gather_first", [a_w_in[0], small_pack(a_b_in, a_w_dw, a_b_dw, a_ln_g, a_ln_b, a_b_out)], [BF16, F32],
        casts=[a_w_out[0], kv_w, b_w_in[0], b_w_out[0]])
    ba_in = sm[:, 0:3, :].reshape(1, 3 * D)
    w_dw = jnp.concatenate([sm[:, 3:3 + CONV_W, :].transpose(1, 0, 2).reshape(CONV_W, D), jnp.zeros((1, D), F32)], axis=0)
    b_dw, ln_g, ln_b, ba_out = (sm[:, 34 + k, :].reshape(1, D) for k in range(4))
    pg0, pg1 = post_ln_g[0:1], post_ln_g[1:2]
    pb0, pb1 = post_ln_b[0:1], post_ln_b[1:2]

    h_a, u0, xb = _a_in_proj(xs, wa_in, ba_in, 512)
    (u1, g_a), (wa_out, wkv, wb_in, wb_out) = _a_conv_gate(u0, h_a, w_dw, b_dw, ln_g, ln_b, 256, _gather_plan(later))
    wa_out = wa_out.reshape(D, D)
    wb_out = wb_out.reshape(D, D)
    n1, rstd1, x1b = _a_out_proj(g_a, wa_out, ba_out, xs, pg0, pb0, 512)
    k_all, v_all = _kv_proj(x1b, wkv, 512)
    q, z_b = _b_in_proj(x1b, wb_in, 512)
    per_group = [_attn_forward(q, k_all, v_all, None, None, None, g, True, False) for g in range(3)]
    o, lse, g_b = _attn_merge([og for og, _ in per_group], [lg for _, lg in per_group], z_b, 512)
    dr2, dr2b, loss8, dpg1, dpb1, dbb_out = _b_out_loss(g_b, wb_out, b_b_out, n1, pg0, pb0, pg1, pb1, tgt, 512)

    do, dhb, dd = _b_out_bwd(dr2b, wb_out, z_b, o, 512)
    dk_all = dv_all = None
    for g in range(3):
        dhb, dk_all, dv_all = _attn_backward(q, k_all, v_all, do, lse, dd, dhb, dk_all, dv_all, g)
    dr1, dr1b, dpg0, dpb0, dba_out = _b_in_bwd(dr2, dhb, dk_all, dv_all, wb_in, wkv, n1, rstd1, pg0, 256)

    p_kv = _wgrad("wgrad_k", x1b, dk_all, 768, 1024, total=N_DEV)
    p_kv = _wgrad("wgrad_v", x1b, dv_all, 768, 1024, total=N_DEV, at=N_DEV // 2, into=p_kv)
    p_b_in = _wgrad("wgrad_b_in", x1b, dhb, 512, 512)
    p_a_out = _wgrad("wgrad_a_out", g_a, dr1b, D, 1024).reshape(N_DEV, BLK, D)
    p_b_out = _wgrad("wgrad_b_out", g_b, dr2b, D, 1024).reshape(N_DEV, BLK, D)
    parts = [p_kv, p_b_in, p_a_out, p_b_out]
    (du1, dha, dln_g, dln_b, dbz), from_sibling = _a_out_bwd(dr1b, wa_out, u1, h_a, ln_g, ln_b, 512, _sibling_plan(parts))
    chip_sums = [_pair_add(f"pair_add_{k}", p, r, c_idx) for k, (p, r) in enumerate(zip(parts, from_sibling))]
    (dha, dw_dw, db_dw, dba, dbg), from_chips = _a_conv_bwd(du1, u0, h_a, dha, w_dw, 256, _chips_plan(chip_sums))
    p_a_in = _wgrad("wgrad_a_in", xb, dha, 384, 1024)
    (from_sibling_a,) = _exchange_sibling("reduce_sibling_a_in", [p_a_in])
    sum_a = _pair_add("pair_add_a_in", p_a_in, from_sibling_a, c_idx)
    (grad_x,), (from_chips_a,) = _a_in_bwd(dr1, dha, wa_in, 512, _chips_plan([sum_a]))

    reduced = [from_chips_a] + from_chips
    big_w = [a_w_in[0], kv_w, b_w_in[0], a_w_out[0], b_w_out[0]]
    big_m = [m_a_w_in[0], m_kv_w, m_b_w_in[0], m_a_w_out[0], m_b_w_out[0]]
    big_v = [v_a_w_in[0], v_kv_w, v_b_w_in[0], v_a_w_out[0], v_b_w_out[0]]
    big = [_sum_adamw(f"adamw_{k}", reduced[k], big_w[k], big_m[k], big_v[k]) for k in range(5)]

    rows = [dba[0:1], dbg[0:1], dbz[0:1], dw_dw[0:CONV_W], db_dw[0:1], dln_g[0:1], dln_b[0:1], dba_out[0:1],
            dbb_out[0:1], dpg0[0:1], dpg1[0:1], dpb0[0:1], dpb1[0:1], loss8[0:1]]
    n_rows = sum(r.shape[0] for r in rows)
    gpack = jnp.concatenate(rows + [jnp.zeros((GRAD_ROWS - n_rows, D), F32)], axis=0)
    (gall,) = _all_gather("gather_small_grads", [gpack], [F32])
    gs = _small_sum("small_sum", gall)
    loss = gs[43, 0]

    def my(vec, width):
        return lax.dynamic_slice_in_dim(vec, me * width, width, axis=-1)

    g_small = [my(gs[0:3].reshape(1, 3 * D), 384), my(gs[3:34], BLK)[None], my(gs[34:35], BLK), my(gs[35:36], BLK),
               my(gs[36:37], BLK), my(gs[37:38], BLK), gs[38:39], gs[39:41], gs[41:43]]
    w_small = [a_b_in, a_w_dw, a_b_dw, a_ln_g, a_ln_b, a_b_out, b_b_out, post_ln_g, post_ln_b]
    m_small = [m_a_b_in, m_a_w_dw, m_a_b_dw, m_a_ln_g, m_a_ln_b, m_a_b_out, m_b_b_out, m_post_ln_g, m_post_ln_b]
    v_small = [v_a_b_in, v_a_w_dw, v_a_b_dw, v_a_ln_g, v_a_ln_b, v_a_b_out, v_b_b_out, v_post_ln_g, v_post_ln_b]
    sizes = [math.prod(w.shape) for w in w_small]
    total = sum(sizes)
    padded = -(-total // (8 * BLK)) * (8 * BLK)

    def flat(parts_):
        return jnp.concatenate([p.reshape(-1) for p in parts_] + [jnp.ones((padded - total,), F32)]).reshape(-1, BLK)

    sd, sm_new, sv_new = _small_adamw("adamw_small", flat(w_small), flat(g_small), flat(m_small), flat(v_small))

    def unflat(packed):
        out, pos = [], 0
        vec = packed.reshape(-1)
        for w, n in zip(w_small, sizes):
            out.append(vec[pos:pos + n].reshape(w.shape))
            pos += n
        return out

    g_small = [g.reshape(w.shape) for g, w in zip(g_small, w_small)]
    d_small, nm_small, nv_small = unflat(sd), unflat(sm_new), unflat(sv_new)

    def ordered(bigs, smalls):
        a_in, kvw, b_in, a_out, b_out = bigs
        return [a_in[None], smalls[0], smalls[1], smalls[2], smalls[3], smalls[4], a_out[None], smalls[5],
                kvw, b_in[None], b_out[None], smalls[6], smalls[7], smalls[8]]

    grads = ordered([b[0] for b in big], g_small)
    deltas = ordered([b[1] for b in big], d_small)
    new_m = ordered([b[2] for b in big], nm_small)
    new_v = ordered([b[3] for b in big], nv_small)
    return (loss, grad_x.reshape(1, s, D), *grads, *deltas, *new_m, *new_v)
```

```python
import math

import numpy as np
import jax
import jax.numpy as jnp
from jax import lax
from jax.experimental import pallas as pl
from jax.experimental.pallas import tpu as pltpu

F32 = jnp.float32
BF16 = jnp.bfloat16
MESH = pl.DeviceIdType.MESH

D = 1024
N_DEV = 8
HEAD_DIM = 64
N_HEADS = 16
DILATIONS = (1, 4, 16)
BLK = 128
CONV_W = 31
ALPHA = (2.0 * 2) ** 0.25
LN_EPS = 1e-5
SLOPES = tuple(2.0 ** (-8.0 * (h + 1) / N_HEADS) for h in range(N_HEADS))
NEG = -1e30

ADAM_LR = 0.001
ADAM_B1 = 0.9
ADAM_B2 = 0.999
ADAM_EPS = 1e-08
ADAM_WD = 0.01
ADAM_STEP = 10

VMEM_CAP_MB = 64


def _params(vmem_mb, n_grid=0):
    sem = ("arbitrary",) * n_grid if n_grid else None
    return pltpu.CompilerParams(dimension_semantics=sem, vmem_limit_bytes=min(vmem_mb, VMEM_CAP_MB - 6) * 2 ** 20)


def _sds(shape, dtype):
    return jax.ShapeDtypeStruct(tuple(shape), dtype)


def _sigmoid(v):
    return jax.nn.sigmoid(v)


def _dsilu(v, s):
    return s * (1.0 + v * (1.0 - s))


def _ln_stats(r):
    mu = jnp.mean(r, axis=-1, keepdims=True)
    xc = r - mu
    var = jnp.mean(xc * xc, axis=-1, keepdims=True)
    rstd = lax.rsqrt(var + LN_EPS)
    return xc * rstd, rstd


def _ln_bwd(dn, n, rstd):
    m1 = jnp.mean(dn, axis=-1, keepdims=True)
    m2 = jnp.mean(dn * n, axis=-1, keepdims=True)
    return rstd * (dn - m1 - n * m2)


def _rowsum8(v):
    tm, c = v.shape
    return v.reshape(tm // 8, 8, c).sum(axis=0)


def _acc_init(i, *refs):
    @pl.when(i == 0)
    def _():
        for r in refs:
            r[...] = jnp.zeros(r.shape, r.dtype)


def _acc_finish(i, last, *refs):
    @pl.when(i == last)
    def _():
        for r in refs:
            r[...] = jnp.broadcast_to(jnp.sum(r[...], axis=0, keepdims=True), r.shape)


def _dot(a, b):
    return jnp.dot(a, b, preferred_element_type=F32)


def _dot_nt(a, b):
    return lax.dot_general(a, b, (((1,), (1,)), ((), ())), preferred_element_type=F32)


def _place():
    return lax.axis_index("x"), lax.axis_index("y"), lax.axis_index("c")


def _all_gather(name, arrays, dtypes, casts=()):
    n = len(arrays)
    nc = len(casts)

    def body(*refs):
        ins, cast_ins = refs[:n], refs[n:n + nc]
        outs, cast_outs = refs[n + nc:2 * n + nc], refs[2 * n + nc:2 * (n + nc)]
        stages = refs[2 * (n + nc):3 * n + 2 * nc]
        send_sems, recv_sems, local_sems = refs[3 * n + 2 * nc:]
        x, y, c = _place()
        me, sibling = (x, y, c), (x, y, 1 - c)
        chips = [(1 - x, y), (x, 1 - y), (1 - x, 1 - y)]

        def slot(ref, p):
            return ref.at[4 * p[0] + 2 * p[1] + p[2]]

        def copy(a, k, block, to, src=None):
            return pltpu.make_async_remote_copy(
                src_ref=slot(outs[a], block) if src is None else src, dst_ref=slot(outs[a], block),
                send_sem=send_sems.at[a, k], recv_sem=recv_sems.at[a, k], device_id=to, device_id_type=MESH)

        first, mine = [], []
        for a in range(n):
            stages[a][...] = ins[a][...].astype(stages[a].dtype)
            cp = pltpu.make_async_copy(stages[a], slot(outs[a], me), local_sems.at[a])
            cp.start()
            mine.append(cp)
            first.append(copy(a, 0, me, sibling, src=stages[a]))
            first += [copy(a, 1 + j, me, (*chip, c), src=stages[a]) for j, chip in enumerate(chips)]
        for cp in first:
            cp.start()
        for src, dst in zip(cast_ins, cast_outs):
            dst[...] = src[...].astype(BF16)
        passed = []
        for j, chip in enumerate(chips):
            for a in range(n):
                copy(a, 1 + j, (*chip, c), me).wait_recv()
                cp = copy(a, 4 + j, (*chip, c), sibling)
                cp.start()
                passed.append(cp)
        for a in range(n):
            copy(a, 0, sibling, me).wait_recv()
            for j, chip in enumerate(chips):
                copy(a, 4 + j, (*chip, 1 - c), me).wait_recv()
        for cp in first + passed:
            cp.wait_send()
        for cp in mine:
            cp.wait()

    vmem_bytes = sum(math.prod(a.shape) * (jnp.dtype(a.dtype).itemsize + jnp.dtype(dt).itemsize)
                     for a, dt in zip(arrays, dtypes)) + sum(math.prod(a.shape) * 6 for a in casts)
    vm = pl.BlockSpec(memory_space=pltpu.VMEM)
    return pl.pallas_call(
        body, name=name,
        out_shape=[_sds((N_DEV,) + a.shape, dt) for a, dt in zip(arrays, dtypes)] + [_sds(a.shape, BF16) for a in casts],
        in_specs=[vm] * (n + nc),
        out_specs=[pl.BlockSpec(memory_space=pl.ANY)] * n + [vm] * nc,
        scratch_shapes=[pltpu.VMEM(a.shape, dt) for a, dt in zip(arrays, dtypes)]
        + [pltpu.SemaphoreType.DMA((n, 7)), pltpu.SemaphoreType.DMA((n, 7)), pltpu.SemaphoreType.DMA((n,))],
        compiler_params=_params(vmem_bytes // 2 ** 20 + 8),
    )(*arrays, *casts)


class _Plan:
    def __init__(self, args, out_shape, scratch, start, mid, finish):
        self.args, self.out_shape, self.scratch = list(args), list(out_shape), list(scratch)
        self.start, self.mid, self.finish = start, mid, finish


def _gather_plan(shards):
    n = len(shards)

    def copies(ins, outs, sems):
        send_sems, recv_sems, local_sems = sems
        x, y, c = _place()
        me, sibling = (x, y, c), (x, y, 1 - c)
        chips = [(1 - x, y), (x, 1 - y), (1 - x, 1 - y)]

        def slot(ref, p):
            return ref.at[4 * p[0] + 2 * p[1] + p[2]]

        def copy(a, k, block, to, src=None):
            return pltpu.make_async_remote_copy(
                src_ref=slot(outs[a], block) if src is None else src, dst_ref=slot(outs[a], block),
                send_sem=send_sems.at[a, k], recv_sem=recv_sems.at[a, k], device_id=to, device_id_type=MESH)

        mine = [pltpu.make_async_copy(ins[a], slot(outs[a], me), local_sems.at[a]) for a in range(n)]
        first = [copy(a, 0, me, sibling, src=ins[a]) for a in range(n)]
        first += [copy(a, 1 + j, me, (*chip, c), src=ins[a]) for a in range(n) for j, chip in enumerate(chips)]
        arrive = [copy(a, 1 + j, (*chip, c), me) for j, chip in enumerate(chips) for a in range(n)]
        passed = [copy(a, 4 + j, (*chip, c), sibling) for j, chip in enumerate(chips) for a in range(n)]
        from_sibling = [copy(a, 0, sibling, me) for a in range(n)]
        from_sibling += [copy(a, 4 + j, (*chip, 1 - c), me) for a in range(n) for j, chip in enumerate(chips)]
        return mine, first, arrive, passed, from_sibling

    def start(ins, outs, sems):
        mine, first, _, _, _ = copies(ins, outs, sems)
        for cp in mine + first:
            cp.start()

    def mid(ins, outs, sems):
        _, _, arrive, passed, _ = copies(ins, outs, sems)
        for got, on in zip(arrive, passed):
            got.wait_recv()
            on.start()

    def finish(ins, outs, sems):
        mine, first, _, passed, from_sibling = copies(ins, outs, sems)
        for cp in from_sibling:
            cp.wait_recv()
        for cp in first + passed:
            cp.wait_send()
        for cp in mine:
            cp.wait()

    return _Plan(shards, [_sds((N_DEV,) + a.shape, a.dtype) for a in shards],
                 [pltpu.SemaphoreType.DMA((n, 7)), pltpu.SemaphoreType.DMA((n, 7)), pltpu.SemaphoreType.DMA((n,))],
                 start, mid, finish)


def _sibling_plan(parts):
    n = len(parts)

    def copies(ins, outs, sems):
        send_sems, recv_sems = sems
        x, y, c = _place()
        return [pltpu.make_async_remote_copy(
            src_ref=ins[a].at[2 * p + 1 - c], dst_ref=outs[a].at[p], send_sem=send_sems.at[a, p],
            recv_sem=recv_sems.at[a, p], device_id=(x, y, 1 - c), device_id_type=MESH)
            for a in range(n) for p in range(4)]

    def start(ins, outs, sems):
        for cp in copies(ins, outs, sems):
            cp.start()

    def finish(ins, outs, sems):
        cps = copies(ins, outs, sems)
        for cp in cps:
            cp.wait_recv()
        for cp in cps:
            cp.wait_send()

    return _Plan(parts, [_sds((4,) + p.shape[1:], p.dtype) for p in parts],
                 [pltpu.SemaphoreType.DMA((n, 4)), pltpu.SemaphoreType.DMA((n, 4))], start, None, finish)


def _chips_plan(sums):
    n = len(sums)

    def copies(ins, outs, sems):
        send_sems, recv_sems, local_sems = sems
        x, y, c = _place()
        my_chip = 2 * x + y
        chips = [(1 - x, y), (x, 1 - y), (1 - x, 1 - y)]
        mine = [pltpu.make_async_copy(ins[a].at[my_chip], outs[a].at[my_chip], local_sems.at[a]) for a in range(n)]
        remote = [pltpu.make_async_remote_copy(
            src_ref=ins[a].at[2 * px + py], dst_ref=outs[a].at[my_chip], send_sem=send_sems.at[a, k],
            recv_sem=recv_sems.at[a, k], device_id=(px, py, c), device_id_type=MESH)
            for a in range(n) for k, (px, py) in enumerate(chips)]
        return mine, remote

    def start(ins, outs, sems):
        mine, remote = copies(ins, outs, sems)
        for cp in mine + remote:
            cp.start()

    def finish(ins, outs, sems):
        mine, remote = copies(ins, outs, sems)
        for cp in remote:
            cp.wait_recv()
        for cp in remote:
            cp.wait_send()
        for cp in mine:
            cp.wait()

    return _Plan(sums, [_sds(s.shape, s.dtype) for s in sums],
                 [pltpu.SemaphoreType.DMA((n, 3)), pltpu.SemaphoreType.DMA((n, 3)), pltpu.SemaphoreType.DMA((n,))],
                 start, None, finish)


def _planned_call(plan, args, body, *, name, grid, in_specs, out_specs, out_shape, scratch_shapes=(), mid_step=None,
                  **kw):
    in_specs, out_specs, out_shape = list(in_specs), list(out_specs), list(out_shape)
    scratch_shapes = list(scratch_shapes)
    if plan is None:
        res = pl.pallas_call(body, name=name, grid=grid, in_specs=in_specs, out_specs=out_specs, out_shape=out_shape,
                             scratch_shapes=scratch_shapes, **kw)(*args)
        return list(res), []
    n_in, n_out, n_scr = len(in_specs), len(out_specs), len(scratch_shapes)
    p_in, p_out = len(plan.args), len(plan.out_shape)
    steps = grid[0]

    def fused(*refs):
        ins, pins = refs[:n_in], refs[n_in:n_in + p_in]
        o0 = n_in + p_in
        outs, pouts = refs[o0:o0 + n_out], refs[o0 + n_out:o0 + n_out + p_out]
        s0 = o0 + n_out + p_out
        scr, pscr = refs[s0:s0 + n_scr], refs[s0 + n_scr:]
        i = pl.program_id(0)

        @pl.when(i == 0)
        def _():
            plan.start(pins, pouts, pscr)

        body(*ins, *outs, *scr)
        if plan.mid is not None:
            @pl.when(i == mid_step)
            def _():
                plan.mid(pins, pouts, pscr)

        @pl.when(i == steps - 1)
        def _():
            plan.finish(pins, pouts, pscr)

    hbm = pl.BlockSpec(memory_space=pl.ANY)
    res = pl.pallas_call(
        fused, name=name, grid=grid, in_specs=in_specs + [hbm] * p_in, out_specs=out_specs + [hbm] * p_out,
        out_shape=out_shape + plan.out_shape, scratch_shapes=scratch_shapes + plan.scratch, **kw)(*args, *plan.args)
    return list(res[:n_out]), list(res[n_out:])


def _exchange_sibling(name, parts):
    n = len(parts)

    def body(*refs):
        ins, outs = refs[:n], refs[n:2 * n]
        send_sems, recv_sems = refs[2 * n:]
        x, y, c = _place()
        copies = []
        for a in range(n):
            for p in range(4):
                copies.append(pltpu.make_async_remote_copy(
                    src_ref=ins[a].at[2 * p + 1 - c], dst_ref=outs[a].at[p],
                    send_sem=send_sems.at[a, p], recv_sem=recv_sems.at[a, p],
                    device_id=(x, y, 1 - c), device_id_type=MESH))
        for cp in copies:
            cp.start()
        for cp in copies:
            cp.wait_recv()
        for cp in copies:
            cp.wait_send()

    return pl.pallas_call(
        body, name=name,
        out_shape=[_sds((4,) + p.shape[1:], p.dtype) for p in parts],
        in_specs=[pl.BlockSpec(memory_space=pl.ANY)] * n,
        out_specs=[pl.BlockSpec(memory_space=pl.ANY)] * n,
        scratch_shapes=[pltpu.SemaphoreType.DMA((n, 4)), pltpu.SemaphoreType.DMA((n, 4))],
    )(*parts)


def _exchange_chips(name, sums):
    n = len(sums)

    def body(*refs):
        ins, outs = refs[:n], refs[n:2 * n]
        send_sems, recv_sems, local_sems = refs[2 * n:]
        x, y, c = _place()
        my_chip = 2 * x + y
        chips = [(1 - x, y), (x, 1 - y), (1 - x, 1 - y)]
        copies, mine = [], []
        for a in range(n):
            cp = pltpu.make_async_copy(ins[a].at[my_chip], outs[a].at[my_chip], local_sems.at[a])
            cp.start()
            mine.append(cp)
            for k, (px, py) in enumerate(chips):
                copies.append(pltpu.make_async_remote_copy(
                    src_ref=ins[a].at[2 * px + py], dst_ref=outs[a].at[my_chip],
                    send_sem=send_sems.at[a, k], recv_sem=recv_sems.at[a, k],
                    device_id=(px, py, c), device_id_type=MESH))
        for cp in copies:
            cp.start()
        for cp in copies:
            cp.wait_recv()
        for cp in copies:
            cp.wait_send()
        for cp in mine:
            cp.wait()

    return pl.pallas_call(
        body, name=name,
        out_shape=[_sds(s.shape, s.dtype) for s in sums],
        in_specs=[pl.BlockSpec(memory_space=pl.ANY)] * n,
        out_specs=[pl.BlockSpec(memory_space=pl.ANY)] * n,
        scratch_shapes=[pltpu.SemaphoreType.DMA((n, 3)), pltpu.SemaphoreType.DMA((n, 3)),
                        pltpu.SemaphoreType.DMA((n,))],
    )(*sums)


def _pair_add(name, part, recv, c_idx):
    _, r, c = part.shape
    tr = min(r, 256)

    def body(c_ref, a_ref, b_ref, o_ref):
        o_ref[...] = a_ref[...] + b_ref[...]

    return pl.pallas_call(
        body, name=name,
        grid_spec=pltpu.PrefetchScalarGridSpec(
            num_scalar_prefetch=1, grid=(4, r // tr),
            in_specs=[pl.BlockSpec((1, tr, c), lambda p, i, cr: (2 * p + cr[0], i, 0)),
                      pl.BlockSpec((1, tr, c), lambda p, i, cr: (p, i, 0))],
            out_specs=pl.BlockSpec((1, tr, c), lambda p, i, cr: (p, i, 0))),
        out_shape=_sds((4, r, c), F32),
        compiler_params=_params(16, 2),
    )(c_idx, part, recv)


def _adamw_math(w, g, m, v):
    m = ADAM_B1 * m + (1.0 - ADAM_B1) * g
    v = ADAM_B2 * v + (1.0 - ADAM_B2) * (g * g)
    m_hat = m / (1.0 - ADAM_B1 ** ADAM_STEP)
    v_hat = v / (1.0 - ADAM_B2 ** ADAM_STEP)
    delta = -ADAM_LR * (m_hat / (jnp.sqrt(v_hat) + ADAM_EPS) + ADAM_WD * w)
    return delta, m, v


def _sum_adamw(name, recv, w, m, v):
    r, c = w.shape
    tr = min(r, 256)

    def body(p_ref, w_ref, m_ref, v_ref, g_ref, d_ref, nm_ref, nv_ref):
        g = (p_ref[0] + p_ref[1]) + (p_ref[2] + p_ref[3])
        g_ref[...] = g
        d_ref[...], nm_ref[...], nv_ref[...] = _adamw_math(w_ref[...], g, m_ref[...], v_ref[...])

    blk = pl.BlockSpec((tr, c), lambda i: (i, 0))
    return pl.pallas_call(
        body, name=name, grid=(r // tr,),
        in_specs=[pl.BlockSpec((4, tr, c), lambda i: (0, i, 0)), blk, blk, blk],
        out_specs=[blk] * 4, out_shape=[_sds((r, c), F32)] * 4,
        compiler_params=_params(24, 1),
    )(recv, w, m, v)


def _small_finish(gall, me, ws, ms, vs):
    n = len(ws)
    rows = gall.shape[1]

    def body(me_ref, gall_ref, *refs):
        w_refs, m_refs, v_refs = refs[:n], refs[n:2 * n], refs[2 * n:3 * n]
        loss_ref = refs[3 * n]
        outs = refs[3 * n + 1:7 * n + 1]
        gs, flat = refs[7 * n + 1:]
        acc = gall_ref[0]
        for j in range(1, N_DEV):
            acc = acc + gall_ref[j]
        gs[...] = acc
        loss_ref[...] = gs[43:44, 0:1]
        for k in range(3):
            flat[:, k * D:(k + 1) * D] = gs[k:k + 1, :]
        dev = me_ref[0]
        mine = pl.ds(pl.multiple_of(dev * BLK, BLK), BLK)
        grads = [flat[:, pl.ds(pl.multiple_of(dev * 384, BLK), 384)], gs[3:3 + CONV_W, mine][None]]
        grads += [gs[34 + k:35 + k, mine] for k in range(4)]
        grads += [gs[38:39, :], gs[39:41, :], gs[41:43, :]]
        for k in range(n):
            g = grads[k]
            outs[k][...] = g
            outs[n + k][...], outs[2 * n + k][...], outs[3 * n + k][...] = _adamw_math(
                w_refs[k][...], g, m_refs[k][...], v_refs[k][...])

    def whole(a):
        nd = a.ndim
        return pl.BlockSpec(a.shape, lambda i, me_ref: (0,) * nd)

    res = pl.pallas_call(
        body, name="small_finish",
        grid_spec=pltpu.PrefetchScalarGridSpec(
            num_scalar_prefetch=1, grid=(1,),
            in_specs=[whole(gall)] + [whole(a) for a in list(ws) + list(ms) + list(vs)],
            out_specs=[pl.BlockSpec((1, 1), lambda i, me_ref: (0, 0))] + [whole(a) for a in ws] * 4,
            scratch_shapes=[pltpu.VMEM((rows, D), F32), pltpu.VMEM((1, 3 * D), F32)]),
        out_shape=[_sds((1, 1), F32)] + [_sds(a.shape, F32) for a in ws] * 4,
    )(me, gall, *ws, *ms, *vs)
    return res[0], res[1:n + 1], res[n + 1:2 * n + 1], res[2 * n + 1:3 * n + 1], res[3 * n + 1:]


def _small_sum(name, gathered):
    _, r, c = gathered.shape

    def body(p_ref, o_ref):
        acc = p_ref[0]
        for j in range(1, N_DEV):
            acc = acc + p_ref[j]
        o_ref[...] = acc

    return pl.pallas_call(body, name=name, out_shape=_sds((r, c), F32),
                          in_specs=[pl.BlockSpec(memory_space=pltpu.VMEM)],
                          out_specs=pl.BlockSpec(memory_space=pltpu.VMEM))(gathered)


def _small_adamw(name, w, g, m, v):
    def body(w_ref, g_ref, m_ref, v_ref, d_ref, nm_ref, nv_ref):
        d_ref[...], nm_ref[...], nv_ref[...] = _adamw_math(w_ref[...], g_ref[...], m_ref[...], v_ref[...])

    vm = pl.BlockSpec(memory_space=pltpu.VMEM)
    return pl.pallas_call(body, name=name, out_shape=[_sds(w.shape, F32)] * 3,
                          in_specs=[vm] * 4, out_specs=[vm] * 3)(w, g, m, v)


def _a_in_proj(x, w_g, b_full, tm):
    s = x.shape[0]
    npd = w_g.shape[2]

    def body(x_ref, w_ref, b_ref, h_ref, u0_ref, xb_ref):
        xb = x_ref[...].astype(BF16)
        xb_ref[...] = xb
        for j in range(N_DEV):
            sl = slice(npd * j, npd * (j + 1))
            h_ref[:, sl] = _dot(xb, w_ref[j]) + b_ref[:, sl]
        u0_ref[...] = h_ref[:, 0:D] * _sigmoid(h_ref[:, D:2 * D])

    row = lambda i: (i, 0)
    return pl.pallas_call(
        body, name="a_in_proj", grid=(s // tm,),
        in_specs=[pl.BlockSpec((tm, D), row), pl.BlockSpec(w_g.shape, lambda i: (0, 0, 0)),
                  pl.BlockSpec((1, 3 * D), lambda i: (0, 0))],
        out_specs=[pl.BlockSpec((tm, 3 * D), row), pl.BlockSpec((tm, D), row), pl.BlockSpec((tm, D), row)],
        out_shape=[_sds((s, 3 * D), F32), _sds((s, D), F32), _sds((s, D), BF16)],
        compiler_params=_params(44, 1),
    )(x, w_g, b_full)


CONV_HALO = 32
CONV_CHUNK = 16
SUBLANES = 8
COPY_ROWS = 56


def _shifted_copies(buf, shs, tm):
    n = tm + CONV_HALO - SUBLANES
    for s in range(1, SUBLANES):
        for c0 in range(0, n, COPY_ROWS):
            c1 = min(c0 + COPY_ROWS, n)
            shs[s - 1, c0:c1, :] = buf[c0 + s:c1 + s, :]


LANES = 128
CONV_ROWS = 128


def _tap_windows(offsets):
    out = []
    for s in range(SUBLANES):
        taps = [(j, o // SUBLANES) for j, o in enumerate(offsets) if o % SUBLANES == s]
        lo, hi = min(a for _, a in taps), max(a for _, a in taps)
        out.append((s, lo, hi - lo, [(j, a - lo) for j, a in taps]))
    return out


def _window(buf, shs, s, r0, lo, span, lanes):
    n = CONV_ROWS + SUBLANES * span
    rows = pl.ds(r0 + SUBLANES * lo, n)
    v = buf[rows, lanes] if s == 0 else shs[s - 1, rows, lanes]
    return v.reshape(n // SUBLANES, SUBLANES, LANES)


def _spread_taps(i, w_ref, wb):
    @pl.when(i == 0)
    def _():
        for j in range(CONV_W):
            wb[j] = jnp.broadcast_to(w_ref[j:j + 1, :], (SUBLANES, D))


def _a_conv_gate(u0, h, w_dw, b_dw, ln_g, ln_b, tm, plan=None):
    s = u0.shape[0]
    per = tm // CONV_HALO

    def body(u0_ref, halo_ref, z_ref, w_ref, bdw_ref, g_ref, b_ref, u1_ref, ga_ref, buf, shs, wb):
        i = pl.program_id(0)
        _spread_taps(i, w_ref, wb)
        buf[0:CONV_HALO, :] = jnp.where(i > 0, halo_ref[...], 0.0)
        buf[CONV_HALO:, :] = u0_ref[...]
        _shifted_copies(buf, shs, tm)
        off = CONV_HALO - (CONV_W - 1)

        windows = _tap_windows([off + j for j in range(CONV_W)])
        grp = CONV_ROWS // SUBLANES

        def chunk(it, carry):
            r0 = pl.multiple_of((it // (D // LANES)) * CONV_ROWS, CONV_ROWS)
            lanes = pl.ds(pl.multiple_of((it % (D // LANES)) * LANES, LANES), LANES)
            acc = jnp.broadcast_to(bdw_ref[:, lanes], (CONV_ROWS, LANES)).reshape(grp, SUBLANES, LANES)
            for s, lo, span, taps in windows:
                win = _window(buf, shs, s, r0, lo, span, lanes)
                for j, a in taps:
                    acc = acc + wb[j, :, lanes][None] * win[a:a + grp]
            u1_ref[pl.ds(r0, CONV_ROWS), lanes] = acc.reshape(CONV_ROWS, LANES)
            return carry

        lax.fori_loop(0, (tm // CONV_ROWS) * (D // LANES), chunk, 0)
        n, _ = _ln_stats(u1_ref[...])
        pre = n * g_ref[...] + b_ref[...]
        z = z_ref[...]
        ga_ref[...] = ((pre * _sigmoid(pre)) * (z * _sigmoid(z))).astype(BF16)

    row = lambda i: (i, 0)
    vec = pl.BlockSpec((1, D), lambda i: (0, 0))
    steps = s // tm
    return _planned_call(
        plan, (u0, u0, h, w_dw, b_dw, ln_g, ln_b), body, name="a_conv_gate", grid=(steps,), mid_step=steps // 2,
        in_specs=[pl.BlockSpec((tm, D), row),
                  pl.BlockSpec((CONV_HALO, D), lambda i: (jnp.maximum(i * per - 1, 0), 0)),
                  pl.BlockSpec((tm, D), lambda i: (i, 2)),
                  pl.BlockSpec((32, D), lambda i: (0, 0)), vec, vec, vec],
        out_specs=[pl.BlockSpec((tm, D), row), pl.BlockSpec((tm, D), row)],
        out_shape=[_sds((s, D), F32), _sds((s, D), BF16)],
        scratch_shapes=[pltpu.VMEM((tm + CONV_HALO, D), F32), pltpu.VMEM((SUBLANES - 1, tm + CONV_HALO, D), F32),
                        pltpu.VMEM((CONV_W, SUBLANES, D), F32)],
        compiler_params=_params(40, 1))


def _a_out_proj(ga, w, b, x, pg, pb, tm):
    s = x.shape[0]

    def body(ga_ref, w_ref, b_ref, x_ref, pg_ref, pb_ref, n_ref, rstd_ref, xb_ref):
        r = ALPHA * x_ref[...] + (_dot(ga_ref[...], w_ref[...]) + b_ref[...])
        n, rstd = _ln_stats(r)
        n_ref[...] = n
        rstd_ref[...] = rstd
        xb_ref[...] = (n * pg_ref[...] + pb_ref[...]).astype(BF16)

    row = lambda i: (i, 0)
    vec = pl.BlockSpec((1, D), lambda i: (0, 0))
    return pl.pallas_call(
        body, name="a_out_proj", grid=(s // tm,),
        in_specs=[pl.BlockSpec((tm, D), row), pl.BlockSpec((D, D), lambda i: (0, 0)), vec,
                  pl.BlockSpec((tm, D), row), vec, vec],
        out_specs=[pl.BlockSpec((tm, D), row), pl.BlockSpec((tm, 1), row), pl.BlockSpec((tm, D), row)],
        out_shape=[_sds((s, D), F32), _sds((s, 1), F32), _sds((s, D), BF16)],
        compiler_params=_params(32, 1),
    )(ga, w, b, x, pg, pb)


def _kv_proj(xb, w_g, tm):
    s = xb.shape[0]
    npd = w_g.shape[2]
    half = N_DEV // 2

    def body(x_ref, w_ref, k_ref, v_ref):
        xv = x_ref[...]
        for j in range(N_DEV):
            o_ref = k_ref if j < half else v_ref
            jj = j % half
            o_ref[:, npd * jj:npd * (jj + 1)] = _dot(xv, w_ref[j]).astype(BF16)

    row = lambda i: (i, 0)
    return pl.pallas_call(
        body, name="kv_proj", grid=(s // tm,),
        in_specs=[pl.BlockSpec((tm, D), row), pl.BlockSpec(w_g.shape, lambda i: (0, 0, 0))],
        out_specs=[pl.BlockSpec((tm, 3 * D), row), pl.BlockSpec((tm, 3 * D), row)],
        out_shape=[_sds((s, 3 * D), BF16), _sds((s, 3 * D), BF16)],
        compiler_params=_params(52, 1),
    )(xb, w_g)


def _b_in_proj(xb, w_g, tm):
    s = xb.shape[0]
    npd = w_g.shape[2]
    scale = HEAD_DIM ** -0.5

    def body(x_ref, w_ref, q_ref, z_ref):
        xv = x_ref[...]
        for j in range(N_DEV):
            hj = _dot(xv, w_ref[j])
            if j < 6:
                q_ref[:, npd * j:npd * (j + 1)] = (hj.astype(BF16) * scale).astype(BF16)
            else:
                z_ref[:, npd * (j - 6):npd * (j - 5)] = hj

    row = lambda i: (i, 0)
    return pl.pallas_call(
        body, name="b_in_proj", grid=(s // tm,),
        in_specs=[pl.BlockSpec((tm, D), row), pl.BlockSpec(w_g.shape, lambda i: (0, 0, 0))],
        out_specs=[pl.BlockSpec((tm, 3 * D), row), pl.BlockSpec((tm, D), row)],
        out_shape=[_sds((s, 3 * D), BF16), _sds((s, D), F32)],
        compiler_params=_params(44, 1),
    )(xb, w_g)


ATTN_NQ = {1: 8, 4: 2, 16: 1}
N_PAIRS = N_HEADS // 2


def _band_table(d):
    qi = np.arange(BLK)[:, None]
    kj = np.arange(2 * BLK)[None, :]
    dist = qi + BLK - kj
    ok = (dist >= 0) & (dist <= BLK)
    return jnp.asarray(np.where(ok, -(d * dist).astype(np.float32), np.float32(NEG)), dtype=F32)


def _slope_table():
    t = np.zeros((N_PAIRS, 8, 2 * BLK), np.float32)
    for h in range(N_HEADS):
        t[h // 2, h % 2, :] = SLOPES[h]
    return jnp.asarray(t)


def _head_masks():
    lane = lax.broadcasted_iota(jnp.int32, (1, BLK), 1)
    lo = (lane < HEAD_DIM).astype(BF16)
    return (lo, (1.0 - lo).astype(BF16))


def _pick_col(tile, lane, h):
    return jnp.sum(jnp.where(lane == h, tile, 0.0), axis=1, keepdims=True)


def _rows(base, n, d):
    return pl.ds(base, n) if d == 1 else pl.ds(base, n, stride=d)


def _attn_fwd(q, k_all, v_all, o_acc, lse_acc, z, g, first, last):
    d = DILATIONS[g]
    s = q.shape[0]
    nq = ATTN_NQ[d]
    halo = BLK * d
    tile = nq * halo
    assert s % tile == 0

    def body(*refs):
        q_ref, k_ref, kh_ref, v_ref, vh_ref, nd_ref, sl_ref = refs[:7]
        k = 7
        if not first:
            oa_ref, la_ref = refs[k:k + 2]
            k += 2
        if last:
            z_ref = refs[k]
            k += 1
        o_ref, l_ref = refs[k:k + 2]
        k += 2
        if last:
            gb_ref = refs[k]
            k += 1
        qf, kf, vf = refs[k:k + 3]
        if last:
            gf = refs[k + 3]
        n = pl.program_id(0)
        hp = pl.program_id(1)
        qf[...] = q_ref[...].astype(F32)
        kf[0:halo, :] = kh_ref[...].astype(F32)
        kf[halo:, :] = k_ref[...].astype(F32)
        vf[0:halo, :] = vh_ref[...].astype(F32)
        vf[halo:, :] = v_ref[...].astype(F32)

        @pl.when(hp == 0)
        def _():
            l_ref[...] = jnp.zeros(l_ref.shape, F32) if first else la_ref[...]

        col = lax.broadcasted_iota(jnp.int32, (2 * BLK, 2 * BLK), 1)
        lane = lax.broadcasted_iota(jnp.int32, (BLK, BLK), 1)
        masks = _head_masks()
        bias = jnp.concatenate([sl_ref[0, e:e + 1, :] * nd_ref[...] for e in range(2)], axis=0)
        bias0 = bias + jnp.where((n == 0) & (col < BLK), NEG, 0.0)
        for b in range(nq):
            for r in range(d):
                rq = _rows(b * halo + r, BLK, d)
                rk = _rows(b * halo + r, 2 * BLK, d)
                q2 = qf[rq, :].astype(BF16)
                kcat = kf[rk, :].astype(BF16)
                vcat = vf[rk, :].astype(BF16)
                lt = l_ref[rq, :]
                sc = _dot_nt(jnp.concatenate([q2 * masks[0], q2 * masks[1]], axis=0), kcat)
                sc = sc + (bias0 if b == 0 else bias)
                m = jnp.max(sc, axis=1, keepdims=True)
                p = jnp.exp(sc - m)
                l = jnp.sum(p, axis=1, keepdims=True)
                oh = _dot(p.astype(BF16), vcat) / l
                lse = m + jnp.log(l)
                if not first:
                    old = jnp.concatenate([_pick_col(lt, lane, 2 * hp + e) for e in range(2)], axis=0)
                    mx = jnp.maximum(old, lse)
                    new = mx + jnp.log(jnp.exp(old - mx) + jnp.exp(lse - mx))
                    keep = jnp.exp(old - new)
                    oh = oh * jnp.exp(lse - new)
                    lse = new
                o2 = jnp.where(lane < HEAD_DIM, oh[0:BLK], oh[BLK:])
                if not first:
                    o2 = o2 + oa_ref[rq, :] * jnp.where(lane < HEAD_DIM, keep[0:BLK], keep[BLK:])
                lt = jnp.where(lane == 2 * hp, lse[0:BLK], lt)
                lt = jnp.where(lane == 2 * hp + 1, lse[BLK:], lt)
                o_ref[rq, :] = o2
                l_ref[rq, :] = lt
                if last:
                    zz = z_ref[rq, :]
                    gf[rq, :] = o2 * (zz * _sigmoid(zz))
        if last:
            gb_ref[...] = gf[...].astype(BF16)

    col_blk = lambda n, hp: (n, g * N_PAIRS + hp)
    halo_blk = lambda n, hp: (jnp.maximum(n * nq - 1, 0), g * N_PAIRS + hp)
    own = pl.BlockSpec((tile, BLK), lambda n, hp: (n, hp))
    own_l = pl.BlockSpec((tile, BLK), lambda n, hp: (n, 0))
    in_specs = [pl.BlockSpec((tile, BLK), col_blk),
                pl.BlockSpec((tile, BLK), col_blk), pl.BlockSpec((halo, BLK), halo_blk),
                pl.BlockSpec((tile, BLK), col_blk), pl.BlockSpec((halo, BLK), halo_blk),
                pl.BlockSpec((BLK, 2 * BLK), lambda n, hp: (0, 0)),
                pl.BlockSpec((1, 8, 2 * BLK), lambda n, hp: (hp, 0, 0))]
    args = [q, k_all, k_all, v_all, v_all, _band_table(d), _slope_table()]
    if not first:
        in_specs += [own, own_l]
        args += [o_acc, lse_acc]
    if last:
        in_specs += [own]
        args += [z]
    out_specs = [own, own_l] + ([own] if last else [])
    out_shape = [_sds((s, D), F32), _sds((s, BLK), F32)] + ([_sds((s, D), BF16)] if last else [])
    scratch = [pltpu.VMEM((tile, BLK), F32), pltpu.VMEM((tile + halo, BLK), F32), pltpu.VMEM((tile + halo, BLK), F32)]
    if last:
        scratch.append(pltpu.VMEM((tile, BLK), F32))
    return pl.pallas_call(
        body, name=f"attn_fwd_g{g}", grid=(s // tile, N_PAIRS), in_specs=in_specs, out_specs=out_specs,
        out_shape=out_shape, scratch_shapes=scratch, compiler_params=_params(32, 2),
    )(*args)


ATTN_PRE = {1: 1, 4: 1, 16: 4}


def _regroup(src, dst, d1):
    n = src.shape[0] // d1
    for r1 in range(d1):
        dst[r1] = src[pl.ds(r1, n, stride=d1), :]


def _ungroup(src, dst, d1):
    n = dst.shape[0] // d1
    for r1 in range(d1):
        dst[pl.ds(r1, n, stride=d1), :] = src[r1]


def _grouped(shape, d1):
    return pltpu.VMEM((d1, shape[0] // d1, shape[1]), F32)


def _with_halo(halo_ref, tile_ref, b):
    if b == 0:
        return jnp.concatenate([halo_ref[...], tile_ref[0:BLK, :]], axis=0)
    return tile_ref[(b - 1) * BLK:(b + 1) * BLK, :]


def _attn_forward(q, k_all, v_all, o_acc, lse_acc, z, g, first, last):
    d = DILATIONS[g]
    s = q.shape[0]
    nq = ATTN_NQ[d]
    d1 = ATTN_PRE[d]
    d2 = d // d1
    halo = BLK * d
    tile = nq * halo
    assert s % tile == 0
    pre = d1 > 1

    def body(*refs):
        refs = list(refs)
        q_ref, k_ref, kh_ref, v_ref, vh_ref, nd_ref, sl_ref = refs[:7]
        del refs[:7]
        oa_ref, la_ref = (refs.pop(0), refs.pop(0)) if not first else (None, None)
        z_ref = refs.pop(0) if last else None
        o_ref, l_ref = refs.pop(0), refs.pop(0)
        gb_ref = refs.pop(0) if last else None
        qf, kf, vf = refs.pop(0), refs.pop(0), refs.pop(0)
        gf = refs.pop(0) if last else None
        q1 = k1 = v1 = l1 = o1 = oa1 = z1 = g1 = None
        if pre:
            q1, k1, v1, l1, o1 = (refs.pop(0) for _ in range(5))
            oa1 = refs.pop(0) if not first else None
            z1, g1 = (refs.pop(0), refs.pop(0)) if last else (None, None)
        n = pl.program_id(0)
        hp = pl.program_id(1)
        if d > 1:
            qf[...] = q_ref[...].astype(F32)
            kf[0:halo, :] = kh_ref[...].astype(F32)
            kf[halo:, :] = k_ref[...].astype(F32)
            vf[0:halo, :] = vh_ref[...].astype(F32)
            vf[halo:, :] = v_ref[...].astype(F32)

        @pl.when(hp == 0)
        def _():
            l_ref[...] = jnp.zeros(l_ref.shape, F32) if first else la_ref[...]

        if pre:
            for src, dst in ((qf, q1), (kf, k1), (vf, v1), (l_ref, l1), (oa_ref, oa1), (z_ref, z1)):
                if src is not None:
                    _regroup(src, dst, d1)

        def pick(nat, grp, r1):
            return grp.at[r1] if pre else nat

        def halves(colv):
            return jnp.where(lane < HEAD_DIM, colv[0:BLK], colv[BLK:])

        def put(tile_v, colv):
            tile_v = jnp.where(lane == 2 * hp, colv[0:BLK], tile_v)
            return jnp.where(lane == 2 * hp + 1, colv[BLK:], tile_v)

        col = lax.broadcasted_iota(jnp.int32, (2 * BLK, 2 * BLK), 1)
        lane = lax.broadcasted_iota(jnp.int32, (BLK, BLK), 1)
        masks = _head_masks()
        bias = jnp.concatenate([sl_ref[0, e:e + 1, :] * nd_ref[...] for e in range(2)], axis=0)
        bias0 = bias + jnp.where((n == 0) & (col < BLK), NEG, 0.0)
        for b in range(nq):
            for r in range(d):
                r1, r2 = r % d1, r // d1
                rq = _rows(b * (halo // d1) + r2, BLK, d2)
                rk = _rows(b * (halo // d1) + r2, 2 * BLK, d2)
                if d > 1:
                    q2 = pick(qf, q1, r1)[rq, :].astype(BF16)
                    kcat = pick(kf, k1, r1)[rk, :].astype(BF16)
                    vcat = pick(vf, v1, r1)[rk, :].astype(BF16)
                else:
                    q2 = q_ref[rq, :]
                    kcat = _with_halo(kh_ref, k_ref, b)
                    vcat = _with_halo(vh_ref, v_ref, b)
                sc = _dot_nt(jnp.concatenate([q2 * masks[0], q2 * masks[1]], axis=0), kcat)
                sc = sc + (bias0 if b == 0 else bias)
                m = jnp.max(sc, axis=1, keepdims=True)
                p = jnp.exp(sc - m)
                l = jnp.sum(p, axis=1, keepdims=True)
                oh = _dot(p.astype(BF16), vcat) / l
                lse = m + jnp.log(l)
                lt = pick(l_ref, l1, r1)[rq, :]
                if not first:
                    old = jnp.concatenate([_pick_col(lt, lane, 2 * hp + e) for e in range(2)], axis=0)
                    mx = jnp.maximum(old, lse)
                    new = mx + jnp.log(jnp.exp(old - mx) + jnp.exp(lse - mx))
                    o2 = halves(oh) * halves(jnp.exp(lse - new)) + pick(oa_ref, oa1, r1)[rq, :] * halves(jnp.exp(old - new))
                    lse = new
                else:
                    o2 = halves(oh)
                pick(o_ref, o1, r1)[rq, :] = o2
                pick(l_ref, l1, r1)[rq, :] = put(lt, lse)
                if last:
                    zz = pick(z_ref, z1, r1)[rq, :]
                    pick(gf, g1, r1)[rq, :] = o2 * (zz * _sigmoid(zz))
        if pre:
            _ungroup(o1, o_ref, d1)
            _ungroup(l1, l_ref, d1)
            if last:
                _ungroup(g1, gf, d1)
        if last:
            gb_ref[...] = gf[...].astype(BF16)

    col_blk = lambda n, hp: (n, g * N_PAIRS + hp)
    halo_blk = lambda n, hp: (jnp.maximum(n * nq - 1, 0), g * N_PAIRS + hp)
    own = pl.BlockSpec((tile, BLK), lambda n, hp: (n, hp))
    own_l = pl.BlockSpec((tile, BLK), lambda n, hp: (n, 0))
    in_specs = [pl.BlockSpec((tile, BLK), col_blk),
                pl.BlockSpec((tile, BLK), col_blk), pl.BlockSpec((halo, BLK), halo_blk),
                pl.BlockSpec((tile, BLK), col_blk), pl.BlockSpec((halo, BLK), halo_blk),
                pl.BlockSpec((BLK, 2 * BLK), lambda n, hp: (0, 0)),
                pl.BlockSpec((1, 8, 2 * BLK), lambda n, hp: (hp, 0, 0))]
    args = [q, k_all, k_all, v_all, v_all, _band_table(d), _slope_table()]
    if not first:
        in_specs += [own, own_l]
        args += [o_acc, lse_acc]
    if last:
        in_specs += [own]
        args += [z]
    out_specs = [own, own_l] + ([own] if last else [])
    out_shape = [_sds((s, D), F32), _sds((s, BLK), F32)] + ([_sds((s, D), BF16)] if last else [])
    t_shape, w_shape = (tile, BLK), (tile + halo, BLK)
    scratch = [pltpu.VMEM(t_shape, F32), pltpu.VMEM(w_shape, F32), pltpu.VMEM(w_shape, F32)]
    if last:
        scratch.append(pltpu.VMEM(t_shape, F32))
    if pre:
        scratch += [_grouped(t_shape, d1), _grouped(w_shape, d1), _grouped(w_shape, d1)]
        scratch += [_grouped(t_shape, d1)] * (2 + (0 if first else 1) + (2 if last else 0))
    return pl.pallas_call(
        body, name=f"attn_fwd_g{g}", grid=(s // tile, N_PAIRS), in_specs=in_specs, out_specs=out_specs,
        out_shape=out_shape, scratch_shapes=scratch, compiler_params=_params(56, 2),
    )(*args)


def _attn_backward(q, k_all, v_all, do, lse, dd, dhb, dk_all, dv_all, g):
    d = DILATIONS[g]
    s = q.shape[0]
    nq = ATTN_NQ[d]
    d1 = ATTN_PRE[d]
    d2 = d // d1
    halo = BLK * d
    tile = nq * halo
    nt = s // tile
    first = dk_all is None
    pre = d1 > 1

    def body(*refs):
        refs = list(refs)
        q_ref, k_ref, kh_ref, v_ref, vh_ref, nd_ref, sl_ref, do_ref, l_ref, dd_ref = refs[:10]
        del refs[:10 + (1 if first else 3)]
        dq_ref, dk_ref, dv_ref = refs[:3]
        qf, dof, kf, vf, dqf, dkf, dvf, ck, cv = refs[3:12]
        del refs[:12]
        if pre:
            q1, do1, k1, v1, l1, dd1, dq1, dk1, dv1 = refs
        else:
            q1 = do1 = k1 = v1 = l1 = dd1 = dq1 = None
            dk1, dv1 = dkf, dvf
        hp = pl.program_id(0)
        n = pl.program_id(1)

        @pl.when(n == 0)
        def _():
            ck[...] = jnp.zeros(ck.shape, F32)
            cv[...] = jnp.zeros(cv.shape, F32)

        dk1[...] = jnp.zeros(dk1.shape, F32)
        dv1[...] = jnp.zeros(dv1.shape, F32)

        def pick(nat, grp, r1):
            return grp.at[r1] if pre else nat

        @pl.when(n < nt)
        def _():
            if d > 1:
                qf[...] = q_ref[...].astype(F32)
                dof[...] = do_ref[...].astype(F32)
                kf[0:halo, :] = kh_ref[...].astype(F32)
                kf[halo:, :] = k_ref[...].astype(F32)
                vf[0:halo, :] = vh_ref[...].astype(F32)
                vf[halo:, :] = v_ref[...].astype(F32)
            if pre:
                for src, dst in ((qf, q1), (dof, do1), (kf, k1), (vf, v1), (l_ref, l1), (dd_ref, dd1)):
                    _regroup(src, dst, d1)
            col = lax.broadcasted_iota(jnp.int32, (2 * BLK, 2 * BLK), 1)
            lane = lax.broadcasted_iota(jnp.int32, (BLK, BLK), 1)
            masks = _head_masks()
            bias = jnp.concatenate([sl_ref[0, e:e + 1, :] * nd_ref[...] for e in range(2)], axis=0)
            bias0 = bias + jnp.where((n == 0) & (col < BLK), NEG, 0.0)
            for b in range(nq):
                for r in range(d):
                    r1, r2 = r % d1, r // d1
                    rq = _rows(b * (halo // d1) + r2, BLK, d2)
                    rk = _rows(b * (halo // d1) + r2, 2 * BLK, d2)
                    if d > 1:
                        q2 = pick(qf, q1, r1)[rq, :].astype(BF16)
                        do2 = pick(dof, do1, r1)[rq, :].astype(BF16)
                        kcat = pick(kf, k1, r1)[rk, :].astype(BF16)
                        vcat = pick(vf, v1, r1)[rk, :].astype(BF16)
                    else:
                        q2 = q_ref[rq, :]
                        do2 = do_ref[rq, :]
                        kcat = _with_halo(kh_ref, k_ref, b)
                        vcat = _with_halo(vh_ref, v_ref, b)
                    lt = pick(l_ref, l1, r1)[rq, :]
                    dt = pick(dd_ref, dd1, r1)[rq, :]
                    qs = jnp.concatenate([q2 * masks[0], q2 * masks[1]], axis=0)
                    dos = jnp.concatenate([do2 * masks[0], do2 * masks[1]], axis=0)
                    lcol = jnp.concatenate([_pick_col(lt, lane, 2 * hp + e) for e in range(2)], axis=0)
                    dcol = jnp.concatenate([_pick_col(dt, lane, 2 * hp + e) for e in range(2)], axis=0)
                    sc = _dot_nt(qs, kcat) + (bias0 if b == 0 else bias)
                    p = jnp.exp(sc - lcol)
                    ds = p * (_dot_nt(dos, vcat) - dcol)
                    dsb = ds.astype(BF16)
                    dq = _dot(dsb, kcat)
                    dq2 = (HEAD_DIM ** -0.5) * jnp.where(lane < HEAD_DIM, dq[0:BLK], dq[BLK:])
                    pick(dqf, dq1, r1)[rq, :] = dq2
                    pick(dkf, dk1, r1)[rk, :] += _dot(dsb.T, qs)
                    pick(dvf, dv1, r1)[rk, :] += _dot(p.astype(BF16).T, dos)
            if pre:
                _ungroup(dq1, dqf, d1)
            dq_ref[...] = dqf[...].astype(BF16)

        if pre:
            _ungroup(dk1, dkf, d1)
            _ungroup(dv1, dvf, d1)
        if tile > halo:
            dk_ref[0:tile - halo, :] = ck[0:tile - halo, :].astype(BF16)
            dv_ref[0:tile - halo, :] = cv[0:tile - halo, :].astype(BF16)
        dk_ref[tile - halo:, :] = (ck[tile - halo:, :] + dkf[0:halo, :]).astype(BF16)
        dv_ref[tile - halo:, :] = (cv[tile - halo:, :] + dvf[0:halo, :]).astype(BF16)
        ck[...] = dkf[halo:, :]
        cv[...] = dvf[halo:, :]

    cur = lambda n: jnp.minimum(n, nt - 1)
    col_blk = lambda hp, n: (cur(n), g * N_PAIRS + hp)
    halo_blk = lambda hp, n: (jnp.maximum(cur(n) * nq - 1, 0), g * N_PAIRS + hp)
    out_kv = lambda hp, n: (jnp.maximum(n - 1, 0), g * N_PAIRS + hp)
    small = pl.BlockSpec((tile, BLK), lambda hp, n: (cur(n), 0))
    hbm = pl.BlockSpec(memory_space=pl.ANY)
    in_specs = [pl.BlockSpec((tile, BLK), col_blk),
                pl.BlockSpec((tile, BLK), col_blk), pl.BlockSpec((halo, BLK), halo_blk),
                pl.BlockSpec((tile, BLK), col_blk), pl.BlockSpec((halo, BLK), halo_blk),
                pl.BlockSpec((BLK, 2 * BLK), lambda hp, n: (0, 0)),
                pl.BlockSpec((1, 8, 2 * BLK), lambda hp, n: (hp, 0, 0)),
                pl.BlockSpec((tile, BLK), lambda hp, n: (cur(n), hp)), small, small, hbm]
    args = [q, k_all, k_all, v_all, v_all, _band_table(d), _slope_table(), do, lse, dd, dhb]
    aliases = {10: 0}
    if not first:
        in_specs += [hbm, hbm]
        args += [dk_all, dv_all]
        aliases.update({11: 1, 12: 2})
    t_shape, w_shape = (tile, BLK), (tile + halo, BLK)
    tile_f32, wide_f32 = pltpu.VMEM(t_shape, F32), pltpu.VMEM(w_shape, F32)
    scratch = [tile_f32, tile_f32, wide_f32, wide_f32, tile_f32, wide_f32, wide_f32, tile_f32, tile_f32]
    if pre:
        tg, wg = _grouped(t_shape, d1), _grouped(w_shape, d1)
        scratch += [tg, tg, wg, wg, tg, tg, tg, wg, wg]
    return pl.pallas_call(
        body, name=f"attn_bwd_g{g}", grid=(N_PAIRS, nt + 1), in_specs=in_specs,
        out_specs=[pl.BlockSpec((tile, BLK), col_blk), pl.BlockSpec((tile, BLK), out_kv),
                   pl.BlockSpec((tile, BLK), out_kv)],
        out_shape=[_sds((s, 4 * D), BF16), _sds((s, 3 * D), BF16), _sds((s, 3 * D), BF16)],
        scratch_shapes=scratch, input_output_aliases=aliases, compiler_params=_params(48, 2),
    )(*args)


def _head_spread():
    spread = (np.arange(BLK)[:, None] == np.arange(D)[None, :] // HEAD_DIM).astype(np.float32)
    return jnp.asarray(spread, dtype=BF16)


def _attn_merge(os_, lses, z, tm):
    s = z.shape[0]

    def body(o0_ref, o1_ref, o2_ref, l0_ref, l1_ref, l2_ref, z_ref, e_ref, o_ref, l_ref, gb_ref):
        ls = [l0_ref[...], l1_ref[...], l2_ref[...]]
        m = jnp.maximum(jnp.maximum(ls[0], ls[1]), ls[2])
        lse = m + jnp.log(jnp.exp(ls[0] - m) + jnp.exp(ls[1] - m) + jnp.exp(ls[2] - m))
        l_ref[...] = lse
        o = jnp.zeros((tm, D), F32)
        for l_g, og_ref in zip(ls, (o0_ref, o1_ref, o2_ref)):
            wg = jnp.exp(l_g - lse)
            hi = wg.astype(BF16)
            lo = (wg - hi.astype(F32)).astype(BF16)
            o = o + (_dot(hi, e_ref[...]) + _dot(lo, e_ref[...])) * og_ref[...]
        o_ref[...] = o
        zz = z_ref[...]
        gb_ref[...] = (o * (zz * _sigmoid(zz))).astype(BF16)

    row = lambda i: (i, 0)
    wide = pl.BlockSpec((tm, D), row)
    stat = pl.BlockSpec((tm, BLK), row)
    return pl.pallas_call(
        body, name="attn_merge", grid=(s // tm,),
        in_specs=[wide, wide, wide, stat, stat, stat, wide, pl.BlockSpec((BLK, D), lambda i: (0, 0))],
        out_specs=[wide, stat, wide],
        out_shape=[_sds((s, D), F32), _sds((s, BLK), F32), _sds((s, D), BF16)],
        compiler_params=_params(40, 1),
    )(*os_, *lses, z, _head_spread())


def _b_out_loss(gb, w, b, n1, pg0, pb0, pg1, pb1, tgt, tm):
    s = gb.shape[0]
    last = s // tm - 1

    def body(gb_ref, w_ref, b_ref, n1_ref, pg0_ref, pb0_ref, pg1_ref, pb1_ref, t_ref,
             dr_ref, drb_ref, loss_ref, dpg_ref, dpb_ref, dbo_ref):
        i = pl.program_id(0)
        _acc_init(i, loss_ref, dpg_ref, dpb_ref, dbo_ref)
        x1 = n1_ref[...] * pg0_ref[...] + pb0_ref[...]
        r = ALPHA * x1 + (_dot(gb_ref[...], w_ref[...]) + b_ref[...])
        n, rstd = _ln_stats(r)
        err = (n * pg1_ref[...] + pb1_ref[...]) - t_ref[...]
        loss_ref[...] += _rowsum8(err * err)
        dx2 = err * (1.0 / D)
        dpg_ref[...] += _rowsum8(dx2 * n)
        dpb_ref[...] += _rowsum8(dx2)
        dr = _ln_bwd(dx2 * pg1_ref[...], n, rstd)
        dr_ref[...] = dr
        drb_ref[...] = dr.astype(BF16)
        dbo_ref[...] += _rowsum8(dr)
        _acc_finish(i, last, dpg_ref, dpb_ref, dbo_ref)

        @pl.when(i == last)
        def _():
            loss_ref[...] = jnp.broadcast_to((0.5 / D) * jnp.sum(loss_ref[...], keepdims=True), loss_ref.shape)

    row = lambda i: (i, 0)
    vec = pl.BlockSpec((1, D), lambda i: (0, 0))
    acc = pl.BlockSpec((8, D), lambda i: (0, 0))
    return pl.pallas_call(
        body, name="b_out_loss", grid=(s // tm,),
        in_specs=[pl.BlockSpec((tm, D), row), pl.BlockSpec((D, D), lambda i: (0, 0)), vec,
                  pl.BlockSpec((tm, D), row), vec, vec, vec, vec, pl.BlockSpec((tm, D), row)],
        out_specs=[pl.BlockSpec((tm, D), row), pl.BlockSpec((tm, D), row), acc, acc, acc, acc],
        out_shape=[_sds((s, D), F32), _sds((s, D), BF16)] + [_sds((8, D), F32)] * 4,
        compiler_params=_params(36, 1),
    )(gb, w, b, n1, pg0, pb0, pg1, pb1, tgt)


def _head_selector():
    sel = (np.arange(D)[:, None] // HEAD_DIM == np.arange(BLK)[None, :]).astype(np.float32)
    return jnp.asarray(sel, dtype=BF16)


def _b_out_bwd(drb, w, z, o, tm):
    s = drb.shape[0]

    def body(dr_ref, w_ref, z_ref, o_ref, sel_ref, do_ref, dh_ref, dd_ref):
        dg = _dot_nt(dr_ref[...], w_ref[...])
        zz = z_ref[...]
        sg = _sigmoid(zz)
        do = dg * (zz * sg)
        ov = o_ref[...]
        do_ref[...] = do.astype(BF16)
        dh_ref[...] = (dg * ov * _dsilu(zz, sg)).astype(BF16)
        prod = do * ov
        hi = prod.astype(BF16)
        lo = (prod - hi.astype(F32)).astype(BF16)
        dd_ref[...] = _dot(hi, sel_ref[...]) + _dot(lo, sel_ref[...])

    row = lambda i: (i, 0)
    return pl.pallas_call(
        body, name="b_out_bwd", grid=(s // tm,),
        in_specs=[pl.BlockSpec((tm, D), row), pl.BlockSpec((D, D), lambda i: (0, 0)),
                  pl.BlockSpec((tm, D), row), pl.BlockSpec((tm, D), row), pl.BlockSpec((D, BLK), lambda i: (0, 0))],
        out_specs=[pl.BlockSpec((tm, D), row), pl.BlockSpec((tm, D), lambda i: (i, 3)),
                   pl.BlockSpec((tm, BLK), row)],
        out_shape=[_sds((s, D), BF16), _sds((s, 4 * D), BF16), _sds((s, BLK), F32)],
        compiler_params=_params(36, 1),
    )(drb, w, z, o, _head_selector())


def _attn_bwd(q, k_all, v_all, do, lse, dd, dhb, dk_all, dv_all, g):
    d = DILATIONS[g]
    s = q.shape[0]
    nq = ATTN_NQ[d]
    halo = BLK * d
    tile = nq * halo
    nt = s // tile
    first = dk_all is None

    def body(*refs):
        q_ref, k_ref, kh_ref, v_ref, vh_ref, nd_ref, sl_ref, do_ref, l_ref, dd_ref = refs[:10]
        k = 10 + (1 if first else 3)
        dq_ref, dk_ref, dv_ref = refs[k:k + 3]
        qf, dof, kf, vf, dqf, dkf, dvf, ck, cv = refs[k + 3:k + 12]
        hp = pl.program_id(0)
        n = pl.program_id(1)

        @pl.when(n == 0)
        def _():
            ck[...] = jnp.zeros(ck.shape, F32)
            cv[...] = jnp.zeros(cv.shape, F32)

        dkf[...] = jnp.zeros(dkf.shape, F32)
        dvf[...] = jnp.zeros(dvf.shape, F32)

        @pl.when(n < nt)
        def _():
            qf[...] = q_ref[...].astype(F32)
            dof[...] = do_ref[...].astype(F32)
            kf[0:halo, :] = kh_ref[...].astype(F32)
            kf[halo:, :] = k_ref[...].astype(F32)
            vf[0:halo, :] = vh_ref[...].astype(F32)
            vf[halo:, :] = v_ref[...].astype(F32)
            col = lax.broadcasted_iota(jnp.int32, (2 * BLK, 2 * BLK), 1)
            lane = lax.broadcasted_iota(jnp.int32, (BLK, BLK), 1)
            masks = _head_masks()
            bias = jnp.concatenate([sl_ref[0, e:e + 1, :] * nd_ref[...] for e in range(2)], axis=0)
            bias0 = bias + jnp.where((n == 0) & (col < BLK), NEG, 0.0)
            for b in range(nq):
                for r in range(d):
                    rq = _rows(b * halo + r, BLK, d)
                    rk = _rows(b * halo + r, 2 * BLK, d)
                    q2 = qf[rq, :].astype(BF16)
                    do2 = dof[rq, :].astype(BF16)
                    kcat = kf[rk, :].astype(BF16)
                    vcat = vf[rk, :].astype(BF16)
                    lt = l_ref[rq, :]
                    dt = dd_ref[rq, :]
                    qs = jnp.concatenate([q2 * masks[0], q2 * masks[1]], axis=0)
                    dos = jnp.concatenate([do2 * masks[0], do2 * masks[1]], axis=0)
                    lcol = jnp.concatenate([_pick_col(lt, lane, 2 * hp + e) for e in range(2)], axis=0)
                    dcol = jnp.concatenate([_pick_col(dt, lane, 2 * hp + e) for e in range(2)], axis=0)
                    sc = _dot_nt(qs, kcat) + (bias0 if b == 0 else bias)
                    p = jnp.exp(sc - lcol)
                    ds = p * (_dot_nt(dos, vcat) - dcol)
                    dsb = ds.astype(BF16)
                    dq = _dot(dsb, kcat)
                    dqf[rq, :] = (HEAD_DIM ** -0.5) * jnp.where(lane < HEAD_DIM, dq[0:BLK], dq[BLK:])
                    dkf[rk, :] += _dot(dsb.T, qs)
                    dvf[rk, :] += _dot(p.astype(BF16).T, dos)
            dq_ref[...] = dqf[...].astype(BF16)

        if tile > halo:
            dk_ref[0:tile - halo, :] = ck[0:tile - halo, :].astype(BF16)
            dv_ref[0:tile - halo, :] = cv[0:tile - halo, :].astype(BF16)
        dk_ref[tile - halo:, :] = (ck[tile - halo:, :] + dkf[0:halo, :]).astype(BF16)
        dv_ref[tile - halo:, :] = (cv[tile - halo:, :] + dvf[0:halo, :]).astype(BF16)
        ck[...] = dkf[halo:, :]
        cv[...] = dvf[halo:, :]

    cur = lambda n: jnp.minimum(n, nt - 1)
    col_blk = lambda hp, n: (cur(n), g * N_PAIRS + hp)
    halo_blk = lambda hp, n: (jnp.maximum(cur(n) * nq - 1, 0), g * N_PAIRS + hp)
    out_kv = lambda hp, n: (jnp.maximum(n - 1, 0), g * N_PAIRS + hp)
    small = pl.BlockSpec((tile, BLK), lambda hp, n: (cur(n), 0))
    hbm = pl.BlockSpec(memory_space=pl.ANY)
    in_specs = [pl.BlockSpec((tile, BLK), col_blk),
                pl.BlockSpec((tile, BLK), col_blk), pl.BlockSpec((halo, BLK), halo_blk),
                pl.BlockSpec((tile, BLK), col_blk), pl.BlockSpec((halo, BLK), halo_blk),
                pl.BlockSpec((BLK, 2 * BLK), lambda hp, n: (0, 0)),
                pl.BlockSpec((1, 8, 2 * BLK), lambda hp, n: (hp, 0, 0)),
                pl.BlockSpec((tile, BLK), lambda hp, n: (cur(n), hp)), small, small, hbm]
    args = [q, k_all, k_all, v_all, v_all, _band_table(d), _slope_table(), do, lse, dd, dhb]
    aliases = {10: 0}
    if not first:
        in_specs += [hbm, hbm]
        args += [dk_all, dv_all]
        aliases.update({11: 1, 12: 2})
    tile_f32 = pltpu.VMEM((tile, BLK), F32)
    wide_f32 = pltpu.VMEM((tile + halo, BLK), F32)
    return pl.pallas_call(
        body, name=f"attn_bwd_g{g}", grid=(N_PAIRS, nt + 1), in_specs=in_specs,
        out_specs=[pl.BlockSpec((tile, BLK), col_blk), pl.BlockSpec((tile, BLK), out_kv),
                   pl.BlockSpec((tile, BLK), out_kv)],
        out_shape=[_sds((s, 4 * D), BF16), _sds((s, 3 * D), BF16), _sds((s, 3 * D), BF16)],
        scratch_shapes=[tile_f32, tile_f32, wide_f32, wide_f32, tile_f32, wide_f32, wide_f32, tile_f32, tile_f32],
        input_output_aliases=aliases, compiler_params=_params(40, 2),
    )(*args)


def _b_in_bwd(dr2, dhb, dk_all, dv_all, wb_g, wkv_g, n1, rstd1, pg0, tm):
    s = dr2.shape[0]
    last = s // tm - 1
    nb_, nkv = wb_g.shape[2], wkv_g.shape[2]
    half = N_DEV // 2

    def body(dr2_ref, dh_ref, dk_ref, dv_ref, wb_hbm, wkv_hbm, n_ref, rstd_ref, pg_ref,
             dr_ref, drb_ref, dpg_ref, dpb_ref, dbo_ref, wb, wkv):
        i = pl.program_id(0)

        @pl.when(i == 0)
        def _():
            pltpu.sync_copy(wb_hbm, wb)
            pltpu.sync_copy(wkv_hbm, wkv)

        _acc_init(i, dpg_ref, dpb_ref, dbo_ref)
        acc = ALPHA * dr2_ref[...]
        for j in range(N_DEV):
            acc = acc + _dot_nt(dh_ref[:, nb_ * j:nb_ * (j + 1)], wb[j])
            src = dk_ref if j < half else dv_ref
            jj = j % half
            acc = acc + _dot_nt(src[:, nkv * jj:nkv * (jj + 1)], wkv[j])
        n = n_ref[...]
        dpg_ref[...] += _rowsum8(acc * n)
        dpb_ref[...] += _rowsum8(acc)
        dr = _ln_bwd(acc * pg_ref[...], n, rstd_ref[...])
        dr_ref[...] = dr
        drb_ref[...] = dr.astype(BF16)
        dbo_ref[...] += _rowsum8(dr)
        _acc_finish(i, last, dpg_ref, dpb_ref, dbo_ref)

    row = lambda i: (i, 0)
    hbm = pl.BlockSpec(memory_space=pl.ANY)
    acc_spec = pl.BlockSpec((8, D), lambda i: (0, 0))
    return pl.pallas_call(
        body, name="b_in_bwd", grid=(s // tm,),
        in_specs=[pl.BlockSpec((tm, D), row), pl.BlockSpec((tm, 4 * D), row), pl.BlockSpec((tm, 3 * D), row),
                  pl.BlockSpec((tm, 3 * D), row), hbm, hbm, pl.BlockSpec((tm, D), row), pl.BlockSpec((tm, 1), row),
                  pl.BlockSpec((1, D), lambda i: (0, 0))],
        out_specs=[pl.BlockSpec((tm, D), row), pl.BlockSpec((tm, D), row), acc_spec, acc_spec, acc_spec],
        out_shape=[_sds((s, D), F32), _sds((s, D), BF16)] + [_sds((8, D), F32)] * 3,
        scratch_shapes=[pltpu.VMEM(wb_g.shape, BF16), pltpu.VMEM(wkv_g.shape, BF16)],
        compiler_params=_params(56, 1),
    )(dr2, dhb, dk_all, dv_all, wb_g, wkv_g, n1, rstd1, pg0)


def _a_out_bwd(drb, w, u1, h, ln_g, ln_b, tm, plan=None):
    s = drb.shape[0]
    last = s // tm - 1

    def body(dr_ref, w_ref, u1_ref, z_ref, g_ref, b_ref, du1_ref, dh_ref, dg_ref, db_ref, dbz_ref):
        i = pl.program_id(0)
        _acc_init(i, dg_ref, db_ref, dbz_ref)
        dga = _dot_nt(dr_ref[...], w_ref[...])
        n, rstd = _ln_stats(u1_ref[...])
        pre = n * g_ref[...] + b_ref[...]
        sp = _sigmoid(pre)
        zz = z_ref[...]
        sz = _sigmoid(zz)
        dz = dga * (pre * sp) * _dsilu(zz, sz)
        dh_ref[...] = dz.astype(BF16)
        dbz_ref[...] += _rowsum8(dz)
        dpre = dga * (zz * sz) * _dsilu(pre, sp)
        dg_ref[...] += _rowsum8(dpre * n)
        db_ref[...] += _rowsum8(dpre)
        du1_ref[...] = _ln_bwd(dpre * g_ref[...], n, rstd)
        _acc_finish(i, last, dg_ref, db_ref, dbz_ref)

    row = lambda i: (i, 0)
    vec = pl.BlockSpec((1, D), lambda i: (0, 0))
    acc_spec = pl.BlockSpec((8, D), lambda i: (0, 0))
    return _planned_call(
        plan, (drb, w, u1, h, ln_g, ln_b), body, name="a_out_bwd", grid=(s // tm,),
        in_specs=[pl.BlockSpec((tm, D), row), pl.BlockSpec((D, D), lambda i: (0, 0)), pl.BlockSpec((tm, D), row),
                  pl.BlockSpec((tm, D), lambda i: (i, 2)), vec, vec],
        out_specs=[pl.BlockSpec((tm, D), row), pl.BlockSpec((tm, D), lambda i: (i, 2)),
                   acc_spec, acc_spec, acc_spec],
        out_shape=[_sds((s, D), F32), _sds((s, 3 * D), BF16)] + [_sds((8, D), F32)] * 3,
        compiler_params=_params(32, 1))


def _a_conv_bwd(du1, u0, h, dha, w_dw, tm, plan=None):
    s = du1.shape[0]
    steps = s // tm
    per = tm // CONV_HALO
    pad = CONV_W - 1

    def body(du_ref, dun_ref, u0_ref, u0p_ref, h_ref, w_ref, dha_hbm,
             dh_ref, dw_ref, dbdw_ref, dba_ref, dbg_ref, dbuf, ubuf, wacc, dshs, ushs, wb, du0_buf):
        i = pl.program_id(0)
        _spread_taps(i, w_ref, wb)
        _acc_init(i, dbdw_ref, dba_ref, dbg_ref, wacc)
        dbuf[0:tm, :] = du_ref[...]
        dbuf[tm:, :] = jnp.where(i < steps - 1, dun_ref[...], 0.0)
        ubuf[0:CONV_HALO, :] = jnp.where(i > 0, u0p_ref[...], 0.0)
        ubuf[CONV_HALO:, :] = u0_ref[...]
        _shifted_copies(dbuf, dshs, tm)
        _shifted_copies(ubuf, ushs, tm)
        off = CONV_HALO - pad
        grp = CONV_ROWS // SUBLANES
        d_windows = _tap_windows([pad - j for j in range(CONV_W)])
        u_windows = _tap_windows([off + j for j in range(CONV_W)])

        def chunk(it, carry):
            r0 = pl.multiple_of((it // (D // LANES)) * CONV_ROWS, CONV_ROWS)
            lanes = pl.ds(pl.multiple_of((it % (D // LANES)) * LANES, LANES), LANES)
            rows = pl.ds(r0, CONV_ROWS)
            acc = jnp.zeros((grp, SUBLANES, LANES), F32)
            for s, lo, span, taps in d_windows:
                win = _window(dbuf, dshs, s, r0, lo, span, lanes)
                for j, a in taps:
                    acc = acc + wb[j, :, lanes][None] * win[a:a + grp]
            du0_buf[rows, lanes] = acc.reshape(CONV_ROWS, LANES)
            du3 = du_ref[rows, lanes].reshape(grp, SUBLANES, LANES)
            for s, lo, span, taps in u_windows:
                win = _window(ubuf, ushs, s, r0, lo, span, lanes)
                for j, a in taps:
                    wacc[j, :, lanes] += (du3 * win[a:a + grp]).sum(axis=0)
            return carry

        lax.fori_loop(0, (tm // CONV_ROWS) * (D // LANES), chunk, 0)
        du0 = du0_buf[...]
        sg = _sigmoid(h_ref[:, D:2 * D])
        da = du0 * sg
        dag = du0 * h_ref[:, 0:D] * (sg * (1.0 - sg))
        dh_ref[:, 0:D] = da.astype(BF16)
        dh_ref[:, D:2 * D] = dag.astype(BF16)
        dbdw_ref[...] += _rowsum8(du_ref[...])
        dba_ref[...] += _rowsum8(da)
        dbg_ref[...] += _rowsum8(dag)
        _acc_finish(i, steps - 1, dbdw_ref, dba_ref, dbg_ref)

        @pl.when(i == steps - 1)
        def _():
            for j in range(CONV_W):
                dw_ref[j:j + 1, :] = jnp.sum(wacc[j], axis=0, keepdims=True)
            dw_ref[CONV_W:, :] = jnp.zeros((32 - CONV_W, D), F32)

    row = lambda i: (i, 0)
    acc_spec = pl.BlockSpec((8, D), lambda i: (0, 0))
    return _planned_call(
        plan, (du1, du1, u0, u0, h, w_dw, dha), body, name="a_conv_bwd", grid=(steps,),
        in_specs=[pl.BlockSpec((tm, D), row),
                  pl.BlockSpec((CONV_HALO, D), lambda i: (jnp.minimum((i + 1) * per, s // CONV_HALO - 1), 0)),
                  pl.BlockSpec((tm, D), row),
                  pl.BlockSpec((CONV_HALO, D), lambda i: (jnp.maximum(i * per - 1, 0), 0)),
                  pl.BlockSpec((tm, 2 * D), lambda i: (i, 0)),
                  pl.BlockSpec((32, D), lambda i: (0, 0)), pl.BlockSpec(memory_space=pl.ANY)],
        out_specs=[pl.BlockSpec((tm, 2 * D), lambda i: (i, 0)),
                   pl.BlockSpec((32, D), lambda i: (0, 0)), acc_spec, acc_spec, acc_spec],
        out_shape=[_sds((s, 3 * D), BF16), _sds((32, D), F32)] + [_sds((8, D), F32)] * 3,
        scratch_shapes=[pltpu.VMEM((tm + CONV_HALO, D), F32), pltpu.VMEM((tm + CONV_HALO, D), F32),
                        pltpu.VMEM((CONV_W, 8, D), F32),
                        pltpu.VMEM((SUBLANES - 1, tm + CONV_HALO, D), F32),
                        pltpu.VMEM((SUBLANES - 1, tm + CONV_HALO, D), F32),
                        pltpu.VMEM((CONV_W, SUBLANES, D), F32), pltpu.VMEM((tm, D), F32)],
        input_output_aliases={6: 0}, compiler_params=_params(56, 1))


def _a_in_bwd(name, dr1, dha, w_g, tm, half, plan=None, into=None):
    s = dr1.shape[0]
    npd = w_g.shape[2]
    steps = s // tm // 2
    first = half * steps

    def body(*refs):
        dr_ref, dh_ref, w_ref = refs[:3]
        o_ref = refs[-1]
        acc = ALPHA * dr_ref[...]
        for j in range(N_DEV):
            acc = acc + _dot_nt(dh_ref[:, npd * j:npd * (j + 1)], w_ref[j])
        o_ref[...] = acc

    row = lambda i: (first + i, 0)
    in_specs = [pl.BlockSpec((tm, D), row), pl.BlockSpec((tm, 3 * D), row),
                pl.BlockSpec(w_g.shape, lambda i: (0, 0, 0))]
    args = (dr1, dha, w_g)
    kw = {}
    if into is not None:
        in_specs.append(pl.BlockSpec(memory_space=pl.ANY))
        args += (into,)
        kw["input_output_aliases"] = {3: 0}
    return _planned_call(
        plan, args, body, name=name, grid=(steps,), in_specs=in_specs,
        out_specs=[pl.BlockSpec((tm, D), row)], out_shape=[_sds((s, D), F32)],
        compiler_params=_params(36, 1), **kw)


def _wgrad(name, a, b, npd, ts, total=None, at=0, into=None):
    s = a.shape[0]
    n_blk = b.shape[1] // npd
    total = n_blk if total is None else total
    assert at % n_blk == 0

    def body(*refs):
        a_ref, b_ref = refs[:2]
        o_ref = refs[-1]
        si = pl.program_id(0)

        @pl.when(si == 0)
        def _():
            o_ref[...] = jnp.zeros(o_ref.shape, F32)

        a_t = a_ref[...].T
        for j in range(n_blk):
            o_ref[j] += _dot(a_t, b_ref[:, npd * j:npd * (j + 1)])

    in_specs = [pl.BlockSpec((ts, D), lambda si: (si, 0)), pl.BlockSpec((ts, n_blk * npd), lambda si: (si, 0))]
    args = [a, b]
    aliases = {}
    if into is not None:
        in_specs.append(pl.BlockSpec(memory_space=pl.ANY))
        args.append(into)
        aliases = {2: 0}
    return pl.pallas_call(
        body, name=name, grid=(s // ts,), in_specs=in_specs,
        out_specs=pl.BlockSpec((n_blk, D, npd), lambda si: (at // n_blk, 0, 0)),
        out_shape=_sds((total, D, npd), F32), input_output_aliases=aliases,
        compiler_params=_params(56, 1),
    )(*args)


SMALL_ROWS = 40
GRAD_ROWS = 48


def kernel(x, a_w_in, a_b_in, a_w_dw, a_b_dw, a_ln_g, a_ln_b, a_w_out, a_b_out, kv_w, b_w_in, b_w_out, b_b_out, post_ln_g, post_ln_b, loss_target, m_a_w_in, m_a_b_in, m_a_w_dw, m_a_b_dw, m_a_ln_g, m_a_ln_b, m_a_w_out, m_a_b_out, m_kv_w, m_b_w_in, m_b_w_out, m_b_b_out, m_post_ln_g, m_post_ln_b, v_a_w_in, v_a_b_in, v_a_w_dw, v_a_b_dw, v_a_ln_g, v_a_ln_b, v_a_w_out, v_a_b_out, v_kv_w, v_b_w_in, v_b_w_out, v_b_b_out, v_post_ln_g, v_post_ln_b):
    s = x.shape[1]
    assert x.shape == (1, s, D) and s % (DILATIONS[-1] * BLK) == 0
    xs = x.reshape(s, D)
    tgt = loss_target.reshape(s, D)
    me = 4 * lax.axis_index("x") + 2 * lax.axis_index("y") + lax.axis_index("c")
    c_idx = lax.axis_index("c").astype(jnp.int32).reshape(1)

    def small_pack(b_in, w_dw, b_dw, ln_g, ln_b, b_out):
        rows = [b_in.reshape(3, BLK), w_dw.reshape(CONV_W, BLK), b_dw.reshape(1, BLK), ln_g.reshape(1, BLK),
                ln_b.reshape(1, BLK), b_out.reshape(1, BLK)]
        n = sum(r.shape[0] for r in rows)
        return jnp.concatenate(rows + [jnp.zeros((SMALL_ROWS - n, BLK), F32)], axis=0)

    wa_in, sm, *later = _all_gather(
        "gather_first", [a_w_in[0], small_pack(a_b_in, a_w_dw, a_b_dw, a_ln_g, a_ln_b, a_b_out)], [BF16, F32],
        casts=[a_w_out[0], kv_w, b_w_in[0], b_w_out[0]])
    ba_in = sm[:, 0:3, :].reshape(1, 3 * D)
    w_dw = jnp.concatenate([sm[:, 3:3 + CONV_W, :].transpose(1, 0, 2).reshape(CONV_W, D), jnp.zeros((1, D), F32)], axis=0)
    b_dw, ln_g, ln_b, ba_out = (sm[:, 34 + k, :].reshape(1, D) for k in range(4))
    pg0, pg1 = post_ln_g[0:1], post_ln_g[1:2]
    pb0, pb1 = post_ln_b[0:1], post_ln_b[1:2]

    h_a, u0, xb = _a_in_proj(xs, wa_in, ba_in, 512)
    (u1, g_a), (wa_out, wkv, wb_in, wb_out) = _a_conv_gate(u0, h_a, w_dw, b_dw, ln_g, ln_b, 256, _gather_plan(later))
    wa_out = wa_out.reshape(D, D)
    wb_out = wb_out.reshape(D, D)
    n1, rstd1, x1b = _a_out_proj(g_a, wa_out, ba_out, xs, pg0, pb0, 512)
    k_all, v_all = _kv_proj(x1b, wkv, 512)
    q, z_b = _b_in_proj(x1b, wb_in, 512)
    per_group = [_attn_forward(q, k_all, v_all, None, None, None, g, True, False) for g in range(3)]
    o, lse, g_b = _attn_merge([og for og, _ in per_group], [lg for _, lg in per_group], z_b, 512)
    dr2, dr2b, loss8, dpg1, dpb1, dbb_out = _b_out_loss(g_b, wb_out, b_b_out, n1, pg0, pb0, pg1, pb1, tgt, 512)

    do, dhb, dd = _b_out_bwd(dr2b, wb_out, z_b, o, 512)
    dk_all = dv_all = None
    for g in range(3):
        dhb, dk_all, dv_all = _attn_backward(q, k_all, v_all, do, lse, dd, dhb, dk_all, dv_all, g)
    dr1, dr1b, dpg0, dpb0, dba_out = _b_in_bwd(dr2, dhb, dk_all, dv_all, wb_in, wkv, n1, rstd1, pg0, 256)

    p_kv = _wgrad("wgrad_k", x1b, dk_all, 768, 1024, total=N_DEV)
    p_kv = _wgrad("wgrad_v", x1b, dv_all, 768, 1024, total=N_DEV, at=N_DEV // 2, into=p_kv)
    p_b_in = _wgrad("wgrad_b_in", x1b, dhb, 512, 512)
    p_a_out = _wgrad("wgrad_a_out", g_a, dr1b, D, 1024).reshape(N_DEV, BLK, D)
    p_b_out = _wgrad("wgrad_b_out", g_b, dr2b, D, 1024).reshape(N_DEV, BLK, D)
    parts = [p_kv, p_b_in, p_a_out, p_b_out]
    (du1, dha, dln_g, dln_b, dbz), from_sibling = _a_out_bwd(dr1b, wa_out, u1, h_a, ln_g, ln_b, 512, _sibling_plan(parts))
    chip_sums = [_pair_add(f"pair_add_{k}", p, r, c_idx) for k, (p, r) in enumerate(zip(parts, from_sibling))]
    (dha, dw_dw, db_dw, dba, dbg), from_chips = _a_conv_bwd(du1, u0, h_a, dha, w_dw, 256, _chips_plan(chip_sums))
    p_a_in = _wgrad("wgrad_a_in", xb, dha, 384, 1024)
    (from_sibling_a,) = _exchange_sibling("reduce_sibling_a_in", [p_a_in])
    sum_a = _pair_add("pair_add_a_in", p_a_in, from_sibling_a, c_idx)
    (grad_x,), (from_chips_a,) = _a_in_bwd("a_in_bwd_0", dr1, dha, wa_in, 512, 0, plan=_chips_plan([sum_a]))
    (grad_x,), _ = _a_in_bwd("a_in_bwd_1", dr1, dha, wa_in, 512, 1, into=grad_x)

    reduced = [from_chips_a] + from_chips
    big_w = [a_w_in[0], kv_w, b_w_in[0], a_w_out[0], b_w_out[0]]
    big_m = [m_a_w_in[0], m_kv_w, m_b_w_in[0], m_a_w_out[0], m_b_w_out[0]]
    big_v = [v_a_w_in[0], v_kv_w, v_b_w_in[0], v_a_w_out[0], v_b_w_out[0]]
    big = [_sum_adamw(f"adamw_{k}", reduced[k], big_w[k], big_m[k], big_v[k]) for k in range(5)]

    rows = [dba[0:1], dbg[0:1], dbz[0:1], dw_dw[0:CONV_W], db_dw[0:1], dln_g[0:1], dln_b[0:1], dba_out[0:1],
            dbb_out[0:1], dpg0[0:1], dpg1[0:1], dpb0[0:1], dpb1[0:1], loss8[0:1]]
    n_rows = sum(r.shape[0] for r in rows)
    gpack = jnp.concatenate(rows + [jnp.zeros((GRAD_ROWS - n_rows, D), F32)], axis=0)
    (gall,) = _all_gather("gather_small_grads", [gpack], [F32])
    w_small = [a_b_in, a_w_dw, a_b_dw, a_ln_g, a_ln_b, a_b_out, b_b_out, post_ln_g, post_ln_b]
    m_small = [m_a_b_in, m_a_w_dw, m_a_b_dw, m_a_ln_g, m_a_ln_b, m_a_b_out, m_b_b_out, m_post_ln_g, m_post_ln_b]
    v_small = [v_a_b_in, v_a_w_dw, v_a_b_dw, v_a_ln_g, v_a_ln_b, v_a_b_out, v_b_b_out, v_post_ln_g, v_post_ln_b]
    loss11, g_small, d_small, nm_small, nv_small = _small_finish(gall, me.astype(jnp.int32).reshape(1), w_small, m_small, v_small)
    loss = loss11[0, 0]

    def ordered(bigs, smalls):
        a_in, kvw, b_in, a_out, b_out = bigs
        return [a_in[None], smalls[0], smalls[1], smalls[2], smalls[3], smalls[4], a_out[None], smalls[5],
                kvw, b_in[None], b_out[None], smalls[6], smalls[7], smalls[8]]

    grads = ordered([b[0] for b in big], g_small)
    deltas = ordered([b[1] for b in big], d_small)
    new_m = ordered([b[2] for b in big], nm_small)
    new_v = ordered([b[3] for b in big], nv_small)
    return (loss, grad_x.reshape(1, s, D), *grads, *deltas, *new_m, *new_v)
```

```python
import math

import numpy as np
import jax
import jax.numpy as jnp
from jax import lax
from jax.experimental import pallas as pl
from jax.experimental.pallas import tpu as pltpu

F32 = jnp.float32
BF16 = jnp.bfloat16
MESH = pl.DeviceIdType.MESH

D = 1024
N_DEV = 8
HEAD_DIM = 64
N_HEADS = 16
DILATIONS = (1, 4, 16)
BLK = 128
CONV_W = 31
ALPHA = (2.0 * 2) ** 0.25
LN_EPS = 1e-5
SLOPES = tuple(2.0 ** (-8.0 * (h + 1) / N_HEADS) for h in range(N_HEADS))
NEG = -1e30

ADAM_LR = 0.001
ADAM_B1 = 0.9
ADAM_B2 = 0.999
ADAM_EPS = 1e-08
ADAM_WD = 0.01
ADAM_STEP = 10

VMEM_CAP_MB = 64


def _params(vmem_mb, n_grid=0):
    sem = ("arbitrary",) * n_grid if n_grid else None
    return pltpu.CompilerParams(dimension_semantics=sem, vmem_limit_bytes=min(vmem_mb, VMEM_CAP_MB - 6) * 2 ** 20)


def _sds(shape, dtype):
    return jax.ShapeDtypeStruct(tuple(shape), dtype)


def _sigmoid(v):
    return jax.nn.sigmoid(v)


def _dsilu(v, s):
    return s * (1.0 + v * (1.0 - s))


def _ln_stats(r):
    mu = jnp.mean(r, axis=-1, keepdims=True)
    xc = r - mu
    var = jnp.mean(xc * xc, axis=-1, keepdims=True)
    rstd = lax.rsqrt(var + LN_EPS)
    return xc * rstd, rstd


def _ln_bwd(dn, n, rstd):
    m1 = jnp.mean(dn, axis=-1, keepdims=True)
    m2 = jnp.mean(dn * n, axis=-1, keepdims=True)
    return rstd * (dn - m1 - n * m2)


def _rowsum8(v):
    tm, c = v.shape
    return v.reshape(tm // 8, 8, c).sum(axis=0)


def _acc_init(i, *refs):
    @pl.when(i == 0)
    def _():
        for r in refs:
            r[...] = jnp.zeros(r.shape, r.dtype)


def _acc_finish(i, last, *refs):
    @pl.when(i == last)
    def _():
        for r in refs:
            r[...] = jnp.broadcast_to(jnp.sum(r[...], axis=0, keepdims=True), r.shape)


def _dot(a, b):
    return jnp.dot(a, b, preferred_element_type=F32)


def _dot_nt(a, b):
    return lax.dot_general(a, b, (((1,), (1,)), ((), ())), preferred_element_type=F32)


def _place():
    return lax.axis_index("x"), lax.axis_index("y"), lax.axis_index("c")


def _all_gather(name, arrays, dtypes, casts=()):
    n = len(arrays)
    nc = len(casts)

    def body(*refs):
        ins, cast_ins = refs[:n], refs[n:n + nc]
        outs, cast_outs = refs[n + nc:2 * n + nc], refs[2 * n + nc:2 * (n + nc)]
        stages = refs[2 * (n + nc):3 * n + 2 * nc]
        send_sems, recv_sems, local_sems = refs[3 * n + 2 * nc:]
        x, y, c = _place()
        me, sibling = (x, y, c), (x, y, 1 - c)
        chips = [(1 - x, y), (x, 1 - y), (1 - x, 1 - y)]

        def slot(ref, p):
            return ref.at[4 * p[0] + 2 * p[1] + p[2]]

        def copy(a, k, block, to, src=None):
            return pltpu.make_async_remote_copy(
                src_ref=slot(outs[a], block) if src is None else src, dst_ref=slot(outs[a], block),
                send_sem=send_sems.at[a, k], recv_sem=recv_sems.at[a, k], device_id=to, device_id_type=MESH)

        first, mine = [], []
        for a in range(n):
            stages[a][...] = ins[a][...].astype(stages[a].dtype)
            cp = pltpu.make_async_copy(stages[a], slot(outs[a], me), local_sems.at[a])
            cp.start()
            mine.append(cp)
            first.append(copy(a, 0, me, sibling, src=stages[a]))
            first += [copy(a, 1 + j, me, (*chip, c), src=stages[a]) for j, chip in enumerate(chips)]
        for cp in first:
            cp.start()
        for src, dst in zip(cast_ins, cast_outs):
            dst[...] = src[...].astype(BF16)
        passed = []
        for j, chip in enumerate(chips):
            for a in range(n):
                copy(a, 1 + j, (*chip, c), me).wait_recv()
                cp = copy(a, 4 + j, (*chip, c), sibling)
                cp.start()
                passed.append(cp)
        for a in range(n):
            copy(a, 0, sibling, me).wait_recv()
            for j, chip in enumerate(chips):
                copy(a, 4 + j, (*chip, 1 - c), me).wait_recv()
        for cp in first + passed:
            cp.wait_send()
        for cp in mine:
            cp.wait()

    vmem_bytes = sum(math.prod(a.shape) * (jnp.dtype(a.dtype).itemsize + jnp.dtype(dt).itemsize)
                     for a, dt in zip(arrays, dtypes)) + sum(math.prod(a.shape) * 6 for a in casts)
    vm = pl.BlockSpec(memory_space=pltpu.VMEM)
    return pl.pallas_call(
        body, name=name,
        out_shape=[_sds((N_DEV,) + a.shape, dt) for a, dt in zip(arrays, dtypes)] + [_sds(a.shape, BF16) for a in casts],
        in_specs=[vm] * (n + nc),
        out_specs=[pl.BlockSpec(memory_space=pl.ANY)] * n + [vm] * nc,
        scratch_shapes=[pltpu.VMEM(a.shape, dt) for a, dt in zip(arrays, dtypes)]
        + [pltpu.SemaphoreType.DMA((n, 7)), pltpu.SemaphoreType.DMA((n, 7)), pltpu.SemaphoreType.DMA((n,))],
        compiler_params=_params(vmem_bytes // 2 ** 20 + 8),
    )(*arrays, *casts)


class _Plan:
    def __init__(self, args, out_shape, scratch, start, mid, finish):
        self.args, self.out_shape, self.scratch = list(args), list(out_shape), list(scratch)
        self.start, self.mid, self.finish = start, mid, finish


def _gather_plan(shards):
    n = len(shards)

    def copies(ins, outs, sems):
        send_sems, recv_sems, local_sems = sems
        x, y, c = _place()
        me, sibling = (x, y, c), (x, y, 1 - c)
        chips = [(1 - x, y), (x, 1 - y), (1 - x, 1 - y)]

        def slot(ref, p):
            return ref.at[4 * p[0] + 2 * p[1] + p[2]]

        def copy(a, k, block, to, src=None):
            return pltpu.make_async_remote_copy(
                src_ref=slot(outs[a], block) if src is None else src, dst_ref=slot(outs[a], block),
                send_sem=send_sems.at[a, k], recv_sem=recv_sems.at[a, k], device_id=to, device_id_type=MESH)

        mine = [pltpu.make_async_copy(ins[a], slot(outs[a], me), local_sems.at[a]) for a in range(n)]
        first = [copy(a, 0, me, sibling, src=ins[a]) for a in range(n)]
        first += [copy(a, 1 + j, me, (*chip, c), src=ins[a]) for a in range(n) for j, chip in enumerate(chips)]
        arrive = [copy(a, 1 + j, (*chip, c), me) for j, chip in enumerate(chips) for a in range(n)]
        passed = [copy(a, 4 + j, (*chip, c), sibling) for j, chip in enumerate(chips) for a in range(n)]
        from_sibling = [copy(a, 0, sibling, me) for a in range(n)]
        from_sibling += [copy(a, 4 + j, (*chip, 1 - c), me) for a in range(n) for j, chip in enumerate(chips)]
        return mine, first, arrive, passed, from_sibling

    def start(ins, outs, sems):
        mine, first, _, _, _ = copies(ins, outs, sems)
        for cp in mine + first:
            cp.start()

    def mid(ins, outs, sems):
        _, _, arrive, passed, _ = copies(ins, outs, sems)
        for got, on in zip(arrive, passed):
            got.wait_recv()
            on.start()

    def finish(ins, outs, sems):
        mine, first, _, passed, from_sibling = copies(ins, outs, sems)
        for cp in from_sibling:
            cp.wait_recv()
        for cp in first + passed:
            cp.wait_send()
        for cp in mine:
            cp.wait()

    return _Plan(shards, [_sds((N_DEV,) + a.shape, a.dtype) for a in shards],
                 [pltpu.SemaphoreType.DMA((n, 7)), pltpu.SemaphoreType.DMA((n, 7)), pltpu.SemaphoreType.DMA((n,))],
                 start, mid, finish)


def _sibling_plan(parts):
    n = len(parts)

    def copies(ins, outs, sems):
        send_sems, recv_sems = sems
        x, y, c = _place()
        return [pltpu.make_async_remote_copy(
            src_ref=ins[a].at[2 * p + 1 - c], dst_ref=outs[a].at[p], send_sem=send_sems.at[a, p],
            recv_sem=recv_sems.at[a, p], device_id=(x, y, 1 - c), device_id_type=MESH)
            for a in range(n) for p in range(4)]

    def start(ins, outs, sems):
        for cp in copies(ins, outs, sems):
            cp.start()

    def finish(ins, outs, sems):
        cps = copies(ins, outs, sems)
        for cp in cps:
            cp.wait_recv()
        for cp in cps:
            cp.wait_send()

    return _Plan(parts, [_sds((4,) + p.shape[1:], p.dtype) for p in parts],
                 [pltpu.SemaphoreType.DMA((n, 4)), pltpu.SemaphoreType.DMA((n, 4))], start, None, finish)


def _chips_plan(sums):
    n = len(sums)

    def copies(ins, outs, sems):
        send_sems, recv_sems, local_sems = sems
        x, y, c = _place()
        my_chip = 2 * x + y
        chips = [(1 - x, y), (x, 1 - y), (1 - x, 1 - y)]
        mine = [pltpu.make_async_copy(ins[a].at[my_chip], outs[a].at[my_chip], local_sems.at[a]) for a in range(n)]
        remote = [pltpu.make_async_remote_copy(
            src_ref=ins[a].at[2 * px + py], dst_ref=outs[a].at[my_chip], send_sem=send_sems.at[a, k],
            recv_sem=recv_sems.at[a, k], device_id=(px, py, c), device_id_type=MESH)
            for a in range(n) for k, (px, py) in enumerate(chips)]
        return mine, remote

    def start(ins, outs, sems):
        mine, remote = copies(ins, outs, sems)
        for cp in mine + remote:
            cp.start()

    def finish(ins, outs, sems):
        mine, remote = copies(ins, outs, sems)
        for cp in remote:
            cp.wait_recv()
        for cp in remote:
            cp.wait_send()
        for cp in mine:
            cp.wait()

    return _Plan(sums, [_sds(s.shape, s.dtype) for s in sums],
                 [pltpu.SemaphoreType.DMA((n, 3)), pltpu.SemaphoreType.DMA((n, 3)), pltpu.SemaphoreType.DMA((n,))],
                 start, None, finish)


def _planned_call(plan, args, body, *, name, grid, in_specs, out_specs, out_shape, scratch_shapes=(), mid_step=None,
                  **kw):
    in_specs, out_specs, out_shape = list(in_specs), list(out_specs), list(out_shape)
    scratch_shapes = list(scratch_shapes)
    if plan is None:
        res = pl.pallas_call(body, name=name, grid=grid, in_specs=in_specs, out_specs=out_specs, out_shape=out_shape,
                             scratch_shapes=scratch_shapes, **kw)(*args)
        return list(res), []
    n_in, n_out, n_scr = len(in_specs), len(out_specs), len(scratch_shapes)
    p_in, p_out = len(plan.args), len(plan.out_shape)
    steps = grid[0]

    def fused(*refs):
        ins, pins = refs[:n_in], refs[n_in:n_in + p_in]
        o0 = n_in + p_in
        outs, pouts = refs[o0:o0 + n_out], refs[o0 + n_out:o0 + n_out + p_out]
        s0 = o0 + n_out + p_out
        scr, pscr = refs[s0:s0 + n_scr], refs[s0 + n_scr:]
        i = pl.program_id(0)

        @pl.when(i == 0)
        def _():
            plan.start(pins, pouts, pscr)

        body(*ins, *outs, *scr)
        if plan.mid is not None:
            @pl.when(i == mid_step)
            def _():
                plan.mid(pins, pouts, pscr)

        @pl.when(i == steps - 1)
        def _():
            plan.finish(pins, pouts, pscr)

    hbm = pl.BlockSpec(memory_space=pl.ANY)
    res = pl.pallas_call(
        fused, name=name, grid=grid, in_specs=in_specs + [hbm] * p_in, out_specs=out_specs + [hbm] * p_out,
        out_shape=out_shape + plan.out_shape, scratch_shapes=scratch_shapes + plan.scratch, **kw)(*args, *plan.args)
    return list(res[:n_out]), list(res[n_out:])


def _exchange_sibling(name, parts):
    n = len(parts)

    def body(*refs):
        ins, outs = refs[:n], refs[n:2 * n]
        send_sems, recv_sems = refs[2 * n:]
        x, y, c = _place()
        copies = []
        for a in range(n):
            for p in range(4):
                copies.append(pltpu.make_async_remote_copy(
                    src_ref=ins[a].at[2 * p + 1 - c], dst_ref=outs[a].at[p],
                    send_sem=send_sems.at[a, p], recv_sem=recv_sems.at[a, p],
                    device_id=(x, y, 1 - c), device_id_type=MESH))
        for cp in copies:
            cp.start()
        for cp in copies:
            cp.wait_recv()
        for cp in copies:
            cp.wait_send()

    return pl.pallas_call(
        body, name=name,
        out_shape=[_sds((4,) + p.shape[1:], p.dtype) for p in parts],
        in_specs=[pl.BlockSpec(memory_space=pl.ANY)] * n,
        out_specs=[pl.BlockSpec(memory_space=pl.ANY)] * n,
        scratch_shapes=[pltpu.SemaphoreType.DMA((n, 4)), pltpu.SemaphoreType.DMA((n, 4))],
    )(*parts)


def _pair_add(name, part, recv, c_idx):
    _, r, c = part.shape
    tr = min(r, 256)

    def body(c_ref, a_ref, b_ref, o_ref):
        o_ref[...] = a_ref[...] + b_ref[...]

    return pl.pallas_call(
        body, name=name,
        grid_spec=pltpu.PrefetchScalarGridSpec(
            num_scalar_prefetch=1, grid=(4, r // tr),
            in_specs=[pl.BlockSpec((1, tr, c), lambda p, i, cr: (2 * p + cr[0], i, 0)),
                      pl.BlockSpec((1, tr, c), lambda p, i, cr: (p, i, 0))],
            out_specs=pl.BlockSpec((1, tr, c), lambda p, i, cr: (p, i, 0))),
        out_shape=_sds((4, r, c), F32),
        compiler_params=_params(16, 2),
    )(c_idx, part, recv)


def _adamw_math(w, g, m, v):
    m = ADAM_B1 * m + (1.0 - ADAM_B1) * g
    v = ADAM_B2 * v + (1.0 - ADAM_B2) * (g * g)
    m_hat = m / (1.0 - ADAM_B1 ** ADAM_STEP)
    v_hat = v / (1.0 - ADAM_B2 ** ADAM_STEP)
    delta = -ADAM_LR * (m_hat / (jnp.sqrt(v_hat) + ADAM_EPS) + ADAM_WD * w)
    return delta, m, v


def _sum_adamw(name, recv, w, m, v):
    r, c = w.shape
    tr = min(r, 256)

    def body(p_ref, w_ref, m_ref, v_ref, g_ref, d_ref, nm_ref, nv_ref):
        g = (p_ref[0] + p_ref[1]) + (p_ref[2] + p_ref[3])
        g_ref[...] = g
        d_ref[...], nm_ref[...], nv_ref[...] = _adamw_math(w_ref[...], g, m_ref[...], v_ref[...])

    blk = pl.BlockSpec((tr, c), lambda i: (i, 0))
    return pl.pallas_call(
        body, name=name, grid=(r // tr,),
        in_specs=[pl.BlockSpec((4, tr, c), lambda i: (0, i, 0)), blk, blk, blk],
        out_specs=[blk] * 4, out_shape=[_sds((r, c), F32)] * 4,
        compiler_params=_params(24, 1),
    )(recv, w, m, v)


def _small_finish(gall, me, ws, ms, vs):
    n = len(ws)
    rows = gall.shape[1]

    def body(me_ref, gall_ref, *refs):
        w_refs, m_refs, v_refs = refs[:n], refs[n:2 * n], refs[2 * n:3 * n]
        loss_ref = refs[3 * n]
        outs = refs[3 * n + 1:7 * n + 1]
        gs, flat = refs[7 * n + 1:]
        acc = gall_ref[0]
        for j in range(1, N_DEV):
            acc = acc + gall_ref[j]
        gs[...] = acc
        loss_ref[...] = gs[43:44, 0:1]
        for k in range(3):
            flat[:, k * D:(k + 1) * D] = gs[k:k + 1, :]
        dev = me_ref[0]
        mine = pl.ds(pl.multiple_of(dev * BLK, BLK), BLK)
        grads = [flat[:, pl.ds(pl.multiple_of(dev * 384, BLK), 384)], gs[3:3 + CONV_W, mine][None]]
        grads += [gs[34 + k:35 + k, mine] for k in range(4)]
        grads += [gs[38:39, :], gs[39:41, :], gs[41:43, :]]
        for k in range(n):
            g = grads[k]
            outs[k][...] = g
            outs[n + k][...], outs[2 * n + k][...], outs[3 * n + k][...] = _adamw_math(
                w_refs[k][...], g, m_refs[k][...], v_refs[k][...])

    def whole(a):
        nd = a.ndim
        return pl.BlockSpec(a.shape, lambda i, me_ref: (0,) * nd)

    res = pl.pallas_call(
        body, name="small_finish",
        grid_spec=pltpu.PrefetchScalarGridSpec(
            num_scalar_prefetch=1, grid=(1,),
            in_specs=[whole(gall)] + [whole(a) for a in list(ws) + list(ms) + list(vs)],
            out_specs=[pl.BlockSpec((1, 1), lambda i, me_ref: (0, 0))] + [whole(a) for a in ws] * 4,
            scratch_shapes=[pltpu.VMEM((rows, D), F32), pltpu.VMEM((1, 3 * D), F32)]),
        out_shape=[_sds((1, 1), F32)] + [_sds(a.shape, F32) for a in ws] * 4,
    )(me, gall, *ws, *ms, *vs)
    return res[0], res[1:n + 1], res[n + 1:2 * n + 1], res[2 * n + 1:3 * n + 1], res[3 * n + 1:]


def _a_in_proj(x, w_g, b_full, tm):
    s = x.shape[0]
    npd = w_g.shape[2]

    def body(x_ref, w_ref, b_ref, h_ref, u0_ref, xb_ref):
        xb = x_ref[...].astype(BF16)
        xb_ref[...] = xb
        for j in range(N_DEV):
            sl = slice(npd * j, npd * (j + 1))
            h_ref[:, sl] = _dot(xb, w_ref[j]) + b_ref[:, sl]
        u0_ref[...] = h_ref[:, 0:D] * _sigmoid(h_ref[:, D:2 * D])

    row = lambda i: (i, 0)
    return pl.pallas_call(
        body, name="a_in_proj", grid=(s // tm,),
        in_specs=[pl.BlockSpec((tm, D), row), pl.BlockSpec(w_g.shape, lambda i: (0, 0, 0)),
                  pl.BlockSpec((1, 3 * D), lambda i: (0, 0))],
        out_specs=[pl.BlockSpec((tm, 3 * D), row), pl.BlockSpec((tm, D), row), pl.BlockSpec((tm, D), row)],
        out_shape=[_sds((s, 3 * D), F32), _sds((s, D), F32), _sds((s, D), BF16)],
        compiler_params=_params(44, 1),
    )(x, w_g, b_full)


CONV_HALO = 32
SUBLANES = 8
COPY_ROWS = 56


def _shifted_copies(buf, shs, tm):
    n = tm + CONV_HALO - SUBLANES
    for s in range(1, SUBLANES):
        for c0 in range(0, n, COPY_ROWS):
            c1 = min(c0 + COPY_ROWS, n)
            shs[s - 1, c0:c1, :] = buf[c0 + s:c1 + s, :]


LANES = 128
CONV_ROWS = 128


def _tap_windows(offsets):
    out = []
    for s in range(SUBLANES):
        taps = [(j, o // SUBLANES) for j, o in enumerate(offsets) if o % SUBLANES == s]
        lo, hi = min(a for _, a in taps), max(a for _, a in taps)
        out.append((s, lo, hi - lo, [(j, a - lo) for j, a in taps]))
    return out


def _window(buf, shs, s, r0, lo, span, lanes):
    n = CONV_ROWS + SUBLANES * span
    rows = pl.ds(r0 + SUBLANES * lo, n)
    v = buf[rows, lanes] if s == 0 else shs[s - 1, rows, lanes]
    return v.reshape(n // SUBLANES, SUBLANES, LANES)


def _spread_taps(i, w_ref, wb):
    @pl.when(i == 0)
    def _():
        for j in range(CONV_W):
            wb[j] = jnp.broadcast_to(w_ref[j:j + 1, :], (SUBLANES, D))


def _a_conv_gate(u0, h, w_dw, b_dw, ln_g, ln_b, tm, plan=None):
    s = u0.shape[0]
    per = tm // CONV_HALO

    def body(u0_ref, halo_ref, z_ref, w_ref, bdw_ref, g_ref, b_ref, u1_ref, ga_ref, buf, shs, wb):
        i = pl.program_id(0)
        _spread_taps(i, w_ref, wb)
        buf[0:CONV_HALO, :] = jnp.where(i > 0, halo_ref[...], 0.0)
        buf[CONV_HALO:, :] = u0_ref[...]
        _shifted_copies(buf, shs, tm)
        off = CONV_HALO - (CONV_W - 1)

        windows = _tap_windows([off + j for j in range(CONV_W)])
        grp = CONV_ROWS // SUBLANES

        def chunk(it, carry):
            r0 = pl.multiple_of((it // (D // LANES)) * CONV_ROWS, CONV_ROWS)
            lanes = pl.ds(pl.multiple_of((it % (D // LANES)) * LANES, LANES), LANES)
            acc = jnp.broadcast_to(bdw_ref[:, lanes], (CONV_ROWS, LANES)).reshape(grp, SUBLANES, LANES)
            for s, lo, span, taps in windows:
                win = _window(buf, shs, s, r0, lo, span, lanes)
                for j, a in taps:
                    acc = acc + wb[j, :, lanes][None] * win[a:a + grp]
            u1_ref[pl.ds(r0, CONV_ROWS), lanes] = acc.reshape(CONV_ROWS, LANES)
            return carry

        lax.fori_loop(0, (tm // CONV_ROWS) * (D // LANES), chunk, 0)
        n, _ = _ln_stats(u1_ref[...])
        pre = n * g_ref[...] + b_ref[...]
        z = z_ref[...]
        ga_ref[...] = ((pre * _sigmoid(pre)) * (z * _sigmoid(z))).astype(BF16)

    row = lambda i: (i, 0)
    vec = pl.BlockSpec((1, D), lambda i: (0, 0))
    steps = s // tm
    return _planned_call(
        plan, (u0, u0, h, w_dw, b_dw, ln_g, ln_b), body, name="a_conv_gate", grid=(steps,), mid_step=steps // 2,
        in_specs=[pl.BlockSpec((tm, D), row),
                  pl.BlockSpec((CONV_HALO, D), lambda i: (jnp.maximum(i * per - 1, 0), 0)),
                  pl.BlockSpec((tm, D), lambda i: (i, 2)),
                  pl.BlockSpec((32, D), lambda i: (0, 0)), vec, vec, vec],
        out_specs=[pl.BlockSpec((tm, D), row), pl.BlockSpec((tm, D), row)],
        out_shape=[_sds((s, D), F32), _sds((s, D), BF16)],
        scratch_shapes=[pltpu.VMEM((tm + CONV_HALO, D), F32), pltpu.VMEM((SUBLANES - 1, tm + CONV_HALO, D), F32),
                        pltpu.VMEM((CONV_W, SUBLANES, D), F32)],
        compiler_params=_params(40, 1))


def _a_out_proj(ga, w, b, x, pg, pb, tm):
    s = x.shape[0]

    def body(ga_ref, w_ref, b_ref, x_ref, pg_ref, pb_ref, n_ref, rstd_ref, xb_ref):
        r = ALPHA * x_ref[...] + (_dot(ga_ref[...], w_ref[...]) + b_ref[...])
        n, rstd = _ln_stats(r)
        n_ref[...] = n
        rstd_ref[...] = rstd
        xb_ref[...] = (n * pg_ref[...] + pb_ref[...]).astype(BF16)

    row = lambda i: (i, 0)
    vec = pl.BlockSpec((1, D), lambda i: (0, 0))
    return pl.pallas_call(
        body, name="a_out_proj", grid=(s // tm,),
        in_specs=[pl.BlockSpec((tm, D), row), pl.BlockSpec((D, D), lambda i: (0, 0)), vec,
                  pl.BlockSpec((tm, D), row), vec, vec],
        out_specs=[pl.BlockSpec((tm, D), row), pl.BlockSpec((tm, 1), row), pl.BlockSpec((tm, D), row)],
        out_shape=[_sds((s, D), F32), _sds((s, 1), F32), _sds((s, D), BF16)],
        compiler_params=_params(32, 1),
    )(ga, w, b, x, pg, pb)


def _kv_proj(xb, w_g, tm):
    s = xb.shape[0]
    npd = w_g.shape[2]
    half = N_DEV // 2

    def body(x_ref, w_ref, k_ref, v_ref):
        xv = x_ref[...]
        for j in range(N_DEV):
            o_ref = k_ref if j < half else v_ref
            jj = j % half
            o_ref[:, npd * jj:npd * (jj + 1)] = _dot(xv, w_ref[j]).astype(BF16)

    row = lambda i: (i, 0)
    return pl.pallas_call(
        body, name="kv_proj", grid=(s // tm,),
        in_specs=[pl.BlockSpec((tm, D), row), pl.BlockSpec(w_g.shape, lambda i: (0, 0, 0))],
        out_specs=[pl.BlockSpec((tm, 3 * D), row), pl.BlockSpec((tm, 3 * D), row)],
        out_shape=[_sds((s, 3 * D), BF16), _sds((s, 3 * D), BF16)],
        compiler_params=_params(52, 1),
    )(xb, w_g)


def _b_in_proj(xb, w_g, tm):
    s = xb.shape[0]
    npd = w_g.shape[2]
    scale = HEAD_DIM ** -0.5

    def body(x_ref, w_ref, q_ref, z_ref):
        xv = x_ref[...]
        for j in range(N_DEV):
            hj = _dot(xv, w_ref[j])
            if j < 6:
                q_ref[:, npd * j:npd * (j + 1)] = (hj.astype(BF16) * scale).astype(BF16)
            else:
                z_ref[:, npd * (j - 6):npd * (j - 5)] = hj

    row = lambda i: (i, 0)
    return pl.pallas_call(
        body, name="b_in_proj", grid=(s // tm,),
        in_specs=[pl.BlockSpec((tm, D), row), pl.BlockSpec(w_g.shape, lambda i: (0, 0, 0))],
        out_specs=[pl.BlockSpec((tm, 3 * D), row), pl.BlockSpec((tm, D), row)],
        out_shape=[_sds((s, 3 * D), BF16), _sds((s, D), F32)],
        compiler_params=_params(44, 1),
    )(xb, w_g)


ATTN_NQ = {1: 8, 4: 2, 16: 1}
N_PAIRS = N_HEADS // 2


def _band_table(d):
    qi = np.arange(BLK)[:, None]
    kj = np.arange(2 * BLK)[None, :]
    dist = qi + BLK - kj
    ok = (dist >= 0) & (dist <= BLK)
    return jnp.asarray(np.where(ok, -(d * dist).astype(np.float32), np.float32(NEG)), dtype=F32)


def _slope_table():
    t = np.zeros((N_PAIRS, 8, 2 * BLK), np.float32)
    for h in range(N_HEADS):
        t[h // 2, h % 2, :] = SLOPES[h]
    return jnp.asarray(t)


def _head_masks():
    lane = lax.broadcasted_iota(jnp.int32, (1, BLK), 1)
    lo = (lane < HEAD_DIM).astype(BF16)
    return (lo, (1.0 - lo).astype(BF16))


def _pick_col(tile, lane, h):
    return jnp.sum(jnp.where(lane == h, tile, 0.0), axis=1, keepdims=True)


def _rows(base, n, d):
    return pl.ds(base, n) if d == 1 else pl.ds(base, n, stride=d)


ATTN_PRE = {1: 1, 4: 1, 16: 4}


def _regroup(src, dst, d1):
    n = src.shape[0] // d1
    for r1 in range(d1):
        dst[r1] = src[pl.ds(r1, n, stride=d1), :]


def _ungroup(src, dst, d1):
    n = dst.shape[0] // d1
    for r1 in range(d1):
        dst[pl.ds(r1, n, stride=d1), :] = src[r1]


def _grouped(shape, d1):
    return pltpu.VMEM((d1, shape[0] // d1, shape[1]), F32)


def _with_halo(halo_ref, tile_ref, b):
    if b == 0:
        return jnp.concatenate([halo_ref[...], tile_ref[0:BLK, :]], axis=0)
    return tile_ref[(b - 1) * BLK:(b + 1) * BLK, :]


def _attn_forward(q, k_all, v_all, g):
    d = DILATIONS[g]
    s = q.shape[0]
    nq = ATTN_NQ[d]
    d1 = ATTN_PRE[d]
    d2 = d // d1
    halo = BLK * d
    tile = nq * halo
    assert s % tile == 0
    pre = d1 > 1

    def body(q_ref, k_ref, kh_ref, v_ref, vh_ref, nd_ref, sl_ref, o_ref, l_ref, qf, kf, vf, *grouped):
        q1, k1, v1, l1, o1 = grouped if pre else (None,) * 5
        n = pl.program_id(0)
        hp = pl.program_id(1)
        if d > 1:
            qf[...] = q_ref[...].astype(F32)
            kf[0:halo, :] = kh_ref[...].astype(F32)
            kf[halo:, :] = k_ref[...].astype(F32)
            vf[0:halo, :] = vh_ref[...].astype(F32)
            vf[halo:, :] = v_ref[...].astype(F32)

        @pl.when(hp == 0)
        def _():
            l_ref[...] = jnp.zeros(l_ref.shape, F32)

        if pre:
            for src, dst in ((qf, q1), (kf, k1), (vf, v1), (l_ref, l1)):
                _regroup(src, dst, d1)

        def pick(nat, grp, r1):
            return grp.at[r1] if pre else nat

        col = lax.broadcasted_iota(jnp.int32, (2 * BLK, 2 * BLK), 1)
        lane = lax.broadcasted_iota(jnp.int32, (BLK, BLK), 1)
        masks = _head_masks()
        bias = jnp.concatenate([sl_ref[0, e:e + 1, :] * nd_ref[...] for e in range(2)], axis=0)
        bias0 = bias + jnp.where((n == 0) & (col < BLK), NEG, 0.0)
        for b in range(nq):
            for r in range(d):
                r1, r2 = r % d1, r // d1
                rq = _rows(b * (halo // d1) + r2, BLK, d2)
                rk = _rows(b * (halo // d1) + r2, 2 * BLK, d2)
                if d > 1:
                    q2 = pick(qf, q1, r1)[rq, :].astype(BF16)
                    kcat = pick(kf, k1, r1)[rk, :].astype(BF16)
                    vcat = pick(vf, v1, r1)[rk, :].astype(BF16)
                else:
                    q2 = q_ref[rq, :]
                    kcat = _with_halo(kh_ref, k_ref, b)
                    vcat = _with_halo(vh_ref, v_ref, b)
                sc = _dot_nt(jnp.concatenate([q2 * masks[0], q2 * masks[1]], axis=0), kcat)
                sc = sc + (bias0 if b == 0 else bias)
                m = jnp.max(sc, axis=1, keepdims=True)
                p = jnp.exp(sc - m)
                l = jnp.sum(p, axis=1, keepdims=True)
                oh = _dot(p.astype(BF16), vcat) / l
                lse = m + jnp.log(l)
                lt = pick(l_ref, l1, r1)[rq, :]
                lt = jnp.where(lane == 2 * hp, lse[0:BLK], lt)
                lt = jnp.where(lane == 2 * hp + 1, lse[BLK:], lt)
                pick(o_ref, o1, r1)[rq, :] = jnp.where(lane < HEAD_DIM, oh[0:BLK], oh[BLK:])
                pick(l_ref, l1, r1)[rq, :] = lt
        if pre:
            _ungroup(o1, o_ref, d1)
            _ungroup(l1, l_ref, d1)

    col_blk = lambda n, hp: (n, g * N_PAIRS + hp)
    halo_blk = lambda n, hp: (jnp.maximum(n * nq - 1, 0), g * N_PAIRS + hp)
    t_shape, w_shape = (tile, BLK), (tile + halo, BLK)
    scratch = [pltpu.VMEM(t_shape, F32), pltpu.VMEM(w_shape, F32), pltpu.VMEM(w_shape, F32)]
    if pre:
        scratch += [_grouped(t_shape, d1), _grouped(w_shape, d1), _grouped(w_shape, d1), _grouped(t_shape, d1),
                    _grouped(t_shape, d1)]
    return pl.pallas_call(
        body, name=f"attn_fwd_g{g}", grid=(s // tile, N_PAIRS),
        in_specs=[pl.BlockSpec((tile, BLK), col_blk),
                  pl.BlockSpec((tile, BLK), col_blk), pl.BlockSpec((halo, BLK), halo_blk),
                  pl.BlockSpec((tile, BLK), col_blk), pl.BlockSpec((halo, BLK), halo_blk),
                  pl.BlockSpec((BLK, 2 * BLK), lambda n, hp: (0, 0)),
                  pl.BlockSpec((1, 8, 2 * BLK), lambda n, hp: (hp, 0, 0))],
        out_specs=[pl.BlockSpec((tile, BLK), lambda n, hp: (n, hp)), pl.BlockSpec((tile, BLK), lambda n, hp: (n, 0))],
        out_shape=[_sds((s, D), F32), _sds((s, BLK), F32)],
        scratch_shapes=scratch, compiler_params=_params(40, 2),
    )(q, k_all, k_all, v_all, v_all, _band_table(d), _slope_table())


def _attn_backward(q, k_all, v_all, do, lse, dd, dhb, dk_all, dv_all, g):
    d = DILATIONS[g]
    s = q.shape[0]
    nq = ATTN_NQ[d]
    d1 = ATTN_PRE[d]
    d2 = d // d1
    halo = BLK * d
    tile = nq * halo
    nt = s // tile
    first = dk_all is None
    pre = d1 > 1

    def body(*refs):
        refs = list(refs)
        q_ref, k_ref, kh_ref, v_ref, vh_ref, nd_ref, sl_ref, do_ref, l_ref, dd_ref = refs[:10]
        del refs[:10 + (1 if first else 3)]
        dq_ref, dk_ref, dv_ref = refs[:3]
        qf, dof, kf, vf, dqf, dkf, dvf, ck, cv = refs[3:12]
        del refs[:12]
        if pre:
            q1, do1, k1, v1, l1, dd1, dq1, dk1, dv1 = refs
        else:
            q1 = do1 = k1 = v1 = l1 = dd1 = dq1 = None
            dk1, dv1 = dkf, dvf
        hp = pl.program_id(0)
        n = pl.program_id(1)

        @pl.when(n == 0)
        def _():
            ck[...] = jnp.zeros(ck.shape, F32)
            cv[...] = jnp.zeros(cv.shape, F32)

        dk1[...] = jnp.zeros(dk1.shape, F32)
        dv1[...] = jnp.zeros(dv1.shape, F32)

        def pick(nat, grp, r1):
            return grp.at[r1] if pre else nat

        @pl.when(n < nt)
        def _():
            if d > 1:
                qf[...] = q_ref[...].astype(F32)
                dof[...] = do_ref[...].astype(F32)
                kf[0:halo, :] = kh_ref[...].astype(F32)
                kf[halo:, :] = k_ref[...].astype(F32)
                vf[0:halo, :] = vh_ref[...].astype(F32)
                vf[halo:, :] = v_ref[...].astype(F32)
            if pre:
                for src, dst in ((qf, q1), (dof, do1), (kf, k1), (vf, v1), (l_ref, l1), (dd_ref, dd1)):
                    _regroup(src, dst, d1)
            col = lax.broadcasted_iota(jnp.int32, (2 * BLK, 2 * BLK), 1)
            lane = lax.broadcasted_iota(jnp.int32, (BLK, BLK), 1)
            masks = _head_masks()
            bias = jnp.concatenate([sl_ref[0, e:e + 1, :] * nd_ref[...] for e in range(2)], axis=0)
            bias0 = bias + jnp.where((n == 0) & (col < BLK), NEG, 0.0)
            for b in range(nq):
                for r in range(d):
                    r1, r2 = r % d1, r // d1
                    rq = _rows(b * (halo // d1) + r2, BLK, d2)
                    rk = _rows(b * (halo // d1) + r2, 2 * BLK, d2)
                    if d > 1:
                        q2 = pick(qf, q1, r1)[rq, :].astype(BF16)
                        do2 = pick(dof, do1, r1)[rq, :].astype(BF16)
                        kcat = pick(kf, k1, r1)[rk, :].astype(BF16)
                        vcat = pick(vf, v1, r1)[rk, :].astype(BF16)
                    else:
                        q2 = q_ref[rq, :]
                        do2 = do_ref[rq, :]
                        kcat = _with_halo(kh_ref, k_ref, b)
                        vcat = _with_halo(vh_ref, v_ref, b)
                    lt = pick(l_ref, l1, r1)[rq, :]
                    dt = pick(dd_ref, dd1, r1)[rq, :]
                    qs = jnp.concatenate([q2 * masks[0], q2 * masks[1]], axis=0)
                    dos = jnp.concatenate([do2 * masks[0], do2 * masks[1]], axis=0)
                    lcol = jnp.concatenate([_pick_col(lt, lane, 2 * hp + e) for e in range(2)], axis=0)
                    dcol = jnp.concatenate([_pick_col(dt, lane, 2 * hp + e) for e in range(2)], axis=0)
                    sc = _dot_nt(qs, kcat) + (bias0 if b == 0 else bias)
                    p = jnp.exp(sc - lcol)
                    ds = p * (_dot_nt(dos, vcat) - dcol)
                    dsb = ds.astype(BF16)
                    dq = _dot(dsb, kcat)
                    dq2 = (HEAD_DIM ** -0.5) * jnp.where(lane < HEAD_DIM, dq[0:BLK], dq[BLK:])
                    pick(dqf, dq1, r1)[rq, :] = dq2
                    pick(dkf, dk1, r1)[rk, :] += _dot(dsb.T, qs)
                    pick(dvf, dv1, r1)[rk, :] += _dot(p.astype(BF16).T, dos)
            if pre:
                _ungroup(dq1, dqf, d1)
            dq_ref[...] = dqf[...].astype(BF16)

        if pre:
            _ungroup(dk1, dkf, d1)
            _ungroup(dv1, dvf, d1)
        if tile > halo:
            dk_ref[0:tile - halo, :] = ck[0:tile - halo, :].astype(BF16)
            dv_ref[0:tile - halo, :] = cv[0:tile - halo, :].astype(BF16)
        dk_ref[tile - halo:, :] = (ck[tile - halo:, :] + dkf[0:halo, :]).astype(BF16)
        dv_ref[tile - halo:, :] = (cv[tile - halo:, :] + dvf[0:halo, :]).astype(BF16)
        ck[...] = dkf[halo:, :]
        cv[...] = dvf[halo:, :]

    cur = lambda n: jnp.minimum(n, nt - 1)
    col_blk = lambda hp, n: (cur(n), g * N_PAIRS + hp)
    halo_blk = lambda hp, n: (jnp.maximum(cur(n) * nq - 1, 0), g * N_PAIRS + hp)
    out_kv = lambda hp, n: (jnp.maximum(n - 1, 0), g * N_PAIRS + hp)
    small = pl.BlockSpec((tile, BLK), lambda hp, n: (cur(n), 0))
    hbm = pl.BlockSpec(memory_space=pl.ANY)
    in_specs = [pl.BlockSpec((tile, BLK), col_blk),
                pl.BlockSpec((tile, BLK), col_blk), pl.BlockSpec((halo, BLK), halo_blk),
                pl.BlockSpec((tile, BLK), col_blk), pl.BlockSpec((halo, BLK), halo_blk),
                pl.BlockSpec((BLK, 2 * BLK), lambda hp, n: (0, 0)),
                pl.BlockSpec((1, 8, 2 * BLK), lambda hp, n: (hp, 0, 0)),
                pl.BlockSpec((tile, BLK), lambda hp, n: (cur(n), hp)), small, small, hbm]
    args = [q, k_all, k_all, v_all, v_all, _band_table(d), _slope_table(), do, lse, dd, dhb]
    aliases = {10: 0}
    if not first:
        in_specs += [hbm, hbm]
        args += [dk_all, dv_all]
        aliases.update({11: 1, 12: 2})
    t_shape, w_shape = (tile, BLK), (tile + halo, BLK)
    tile_f32, wide_f32 = pltpu.VMEM(t_shape, F32), pltpu.VMEM(w_shape, F32)
    scratch = [tile_f32, tile_f32, wide_f32, wide_f32, tile_f32, wide_f32, wide_f32, tile_f32, tile_f32]
    if pre:
        tg, wg = _grouped(t_shape, d1), _grouped(w_shape, d1)
        scratch += [tg, tg, wg, wg, tg, tg, tg, wg, wg]
    return pl.pallas_call(
        body, name=f"attn_bwd_g{g}", grid=(N_PAIRS, nt + 1), in_specs=in_specs,
        out_specs=[pl.BlockSpec((tile, BLK), col_blk), pl.BlockSpec((tile, BLK), out_kv),
                   pl.BlockSpec((tile, BLK), out_kv)],
        out_shape=[_sds((s, 4 * D), BF16), _sds((s, 3 * D), BF16), _sds((s, 3 * D), BF16)],
        scratch_shapes=scratch, input_output_aliases=aliases, compiler_params=_params(48, 2),
    )(*args)


def _head_spread():
    spread = (np.arange(BLK)[:, None] == np.arange(D)[None, :] // HEAD_DIM).astype(np.float32)
    return jnp.asarray(spread, dtype=BF16)


def _attn_merge(os_, lses, z, tm):
    s = z.shape[0]

    def body(o0_ref, o1_ref, o2_ref, l0_ref, l1_ref, l2_ref, z_ref, e_ref, o_ref, l_ref, gb_ref):
        ls = [l0_ref[...], l1_ref[...], l2_ref[...]]
        m = jnp.maximum(jnp.maximum(ls[0], ls[1]), ls[2])
        lse = m + jnp.log(jnp.exp(ls[0] - m) + jnp.exp(ls[1] - m) + jnp.exp(ls[2] - m))
        l_ref[...] = lse
        o = jnp.zeros((tm, D), F32)
        for l_g, og_ref in zip(ls, (o0_ref, o1_ref, o2_ref)):
            wg = jnp.exp(l_g - lse)
            hi = wg.astype(BF16)
            lo = (wg - hi.astype(F32)).astype(BF16)
            o = o + (_dot(hi, e_ref[...]) + _dot(lo, e_ref[...])) * og_ref[...]
        o_ref[...] = o
        zz = z_ref[...]
        gb_ref[...] = (o * (zz * _sigmoid(zz))).astype(BF16)

    row = lambda i: (i, 0)
    wide = pl.BlockSpec((tm, D), row)
    stat = pl.BlockSpec((tm, BLK), row)
    return pl.pallas_call(
        body, name="attn_merge", grid=(s // tm,),
        in_specs=[wide, wide, wide, stat, stat, stat, wide, pl.BlockSpec((BLK, D), lambda i: (0, 0))],
        out_specs=[wide, stat, wide],
        out_shape=[_sds((s, D), F32), _sds((s, BLK), F32), _sds((s, D), BF16)],
        compiler_params=_params(40, 1),
    )(*os_, *lses, z, _head_spread())


def _b_out_loss(gb, w, b, n1, pg0, pb0, pg1, pb1, tgt, tm):
    s = gb.shape[0]
    last = s // tm - 1

    def body(gb_ref, w_ref, b_ref, n1_ref, pg0_ref, pb0_ref, pg1_ref, pb1_ref, t_ref,
             dr_ref, drb_ref, loss_ref, dpg_ref, dpb_ref, dbo_ref):
        i = pl.program_id(0)
        _acc_init(i, loss_ref, dpg_ref, dpb_ref, dbo_ref)
        x1 = n1_ref[...] * pg0_ref[...] + pb0_ref[...]
        r = ALPHA * x1 + (_dot(gb_ref[...], w_ref[...]) + b_ref[...])
        n, rstd = _ln_stats(r)
        err = (n * pg1_ref[...] + pb1_ref[...]) - t_ref[...]
        loss_ref[...] += _rowsum8(err * err)
        dx2 = err * (1.0 / D)
        dpg_ref[...] += _rowsum8(dx2 * n)
        dpb_ref[...] += _rowsum8(dx2)
        dr = _ln_bwd(dx2 * pg1_ref[...], n, rstd)
        dr_ref[...] = dr
        drb_ref[...] = dr.astype(BF16)
        dbo_ref[...] += _rowsum8(dr)
        _acc_finish(i, last, dpg_ref, dpb_ref, dbo_ref)

        @pl.when(i == last)
        def _():
            loss_ref[...] = jnp.broadcast_to((0.5 / D) * jnp.sum(loss_ref[...], keepdims=True), loss_ref.shape)

    row = lambda i: (i, 0)
    vec = pl.BlockSpec((1, D), lambda i: (0, 0))
    acc = pl.BlockSpec((8, D), lambda i: (0, 0))
    return pl.pallas_call(
        body, name="b_out_loss", grid=(s // tm,),
        in_specs=[pl.BlockSpec((tm, D), row), pl.BlockSpec((D, D), lambda i: (0, 0)), vec,
                  pl.BlockSpec((tm, D), row), vec, vec, vec, vec, pl.BlockSpec((tm, D), row)],
        out_specs=[pl.BlockSpec((tm, D), row), pl.BlockSpec((tm, D), row), acc, acc, acc, acc],
        out_shape=[_sds((s, D), F32), _sds((s, D), BF16)] + [_sds((8, D), F32)] * 4,
        compiler_params=_params(36, 1),
    )(gb, w, b, n1, pg0, pb0, pg1, pb1, tgt)


def _head_selector():
    sel = (np.arange(D)[:, None] // HEAD_DIM == np.arange(BLK)[None, :]).astype(np.float32)
    return jnp.asarray(sel, dtype=BF16)


def _b_out_bwd(drb, w, z, o, tm):
    s = drb.shape[0]

    def body(dr_ref, w_ref, z_ref, o_ref, sel_ref, do_ref, dh_ref, dd_ref):
        dg = _dot_nt(dr_ref[...], w_ref[...])
        zz = z_ref[...]
        sg = _sigmoid(zz)
        do = dg * (zz * sg)
        ov = o_ref[...]
        do_ref[...] = do.astype(BF16)
        dh_ref[...] = (dg * ov * _dsilu(zz, sg)).astype(BF16)
        prod = do * ov
        hi = prod.astype(BF16)
        lo = (prod - hi.astype(F32)).astype(BF16)
        dd_ref[...] = _dot(hi, sel_ref[...]) + _dot(lo, sel_ref[...])

    row = lambda i: (i, 0)
    return pl.pallas_call(
        body, name="b_out_bwd", grid=(s // tm,),
        in_specs=[pl.BlockSpec((tm, D), row), pl.BlockSpec((D, D), lambda i: (0, 0)),
                  pl.BlockSpec((tm, D), row), pl.BlockSpec((tm, D), row), pl.BlockSpec((D, BLK), lambda i: (0, 0))],
        out_specs=[pl.BlockSpec((tm, D), row), pl.BlockSpec((tm, D), lambda i: (i, 3)),
                   pl.BlockSpec((tm, BLK), row)],
        out_shape=[_sds((s, D), BF16), _sds((s, 4 * D), BF16), _sds((s, BLK), F32)],
        compiler_params=_params(36, 1),
    )(drb, w, z, o, _head_selector())


def _b_in_bwd(dr2, dhb, dk_all, dv_all, wb_g, wkv_g, n1, rstd1, pg0, tm):
    s = dr2.shape[0]
    last = s // tm - 1
    nb_, nkv = wb_g.shape[2], wkv_g.shape[2]
    half = N_DEV // 2

    def body(dr2_ref, dh_ref, dk_ref, dv_ref, wb_hbm, wkv_hbm, n_ref, rstd_ref, pg_ref,
             dr_ref, drb_ref, dpg_ref, dpb_ref, dbo_ref, wb, wkv):
        i = pl.program_id(0)

        @pl.when(i == 0)
        def _():
            pltpu.sync_copy(wb_hbm, wb)
            pltpu.sync_copy(wkv_hbm, wkv)

        _acc_init(i, dpg_ref, dpb_ref, dbo_ref)
        acc = ALPHA * dr2_ref[...]
        for j in range(N_DEV):
            acc = acc + _dot_nt(dh_ref[:, nb_ * j:nb_ * (j + 1)], wb[j])
            src = dk_ref if j < half else dv_ref
            jj = j % half
            acc = acc + _dot_nt(src[:, nkv * jj:nkv * (jj + 1)], wkv[j])
        n = n_ref[...]
        dpg_ref[...] += _rowsum8(acc * n)
        dpb_ref[...] += _rowsum8(acc)
        dr = _ln_bwd(acc * pg_ref[...], n, rstd_ref[...])
        dr_ref[...] = dr
        drb_ref[...] = dr.astype(BF16)
        dbo_ref[...] += _rowsum8(dr)
        _acc_finish(i, last, dpg_ref, dpb_ref, dbo_ref)

    row = lambda i: (i, 0)
    hbm = pl.BlockSpec(memory_space=pl.ANY)
    acc_spec = pl.BlockSpec((8, D), lambda i: (0, 0))
    return pl.pallas_call(
        body, name="b_in_bwd", grid=(s // tm,),
        in_specs=[pl.BlockSpec((tm, D), row), pl.BlockSpec((tm, 4 * D), row), pl.BlockSpec((tm, 3 * D), row),
                  pl.BlockSpec((tm, 3 * D), row), hbm, hbm, pl.BlockSpec((tm, D), row), pl.BlockSpec((tm, 1), row),
                  pl.BlockSpec((1, D), lambda i: (0, 0))],
        out_specs=[pl.BlockSpec((tm, D), row), pl.BlockSpec((tm, D), row), acc_spec, acc_spec, acc_spec],
        out_shape=[_sds((s, D), F32), _sds((s, D), BF16)] + [_sds((8, D), F32)] * 3,
        scratch_shapes=[pltpu.VMEM(wb_g.shape, BF16), pltpu.VMEM(wkv_g.shape, BF16)],
        compiler_params=_params(56, 1),
    )(dr2, dhb, dk_all, dv_all, wb_g, wkv_g, n1, rstd1, pg0)


def _a_out_bwd(drb, w, u1, h, ln_g, ln_b, tm, plan=None):
    s = drb.shape[0]
    last = s // tm - 1

    def body(dr_ref, w_ref, u1_ref, z_ref, g_ref, b_ref, du1_ref, dh_ref, dg_ref, db_ref, dbz_ref):
        i = pl.program_id(0)
        _acc_init(i, dg_ref, db_ref, dbz_ref)
        dga = _dot_nt(dr_ref[...], w_ref[...])
        n, rstd = _ln_stats(u1_ref[...])
        pre = n * g_ref[...] + b_ref[...]
        sp = _sigmoid(pre)
        zz = z_ref[...]
        sz = _sigmoid(zz)
        dz = dga * (pre * sp) * _dsilu(zz, sz)
        dh_ref[...] = dz.astype(BF16)
        dbz_ref[...] += _rowsum8(dz)
        dpre = dga * (zz * sz) * _dsilu(pre, sp)
        dg_ref[...] += _rowsum8(dpre * n)
        db_ref[...] += _rowsum8(dpre)
        du1_ref[...] = _ln_bwd(dpre * g_ref[...], n, rstd)
        _acc_finish(i, last, dg_ref, db_ref, dbz_ref)

    row = lambda i: (i, 0)
    vec = pl.BlockSpec((1, D), lambda i: (0, 0))
    acc_spec = pl.BlockSpec((8, D), lambda i: (0, 0))
    return _planned_call(
        plan, (drb, w, u1, h, ln_g, ln_b), body, name="a_out_bwd", grid=(s // tm,),
        in_specs=[pl.BlockSpec((tm, D), row), pl.BlockSpec((D, D), lambda i: (0, 0)), pl.BlockSpec((tm, D), row),
                  pl.BlockSpec((tm, D), lambda i: (i, 2)), vec, vec],
        out_specs=[pl.BlockSpec((tm, D), row), pl.BlockSpec((tm, D), lambda i: (i, 2)),
                   acc_spec, acc_spec, acc_spec],
        out_shape=[_sds((s, D), F32), _sds((s, 3 * D), BF16)] + [_sds((8, D), F32)] * 3,
        compiler_params=_params(32, 1))


def _a_conv_bwd(du1, u0, h, dha, w_dw, tm, plan=None):
    s = du1.shape[0]
    steps = s // tm
    per = tm // CONV_HALO
    pad = CONV_W - 1

    def body(du_ref, dun_ref, u0_ref, u0p_ref, h_ref, w_ref, dha_hbm,
             dh_ref, dw_ref, dbdw_ref, dba_ref, dbg_ref, dbuf, ubuf, wacc, dshs, ushs, wb, du0_buf):
        i = pl.program_id(0)
        _spread_taps(i, w_ref, wb)
        _acc_init(i, dbdw_ref, dba_ref, dbg_ref, wacc)
        dbuf[0:tm, :] = du_ref[...]
        dbuf[tm:, :] = jnp.where(i < steps - 1, dun_ref[...], 0.0)
        ubuf[0:CONV_HALO, :] = jnp.where(i > 0, u0p_ref[...], 0.0)
        ubuf[CONV_HALO:, :] = u0_ref[...]
        _shifted_copies(dbuf, dshs, tm)
        _shifted_copies(ubuf, ushs, tm)
        off = CONV_HALO - pad
        grp = CONV_ROWS // SUBLANES
        d_windows = _tap_windows([pad - j for j in range(CONV_W)])
        u_windows = _tap_windows([off + j for j in range(CONV_W)])

        def chunk(it, carry):
            r0 = pl.multiple_of((it // (D // LANES)) * CONV_ROWS, CONV_ROWS)
            lanes = pl.ds(pl.multiple_of((it % (D // LANES)) * LANES, LANES), LANES)
            rows = pl.ds(r0, CONV_ROWS)
            acc = jnp.zeros((grp, SUBLANES, LANES), F32)
            for s, lo, span, taps in d_windows:
                win = _window(dbuf, dshs, s, r0, lo, span, lanes)
                for j, a in taps:
                    acc = acc + wb[j, :, lanes][None] * win[a:a + grp]
            du0_buf[rows, lanes] = acc.reshape(CONV_ROWS, LANES)
            du3 = du_ref[rows, lanes].reshape(grp, SUBLANES, LANES)
            for s, lo, span, taps in u_windows:
                win = _window(ubuf, ushs, s, r0, lo, span, lanes)
                for j, a in taps:
                    wacc[j, :, lanes] += (du3 * win[a:a + grp]).sum(axis=0)
            return carry

        lax.fori_loop(0, (tm // CONV_ROWS) * (D // LANES), chunk, 0)
        du0 = du0_buf[...]
        sg = _sigmoid(h_ref[:, D:2 * D])
        da = du0 * sg
        dag = du0 * h_ref[:, 0:D] * (sg * (1.0 - sg))
        dh_ref[:, 0:D] = da.astype(BF16)
        dh_ref[:, D:2 * D] = dag.astype(BF16)
        dbdw_ref[...] += _rowsum8(du_ref[...])
        dba_ref[...] += _rowsum8(da)
        dbg_ref[...] += _rowsum8(dag)
        _acc_finish(i, steps - 1, dbdw_ref, dba_ref, dbg_ref)

        @pl.when(i == steps - 1)
        def _():
            for j in range(CONV_W):
                dw_ref[j:j + 1, :] = jnp.sum(wacc[j], axis=0, keepdims=True)
            dw_ref[CONV_W:, :] = jnp.zeros((32 - CONV_W, D), F32)

    row = lambda i: (i, 0)
    acc_spec = pl.BlockSpec((8, D), lambda i: (0, 0))
    return _planned_call(
        plan, (du1, du1, u0, u0, h, w_dw, dha), body, name="a_conv_bwd", grid=(steps,),
        in_specs=[pl.BlockSpec((tm, D), row),
                  pl.BlockSpec((CONV_HALO, D), lambda i: (jnp.minimum((i + 1) * per, s // CONV_HALO - 1), 0)),
                  pl.BlockSpec((tm, D), row),
                  pl.BlockSpec((CONV_HALO, D), lambda i: (jnp.maximum(i * per - 1, 0), 0)),
                  pl.BlockSpec((tm, 2 * D), lambda i: (i, 0)),
                  pl.BlockSpec((32, D), lambda i: (0, 0)), pl.BlockSpec(memory_space=pl.ANY)],
        out_specs=[pl.BlockSpec((tm, 2 * D), lambda i: (i, 0)),
                   pl.BlockSpec((32, D), lambda i: (0, 0)), acc_spec, acc_spec, acc_spec],
        out_shape=[_sds((s, 3 * D), BF16), _sds((32, D), F32)] + [_sds((8, D), F32)] * 3,
        scratch_shapes=[pltpu.VMEM((tm + CONV_HALO, D), F32), pltpu.VMEM((tm + CONV_HALO, D), F32),
                        pltpu.VMEM((CONV_W, 8, D), F32),
                        pltpu.VMEM((SUBLANES - 1, tm + CONV_HALO, D), F32),
                        pltpu.VMEM((SUBLANES - 1, tm + CONV_HALO, D), F32),
                        pltpu.VMEM((CONV_W, SUBLANES, D), F32), pltpu.VMEM((tm, D), F32)],
        input_output_aliases={6: 0}, compiler_params=_params(56, 1))


def _a_in_bwd(name, dr1, dha, w_g, tm, half, plan=None, into=None):
    s = dr1.shape[0]
    npd = w_g.shape[2]
    steps = s // tm // 2
    first = half * steps

    def body(*refs):
        dr_ref, dh_ref, w_ref = refs[:3]
        o_ref = refs[-1]
        acc = ALPHA * dr_ref[...]
        for j in range(N_DEV):
            acc = acc + _dot_nt(dh_ref[:, npd * j:npd * (j + 1)], w_ref[j])
        o_ref[...] = acc

    row = lambda i: (first + i, 0)
    in_specs = [pl.BlockSpec((tm, D), row), pl.BlockSpec((tm, 3 * D), row),
                pl.BlockSpec(w_g.shape, lambda i: (0, 0, 0))]
    args = (dr1, dha, w_g)
    kw = {}
    if into is not None:
        in_specs.append(pl.BlockSpec(memory_space=pl.ANY))
        args += (into,)
        kw["input_output_aliases"] = {3: 0}
    return _planned_call(
        plan, args, body, name=name, grid=(steps,), in_specs=in_specs,
        out_specs=[pl.BlockSpec((tm, D), row)], out_shape=[_sds((s, D), F32)],
        compiler_params=_params(36, 1), **kw)


def _wgrad(name, a, b, npd, ts, total=None, at=0, into=None):
    s = a.shape[0]
    n_blk = b.shape[1] // npd
    total = n_blk if total is None else total
    assert at % n_blk == 0

    def body(*refs):
        a_ref, b_ref = refs[:2]
        o_ref = refs[-1]
        si = pl.program_id(0)

        @pl.when(si == 0)
        def _():
            o_ref[...] = jnp.zeros(o_ref.shape, F32)

        a_t = a_ref[...].T
        for j in range(n_blk):
            o_ref[j] += _dot(a_t, b_ref[:, npd * j:npd * (j + 1)])

    in_specs = [pl.BlockSpec((ts, D), lambda si: (si, 0)), pl.BlockSpec((ts, n_blk * npd), lambda si: (si, 0))]
    args = [a, b]
    aliases = {}
    if into is not None:
        in_specs.append(pl.BlockSpec(memory_space=pl.ANY))
        args.append(into)
        aliases = {2: 0}
    return pl.pallas_call(
        body, name=name, grid=(s // ts,), in_specs=in_specs,
        out_specs=pl.BlockSpec((n_blk, D, npd), lambda si: (at // n_blk, 0, 0)),
        out_shape=_sds((total, D, npd), F32), input_output_aliases=aliases,
        compiler_params=_params(56, 1),
    )(*args)


SMALL_ROWS = 40
GRAD_ROWS = 48


def kernel(x, a_w_in, a_b_in, a_w_dw, a_b_dw, a_ln_g, a_ln_b, a_w_out, a_b_out, kv_w, b_w_in, b_w_out, b_b_out, post_ln_g, post_ln_b, loss_target, m_a_w_in, m_a_b_in, m_a_w_dw, m_a_b_dw, m_a_ln_g, m_a_ln_b, m_a_w_out, m_a_b_out, m_kv_w, m_b_w_in, m_b_w_out, m_b_b_out, m_post_ln_g, m_post_ln_b, v_a_w_in, v_a_b_in, v_a_w_dw, v_a_b_dw, v_a_ln_g, v_a_ln_b, v_a_w_out, v_a_b_out, v_kv_w, v_b_w_in, v_b_w_out, v_b_b_out, v_post_ln_g, v_post_ln_b):
    s = x.shape[1]
    assert x.shape == (1, s, D) and s % (DILATIONS[-1] * BLK) == 0
    xs = x.reshape(s, D)
    tgt = loss_target.reshape(s, D)
    me = 4 * lax.axis_index("x") + 2 * lax.axis_index("y") + lax.axis_index("c")
    c_idx = lax.axis_index("c").astype(jnp.int32).reshape(1)

    def small_pack(b_in, w_dw, b_dw, ln_g, ln_b, b_out):
        rows = [b_in.reshape(3, BLK), w_dw.reshape(CONV_W, BLK), b_dw.reshape(1, BLK), ln_g.reshape(1, BLK),
                ln_b.reshape(1, BLK), b_out.reshape(1, BLK)]
        n = sum(r.shape[0] for r in rows)
        return jnp.concatenate(rows + [jnp.zeros((SMALL_ROWS - n, BLK), F32)], axis=0)

    wa_in, sm, *later = _all_gather(
        "gather_first", [a_w_in[0], small_pack(a_b_in, a_w_dw, a_b_dw, a_ln_g, a_ln_b, a_b_out)], [BF16, F32],
        casts=[a_w_out[0], kv_w, b_w_in[0], b_w_out[0]])
    ba_in = sm[:, 0:3, :].reshape(1, 3 * D)
    w_dw = jnp.concatenate([sm[:, 3:3 + CONV_W, :].transpose(1, 0, 2).reshape(CONV_W, D), jnp.zeros((1, D), F32)], axis=0)
    b_dw, ln_g, ln_b, ba_out = (sm[:, 34 + k, :].reshape(1, D) for k in range(4))
    pg0, pg1 = post_ln_g[0:1], post_ln_g[1:2]
    pb0, pb1 = post_ln_b[0:1], post_ln_b[1:2]

    h_a, u0, xb = _a_in_proj(xs, wa_in, ba_in, 512)
    (u1, g_a), (wa_out, wkv, wb_in, wb_out) = _a_conv_gate(u0, h_a, w_dw, b_dw, ln_g, ln_b, 256, _gather_plan(later))
    wa_out = wa_out.reshape(D, D)
    wb_out = wb_out.reshape(D, D)
    n1, rstd1, x1b = _a_out_proj(g_a, wa_out, ba_out, xs, pg0, pb0, 512)
    k_all, v_all = _kv_proj(x1b, wkv, 512)
    q, z_b = _b_in_proj(x1b, wb_in, 512)
    per_group = [_attn_forward(q, k_all, v_all, g) for g in range(3)]
    o, lse, g_b = _attn_merge([og for og, _ in per_group], [lg for _, lg in per_group], z_b, 512)
    dr2, dr2b, loss8, dpg1, dpb1, dbb_out = _b_out_loss(g_b, wb_out, b_b_out, n1, pg0, pb0, pg1, pb1, tgt, 512)

    do, dhb, dd = _b_out_bwd(dr2b, wb_out, z_b, o, 512)
    dk_all = dv_all = None
    for g in range(3):
        dhb, dk_all, dv_all = _attn_backward(q, k_all, v_all, do, lse, dd, dhb, dk_all, dv_all, g)
    dr1, dr1b, dpg0, dpb0, dba_out = _b_in_bwd(dr2, dhb, dk_all, dv_all, wb_in, wkv, n1, rstd1, pg0, 256)

    p_kv = _wgrad("wgrad_k", x1b, dk_all, 768, 1024, total=N_DEV)
    p_kv = _wgrad("wgrad_v", x1b, dv_all, 768, 1024, total=N_DEV, at=N_DEV // 2, into=p_kv)
    p_b_in = _wgrad("wgrad_b_in", x1b, dhb, 512, 512)
    p_a_out = _wgrad("wgrad_a_out", g_a, dr1b, D, 1024).reshape(N_DEV, BLK, D)
    p_b_out = _wgrad("wgrad_b_out", g_b, dr2b, D, 1024).reshape(N_DEV, BLK, D)
    parts = [p_kv, p_b_in, p_a_out, p_b_out]
    (du1, dha, dln_g, dln_b, dbz), from_sibling = _a_out_bwd(dr1b, wa_out, u1, h_a, ln_g, ln_b, 512, _sibling_plan(parts))
    chip_sums = [_pair_add(f"pair_add_{k}", p, r, c_idx) for k, (p, r) in enumerate(zip(parts, from_sibling))]
    (dha, dw_dw, db_dw, dba, dbg), from_chips = _a_conv_bwd(du1, u0, h_a, dha, w_dw, 256, _chips_plan(chip_sums))
    p_a_in = _wgrad("wgrad_a_in", xb, dha, 384, 1024)
    (from_sibling_a,) = _exchange_sibling("reduce_sibling_a_in", [p_a_in])
    sum_a = _pair_add("pair_add_a_in", p_a_in, from_sibling_a, c_idx)
    (grad_x,), (from_chips_a,) = _a_in_bwd("a_in_bwd_0", dr1, dha, wa_in, 512, 0, plan=_chips_plan([sum_a]))
    (grad_x,), _ = _a_in_bwd("a_in_bwd_1", dr1, dha, wa_in, 512, 1, into=grad_x)

    reduced = [from_chips_a] + from_chips
    big_w = [a_w_in[0], kv_w, b_w_in[0], a_w_out[0], b_w_out[0]]
    big_m = [m_a_w_in[0], m_kv_w, m_b_w_in[0], m_a_w_out[0], m_b_w_out[0]]
    big_v = [v_a_w_in[0], v_kv_w, v_b_w_in[0], v_a_w_out[0], v_b_w_out[0]]
    big = [_sum_adamw(f"adamw_{k}", reduced[k], big_w[k], big_m[k], big_v[k]) for k in range(5)]

    rows = [dba[0:1], dbg[0:1], dbz[0:1], dw_dw[0:CONV_W], db_dw[0:1], dln_g[0:1], dln_b[0:1], dba_out[0:1],
            dbb_out[0:1], dpg0[0:1], dpg1[0:1], dpb0[0:1], dpb1[0:1], loss8[0:1]]
    n_rows = sum(r.shape[0] for r in rows)
    gpack = jnp.concatenate(rows + [jnp.zeros((GRAD_ROWS - n_rows, D), F32)], axis=0)
    (gall,) = _all_gather("gather_small_grads", [gpack], [F32])
    w_small = [a_b_in, a_w_dw, a_b_dw, a_ln_g, a_ln_b, a_b_out, b_b_out, post_ln_g, post_ln_b]
    m_small = [m_a_b_in, m_a_w_dw, m_a_b_dw, m_a_ln_g, m_a_ln_b, m_a_b_out, m_b_b_out, m_post_ln_g, m_post_ln_b]
    v_small = [v_a_b_in, v_a_w_dw, v_a_b_dw, v_a_ln_g, v_a_ln_b, v_a_b_out, v_b_b_out, v_post_ln_g, v_post_ln_b]
    loss11, g_small, d_small, nm_small, nv_small = _small_finish(gall, me.astype(jnp.int32).reshape(1), w_small, m_small, v_small)
    loss = loss11[0, 0]

    def ordered(bigs, smalls):
        a_in, kvw, b_in, a_out, b_out = bigs
        return [a_in[None], smalls[0], smalls[1], smalls[2], smalls[3], smalls[4], a_out[None], smalls[5],
                kvw, b_in[None], b_out[None], smalls[6], smalls[7], smalls[8]]

    grads = ordered([b[0] for b in big], g_small)
    deltas = ordered([b[1] for b in big], d_small)
    new_m = ordered([b[2] for b in big], nm_small)
    new_v = ordered([b[3] for b in big], nv_small)
    return (loss, grad_x.reshape(1, s, D), *grads, *deltas, *new_m, *new_v)
```

```python
import math

import numpy as np
import jax
import jax.numpy as jnp
from jax import lax
from jax.experimental import pallas as pl
from jax.experimental.pallas import tpu as pltpu

F32 = jnp.float32
BF16 = jnp.bfloat16
MESH = pl.DeviceIdType.MESH

D = 1024
N_DEV = 8
HEAD_DIM = 64
N_HEADS = 16
DILATIONS = (1, 4, 16)
BLK = 128
CONV_W = 31
ALPHA = (2.0 * 2) ** 0.25
LN_EPS = 1e-5
SLOPES = tuple(2.0 ** (-8.0 * (h + 1) / N_HEADS) for h in range(N_HEADS))
NEG = -1e30

ADAM_LR = 0.001
ADAM_B1 = 0.9
ADAM_B2 = 0.999
ADAM_EPS = 1e-08
ADAM_WD = 0.01
ADAM_STEP = 10

VMEM_CAP_MB = 64


def _params(vmem_mb, n_grid=0):
    sem = ("arbitrary",) * n_grid if n_grid else None
    return pltpu.CompilerParams(dimension_semantics=sem, vmem_limit_bytes=min(vmem_mb, VMEM_CAP_MB - 6) * 2 ** 20)


def _sds(shape, dtype):
    return jax.ShapeDtypeStruct(tuple(shape), dtype)


def _sigmoid(v):
    return jax.nn.sigmoid(v)


def _dsilu(v, s):
    return s * (1.0 + v * (1.0 - s))


def _ln_stats(r):
    mu = jnp.mean(r, axis=-1, keepdims=True)
    xc = r - mu
    var = jnp.mean(xc * xc, axis=-1, keepdims=True)
    rstd = lax.rsqrt(var + LN_EPS)
    return xc * rstd, rstd


def _ln_bwd(dn, n, rstd):
    m1 = jnp.mean(dn, axis=-1, keepdims=True)
    m2 = jnp.mean(dn * n, axis=-1, keepdims=True)
    return rstd * (dn - m1 - n * m2)


def _rowsum8(v):
    tm, c = v.shape
    return v.reshape(tm // 8, 8, c).sum(axis=0)


def _acc_init(i, *refs):
    @pl.when(i == 0)
    def _():
        for r in refs:
            r[...] = jnp.zeros(r.shape, r.dtype)


def _acc_finish(i, last, *refs):
    @pl.when(i == last)
    def _():
        for r in refs:
            r[...] = jnp.broadcast_to(jnp.sum(r[...], axis=0, keepdims=True), r.shape)


def _dot(a, b):
    return jnp.dot(a, b, preferred_element_type=F32)


def _dot_nt(a, b):
    return lax.dot_general(a, b, (((1,), (1,)), ((), ())), preferred_element_type=F32)


def _place():
    return lax.axis_index("x"), lax.axis_index("y"), lax.axis_index("c")


def _all_gather(name, arrays, dtypes, casts=()):
    n = len(arrays)
    nc = len(casts)

    def body(*refs):
        ins, cast_ins = refs[:n], refs[n:n + nc]
        outs, cast_outs = refs[n + nc:2 * n + nc], refs[2 * n + nc:2 * (n + nc)]
        stages = refs[2 * (n + nc):3 * n + 2 * nc]
        send_sems, recv_sems, local_sems = refs[3 * n + 2 * nc:]
        x, y, c = _place()
        me, sibling = (x, y, c), (x, y, 1 - c)
        chips = [(1 - x, y), (x, 1 - y), (1 - x, 1 - y)]

        def slot(ref, p):
            return ref.at[4 * p[0] + 2 * p[1] + p[2]]

        def copy(a, k, block, to, src=None):
            return pltpu.make_async_remote_copy(
                src_ref=slot(outs[a], block) if src is None else src, dst_ref=slot(outs[a], block),
                send_sem=send_sems.at[a, k], recv_sem=recv_sems.at[a, k], device_id=to, device_id_type=MESH)

        first, mine = [], []
        for a in range(n):
            stages[a][...] = ins[a][...].astype(stages[a].dtype)
            cp = pltpu.make_async_copy(stages[a], slot(outs[a], me), local_sems.at[a])
            cp.start()
            mine.append(cp)
            first.append(copy(a, 0, me, sibling, src=stages[a]))
            first += [copy(a, 1 + j, me, (*chip, c), src=stages[a]) for j, chip in enumerate(chips)]
        for cp in first:
            cp.start()
        for src, dst in zip(cast_ins, cast_outs):
            dst[...] = src[...].astype(BF16)
        passed = []
        for j, chip in enumerate(chips):
            for a in range(n):
                copy(a, 1 + j, (*chip, c), me).wait_recv()
                cp = copy(a, 4 + j, (*chip, c), sibling)
                cp.start()
                passed.append(cp)
        for a in range(n):
            copy(a, 0, sibling, me).wait_recv()
            for j, chip in enumerate(chips):
                copy(a, 4 + j, (*chip, 1 - c), me).wait_recv()
        for cp in first + passed:
            cp.wait_send()
        for cp in mine:
            cp.wait()

    vmem_bytes = sum(math.prod(a.shape) * (jnp.dtype(a.dtype).itemsize + jnp.dtype(dt).itemsize)
                     for a, dt in zip(arrays, dtypes)) + sum(math.prod(a.shape) * 6 for a in casts)
    vm = pl.BlockSpec(memory_space=pltpu.VMEM)
    return pl.pallas_call(
        body, name=name,
        out_shape=[_sds((N_DEV,) + a.shape, dt) for a, dt in zip(arrays, dtypes)] + [_sds(a.shape, BF16) for a in casts],
        in_specs=[vm] * (n + nc),
        out_specs=[pl.BlockSpec(memory_space=pl.ANY)] * n + [vm] * nc,
        scratch_shapes=[pltpu.VMEM(a.shape, dt) for a, dt in zip(arrays, dtypes)]
        + [pltpu.SemaphoreType.DMA((n, 7)), pltpu.SemaphoreType.DMA((n, 7)), pltpu.SemaphoreType.DMA((n,))],
        compiler_params=_params(vmem_bytes // 2 ** 20 + 8),
    )(*arrays, *casts)


class _Plan:
    def __init__(self, args, out_shape, scratch, start, mid, finish):
        self.args, self.out_shape, self.scratch = list(args), list(out_shape), list(scratch)
        self.start, self.mid, self.finish = start, mid, finish


def _gather_plan(shards):
    n = len(shards)

    def copies(ins, outs, sems):
        send_sems, recv_sems, local_sems = sems
        x, y, c = _place()
        me, sibling = (x, y, c), (x, y, 1 - c)
        chips = [(1 - x, y), (x, 1 - y), (1 - x, 1 - y)]

        def slot(ref, p):
            return ref.at[4 * p[0] + 2 * p[1] + p[2]]

        def copy(a, k, block, to, src=None):
            return pltpu.make_async_remote_copy(
                src_ref=slot(outs[a], block) if src is None else src, dst_ref=slot(outs[a], block),
                send_sem=send_sems.at[a, k], recv_sem=recv_sems.at[a, k], device_id=to, device_id_type=MESH)

        mine = [pltpu.make_async_copy(ins[a], slot(outs[a], me), local_sems.at[a]) for a in range(n)]
        first = [copy(a, 0, me, sibling, src=ins[a]) for a in range(n)]
        first += [copy(a, 1 + j, me, (*chip, c), src=ins[a]) for a in range(n) for j, chip in enumerate(chips)]
        arrive = [copy(a, 1 + j, (*chip, c), me) for j, chip in enumerate(chips) for a in range(n)]
        passed = [copy(a, 4 + j, (*chip, c), sibling) for j, chip in enumerate(chips) for a in range(n)]
        from_sibling = [copy(a, 0, sibling, me) for a in range(n)]
        from_sibling += [copy(a, 4 + j, (*chip, 1 - c), me) for a in range(n) for j, chip in enumerate(chips)]
        return mine, first, arrive, passed, from_sibling

    def start(ins, outs, sems):
        mine, first, _, _, _ = copies(ins, outs, sems)
        for cp in mine + first:
            cp.start()

    def mid(ins, outs, sems):
        _, _, arrive, passed, _ = copies(ins, outs, sems)
        for got, on in zip(arrive, passed):
            got.wait_recv()
            on.start()

    def finish(ins, outs, sems):
        mine, first, _, passed, from_sibling = copies(ins, outs, sems)
        for cp in from_sibling:
            cp.wait_recv()
        for cp in first + passed:
            cp.wait_send()
        for cp in mine:
            cp.wait()

    return _Plan(shards, [_sds((N_DEV,) + a.shape, a.dtype) for a in shards],
                 [pltpu.SemaphoreType.DMA((n, 7)), pltpu.SemaphoreType.DMA((n, 7)), pltpu.SemaphoreType.DMA((n,))],
                 start, mid, finish)


def _sibling_plan(parts):
    n = len(parts)

    def copies(ins, outs, sems):
        send_sems, recv_sems = sems
        x, y, c = _place()
        return [pltpu.make_async_remote_copy(
            src_ref=ins[a].at[2 * p + 1 - c], dst_ref=outs[a].at[p], send_sem=send_sems.at[a, p],
            recv_sem=recv_sems.at[a, p], device_id=(x, y, 1 - c), device_id_type=MESH)
            for a in range(n) for p in range(4)]

    def start(ins, outs, sems):
        for cp in copies(ins, outs, sems):
            cp.start()

    def finish(ins, outs, sems):
        cps = copies(ins, outs, sems)
        for cp in cps:
            cp.wait_recv()
        for cp in cps:
            cp.wait_send()

    return _Plan(parts, [_sds((4,) + p.shape[1:], p.dtype) for p in parts],
                 [pltpu.SemaphoreType.DMA((n, 4)), pltpu.SemaphoreType.DMA((n, 4))], start, None, finish)


def _chips_plan(sums):
    n = len(sums)

    def copies(ins, outs, sems):
        send_sems, recv_sems, local_sems = sems
        x, y, c = _place()
        my_chip = 2 * x + y
        chips = [(1 - x, y), (x, 1 - y), (1 - x, 1 - y)]
        mine = [pltpu.make_async_copy(ins[a].at[my_chip], outs[a].at[my_chip], local_sems.at[a]) for a in range(n)]
        remote = [pltpu.make_async_remote_copy(
            src_ref=ins[a].at[2 * px + py], dst_ref=outs[a].at[my_chip], send_sem=send_sems.at[a, k],
            recv_sem=recv_sems.at[a, k], device_id=(px, py, c), device_id_type=MESH)
            for a in range(n) for k, (px, py) in enumerate(chips)]
        return mine, remote

    def start(ins, outs, sems):
        mine, remote = copies(ins, outs, sems)
        for cp in mine + remote:
            cp.start()

    def finish(ins, outs, sems):
        mine, remote = copies(ins, outs, sems)
        for cp in remote:
            cp.wait_recv()
        for cp in remote:
            cp.wait_send()
        for cp in mine:
            cp.wait()

    return _Plan(sums, [_sds(s.shape, s.dtype) for s in sums],
                 [pltpu.SemaphoreType.DMA((n, 3)), pltpu.SemaphoreType.DMA((n, 3)), pltpu.SemaphoreType.DMA((n,))],
                 start, None, finish)


def _planned_call(plan, args, body, *, name, grid, in_specs, out_specs, out_shape, scratch_shapes=(), mid_step=None,
                  **kw):
    in_specs, out_specs, out_shape = list(in_specs), list(out_specs), list(out_shape)
    scratch_shapes = list(scratch_shapes)
    if plan is None:
        res = pl.pallas_call(body, name=name, grid=grid, in_specs=in_specs, out_specs=out_specs, out_shape=out_shape,
                             scratch_shapes=scratch_shapes, **kw)(*args)
        return list(res), []
    n_in, n_out, n_scr = len(in_specs), len(out_specs), len(scratch_shapes)
    p_in, p_out = len(plan.args), len(plan.out_shape)
    steps = grid[0]

    def fused(*refs):
        ins, pins = refs[:n_in], refs[n_in:n_in + p_in]
        o0 = n_in + p_in
        outs, pouts = refs[o0:o0 + n_out], refs[o0 + n_out:o0 + n_out + p_out]
        s0 = o0 + n_out + p_out
        scr, pscr = refs[s0:s0 + n_scr], refs[s0 + n_scr:]
        i = pl.program_id(0)

        @pl.when(i == 0)
        def _():
            plan.start(pins, pouts, pscr)

        body(*ins, *outs, *scr)
        if plan.mid is not None:
            @pl.when(i == mid_step)
            def _():
                plan.mid(pins, pouts, pscr)

        @pl.when(i == steps - 1)
        def _():
            plan.finish(pins, pouts, pscr)

    hbm = pl.BlockSpec(memory_space=pl.ANY)
    res = pl.pallas_call(
        fused, name=name, grid=grid, in_specs=in_specs + [hbm] * p_in, out_specs=out_specs + [hbm] * p_out,
        out_shape=out_shape + plan.out_shape, scratch_shapes=scratch_shapes + plan.scratch, **kw)(*args, *plan.args)
    return list(res[:n_out]), list(res[n_out:])


def _exchange_sibling(name, parts):
    n = len(parts)

    def body(*refs):
        ins, outs = refs[:n], refs[n:2 * n]
        send_sems, recv_sems = refs[2 * n:]
        x, y, c = _place()
        copies = []
        for a in range(n):
            for p in range(4):
                copies.append(pltpu.make_async_remote_copy(
                    src_ref=ins[a].at[2 * p + 1 - c], dst_ref=outs[a].at[p],
                    send_sem=send_sems.at[a, p], recv_sem=recv_sems.at[a, p],
                    device_id=(x, y, 1 - c), device_id_type=MESH))
        for cp in copies:
            cp.start()
        for cp in copies:
            cp.wait_recv()
        for cp in copies:
            cp.wait_send()

    return pl.pallas_call(
        body, name=name,
        out_shape=[_sds((4,) + p.shape[1:], p.dtype) for p in parts],
        in_specs=[pl.BlockSpec(memory_space=pl.ANY)] * n,
        out_specs=[pl.BlockSpec(memory_space=pl.ANY)] * n,
        scratch_shapes=[pltpu.SemaphoreType.DMA((n, 4)), pltpu.SemaphoreType.DMA((n, 4))],
    )(*parts)


def _pair_add(name, part, recv, c_idx):
    _, r, c = part.shape
    tr = min(r, 256)

    def body(c_ref, a_ref, b_ref, o_ref):
        o_ref[...] = a_ref[...] + b_ref[...]

    return pl.pallas_call(
        body, name=name,
        grid_spec=pltpu.PrefetchScalarGridSpec(
            num_scalar_prefetch=1, grid=(4, r // tr),
            in_specs=[pl.BlockSpec((1, tr, c), lambda p, i, cr: (2 * p + cr[0], i, 0)),
                      pl.BlockSpec((1, tr, c), lambda p, i, cr: (p, i, 0))],
            out_specs=pl.BlockSpec((1, tr, c), lambda p, i, cr: (p, i, 0))),
        out_shape=_sds((4, r, c), F32),
        compiler_params=_params(16, 2),
    )(c_idx, part, recv)


def _adamw_math(w, g, m, v):
    m = ADAM_B1 * m + (1.0 - ADAM_B1) * g
    v = ADAM_B2 * v + (1.0 - ADAM_B2) * (g * g)
    m_hat = m / (1.0 - ADAM_B1 ** ADAM_STEP)
    v_hat = v / (1.0 - ADAM_B2 ** ADAM_STEP)
    delta = -ADAM_LR * (m_hat / (jnp.sqrt(v_hat) + ADAM_EPS) + ADAM_WD * w)
    return delta, m, v


def _sum_adamw(name, recv, w, m, v):
    r, c = w.shape
    tr = min(r, 256)

    def body(p_ref, w_ref, m_ref, v_ref, g_ref, d_ref, nm_ref, nv_ref):
        g = (p_ref[0] + p_ref[1]) + (p_ref[2] + p_ref[3])
        g_ref[...] = g
        d_ref[...], nm_ref[...], nv_ref[...] = _adamw_math(w_ref[...], g, m_ref[...], v_ref[...])

    blk = pl.BlockSpec((tr, c), lambda i: (i, 0))
    return pl.pallas_call(
        body, name=name, grid=(r // tr,),
        in_specs=[pl.BlockSpec((4, tr, c), lambda i: (0, i, 0)), blk, blk, blk],
        out_specs=[blk] * 4, out_shape=[_sds((r, c), F32)] * 4,
        compiler_params=_params(24, 1),
    )(recv, w, m, v)


def _small_finish(gall, me, ws, ms, vs):
    n = len(ws)
    rows = gall.shape[1]

    def body(me_ref, gall_ref, *refs):
        w_refs, m_refs, v_refs = refs[:n], refs[n:2 * n], refs[2 * n:3 * n]
        loss_ref = refs[3 * n]
        outs = refs[3 * n + 1:7 * n + 1]
        gs, flat = refs[7 * n + 1:]
        acc = gall_ref[0]
        for j in range(1, N_DEV):
            acc = acc + gall_ref[j]
        gs[...] = acc
        loss_ref[...] = gs[43:44, 0:1]
        for k in range(3):
            flat[:, k * D:(k + 1) * D] = gs[k:k + 1, :]
        dev = me_ref[0]
        mine = pl.ds(pl.multiple_of(dev * BLK, BLK), BLK)
        grads = [flat[:, pl.ds(pl.multiple_of(dev * 384, BLK), 384)], gs[3:3 + CONV_W, mine][None]]
        grads += [gs[34 + k:35 + k, mine] for k in range(4)]
        grads += [gs[38:39, :], gs[39:41, :], gs[41:43, :]]
        for k in range(n):
            g = grads[k]
            outs[k][...] = g
            outs[n + k][...], outs[2 * n + k][...], outs[3 * n + k][...] = _adamw_math(
                w_refs[k][...], g, m_refs[k][...], v_refs[k][...])

    def whole(a):
        nd = a.ndim
        return pl.BlockSpec(a.shape, lambda i, me_ref: (0,) * nd)

    res = pl.pallas_call(
        body, name="small_finish",
        grid_spec=pltpu.PrefetchScalarGridSpec(
            num_scalar_prefetch=1, grid=(1,),
            in_specs=[whole(gall)] + [whole(a) for a in list(ws) + list(ms) + list(vs)],
            out_specs=[pl.BlockSpec((1, 1), lambda i, me_ref: (0, 0))] + [whole(a) for a in ws] * 4,
            scratch_shapes=[pltpu.VMEM((rows, D), F32), pltpu.VMEM((1, 3 * D), F32)]),
        out_shape=[_sds((1, 1), F32)] + [_sds(a.shape, F32) for a in ws] * 4,
    )(me, gall, *ws, *ms, *vs)
    return res[0], res[1:n + 1], res[n + 1:2 * n + 1], res[2 * n + 1:3 * n + 1], res[3 * n + 1:]


def _a_in_proj(x, w_g, b_full, tm):
    s = x.shape[0]
    npd = w_g.shape[2]

    def body(x_ref, w_ref, b_ref, h_ref, u0_ref, xb_ref):
        xb = x_ref[...].astype(BF16)
        xb_ref[...] = xb
        for j in range(N_DEV):
            sl = slice(npd * j, npd * (j + 1))
            h_ref[:, sl] = _dot(xb, w_ref[j]) + b_ref[:, sl]
        u0_ref[...] = h_ref[:, 0:D] * _sigmoid(h_ref[:, D:2 * D])

    row = lambda i: (i, 0)
    return pl.pallas_call(
        body, name="a_in_proj", grid=(s // tm,),
        in_specs=[pl.BlockSpec((tm, D), row), pl.BlockSpec(w_g.shape, lambda i: (0, 0, 0)),
                  pl.BlockSpec((1, 3 * D), lambda i: (0, 0))],
        out_specs=[pl.BlockSpec((tm, 3 * D), row), pl.BlockSpec((tm, D), row), pl.BlockSpec((tm, D), row)],
        out_shape=[_sds((s, 3 * D), F32), _sds((s, D), F32), _sds((s, D), BF16)],
        compiler_params=_params(44, 1),
    )(x, w_g, b_full)


CONV_HALO = 32
SUBLANES = 8
COPY_ROWS = 56


def _shifted_copies(buf, shs, tm):
    n = tm + CONV_HALO - SUBLANES
    for s in range(1, SUBLANES):
        for c0 in range(0, n, COPY_ROWS):
            c1 = min(c0 + COPY_ROWS, n)
            shs[s - 1, c0:c1, :] = buf[c0 + s:c1 + s, :]


LANES = 128
CONV_ROWS = 128


def _tap_windows(offsets):
    out = []
    for s in range(SUBLANES):
        taps = [(j, o // SUBLANES) for j, o in enumerate(offsets) if o % SUBLANES == s]
        lo, hi = min(a for _, a in taps), max(a for _, a in taps)
        out.append((s, lo, hi - lo, [(j, a - lo) for j, a in taps]))
    return out


def _window(buf, shs, s, r0, lo, span, lanes):
    n = CONV_ROWS + SUBLANES * span
    rows = pl.ds(r0 + SUBLANES * lo, n)
    v = buf[rows, lanes] if s == 0 else shs[s - 1, rows, lanes]
    return v.reshape(n // SUBLANES, SUBLANES, LANES)


def _spread_taps(i, w_ref, wb):
    @pl.when(i == 0)
    def _():
        for j in range(CONV_W):
            wb[j] = jnp.broadcast_to(w_ref[j:j + 1, :], (SUBLANES, D))


def _a_conv_gate(u0, h, w_dw, b_dw, ln_g, ln_b, w_out, b_out, x, pg, pb, tm, plan=None):
    s = u0.shape[0]
    per = tm // CONV_HALO

    def body(u0_ref, halo_ref, z_ref, w_ref, bdw_ref, g_ref, b_ref, wo_ref, bo_ref, x_ref, pg_ref, pb_ref,
             u1_ref, ga_ref, n_ref, rstd_ref, xb_ref, buf, shs, wb):
        i = pl.program_id(0)
        _spread_taps(i, w_ref, wb)
        buf[0:CONV_HALO, :] = jnp.where(i > 0, halo_ref[...], 0.0)
        buf[CONV_HALO:, :] = u0_ref[...]
        _shifted_copies(buf, shs, tm)
        off = CONV_HALO - (CONV_W - 1)

        windows = _tap_windows([off + j for j in range(CONV_W)])
        grp = CONV_ROWS // SUBLANES

        def chunk(it, carry):
            r0 = pl.multiple_of((it // (D // LANES)) * CONV_ROWS, CONV_ROWS)
            lanes = pl.ds(pl.multiple_of((it % (D // LANES)) * LANES, LANES), LANES)
            acc = jnp.broadcast_to(bdw_ref[:, lanes], (CONV_ROWS, LANES)).reshape(grp, SUBLANES, LANES)
            for s, lo, span, taps in windows:
                win = _window(buf, shs, s, r0, lo, span, lanes)
                for j, a in taps:
                    acc = acc + wb[j, :, lanes][None] * win[a:a + grp]
            u1_ref[pl.ds(r0, CONV_ROWS), lanes] = acc.reshape(CONV_ROWS, LANES)
            return carry

        lax.fori_loop(0, (tm // CONV_ROWS) * (D // LANES), chunk, 0)
        n, _ = _ln_stats(u1_ref[...])
        pre = n * g_ref[...] + b_ref[...]
        z = z_ref[...]
        ga = ((pre * _sigmoid(pre)) * (z * _sigmoid(z))).astype(BF16)
        ga_ref[...] = ga
        r = ALPHA * x_ref[...] + (_dot(ga, wo_ref[...]) + bo_ref[...])
        n1, rstd = _ln_stats(r)
        n_ref[...] = n1
        rstd_ref[...] = rstd
        xb_ref[...] = (n1 * pg_ref[...] + pb_ref[...]).astype(BF16)

    row = lambda i: (i, 0)
    vec = pl.BlockSpec((1, D), lambda i: (0, 0))
    tile = pl.BlockSpec((tm, D), row)
    steps = s // tm
    return _planned_call(
        plan, (u0, u0, h, w_dw, b_dw, ln_g, ln_b, w_out, b_out, x, pg, pb), body, name="a_conv_gate", grid=(steps,),
        mid_step=steps // 2,
        in_specs=[tile,
                  pl.BlockSpec((CONV_HALO, D), lambda i: (jnp.maximum(i * per - 1, 0), 0)),
                  pl.BlockSpec((tm, D), lambda i: (i, 2)),
                  pl.BlockSpec((32, D), lambda i: (0, 0)), vec, vec, vec,
                  pl.BlockSpec((D, D), lambda i: (0, 0)), vec, tile, vec, vec],
        out_specs=[tile, tile, tile, pl.BlockSpec((tm, 1), row), tile],
        out_shape=[_sds((s, D), F32), _sds((s, D), BF16), _sds((s, D), F32), _sds((s, 1), F32), _sds((s, D), BF16)],
        scratch_shapes=[pltpu.VMEM((tm + CONV_HALO, D), F32), pltpu.VMEM((SUBLANES - 1, tm + CONV_HALO, D), F32),
                        pltpu.VMEM((CONV_W, SUBLANES, D), F32)],
        compiler_params=_params(52, 1))


def _kv_proj(xb, w_g, tm):
    s = xb.shape[0]
    npd = w_g.shape[2]
    half = N_DEV // 2

    def body(x_ref, w_ref, k_ref, v_ref):
        xv = x_ref[...]
        for j in range(N_DEV):
            o_ref = k_ref if j < half else v_ref
            jj = j % half
            o_ref[:, npd * jj:npd * (jj + 1)] = _dot(xv, w_ref[j]).astype(BF16)

    row = lambda i: (i, 0)
    return pl.pallas_call(
        body, name="kv_proj", grid=(s // tm,),
        in_specs=[pl.BlockSpec((tm, D), row), pl.BlockSpec(w_g.shape, lambda i: (0, 0, 0))],
        out_specs=[pl.BlockSpec((tm, 3 * D), row), pl.BlockSpec((tm, 3 * D), row)],
        out_shape=[_sds((s, 3 * D), BF16), _sds((s, 3 * D), BF16)],
        compiler_params=_params(52, 1),
    )(xb, w_g)


def _b_in_proj(xb, w_g, tm):
    s = xb.shape[0]
    npd = w_g.shape[2]
    scale = HEAD_DIM ** -0.5

    def body(x_ref, w_ref, q_ref, z_ref):
        xv = x_ref[...]
        for j in range(N_DEV):
            hj = _dot(xv, w_ref[j])
            if j < 6:
                q_ref[:, npd * j:npd * (j + 1)] = (hj.astype(BF16) * scale).astype(BF16)
            else:
                z_ref[:, npd * (j - 6):npd * (j - 5)] = hj

    row = lambda i: (i, 0)
    return pl.pallas_call(
        body, name="b_in_proj", grid=(s // tm,),
        in_specs=[pl.BlockSpec((tm, D), row), pl.BlockSpec(w_g.shape, lambda i: (0, 0, 0))],
        out_specs=[pl.BlockSpec((tm, 3 * D), row), pl.BlockSpec((tm, D), row)],
        out_shape=[_sds((s, 3 * D), BF16), _sds((s, D), F32)],
        compiler_params=_params(44, 1),
    )(xb, w_g)


ATTN_NQ = {1: 8, 4: 2, 16: 1}
N_PAIRS = N_HEADS // 2


def _band_table(d):
    qi = np.arange(BLK)[:, None]
    kj = np.arange(2 * BLK)[None, :]
    dist = qi + BLK - kj
    ok = (dist >= 0) & (dist <= BLK)
    return jnp.asarray(np.where(ok, -(d * dist).astype(np.float32), np.float32(NEG)), dtype=F32)


def _slope_table():
    t = np.zeros((N_PAIRS, 8, 2 * BLK), np.float32)
    for h in range(N_HEADS):
        t[h // 2, h % 2, :] = SLOPES[h]
    return jnp.asarray(t)


def _head_masks():
    lane = lax.broadcasted_iota(jnp.int32, (1, BLK), 1)
    lo = (lane < HEAD_DIM).astype(BF16)
    return (lo, (1.0 - lo).astype(BF16))


def _pick_col(tile, lane, h):
    return jnp.sum(jnp.where(lane == h, tile, 0.0), axis=1, keepdims=True)


def _rows(base, n, d):
    return pl.ds(base, n) if d == 1 else pl.ds(base, n, stride=d)


ATTN_PRE = {1: 1, 4: 1, 16: 4}


def _regroup(src, dst, d1):
    n = src.shape[0] // d1
    for r1 in range(d1):
        dst[r1] = src[pl.ds(r1, n, stride=d1), :]


def _ungroup(src, dst, d1):
    n = dst.shape[0] // d1
    for r1 in range(d1):
        dst[pl.ds(r1, n, stride=d1), :] = src[r1]


def _grouped(shape, d1):
    return pltpu.VMEM((d1, shape[0] // d1, shape[1]), F32)


def _with_halo(halo_ref, tile_ref, b):
    if b == 0:
        return jnp.concatenate([halo_ref[...], tile_ref[0:BLK, :]], axis=0)
    return tile_ref[(b - 1) * BLK:(b + 1) * BLK, :]


def _attn_forward(q, k_all, v_all, g):
    d = DILATIONS[g]
    s = q.shape[0]
    nq = ATTN_NQ[d]
    d1 = ATTN_PRE[d]
    d2 = d // d1
    halo = BLK * d
    tile = nq * halo
    assert s % tile == 0
    pre = d1 > 1

    def body(q_ref, k_ref, kh_ref, v_ref, vh_ref, nd_ref, sl_ref, o_ref, l_ref, qf, kf, vf, *grouped):
        q1, k1, v1, l1, o1 = grouped if pre else (None,) * 5
        n = pl.program_id(0)
        hp = pl.program_id(1)
        if d > 1:
            qf[...] = q_ref[...].astype(F32)
            kf[0:halo, :] = kh_ref[...].astype(F32)
            kf[halo:, :] = k_ref[...].astype(F32)
            vf[0:halo, :] = vh_ref[...].astype(F32)
            vf[halo:, :] = v_ref[...].astype(F32)

        @pl.when(hp == 0)
        def _():
            l_ref[...] = jnp.zeros(l_ref.shape, F32)

        if pre:
            for src, dst in ((qf, q1), (kf, k1), (vf, v1), (l_ref, l1)):
                _regroup(src, dst, d1)

        def pick(nat, grp, r1):
            return grp.at[r1] if pre else nat

        col = lax.broadcasted_iota(jnp.int32, (2 * BLK, 2 * BLK), 1)
        lane = lax.broadcasted_iota(jnp.int32, (BLK, BLK), 1)
        masks = _head_masks()
        bias = jnp.concatenate([sl_ref[0, e:e + 1, :] * nd_ref[...] for e in range(2)], axis=0)
        bias0 = bias + jnp.where((n == 0) & (col < BLK), NEG, 0.0)
        for b in range(nq):
            for r in range(d):
                r1, r2 = r % d1, r // d1
                rq = _rows(b * (halo // d1) + r2, BLK, d2)
                rk = _rows(b * (halo // d1) + r2, 2 * BLK, d2)
                if d > 1:
                    q2 = pick(qf, q1, r1)[rq, :].astype(BF16)
                    kcat = pick(kf, k1, r1)[rk, :].astype(BF16)
                    vcat = pick(vf, v1, r1)[rk, :].astype(BF16)
                else:
                    q2 = q_ref[rq, :]
                    kcat = _with_halo(kh_ref, k_ref, b)
                    vcat = _with_halo(vh_ref, v_ref, b)
                sc = _dot_nt(jnp.concatenate([q2 * masks[0], q2 * masks[1]], axis=0), kcat)
                sc = sc + (bias0 if b == 0 else bias)
                m = jnp.max(sc, axis=1, keepdims=True)
                p = jnp.exp(sc - m)
                l = jnp.sum(p, axis=1, keepdims=True)
                oh = _dot(p.astype(BF16), vcat) / l
                lse = m + jnp.log(l)
                lt = pick(l_ref, l1, r1)[rq, :]
                lt = jnp.where(lane == 2 * hp, lse[0:BLK], lt)
                lt = jnp.where(lane == 2 * hp + 1, lse[BLK:], lt)
                pick(o_ref, o1, r1)[rq, :] = jnp.where(lane < HEAD_DIM, oh[0:BLK], oh[BLK:])
                pick(l_ref, l1, r1)[rq, :] = lt
        if pre:
            _ungroup(o1, o_ref, d1)
            _ungroup(l1, l_ref, d1)

    col_blk = lambda n, hp: (n, g * N_PAIRS + hp)
    halo_blk = lambda n, hp: (jnp.maximum(n * nq - 1, 0), g * N_PAIRS + hp)
    t_shape, w_shape = (tile, BLK), (tile + halo, BLK)
    scratch = [pltpu.VMEM(t_shape, F32), pltpu.VMEM(w_shape, F32), pltpu.VMEM(w_shape, F32)]
    if pre:
        scratch += [_grouped(t_shape, d1), _grouped(w_shape, d1), _grouped(w_shape, d1), _grouped(t_shape, d1),
                    _grouped(t_shape, d1)]
    return pl.pallas_call(
        body, name=f"attn_fwd_g{g}", grid=(s // tile, N_PAIRS),
        in_specs=[pl.BlockSpec((tile, BLK), col_blk),
                  pl.BlockSpec((tile, BLK), col_blk), pl.BlockSpec((halo, BLK), halo_blk),
                  pl.BlockSpec((tile, BLK), col_blk), pl.BlockSpec((halo, BLK), halo_blk),
                  pl.BlockSpec((BLK, 2 * BLK), lambda n, hp: (0, 0)),
                  pl.BlockSpec((1, 8, 2 * BLK), lambda n, hp: (hp, 0, 0))],
        out_specs=[pl.BlockSpec((tile, BLK), lambda n, hp: (n, hp)), pl.BlockSpec((tile, BLK), lambda n, hp: (n, 0))],
        out_shape=[_sds((s, D), F32), _sds((s, BLK), F32)],
        scratch_shapes=scratch, compiler_params=_params(40, 2),
    )(q, k_all, k_all, v_all, v_all, _band_table(d), _slope_table())


def _attn_backward(q, k_all, v_all, do, lse, dd, dhb, dk_all, dv_all, g):
    d = DILATIONS[g]
    s = q.shape[0]
    nq = ATTN_NQ[d]
    d1 = ATTN_PRE[d]
    d2 = d // d1
    halo = BLK * d
    tile = nq * halo
    nt = s // tile
    first = dk_all is None
    pre = d1 > 1

    def body(*refs):
        refs = list(refs)
        q_ref, k_ref, kh_ref, v_ref, vh_ref, nd_ref, sl_ref, do_ref, l_ref, dd_ref = refs[:10]
        del refs[:10 + (1 if first else 3)]
        dq_ref, dk_ref, dv_ref = refs[:3]
        qf, dof, kf, vf, dqf, dkf, dvf, ck, cv = refs[3:12]
        del refs[:12]
        if pre:
            q1, do1, k1, v1, l1, dd1, dq1, dk1, dv1 = refs
        else:
            q1 = do1 = k1 = v1 = l1 = dd1 = dq1 = None
            dk1, dv1 = dkf, dvf
        hp = pl.program_id(0)
        n = pl.program_id(1)

        @pl.when(n == 0)
        def _():
            ck[...] = jnp.zeros(ck.shape, F32)
            cv[...] = jnp.zeros(cv.shape, F32)

        dk1[...] = jnp.zeros(dk1.shape, F32)
        dv1[...] = jnp.zeros(dv1.shape, F32)

        def pick(nat, grp, r1):
            return grp.at[r1] if pre else nat

        @pl.when(n < nt)
        def _():
            if d > 1:
                qf[...] = q_ref[...].astype(F32)
                dof[...] = do_ref[...].astype(F32)
                kf[0:halo, :] = kh_ref[...].astype(F32)
                kf[halo:, :] = k_ref[...].astype(F32)
                vf[0:halo, :] = vh_ref[...].astype(F32)
                vf[halo:, :] = v_ref[...].astype(F32)
            if pre:
                for src, dst in ((qf, q1), (dof, do1), (kf, k1), (vf, v1), (l_ref, l1), (dd_ref, dd1)):
                    _regroup(src, dst, d1)
            col = lax.broadcasted_iota(jnp.int32, (2 * BLK, 2 * BLK), 1)
            lane = lax.broadcasted_iota(jnp.int32, (BLK, BLK), 1)
            masks = _head_masks()
            bias = jnp.concatenate([sl_ref[0, e:e + 1, :] * nd_ref[...] for e in range(2)], axis=0)
            bias0 = bias + jnp.where((n == 0) & (col < BLK), NEG, 0.0)
            for b in range(nq):
                for r in range(d):
                    r1, r2 = r % d1, r // d1
                    rq = _rows(b * (halo // d1) + r2, BLK, d2)
                    rk = _rows(b * (halo // d1) + r2, 2 * BLK, d2)
                    if d > 1:
                        q2 = pick(qf, q1, r1)[rq, :].astype(BF16)
                        do2 = pick(dof, do1, r1)[rq, :].astype(BF16)
                        kcat = pick(kf, k1, r1)[rk, :].astype(BF16)
                        vcat = pick(vf, v1, r1)[rk, :].astype(BF16)
                    else:
                        q2 = q_ref[rq, :]
                        do2 = do_ref[rq, :]
                        kcat = _with_halo(kh_ref, k_ref, b)
                        vcat = _with_halo(vh_ref, v_ref, b)
                    lt = pick(l_ref, l1, r1)[rq, :]
                    dt = pick(dd_ref, dd1, r1)[rq, :]
                    qs = jnp.concatenate([q2 * masks[0], q2 * masks[1]], axis=0)
                    dos = jnp.concatenate([do2 * masks[0], do2 * masks[1]], axis=0)
                    lcol = jnp.concatenate([_pick_col(lt, lane, 2 * hp + e) for e in range(2)], axis=0)
                    dcol = jnp.concatenate([_pick_col(dt, lane, 2 * hp + e) for e in range(2)], axis=0)
                    sc = _dot_nt(qs, kcat) + (bias0 if b == 0 else bias)
                    p = jnp.exp(sc - lcol)
                    ds = p * (_dot_nt(dos, vcat) - dcol)
                    dsb = ds.astype(BF16)
                    dq = _dot(dsb, kcat)
                    dq2 = (HEAD_DIM ** -0.5) * jnp.where(lane < HEAD_DIM, dq[0:BLK], dq[BLK:])
                    pick(dqf, dq1, r1)[rq, :] = dq2
                    pick(dkf, dk1, r1)[rk, :] += _dot(dsb.T, qs)
                    pick(dvf, dv1, r1)[rk, :] += _dot(p.astype(BF16).T, dos)
            if pre:
                _ungroup(dq1, dqf, d1)
            dq_ref[...] = dqf[...].astype(BF16)

        if pre:
            _ungroup(dk1, dkf, d1)
            _ungroup(dv1, dvf, d1)
        if tile > halo:
            dk_ref[0:tile - halo, :] = ck[0:tile - halo, :].astype(BF16)
            dv_ref[0:tile - halo, :] = cv[0:tile - halo, :].astype(BF16)
        dk_ref[tile - halo:, :] = (ck[tile - halo:, :] + dkf[0:halo, :]).astype(BF16)
        dv_ref[tile - halo:, :] = (cv[tile - halo:, :] + dvf[0:halo, :]).astype(BF16)
        ck[...] = dkf[halo:, :]
        cv[...] = dvf[halo:, :]

    cur = lambda n: jnp.minimum(n, nt - 1)
    col_blk = lambda hp, n: (cur(n), g * N_PAIRS + hp)
    halo_blk = lambda hp, n: (jnp.maximum(cur(n) * nq - 1, 0), g * N_PAIRS + hp)
    out_kv = lambda hp, n: (jnp.maximum(n - 1, 0), g * N_PAIRS + hp)
    small = pl.BlockSpec((tile, BLK), lambda hp, n: (cur(n), 0))
    hbm = pl.BlockSpec(memory_space=pl.ANY)
    in_specs = [pl.BlockSpec((tile, BLK), col_blk),
                pl.BlockSpec((tile, BLK), col_blk), pl.BlockSpec((halo, BLK), halo_blk),
                pl.BlockSpec((tile, BLK), col_blk), pl.BlockSpec((halo, BLK), halo_blk),
                pl.BlockSpec((BLK, 2 * BLK), lambda hp, n: (0, 0)),
                pl.BlockSpec((1, 8, 2 * BLK), lambda hp, n: (hp, 0, 0)),
                pl.BlockSpec((tile, BLK), lambda hp, n: (cur(n), hp)), small, small, hbm]
    args = [q, k_all, k_all, v_all, v_all, _band_table(d), _slope_table(), do, lse, dd, dhb]
    aliases = {10: 0}
    if not first:
        in_specs += [hbm, hbm]
        args += [dk_all, dv_all]
        aliases.update({11: 1, 12: 2})
    t_shape, w_shape = (tile, BLK), (tile + halo, BLK)
    tile_f32, wide_f32 = pltpu.VMEM(t_shape, F32), pltpu.VMEM(w_shape, F32)
    scratch = [tile_f32, tile_f32, wide_f32, wide_f32, tile_f32, wide_f32, wide_f32, tile_f32, tile_f32]
    if pre:
        tg, wg = _grouped(t_shape, d1), _grouped(w_shape, d1)
        scratch += [tg, tg, wg, wg, tg, tg, tg, wg, wg]
    return pl.pallas_call(
        body, name=f"attn_bwd_g{g}", grid=(N_PAIRS, nt + 1), in_specs=in_specs,
        out_specs=[pl.BlockSpec((tile, BLK), col_blk), pl.BlockSpec((tile, BLK), out_kv),
                   pl.BlockSpec((tile, BLK), out_kv)],
        out_shape=[_sds((s, 4 * D), BF16), _sds((s, 3 * D), BF16), _sds((s, 3 * D), BF16)],
        scratch_shapes=scratch, input_output_aliases=aliases, compiler_params=_params(48, 2),
    )(*args)


def _head_spread():
    spread = (np.arange(BLK)[:, None] == np.arange(D)[None, :] // HEAD_DIM).astype(np.float32)
    return jnp.asarray(spread, dtype=BF16)


def _attn_merge(os_, lses, z, tm):
    s = z.shape[0]

    def body(o0_ref, o1_ref, o2_ref, l0_ref, l1_ref, l2_ref, z_ref, e_ref, o_ref, l_ref, gb_ref):
        ls = [l0_ref[...], l1_ref[...], l2_ref[...]]
        m = jnp.maximum(jnp.maximum(ls[0], ls[1]), ls[2])
        lse = m + jnp.log(jnp.exp(ls[0] - m) + jnp.exp(ls[1] - m) + jnp.exp(ls[2] - m))
        l_ref[...] = lse
        o = jnp.zeros((tm, D), F32)
        for l_g, og_ref in zip(ls, (o0_ref, o1_ref, o2_ref)):
            wg = jnp.exp(l_g - lse)
            hi = wg.astype(BF16)
            lo = (wg - hi.astype(F32)).astype(BF16)
            o = o + (_dot(hi, e_ref[...]) + _dot(lo, e_ref[...])) * og_ref[...]
        o_ref[...] = o
        zz = z_ref[...]
        gb_ref[...] = (o * (zz * _sigmoid(zz))).astype(BF16)

    row = lambda i: (i, 0)
    wide = pl.BlockSpec((tm, D), row)
    stat = pl.BlockSpec((tm, BLK), row)
    return pl.pallas_call(
        body, name="attn_merge", grid=(s // tm,),
        in_specs=[wide, wide, wide, stat, stat, stat, wide, pl.BlockSpec((BLK, D), lambda i: (0, 0))],
        out_specs=[wide, stat, wide],
        out_shape=[_sds((s, D), F32), _sds((s, BLK), F32), _sds((s, D), BF16)],
        compiler_params=_params(40, 1),
    )(*os_, *lses, z, _head_spread())


def _b_out_loss(gb, w, b, n1, pg0, pb0, pg1, pb1, tgt, tm):
    s = gb.shape[0]
    last = s // tm - 1

    def body(gb_ref, w_ref, b_ref, n1_ref, pg0_ref, pb0_ref, pg1_ref, pb1_ref, t_ref,
             dr_ref, drb_ref, loss_ref, dpg_ref, dpb_ref, dbo_ref):
        i = pl.program_id(0)
        _acc_init(i, loss_ref, dpg_ref, dpb_ref, dbo_ref)
        x1 = n1_ref[...] * pg0_ref[...] + pb0_ref[...]
        r = ALPHA * x1 + (_dot(gb_ref[...], w_ref[...]) + b_ref[...])
        n, rstd = _ln_stats(r)
        err = (n * pg1_ref[...] + pb1_ref[...]) - t_ref[...]
        loss_ref[...] += _rowsum8(err * err)
        dx2 = err * (1.0 / D)
        dpg_ref[...] += _rowsum8(dx2 * n)
        dpb_ref[...] += _rowsum8(dx2)
        dr = _ln_bwd(dx2 * pg1_ref[...], n, rstd)
        dr_ref[...] = dr
        drb_ref[...] = dr.astype(BF16)
        dbo_ref[...] += _rowsum8(dr)
        _acc_finish(i, last, dpg_ref, dpb_ref, dbo_ref)

        @pl.when(i == last)
        def _():
            loss_ref[...] = jnp.broadcast_to((0.5 / D) * jnp.sum(loss_ref[...], keepdims=True), loss_ref.shape)

    row = lambda i: (i, 0)
    vec = pl.BlockSpec((1, D), lambda i: (0, 0))
    acc = pl.BlockSpec((8, D), lambda i: (0, 0))
    return pl.pallas_call(
        body, name="b_out_loss", grid=(s // tm,),
        in_specs=[pl.BlockSpec((tm, D), row), pl.BlockSpec((D, D), lambda i: (0, 0)), vec,
                  pl.BlockSpec((tm, D), row), vec, vec, vec, vec, pl.BlockSpec((tm, D), row)],
        out_specs=[pl.BlockSpec((tm, D), row), pl.BlockSpec((tm, D), row), acc, acc, acc, acc],
        out_shape=[_sds((s, D), F32), _sds((s, D), BF16)] + [_sds((8, D), F32)] * 4,
        compiler_params=_params(36, 1),
    )(gb, w, b, n1, pg0, pb0, pg1, pb1, tgt)


def _head_selector():
    sel = (np.arange(D)[:, None] // HEAD_DIM == np.arange(BLK)[None, :]).astype(np.float32)
    return jnp.asarray(sel, dtype=BF16)


def _b_out_bwd(drb, w, z, o, tm):
    s = drb.shape[0]

    def body(dr_ref, w_ref, z_ref, o_ref, sel_ref, do_ref, dh_ref, dd_ref):
        dg = _dot_nt(dr_ref[...], w_ref[...])
        zz = z_ref[...]
        sg = _sigmoid(zz)
        do = dg * (zz * sg)
        ov = o_ref[...]
        do_ref[...] = do.astype(BF16)
        dh_ref[...] = (dg * ov * _dsilu(zz, sg)).astype(BF16)
        prod = do * ov
        hi = prod.astype(BF16)
        lo = (prod - hi.astype(F32)).astype(BF16)
        dd_ref[...] = _dot(hi, sel_ref[...]) + _dot(lo, sel_ref[...])

    row = lambda i: (i, 0)
    return pl.pallas_call(
        body, name="b_out_bwd", grid=(s // tm,),
        in_specs=[pl.BlockSpec((tm, D), row), pl.BlockSpec((D, D), lambda i: (0, 0)),
                  pl.BlockSpec((tm, D), row), pl.BlockSpec((tm, D), row), pl.BlockSpec((D, BLK), lambda i: (0, 0))],
        out_specs=[pl.BlockSpec((tm, D), row), pl.BlockSpec((tm, D), lambda i: (i, 3)),
                   pl.BlockSpec((tm, BLK), row)],
        out_shape=[_sds((s, D), BF16), _sds((s, 4 * D), BF16), _sds((s, BLK), F32)],
        compiler_params=_params(36, 1),
    )(drb, w, z, o, _head_selector())


def _b_in_bwd(dr2, dhb, dk_all, dv_all, wb_g, wkv_g, n1, rstd1, pg0, tm):
    s = dr2.shape[0]
    last = s // tm - 1
    nb_, nkv = wb_g.shape[2], wkv_g.shape[2]
    half = N_DEV // 2

    def body(dr2_ref, dh_ref, dk_ref, dv_ref, wb_hbm, wkv_hbm, n_ref, rstd_ref, pg_ref,
             dr_ref, drb_ref, dpg_ref, dpb_ref, dbo_ref, wb, wkv):
        i = pl.program_id(0)

        @pl.when(i == 0)
        def _():
            pltpu.sync_copy(wb_hbm, wb)
            pltpu.sync_copy(wkv_hbm, wkv)

        _acc_init(i, dpg_ref, dpb_ref, dbo_ref)
        acc = ALPHA * dr2_ref[...]
        for j in range(N_DEV):
            acc = acc + _dot_nt(dh_ref[:, nb_ * j:nb_ * (j + 1)], wb[j])
            src = dk_ref if j < half else dv_ref
            jj = j % half
            acc = acc + _dot_nt(src[:, nkv * jj:nkv * (jj + 1)], wkv[j])
        n = n_ref[...]
        dpg_ref[...] += _rowsum8(acc * n)
        dpb_ref[...] += _rowsum8(acc)
        dr = _ln_bwd(acc * pg_ref[...], n, rstd_ref[...])
        dr_ref[...] = dr
        drb_ref[...] = dr.astype(BF16)
        dbo_ref[...] += _rowsum8(dr)
        _acc_finish(i, last, dpg_ref, dpb_ref, dbo_ref)

    row = lambda i: (i, 0)
    hbm = pl.BlockSpec(memory_space=pl.ANY)
    acc_spec = pl.BlockSpec((8, D), lambda i: (0, 0))
    return pl.pallas_call(
        body, name="b_in_bwd", grid=(s // tm,),
        in_specs=[pl.BlockSpec((tm, D), row), pl.BlockSpec((tm, 4 * D), row), pl.BlockSpec((tm, 3 * D), row),
                  pl.BlockSpec((tm, 3 * D), row), hbm, hbm, pl.BlockSpec((tm, D), row), pl.BlockSpec((tm, 1), row),
                  pl.BlockSpec((1, D), lambda i: (0, 0))],
        out_specs=[pl.BlockSpec((tm, D), row), pl.BlockSpec((tm, D), row), acc_spec, acc_spec, acc_spec],
        out_shape=[_sds((s, D), F32), _sds((s, D), BF16)] + [_sds((8, D), F32)] * 3,
        scratch_shapes=[pltpu.VMEM(wb_g.shape, BF16), pltpu.VMEM(wkv_g.shape, BF16)],
        compiler_params=_params(56, 1),
    )(dr2, dhb, dk_all, dv_all, wb_g, wkv_g, n1, rstd1, pg0)


def _a_out_bwd(drb, w, u1, h, ln_g, ln_b, tm, plan=None):
    s = drb.shape[0]
    last = s // tm - 1

    def body(dr_ref, w_ref, u1_ref, z_ref, g_ref, b_ref, du1_ref, dh_ref, dg_ref, db_ref, dbz_ref):
        i = pl.program_id(0)
        _acc_init(i, dg_ref, db_ref, dbz_ref)
        dga = _dot_nt(dr_ref[...], w_ref[...])
        n, rstd = _ln_stats(u1_ref[...])
        pre = n * g_ref[...] + b_ref[...]
        sp = _sigmoid(pre)
        zz = z_ref[...]
        sz = _sigmoid(zz)
        dz = dga * (pre * sp) * _dsilu(zz, sz)
        dh_ref[...] = dz.astype(BF16)
        dbz_ref[...] += _rowsum8(dz)
        dpre = dga * (zz * sz) * _dsilu(pre, sp)
        dg_ref[...] += _rowsum8(dpre * n)
        db_ref[...] += _rowsum8(dpre)
        du1_ref[...] = _ln_bwd(dpre * g_ref[...], n, rstd)
        _acc_finish(i, last, dg_ref, db_ref, dbz_ref)

    row = lambda i: (i, 0)
    vec = pl.BlockSpec((1, D), lambda i: (0, 0))
    acc_spec = pl.BlockSpec((8, D), lambda i: (0, 0))
    return _planned_call(
        plan, (drb, w, u1, h, ln_g, ln_b), body, name="a_out_bwd", grid=(s // tm,),
        in_specs=[pl.BlockSpec((tm, D), row), pl.BlockSpec((D, D), lambda i: (0, 0)), pl.BlockSpec((tm, D), row),
                  pl.BlockSpec((tm, D), lambda i: (i, 2)), vec, vec],
        out_specs=[pl.BlockSpec((tm, D), row), pl.BlockSpec((tm, D), lambda i: (i, 2)),
                   acc_spec, acc_spec, acc_spec],
        out_shape=[_sds((s, D), F32), _sds((s, 3 * D), BF16)] + [_sds((8, D), F32)] * 3,
        compiler_params=_params(32, 1))


def _a_conv_bwd(du1, u0, h, dha, w_dw, tm, plan=None):
    s = du1.shape[0]
    steps = s // tm
    per = tm // CONV_HALO
    pad = CONV_W - 1

    def body(du_ref, dun_ref, u0_ref, u0p_ref, h_ref, w_ref, dha_hbm,
             dh_ref, dw_ref, dbdw_ref, dba_ref, dbg_ref, dbuf, ubuf, wacc, dshs, ushs, wb, du0_buf):
        i = pl.program_id(0)
        _spread_taps(i, w_ref, wb)
        _acc_init(i, dbdw_ref, dba_ref, dbg_ref, wacc)
        dbuf[0:tm, :] = du_ref[...]
        dbuf[tm:, :] = jnp.where(i < steps - 1, dun_ref[...], 0.0)
        ubuf[0:CONV_HALO, :] = jnp.where(i > 0, u0p_ref[...], 0.0)
        ubuf[CONV_HALO:, :] = u0_ref[...]
        _shifted_copies(dbuf, dshs, tm)
        _shifted_copies(ubuf, ushs, tm)
        off = CONV_HALO - pad
        grp = CONV_ROWS // SUBLANES
        d_windows = _tap_windows([pad - j for j in range(CONV_W)])
        u_windows = _tap_windows([off + j for j in range(CONV_W)])

        def chunk(it, carry):
            r0 = pl.multiple_of((it // (D // LANES)) * CONV_ROWS, CONV_ROWS)
            lanes = pl.ds(pl.multiple_of((it % (D // LANES)) * LANES, LANES), LANES)
            rows = pl.ds(r0, CONV_ROWS)
            acc = jnp.zeros((grp, SUBLANES, LANES), F32)
            for s, lo, span, taps in d_windows:
                win = _window(dbuf, dshs, s, r0, lo, span, lanes)
                for j, a in taps:
                    acc = acc + wb[j, :, lanes][None] * win[a:a + grp]
            du0_buf[rows, lanes] = acc.reshape(CONV_ROWS, LANES)
            du3 = du_ref[rows, lanes].reshape(grp, SUBLANES, LANES)
            for s, lo, span, taps in u_windows:
                win = _window(ubuf, ushs, s, r0, lo, span, lanes)
                for j, a in taps:
                    wacc[j, :, lanes] += (du3 * win[a:a + grp]).sum(axis=0)
            return carry

        lax.fori_loop(0, (tm // CONV_ROWS) * (D // LANES), chunk, 0)
        du0 = du0_buf[...]
        sg = _sigmoid(h_ref[:, D:2 * D])
        da = du0 * sg
        dag = du0 * h_ref[:, 0:D] * (sg * (1.0 - sg))
        dh_ref[:, 0:D] = da.astype(BF16)
        dh_ref[:, D:2 * D] = dag.astype(BF16)
        dbdw_ref[...] += _rowsum8(du_ref[...])
        dba_ref[...] += _rowsum8(da)
        dbg_ref[...] += _rowsum8(dag)
        _acc_finish(i, steps - 1, dbdw_ref, dba_ref, dbg_ref)

        @pl.when(i == steps - 1)
        def _():
            for j in range(CONV_W):
                dw_ref[j:j + 1, :] = jnp.sum(wacc[j], axis=0, keepdims=True)
            dw_ref[CONV_W:, :] = jnp.zeros((32 - CONV_W, D), F32)

    row = lambda i: (i, 0)
    acc_spec = pl.BlockSpec((8, D), lambda i: (0, 0))
    return _planned_call(
        plan, (du1, du1, u0, u0, h, w_dw, dha), body, name="a_conv_bwd", grid=(steps,),
        in_specs=[pl.BlockSpec((tm, D), row),
                  pl.BlockSpec((CONV_HALO, D), lambda i: (jnp.minimum((i + 1) * per, s // CONV_HALO - 1), 0)),
                  pl.BlockSpec((tm, D), row),
                  pl.BlockSpec((CONV_HALO, D), lambda i: (jnp.maximum(i * per - 1, 0), 0)),
                  pl.BlockSpec((tm, 2 * D), lambda i: (i, 0)),
                  pl.BlockSpec((32, D), lambda i: (0, 0)), pl.BlockSpec(memory_space=pl.ANY)],
        out_specs=[pl.BlockSpec((tm, 2 * D), lambda i: (i, 0)),
                   pl.BlockSpec((32, D), lambda i: (0, 0)), acc_spec, acc_spec, acc_spec],
        out_shape=[_sds((s, 3 * D), BF16), _sds((32, D), F32)] + [_sds((8, D), F32)] * 3,
        scratch_shapes=[pltpu.VMEM((tm + CONV_HALO, D), F32), pltpu.VMEM((tm + CONV_HALO, D), F32),
                        pltpu.VMEM((CONV_W, 8, D), F32),
                        pltpu.VMEM((SUBLANES - 1, tm + CONV_HALO, D), F32),
                        pltpu.VMEM((SUBLANES - 1, tm + CONV_HALO, D), F32),
                        pltpu.VMEM((CONV_W, SUBLANES, D), F32), pltpu.VMEM((tm, D), F32)],
        input_output_aliases={6: 0}, compiler_params=_params(56, 1))


def _a_in_bwd(name, dr1, dha, w_g, tm, half, plan=None, into=None):
    s = dr1.shape[0]
    npd = w_g.shape[2]
    steps = s // tm // 2
    first = half * steps

    def body(*refs):
        dr_ref, dh_ref, w_ref = refs[:3]
        o_ref = refs[-1]
        acc = ALPHA * dr_ref[...]
        for j in range(N_DEV):
            acc = acc + _dot_nt(dh_ref[:, npd * j:npd * (j + 1)], w_ref[j])
        o_ref[...] = acc

    row = lambda i: (first + i, 0)
    in_specs = [pl.BlockSpec((tm, D), row), pl.BlockSpec((tm, 3 * D), row),
                pl.BlockSpec(w_g.shape, lambda i: (0, 0, 0))]
    args = (dr1, dha, w_g)
    kw = {}
    if into is not None:
        in_specs.append(pl.BlockSpec(memory_space=pl.ANY))
        args += (into,)
        kw["input_output_aliases"] = {3: 0}
    return _planned_call(
        plan, args, body, name=name, grid=(steps,), in_specs=in_specs,
        out_specs=[pl.BlockSpec((tm, D), row)], out_shape=[_sds((s, D), F32)],
        compiler_params=_params(36, 1), **kw)


def _wgrad(name, a, b, npd, ts, total=None, at=0, into=None):
    s = a.shape[0]
    n_blk = b.shape[1] // npd
    total = n_blk if total is None else total
    assert at % n_blk == 0

    def body(*refs):
        a_ref, b_ref = refs[:2]
        o_ref = refs[-1]
        si = pl.program_id(0)

        @pl.when(si == 0)
        def _():
            o_ref[...] = jnp.zeros(o_ref.shape, F32)

        a_t = a_ref[...].T
        for j in range(n_blk):
            o_ref[j] += _dot(a_t, b_ref[:, npd * j:npd * (j + 1)])

    in_specs = [pl.BlockSpec((ts, D), lambda si: (si, 0)), pl.BlockSpec((ts, n_blk * npd), lambda si: (si, 0))]
    args = [a, b]
    aliases = {}
    if into is not None:
        in_specs.append(pl.BlockSpec(memory_space=pl.ANY))
        args.append(into)
        aliases = {2: 0}
    return pl.pallas_call(
        body, name=name, grid=(s // ts,), in_specs=in_specs,
        out_specs=pl.BlockSpec((n_blk, D, npd), lambda si: (at // n_blk, 0, 0)),
        out_shape=_sds((total, D, npd), F32), input_output_aliases=aliases,
        compiler_params=_params(56, 1),
    )(*args)


SMALL_ROWS = 40
GRAD_ROWS = 48


def kernel(x, a_w_in, a_b_in, a_w_dw, a_b_dw, a_ln_g, a_ln_b, a_w_out, a_b_out, kv_w, b_w_in, b_w_out, b_b_out, post_ln_g, post_ln_b, loss_target, m_a_w_in, m_a_b_in, m_a_w_dw, m_a_b_dw, m_a_ln_g, m_a_ln_b, m_a_w_out, m_a_b_out, m_kv_w, m_b_w_in, m_b_w_out, m_b_b_out, m_post_ln_g, m_post_ln_b, v_a_w_in, v_a_b_in, v_a_w_dw, v_a_b_dw, v_a_ln_g, v_a_ln_b, v_a_w_out, v_a_b_out, v_kv_w, v_b_w_in, v_b_w_out, v_b_b_out, v_post_ln_g, v_post_ln_b):
    s = x.shape[1]
    assert x.shape == (1, s, D) and s % (DILATIONS[-1] * BLK) == 0
    xs = x.reshape(s, D)
    tgt = loss_target.reshape(s, D)
    me = 4 * lax.axis_index("x") + 2 * lax.axis_index("y") + lax.axis_index("c")
    c_idx = lax.axis_index("c").astype(jnp.int32).reshape(1)

    def small_pack(b_in, w_dw, b_dw, ln_g, ln_b, b_out):
        rows = [b_in.reshape(3, BLK), w_dw.reshape(CONV_W, BLK), b_dw.reshape(1, BLK), ln_g.reshape(1, BLK),
                ln_b.reshape(1, BLK), b_out.reshape(1, BLK)]
        n = sum(r.shape[0] for r in rows)
        return jnp.concatenate(rows + [jnp.zeros((SMALL_ROWS - n, BLK), F32)], axis=0)

    wa_in, wa_out, sm, *later = _all_gather(
        "gather_first", [a_w_in[0], a_w_out[0], small_pack(a_b_in, a_w_dw, a_b_dw, a_ln_g, a_ln_b, a_b_out)],
        [BF16, BF16, F32], casts=[kv_w, b_w_in[0], b_w_out[0]])
    wa_out = wa_out.reshape(D, D)
    ba_in = sm[:, 0:3, :].reshape(1, 3 * D)
    w_dw = jnp.concatenate([sm[:, 3:3 + CONV_W, :].transpose(1, 0, 2).reshape(CONV_W, D), jnp.zeros((1, D), F32)], axis=0)
    b_dw, ln_g, ln_b, ba_out = (sm[:, 34 + k, :].reshape(1, D) for k in range(4))
    pg0, pg1 = post_ln_g[0:1], post_ln_g[1:2]
    pb0, pb1 = post_ln_b[0:1], post_ln_b[1:2]

    h_a, u0, xb = _a_in_proj(xs, wa_in, ba_in, 512)
    (u1, g_a, n1, rstd1, x1b), (wkv, wb_in, wb_out) = _a_conv_gate(
        u0, h_a, w_dw, b_dw, ln_g, ln_b, wa_out, ba_out, xs, pg0, pb0, 256, _gather_plan(later))
    wb_out = wb_out.reshape(D, D)
    k_all, v_all = _kv_proj(x1b, wkv, 512)
    q, z_b = _b_in_proj(x1b, wb_in, 512)
    per_group = [_attn_forward(q, k_all, v_all, g) for g in range(3)]
    o, lse, g_b = _attn_merge([og for og, _ in per_group], [lg for _, lg in per_group], z_b, 512)
    dr2, dr2b, loss8, dpg1, dpb1, dbb_out = _b_out_loss(g_b, wb_out, b_b_out, n1, pg0, pb0, pg1, pb1, tgt, 512)

    do, dhb, dd = _b_out_bwd(dr2b, wb_out, z_b, o, 512)
    dk_all = dv_all = None
    for g in range(3):
        dhb, dk_all, dv_all = _attn_backward(q, k_all, v_all, do, lse, dd, dhb, dk_all, dv_all, g)
    dr1, dr1b, dpg0, dpb0, dba_out = _b_in_bwd(dr2, dhb, dk_all, dv_all, wb_in, wkv, n1, rstd1, pg0, 256)

    p_kv = _wgrad("wgrad_k", x1b, dk_all, 768, 1024, total=N_DEV)
    p_kv = _wgrad("wgrad_v", x1b, dv_all, 768, 1024, total=N_DEV, at=N_DEV // 2, into=p_kv)
    p_b_in = _wgrad("wgrad_b_in", x1b, dhb, 512, 512)
    p_a_out = _wgrad("wgrad_a_out", g_a, dr1b, D, 1024).reshape(N_DEV, BLK, D)
    p_b_out = _wgrad("wgrad_b_out", g_b, dr2b, D, 1024).reshape(N_DEV, BLK, D)
    parts = [p_kv, p_b_in, p_a_out, p_b_out]
    (du1, dha, dln_g, dln_b, dbz), from_sibling = _a_out_bwd(dr1b, wa_out, u1, h_a, ln_g, ln_b, 512, _sibling_plan(parts))
    chip_sums = [_pair_add(f"pair_add_{k}", p, r, c_idx) for k, (p, r) in enumerate(zip(parts, from_sibling))]
    (dha, dw_dw, db_dw, dba, dbg), from_chips = _a_conv_bwd(du1, u0, h_a, dha, w_dw, 256, _chips_plan(chip_sums))
    p_a_in = _wgrad("wgrad_a_in", xb, dha, 384, 1024)
    (from_sibling_a,) = _exchange_sibling("reduce_sibling_a_in", [p_a_in])
    sum_a = _pair_add("pair_add_a_in", p_a_in, from_sibling_a, c_idx)
    (grad_x,), (from_chips_a,) = _a_in_bwd("a_in_bwd_0", dr1, dha, wa_in, 512, 0, plan=_chips_plan([sum_a]))
    (grad_x,), _ = _a_in_bwd("a_in_bwd_1", dr1, dha, wa_in, 512, 1, into=grad_x)

    reduced = [from_chips_a] + from_chips
    big_w = [a_w_in[0], kv_w, b_w_in[0], a_w_out[0], b_w_out[0]]
    big_m = [m_a_w_in[0], m_kv_w, m_b_w_in[0], m_a_w_out[0], m_b_w_out[0]]
    big_v = [v_a_w_in[0], v_kv_w, v_b_w_in[0], v_a_w_out[0], v_b_w_out[0]]
    big = [_sum_adamw(f"adamw_{k}", reduced[k], big_w[k], big_m[k], big_v[k]) for k in range(5)]

    rows = [dba[0:1], dbg[0:1], dbz[0:1], dw_dw[0:CONV_W], db_dw[0:1], dln_g[0:1], dln_b[0:1], dba_out[0:1],
            dbb_out[0:1], dpg0[0:1], dpg1[0:1], dpb0[0:1], dpb1[0:1], loss8[0:1]]
    n_rows = sum(r.shape[0] for r in rows)
    gpack = jnp.concatenate(rows + [jnp.zeros((GRAD_ROWS - n_rows, D), F32)], axis=0)
    (gall,) = _all_gather("gather_small_grads", [gpack], [F32])
    w_small = [a_b_in, a_w_dw, a_b_dw, a_ln_g, a_ln_b, a_b_out, b_b_out, post_ln_g, post_ln_b]
    m_small = [m_a_b_in, m_a_w_dw, m_a_b_dw, m_a_ln_g, m_a_ln_b, m_a_b_out, m_b_b_out, m_post_ln_g, m_post_ln_b]
    v_small = [v_a_b_in, v_a_w_dw, v_a_b_dw, v_a_ln_g, v_a_ln_b, v_a_b_out, v_b_b_out, v_post_ln_g, v_post_ln_b]
    loss11, g_small, d_small, nm_small, nv_small = _small_finish(gall, me.astype(jnp.int32).reshape(1), w_small, m_small, v_small)
    loss = loss11[0, 0]

    def ordered(bigs, smalls):
        a_in, kvw, b_in, a_out, b_out = bigs
        return [a_in[None], smalls[0], smalls[1], smalls[2], smalls[3], smalls[4], a_out[None], smalls[5],
                kvw, b_in[None], b_out[None], smalls[6], smalls[7], smalls[8]]

    grads = ordered([b[0] for b in big], g_small)
    deltas = ordered([b[1] for b in big], d_small)
    new_m = ordered([b[2] for b in big], nm_small)
    new_v = ordered([b[3] for b in big], nv_small)
    return (loss, grad_x.reshape(1, s, D), *grads, *deltas, *new_m, *new_v)
```
